```python
import jax, jax.numpy as jnp
from jax import lax
import numpy as np

D_MODEL = 2048
BATCH = 4
SEQ = 2048
DEPTH = 1
DEC_BATCH = 32
DEC_SEQ = 1
PAST_LEN = 8192
PAGE_SIZE = 128

POOL_WIDTH = D_MODEL // 2
POOL_WINDOWS = (2, 4, 8, 16)
N_POOL_GROUPS = len(POOL_WINDOWS)
POOL_GROUP = POOL_WIDTH // N_POOL_GROUPS
POOL_STATE = max(POOL_WINDOWS) - 1
HEAD_DIM = 128
ATTN_WIDTH = D_MODEL - POOL_WIDTH
N_HEADS = ATTN_WIDTH // HEAD_DIM
DILATED = ((128, 1), (512, 4), (2048, 16))
MAX_WINDOW = max(w for w, _ in DILATED)
ROPE_DIM = HEAD_DIM // 4
ROPE_THETA = 500000.0
IN_WIDTH = POOL_WIDTH + 3 * ATTN_WIDTH
N_MEM = 256
MEM_HEADS = 4
MEM_HEAD_DIM = 128
MEM_WIDTH = MEM_HEADS * MEM_HEAD_DIM
D_FF = 4 * D_MODEL
EPS = 1e-6
BLOCK = 128
NEG_INF = -1e30

kernel_name = 'pool_dilated_hybrid_step'


def rmsnorm(x, g):
    xf = x.astype(jnp.float32)
    y = xf * lax.rsqrt(jnp.mean(xf * xf, axis=-1, keepdims=True) + EPS) * g.astype(jnp.float32)
    return y.astype(x.dtype)


def rope(x, pos):
    half = ROPE_DIM // 2
    inv = jnp.power(jnp.float32(ROPE_THETA), -jnp.arange(half, dtype=jnp.float32) * 2.0 / ROPE_DIM)
    ang = pos.astype(jnp.float32)[:, None] * inv[None, :]
    cos = jnp.cos(ang)[None, :, None, :]
    sin = jnp.sin(ang)[None, :, None, :]
    xf = x.astype(jnp.float32)
    x1, x2, rest = xf[..., :half], xf[..., half:ROPE_DIM], xf[..., ROPE_DIM:]
    out = jnp.concatenate([x1 * cos - x2 * sin, x2 * cos + x1 * sin, rest], axis=-1)
    return out.astype(x.dtype)


def pool_mix(u, prev, pos, w_pool, pool_scale):
    n, t, c = u.shape
    if prev is None:
        prev = jnp.zeros((n, POOL_STATE, c), u.dtype)
    ext = jnp.concatenate([prev, u], axis=1)
    cs = jnp.concatenate([jnp.zeros((n, 1, c), jnp.float32),
                          jnp.cumsum(ext.astype(jnp.float32), axis=1)], axis=1)
    p1 = POOL_STATE + 1
    outs = []
    for g, w in enumerate(POOL_WINDOWS):
        sl = slice(g * POOL_GROUP, (g + 1) * POOL_GROUP)
        wsum = cs[:, p1:p1 + t, sl] - cs[:, p1 - w:p1 - w + t, sl]
        cnt = jnp.minimum(w, pos + 1).astype(jnp.float32)[None, :, None]
        pooled = (wsum / cnt - u[..., sl].astype(jnp.float32)).astype(u.dtype)
        outs.append(jnp.einsum('ntc,ce->nte', pooled, w_pool[g]))
    out = jnp.concatenate(outs, axis=-1) * pool_scale
    return out, ext[:, -POOL_STATE:]


def band_attention(q, k, v, band):
    n, L, h, dh = q.shape
    nb = -(-L // BLOCK)
    pad = nb * BLOCK - L
    qb = jnp.pad(q, ((0, 0), (0, pad), (0, 0), (0, 0))).reshape(n, nb, BLOCK, h, dh)
    kb = jnp.pad(k, ((0, 0), (BLOCK, pad), (0, 0), (0, 0))).reshape(n, nb + 1, BLOCK, h, dh)
    vb = jnp.pad(v, ((0, 0), (BLOCK, pad), (0, 0), (0, 0))).reshape(n, nb + 1, BLOCK, h, dh)
    k2 = jnp.concatenate([kb[:, :-1], kb[:, 1:]], axis=2)
    v2 = jnp.concatenate([vb[:, :-1], vb[:, 1:]], axis=2)
    s = jnp.einsum('nbqhd,nbkhd->nbhqk', qb, k2, preferred_element_type=jnp.float32) * (dh ** -0.5)
    qi = jnp.arange(BLOCK)[:, None]
    kj = jnp.arange(2 * BLOCK)[None, :] - BLOCK
    dist = qi - kj
    key_pos = jnp.arange(nb)[:, None, None] * BLOCK + kj[None]
    mask = (dist >= 0)[None] & (dist <= band)[None] & (key_pos >= 0)
    s = jnp.where(mask[None, :, None], s, NEG_INF)
    m = jnp.max(s, axis=-1, keepdims=True)
    p = jnp.exp(s - m)
    den = jnp.sum(p, axis=-1, keepdims=True)
    o = jnp.einsum('nbhqk,nbkhd->nbqhd', p / den, v2.astype(jnp.float32))
    lse = (m + jnp.log(den))[..., 0].transpose(0, 1, 3, 2)
    return o.reshape(n, nb * BLOCK, h, dh)[:, :L], lse.reshape(n, nb * BLOCK, h)[:, :L]


def combine_branches(outs, lses):
    alpha = jax.nn.softmax(jnp.stack(lses, axis=0), axis=0)
    return jnp.einsum('gnth,gnthd->nthd', alpha, jnp.stack(outs, axis=0))


def dilated_attention_prompt(q, k, v):
    n, t, h, dh = q.shape
    outs, lses = [], []
    for w, d in DILATED:
        L = t // d
        def to_sub(a):
            return a.reshape(n, L, d, h, dh).transpose(0, 2, 1, 3, 4).reshape(n * d, L, h, dh)
        o, lse = band_attention(to_sub(q), to_sub(k), to_sub(v), w // d)
        outs.append(o.reshape(n, d, L, h, dh).transpose(0, 2, 1, 3, 4).reshape(n, t, h, dh))
        lses.append(lse.reshape(n, d, L, h).transpose(0, 2, 1, 3).reshape(n, t, h))
    return combine_branches(outs, lses)


def dilated_attention_sample(q, k, v, k_past, v_past):
    n, t, h, dh = q.shape
    win = k_past.shape[1]
    k_all = jnp.concatenate([k_past, k], axis=1)
    v_all = jnp.concatenate([v_past, v], axis=1)
    ti = jnp.arange(t)
    outs, lses = [], []
    for w, d in DILATED:
        offs = jnp.arange(w // d + 1) * d
        idx = win + ti[:, None] - offs[None, :]
        valid = idx >= 0
        idx = jnp.maximum(idx, 0)
        kg = k_all[:, idx]
        vg = v_all[:, idx]
        s = jnp.einsum('nthd,ntkhd->nthk', q, kg, preferred_element_type=jnp.float32) * (dh ** -0.5)
        s = jnp.where(valid[None, :, None, :], s, NEG_INF)
        m = jnp.max(s, axis=-1, keepdims=True)
        p = jnp.exp(s - m)
        den = jnp.sum(p, axis=-1, keepdims=True)
        outs.append(jnp.einsum('nthk,ntkhd->nthd', p / den, vg.astype(jnp.float32)))
        lses.append((m + jnp.log(den))[..., 0])
    return combine_branches(outs, lses)


def memory_kv(mem, g, w_kv):
    n, m, _ = mem.shape
    kv = jnp.einsum('nmd,de->nme', rmsnorm(mem, g), w_kv)
    k = kv[..., :MEM_WIDTH].reshape(n, m, MEM_HEADS, MEM_HEAD_DIM)
    v = kv[..., MEM_WIDTH:].reshape(n, m, MEM_HEADS, MEM_HEAD_DIM)
    return k, v


def memory_attention(h, mem_k, mem_v, w_xq, w_xo):
    n, t, _ = h.shape
    q = jnp.einsum('ntd,de->nte', h, w_xq).reshape(n, t, MEM_HEADS, MEM_HEAD_DIM)
    s = jnp.einsum('nthd,nmhd->nhtm', q, mem_k, preferred_element_type=jnp.float32) * (MEM_HEAD_DIM ** -0.5)
    p = jax.nn.softmax(s, axis=-1).astype(mem_v.dtype)
    o = jnp.einsum('nhtm,nmhd->nthd', p, mem_v).reshape(n, t, MEM_WIDTH)
    return jnp.einsum('nte,ed->ntd', o, w_xo)


def trunk_layer(x, pos, pool_prev, k_past, v_past, mem_k, mem_v, wl):
    (g_mix_pre, g_mix_post, g_mem_pre, g_mem_post, g_ffn_pre, g_ffn_post,
     w_in, w_pool, pool_scale, w_out, w_xq, w_xo, w_ff1, w_ff2) = wl
    n, t, _ = x.shape
    h = rmsnorm(x, g_mix_pre)
    proj = jnp.einsum('ntd,de->nte', h, w_in)
    u = proj[..., :POOL_WIDTH]
    q, k, v = [proj[..., POOL_WIDTH + i * ATTN_WIDTH:POOL_WIDTH + (i + 1) * ATTN_WIDTH].reshape(n, t, N_HEADS, HEAD_DIM)
               for i in range(3)]
    q = rope(q, pos)
    k = rope(k, pos)
    pool_out, pool_state = pool_mix(u, pool_prev, pos, w_pool, pool_scale)
    if k_past is None:
        attn = dilated_attention_prompt(q, k, v)
        keep = min(MAX_WINDOW, t)
        new_k, new_v = k[:, t - keep:], v[:, t - keep:]
    else:
        attn = dilated_attention_sample(q, k, v, k_past, v_past)
        new_k, new_v = k, v
    mixed = jnp.concatenate([pool_out, attn.astype(x.dtype).reshape(n, t, ATTN_WIDTH)], axis=-1)
    x = x + rmsnorm(jnp.einsum('nte,ed->ntd', mixed, w_out), g_mix_post)
    x = x + rmsnorm(memory_attention(rmsnorm(x, g_mem_pre), mem_k, mem_v, w_xq, w_xo), g_mem_post)
    hf = jnp.square(jax.nn.relu(jnp.einsum('ntd,df->ntf', rmsnorm(x, g_ffn_pre), w_ff1)))
    x = x + rmsnorm(jnp.einsum('ntf,fd->ntd', hf, w_ff2), g_ffn_post)
    return x, pool_state, new_k, new_v


def setup_inputs(seed: int = 0) -> dict:
    key = jax.random.key(seed)
    ks = jax.random.split(key, 32)
    f32 = jnp.float32

    def nrm(k, shape, fan_in):
        return jax.random.normal(k, shape, f32) * (fan_in ** -0.5)

    def gain(k):
        return 1.0 + 0.05 * jax.random.normal(k, (DEPTH, D_MODEL), f32)

    win = min(MAX_WINDOW, PAST_LEN)
    return {
        'x_prompt': jax.random.normal(ks[0], (BATCH, SEQ, D_MODEL), f32),
        'x_sample': jax.random.normal(ks[1], (DEC_BATCH, DEC_SEQ, D_MODEL), f32),
        'state_pool': jax.random.normal(ks[2], (DEPTH, DEC_BATCH, POOL_STATE, POOL_WIDTH), f32),
        'cache_attn_k': jax.random.normal(ks[3], (DEPTH, DEC_BATCH, win, N_HEADS, HEAD_DIM), f32),
        'cache_attn_v': jax.random.normal(ks[4], (DEPTH, DEC_BATCH, win, N_HEADS, HEAD_DIM), f32),
        'cache_mem_k': jax.random.normal(ks[5], (DEPTH, DEC_BATCH, N_MEM, MEM_HEADS, MEM_HEAD_DIM), f32),
        'cache_mem_v': jax.random.normal(ks[6], (DEPTH, DEC_BATCH, N_MEM, MEM_HEADS, MEM_HEAD_DIM), f32),
        'mem_prompt': jax.random.normal(ks[7], (BATCH, N_MEM, D_MODEL), f32),
        'g_mix_pre': gain(ks[8]),
        'g_mix_post': gain(ks[9]),
        'g_mem_pre': gain(ks[10]),
        'g_mem_post': gain(ks[11]),
        'g_ffn_pre': gain(ks[12]),
        'g_ffn_post': gain(ks[13]),
        'g_mem_kv': gain(ks[14]),
        'w_in': nrm(ks[15], (DEPTH, D_MODEL, IN_WIDTH), D_MODEL),
        'w_pool': nrm(ks[16], (DEPTH, N_POOL_GROUPS, POOL_GROUP, POOL_GROUP), POOL_GROUP),
        'pool_scale': 1.0 + 0.1 * jax.random.normal(ks[17], (DEPTH, POOL_WIDTH), f32),
        'w_out': nrm(ks[18], (DEPTH, D_MODEL, D_MODEL), D_MODEL),
        'w_xq': nrm(ks[19], (DEPTH, D_MODEL, MEM_WIDTH), D_MODEL),
        'w_mem_kv': nrm(ks[20], (DEPTH, D_MODEL, 2 * MEM_WIDTH), D_MODEL),
        'w_xo': nrm(ks[21], (DEPTH, MEM_WIDTH, D_MODEL), MEM_WIDTH),
        'w_ff1': nrm(ks[22], (DEPTH, D_MODEL, D_FF), D_MODEL),
        'w_ff2': nrm(ks[23], (DEPTH, D_FF, D_MODEL), D_FF),
    }


def reference(x_prompt, x_sample, state_pool, cache_attn_k, cache_attn_v, cache_mem_k, cache_mem_v,
              mem_prompt, g_mix_pre, g_mix_post, g_mem_pre, g_mem_post, g_ffn_pre, g_ffn_post, g_mem_kv,
              w_in, w_pool, pool_scale, w_out, w_xq, w_mem_kv, w_xo, w_ff1, w_ff2):
    pos_p = jnp.arange(x_prompt.shape[1], dtype=jnp.int32)
    pos_s = PAST_LEN + jnp.arange(x_sample.shape[1], dtype=jnp.int32)
    yp, ys = x_prompt, x_sample
    pool_p, pool_s, k_p, v_p, k_s, v_s, mk_p, mv_p = [], [], [], [], [], [], [], []
    for l in range(DEPTH):
        wl = (g_mix_pre[l], g_mix_post[l], g_mem_pre[l], g_mem_post[l], g_ffn_pre[l], g_ffn_post[l],
              w_in[l], w_pool[l], pool_scale[l], w_out[l], w_xq[l], w_xo[l], w_ff1[l], w_ff2[l])
        mk, mv = memory_kv(mem_prompt, g_mem_kv[l], w_mem_kv[l])
        yp, sp, kn, vn = trunk_layer(yp, pos_p, None, None, None, mk, mv, wl)
        ys, ss, ksn, vsn = trunk_layer(ys, pos_s, state_pool[l], cache_attn_k[l], cache_attn_v[l],
                                       cache_mem_k[l], cache_mem_v[l], wl)
        pool_p.append(sp)
        pool_s.append(ss)
        k_p.append(kn)
        v_p.append(vn)
        k_s.append(ksn)
        v_s.append(vsn)
        mk_p.append(mk)
        mv_p.append(mv)
    return (yp, ys, jnp.stack(pool_p), jnp.stack(pool_s), jnp.stack(k_p), jnp.stack(v_p),
            jnp.stack(k_s), jnp.stack(v_s), jnp.stack(mk_p), jnp.stack(mv_p))
```

```python
import functools

import jax
import jax.numpy as jnp
from jax import lax
from jax.experimental import pallas as pl
from jax.experimental.pallas import tpu as pltpu

D_MODEL = 2048
POOL_WIDTH = 1024
POOL_WINDOWS = (2, 4, 8, 16)
POOL_GROUP = POOL_WIDTH // len(POOL_WINDOWS)
POOL_STATE = max(POOL_WINDOWS) - 1
HEAD_DIM = 128
ATTN_WIDTH = 1024
N_HEADS = ATTN_WIDTH // HEAD_DIM
DILATED = ((128, 1), (512, 4), (2048, 16))
ROPE_DIM = HEAD_DIM // 4
ROPE_HALF = ROPE_DIM // 2
ROPE_THETA = 500000.0
N_MEM = 256
MEM_HEADS = 4
MEM_WIDTH = MEM_HEADS * HEAD_DIM
EPS = 1e-6
BLOCK = 128
NEG_INF = -1e30
PAST_LEN = 8192
HALO = 16

F32 = jnp.float32
BF16 = jnp.bfloat16

VMEM_LIMIT_BYTES = 56 * 1024 * 1024

FLAT, HEADS = "flat", "heads"


def _params(n_grid_axes):
    return pltpu.CompilerParams(
        dimension_semantics=("arbitrary",) * n_grid_axes,
        vmem_limit_bytes=VMEM_LIMIT_BYTES,
    )


def _resident(block_shape, index_map):
    return pl.BlockSpec(block_shape, index_map, pipeline_mode=pl.Buffered(1))


def _rmsnorm(x, g):
    ms = jnp.mean(x * x, axis=-1, keepdims=True)
    return x * lax.rsqrt(ms + EPS) * g


def _dot(a, b):
    return jnp.dot(a, b, preferred_element_type=F32)


def _dot_nt(a, b):
    return lax.dot_general(a, b, (((1,), (1,)), ((), ())), preferred_element_type=F32)


def _store_heads(o_ref, layout, h, heads, rows, value):
    if layout == FLAT:
        o_ref[:, h * HEAD_DIM:(h + 1) * HEAD_DIM] = value
    else:
        o_ref[pl.ds(h, rows, stride=heads), :] = value


def _out_block(layout, tm, heads):
    return (tm, heads * HEAD_DIM) if layout == FLAT else (tm * heads, HEAD_DIM)


def _norm_proj_kernel(x_ref, g_ref, w_ref, cos_ref, sin_ref, *out_refs, tm, ncol, rope_blocks, outputs):
    heads = ncol // HEAD_DIM
    xn = _rmsnorm(x_ref[...], g_ref[...]).astype(BF16)
    for c in sorted({c for c, _ in outputs}):
        acc = _dot(xn, w_ref[:, c * ncol:(c + 1) * ncol])
        if c in rope_blocks:
            cos = cos_ref[...]
            sin = sin_ref[...]
            lane = lax.broadcasted_iota(jnp.int32, cos.shape, 1)
        for h in range(heads):
            a = acc[:, h * HEAD_DIM:(h + 1) * HEAD_DIM]
            if c in rope_blocks:
                partner = jnp.where(lane < ROPE_HALF,
                                    pltpu.roll(a, HEAD_DIM - ROPE_HALF, 1),
                                    pltpu.roll(a, ROPE_HALF, 1))
                a = a * cos + partner * sin
            for (oc, layout), o_ref in zip(outputs, out_refs):
                if oc == c:
                    _store_heads(o_ref, layout, h, heads, tm, a)


def _norm_proj(x, g, w, cos, sin, *, tm, ncol, rope_blocks, outputs, name):
    m, d = x.shape
    heads = ncol // HEAD_DIM
    assert w.shape[0] == d and w.shape[1] % ncol == 0 and m % tm == 0
    table_tiles = cos.shape[0] // tm
    kern = functools.partial(_norm_proj_kernel, tm=tm, ncol=ncol, rope_blocks=rope_blocks, outputs=outputs)
    blocks = [_out_block(layout, tm, heads) for _, layout in outputs]
    return pl.pallas_call(
        kern,
        grid=(m // tm,),
        in_specs=[
            pl.BlockSpec((tm, d), lambda i: (i, 0)),
            _resident((1, d), lambda i: (0, 0)),
            _resident(w.shape, lambda i: (0, 0)),
            pl.BlockSpec((tm, HEAD_DIM), lambda i: (i % table_tiles, 0)),
            pl.BlockSpec((tm, HEAD_DIM), lambda i: (i % table_tiles, 0)),
        ],
        out_specs=[pl.BlockSpec(b, lambda i: (i, 0)) for b in blocks],
        out_shape=[jax.ShapeDtypeStruct((m // tm * b[0], b[1]), F32) for b in blocks],
        compiler_params=_params(1),
        name=name,
    )(x, g, w, cos, sin)


def _rope_tables(pos):
    inv = jnp.power(jnp.float32(ROPE_THETA), -jnp.arange(ROPE_HALF, dtype=F32) * 2.0 / ROPE_DIM)
    ang = pos.astype(F32)[:, None] * inv[None, :]
    cos, sin = jnp.cos(ang), jnp.sin(ang)
    n = pos.shape[0]
    pad = HEAD_DIM - ROPE_DIM
    cos_t = jnp.concatenate([cos, cos, jnp.ones((n, pad), F32)], axis=1)
    sin_t = jnp.concatenate([-sin, sin, jnp.zeros((n, pad), F32)], axis=1)
    return cos_t, sin_t


def _pool_prompt_kernel(u_ref, halo_ref, wp_ref, scale_ref, o_ref, ext_ref, *, tm, tiles_per_seq):
    t_in_seq = pl.program_id(0) % tiles_per_seq
    keep = (t_in_seq > 0).astype(F32)
    ext_ref[0:HALO, :] = halo_ref[...] * keep
    ext_ref[HALO:HALO + tm, :] = u_ref[...]
    pos = t_in_seq * tm + lax.broadcasted_iota(jnp.int32, (tm, 1), 0)
    for g, w in enumerate(POOL_WINDOWS):
        cols = slice(g * POOL_GROUP, (g + 1) * POOL_GROUP)
        wsum = ext_ref[HALO:HALO + tm, cols]
        for back in range(1, w):
            wsum = wsum + ext_ref[HALO - back:HALO - back + tm, cols]
        cnt = jnp.minimum(w, pos + 1).astype(F32)
        pooled = (wsum / cnt - u_ref[:, cols]).astype(BF16)
        o_ref[:, cols] = (_dot(pooled, wp_ref[g]) * scale_ref[:, cols]).astype(o_ref.dtype)


def _pool_prompt(u, w_pool, pool_scale, *, tm, seq):
    m, c = u.shape
    tiles_per_seq = seq // tm
    halo_blocks = tm // HALO
    kern = functools.partial(_pool_prompt_kernel, tm=tm, tiles_per_seq=tiles_per_seq)
    return pl.pallas_call(
        kern,
        grid=(m // tm,),
        in_specs=[
            pl.BlockSpec((tm, c), lambda i: (i, 0)),
            pl.BlockSpec((HALO, c), lambda i: (jnp.maximum(i * halo_blocks - 1, 0), 0)),
            _resident(w_pool.shape, lambda i: (0, 0, 0)),
            _resident((1, c), lambda i: (0, 0)),
        ],
        out_specs=pl.BlockSpec((tm, c), lambda i: (i, 0)),
        out_shape=jax.ShapeDtypeStruct((m, c), BF16),
        scratch_shapes=[pltpu.VMEM((HALO + tm, c), F32)],
        compiler_params=_params(1),
        name="pool_prompt",
    )(u, u, w_pool, pool_scale)


def _pool_sample_kernel(u_ref, prev_ref, wp_ref, scale_ref, o_ref, state_ref, *, pos):
    for j in range(POOL_STATE - 1):
        state_ref[j] = prev_ref[j + 1]
    state_ref[POOL_STATE - 1] = u_ref[...]
    for g, w in enumerate(POOL_WINDOWS):
        cols = slice(g * POOL_GROUP, (g + 1) * POOL_GROUP)
        wsum = u_ref[:, cols]
        for back in range(1, w):
            wsum = wsum + prev_ref[POOL_STATE - back, :, cols]
        cnt = float(min(w, pos + 1))
        pooled = (wsum / cnt - u_ref[:, cols]).astype(BF16)
        o_ref[:, cols] = (_dot(pooled, wp_ref[g]) * scale_ref[:, cols]).astype(o_ref.dtype)


def _pool_sample(u, prev_t, w_pool, pool_scale, *, pos):
    n, c = u.shape
    kern = functools.partial(_pool_sample_kernel, pos=pos)
    return pl.pallas_call(
        kern,
        out_shape=[jax.ShapeDtypeStruct((n, c), BF16), jax.ShapeDtypeStruct(prev_t.shape, F32)],
        compiler_params=pltpu.CompilerParams(vmem_limit_bytes=VMEM_LIMIT_BYTES),
        name="pool_sample",
    )(u, prev_t, w_pool, pool_scale)


def _attn_prompt_kernel(q_ref, k_ref, v_ref, o_ref, *scratch, seq):
    n_br = len(DILATED)
    ob_refs, mb_refs, sb_refs = scratch[:n_br], scratch[n_br:2 * n_br], scratch[2 * n_br:]
    scale = HEAD_DIM ** -0.5
    qi = lax.broadcasted_iota(jnp.int32, (BLOCK, 2 * BLOCK), 0)
    kj = lax.broadcasted_iota(jnp.int32, (BLOCK, 2 * BLOCK), 1)
    mask2 = ((kj < BLOCK) & (kj >= qi)) | ((kj >= BLOCK) & (kj - BLOCK <= qi))
    mask1 = (lax.broadcasted_iota(jnp.int32, (BLOCK, BLOCK), 1)
             <= lax.broadcasted_iota(jnp.int32, (BLOCK, BLOCK), 0))

    def rows(block, r, d):
        if d == 1:
            start = block * BLOCK
            return pl.ds(start if isinstance(start, int) else pl.multiple_of(start, BLOCK), BLOCK)
        return pl.ds(block * (BLOCK * d) + r, BLOCK, stride=d)

    def finish(g, s, mask, v, out_rows):
        s = jnp.where(mask, s * scale, NEG_INF)
        m = jnp.max(s, axis=-1, keepdims=True)
        p = jnp.exp(s - m)
        ob_refs[g][out_rows, :] = _dot(p.astype(BF16), v)
        mb_refs[g][out_rows, :] = jnp.broadcast_to(m, (BLOCK, HEAD_DIM))
        sb_refs[g][out_rows, :] = jnp.broadcast_to(jnp.sum(p, axis=-1, keepdims=True), (BLOCK, HEAD_DIM))

    for g, (_, d) in enumerate(DILATED):
        n_blocks = seq // d // BLOCK

        def first_block(r, carry, g=g, d=d):
            rw = rows(0, r, d)
            q = q_ref[rw, :].astype(BF16)
            k = k_ref[rw, :].astype(BF16)
            v = v_ref[rw, :].astype(BF16)
            finish(g, _dot_nt(q, k), mask1, v, rw)
            return carry

        def later_block(idx, carry, g=g, d=d, n_blocks=n_blocks):
            r = idx // (n_blocks - 1)
            b = idx % (n_blocks - 1) + 1
            cur = rows(b, r, d)
            prev = rows(b - 1, r, d)
            q = q_ref[cur, :].astype(BF16)
            k2 = jnp.concatenate([k_ref[prev, :], k_ref[cur, :]], axis=0).astype(BF16)
            v2 = jnp.concatenate([v_ref[prev, :], v_ref[cur, :]], axis=0).astype(BF16)
            finish(g, _dot_nt(q, k2), mask2, v2, cur)
            return carry

        if d == 1:
            first_block(0, 0)
        else:
            lax.fori_loop(0, d, first_block, 0)
        if n_blocks > 1:
            lax.fori_loop(0, d * (n_blocks - 1), later_block, 0)

    chunk = 256

    def combine(c, carry):
        rw = pl.ds(pl.multiple_of(c * chunk, chunk), chunk)
        ms = [mb[rw, :] for mb in mb_refs]
        m = functools.reduce(jnp.maximum, ms)
        num = jnp.zeros((chunk, HEAD_DIM), F32)
        den = jnp.zeros((chunk, HEAD_DIM), F32)
        for g in range(n_br):
            wgt = jnp.exp(ms[g] - m)
            num = num + wgt * ob_refs[g][rw, :]
            den = den + wgt * sb_refs[g][rw, :]
        o_ref[rw, :] = (num / den).astype(o_ref.dtype)
        return carry

    lax.fori_loop(0, seq // chunk, combine, 0)


def _attn_prompt(q, k, v, *, seq):
    m, width = q.shape
    n_seq = m // seq
    n_heads = width // HEAD_DIM
    spec = pl.BlockSpec((seq, HEAD_DIM), lambda n, h: (n, h))
    kern = functools.partial(_attn_prompt_kernel, seq=seq)
    return pl.pallas_call(
        kern,
        grid=(n_seq, n_heads),
        in_specs=[spec, spec, spec],
        out_specs=spec,
        out_shape=jax.ShapeDtypeStruct((m, width), BF16),
        scratch_shapes=[pltpu.VMEM((seq, HEAD_DIM), F32) for _ in range(3 * len(DILATED))],
        compiler_params=_params(2),
        name="attn_prompt",
    )(q, k, v)


def _attn_sample_kernel(q_ref, kn_ref, vn_ref, *refs):
    n_br = len(DILATED)
    kc_refs, vc_refs, o_ref = refs[:n_br], refs[n_br:2 * n_br], refs[2 * n_br]
    scale = HEAD_DIM ** -0.5
    q = q_ref[...]
    s_new = jnp.sum(q * kn_ref[...], axis=-1, keepdims=True) * scale
    s_br = [jnp.sum(kc[...] * q[None], axis=-1, keepdims=True) * scale for kc in kc_refs]
    m = s_new
    for s in s_br:
        m = jnp.maximum(m, jnp.max(s, axis=0))
    p_new = jnp.exp(s_new - m) * float(n_br)
    den = p_new
    num = p_new * vn_ref[...]
    for s, vc in zip(s_br, vc_refs):
        p = jnp.exp(s - m[None])
        den = den + jnp.sum(p, axis=0)
        num = num + jnp.sum(p * vc[...], axis=0)
    o_ref[...] = num / den


def _attn_sample(q, k_new, v_new, k_cache, v_cache):
    n, heads, _ = q.shape
    win = k_cache.shape[1]
    row = pl.BlockSpec((None, heads, HEAD_DIM), lambda i: (i, 0, 0))
    cache_specs, k_views, v_views = [], [], []
    for w, d in DILATED:
        assert w == BLOCK * d and win % w == 0
        last = win // w - 1
        cache_specs.append(pl.BlockSpec((None, BLOCK, None, heads, HEAD_DIM),
                                        lambda i, last=last: (i, last, 0, 0, 0)))
        k_views.append(k_cache.reshape(n, win // d, d, heads, HEAD_DIM))
        v_views.append(v_cache.reshape(n, win // d, d, heads, HEAD_DIM))
    return pl.pallas_call(
        _attn_sample_kernel,
        grid=(n,),
        in_specs=[row, row, row] + cache_specs + cache_specs,
        out_specs=row,
        out_shape=jax.ShapeDtypeStruct((n, heads, HEAD_DIM), F32),
        compiler_params=_params(1),
        name="attn_sample",
    )(q, k_new, v_new, *k_views, *v_views)


def _out_proj_kernel(pool_ref, attn_ref, x_ref, wo_ref, gpost_ref, gpre_ref, wq_ref, x1_ref, qm_ref,
                     *, tm, qm_layout):
    y = _dot(pool_ref[...].astype(BF16), wo_ref[0:POOL_WIDTH, :])
    y = y + _dot(attn_ref[...].astype(BF16), wo_ref[POOL_WIDTH:, :])
    x1 = x_ref[...] + _rmsnorm(y, gpost_ref[...])
    x1_ref[...] = x1
    qm = _dot(_rmsnorm(x1, gpre_ref[...]).astype(BF16), wq_ref[...])
    for h in range(MEM_HEADS):
        _store_heads(qm_ref, qm_layout, h, MEM_HEADS, tm, qm[:, h * HEAD_DIM:(h + 1) * HEAD_DIM])


def _out_proj(pool_out, attn, x, w_out, g_post, g_pre, w_xq, *, tm, qm_layout, name):
    m, d = x.shape
    half = pool_out.shape[1]
    qm_block = _out_block(qm_layout, tm, MEM_HEADS)
    kern = functools.partial(_out_proj_kernel, tm=tm, qm_layout=qm_layout)
    return pl.pallas_call(
        kern,
        grid=(m // tm,),
        in_specs=[
            pl.BlockSpec((tm, half), lambda i: (i, 0)),
            pl.BlockSpec((tm, half), lambda i: (i, 0)),
            pl.BlockSpec((tm, d), lambda i: (i, 0)),
            _resident(w_out.shape, lambda i: (0, 0)),
            _resident((1, d), lambda i: (0, 0)),
            _resident((1, d), lambda i: (0, 0)),
            _resident(w_xq.shape, lambda i: (0, 0)),
        ],
        out_specs=[pl.BlockSpec((tm, d), lambda i: (i, 0)),
                   pl.BlockSpec(qm_block, lambda i: (i, 0))],
        out_shape=[jax.ShapeDtypeStruct((m, d), F32),
                   jax.ShapeDtypeStruct((m // tm * qm_block[0], qm_block[1]), F32)],
        compiler_params=_params(1),
        name=name,
    )(pool_out, attn, x, w_out, g_post, g_pre, w_xq)


def _mem_attn_prompt_kernel(qm_ref, mk_ref, mv_ref, x1_ref, wxo_ref, g_ref, x2_ref):
    scale = HEAD_DIM ** -0.5
    heads = []
    for h in range(MEM_HEADS):
        cols = slice(h * HEAD_DIM, (h + 1) * HEAD_DIM)
        s = _dot_nt(qm_ref[:, cols].astype(BF16), mk_ref[:, cols].astype(BF16)) * scale
        p = jnp.exp(s - jnp.max(s, axis=-1, keepdims=True))
        o = _dot(p.astype(BF16), mv_ref[:, cols].astype(BF16))
        heads.append((o / jnp.sum(p, axis=-1, keepdims=True)).astype(BF16))
    y = _dot(jnp.concatenate(heads, axis=-1), wxo_ref[...])
    x2_ref[...] = x1_ref[...] + _rmsnorm(y, g_ref[...])


def _mem_attn_prompt(qm, mk, mv, x1, w_xo, g_post, *, tm, seq):
    m, d = x1.shape
    tiles_per_seq = seq // tm
    mem_spec = pl.BlockSpec((N_MEM, MEM_WIDTH), lambda i: (i // tiles_per_seq, 0))
    return pl.pallas_call(
        _mem_attn_prompt_kernel,
        grid=(m // tm,),
        in_specs=[
            pl.BlockSpec((tm, MEM_WIDTH), lambda i: (i, 0)),
            mem_spec,
            mem_spec,
            pl.BlockSpec((tm, d), lambda i: (i, 0)),
            _resident(w_xo.shape, lambda i: (0, 0)),
            _resident((1, d), lambda i: (0, 0)),
        ],
        out_specs=pl.BlockSpec((tm, d), lambda i: (i, 0)),
        out_shape=jax.ShapeDtypeStruct((m, d), F32),
        compiler_params=_params(1),
        name="mem_attn_prompt",
    )(qm, mk, mv, x1, w_xo, g_post)


def _mem_attn_sample_kernel(qm_ref, mk_ref, mv_ref, o_ref):
    scale = HEAD_DIM ** -0.5
    s = jnp.sum(mk_ref[...] * qm_ref[...][None], axis=-1, keepdims=True) * scale
    p = jnp.exp(s - jnp.max(s, axis=0)[None])
    o_ref[...] = jnp.sum(p * mv_ref[...], axis=0) / jnp.sum(p, axis=0)


def _mem_attn_sample(qm, mem_k, mem_v):
    n, heads, _ = qm.shape
    row = pl.BlockSpec((None, heads, HEAD_DIM), lambda i: (i, 0, 0))
    mem = pl.BlockSpec((None, N_MEM, heads, HEAD_DIM), lambda i: (i, 0, 0, 0))
    return pl.pallas_call(
        _mem_attn_sample_kernel,
        grid=(n,),
        in_specs=[row, mem, mem],
        out_specs=row,
        out_shape=jax.ShapeDtypeStruct((n, heads, HEAD_DIM), F32),
        compiler_params=_params(1),
        name="mem_attn_sample",
    )(qm, mem_k, mem_v)


def _proj_norm_residual_kernel(a_ref, w_ref, g_ref, r_ref, o_ref):
    y = _dot(a_ref[...].astype(BF16), w_ref[...])
    o_ref[...] = r_ref[...] + _rmsnorm(y, g_ref[...])


def _proj_norm_residual(a, w, g, resid, *, name):
    return pl.pallas_call(
        _proj_norm_residual_kernel,
        out_shape=jax.ShapeDtypeStruct(resid.shape, F32),
        compiler_params=pltpu.CompilerParams(vmem_limit_bytes=VMEM_LIMIT_BYTES),
        name=name,
    )(a, w, g, resid)


def _ffn_kernel(x_ref, gpre_ref, w1_ref, w2_ref, gpost_ref, o_ref, xn_ref, acc_ref):
    f = pl.program_id(1)

    @pl.when(f == 0)
    def _():
        xn_ref[...] = _rmsnorm(x_ref[...], gpre_ref[...]).astype(BF16)
        acc_ref[...] = jnp.zeros_like(acc_ref)

    hidden = jnp.square(jnp.maximum(_dot(xn_ref[...], w1_ref[...]), 0.0)).astype(BF16)
    acc_ref[...] += _dot(hidden, w2_ref[...])

    @pl.when(f == pl.num_programs(1) - 1)
    def _():
        o_ref[...] = x_ref[...] + _rmsnorm(acc_ref[...], gpost_ref[...])


def _ffn(x, g_pre, w1, w2, g_post, *, tm, tf, name):
    m, d = x.shape
    dff = w1.shape[1]
    return pl.pallas_call(
        _ffn_kernel,
        grid=(m // tm, dff // tf),
        in_specs=[
            pl.BlockSpec((tm, d), lambda i, f: (i, 0)),
            _resident((1, d), lambda i, f: (0, 0)),
            pl.BlockSpec((d, tf), lambda i, f: (0, f)),
            pl.BlockSpec((tf, d), lambda i, f: (f, 0)),
            _resident((1, d), lambda i, f: (0, 0)),
        ],
        out_specs=pl.BlockSpec((tm, d), lambda i, f: (i, 0)),
        out_shape=jax.ShapeDtypeStruct((m, d), F32),
        scratch_shapes=[pltpu.VMEM((tm, d), BF16), pltpu.VMEM((tm, d), F32)],
        compiler_params=_params(2),
        name=name,
    )(x, g_pre, w1, w2, g_post)


def kernel(x_prompt, x_sample, state_pool, cache_attn_k, cache_attn_v, cache_mem_k, cache_mem_v, mem_prompt,
           g_mix_pre, g_mix_post, g_mem_pre, g_mem_post, g_ffn_pre, g_ffn_post, g_mem_kv,
           w_in, w_pool, pool_scale, w_out, w_xq, w_mem_kv, w_xo, w_ff1, w_ff2):
    depth = w_in.shape[0]
    assert depth == 1
    batch, seq, d = x_prompt.shape
    dec_batch, dec_seq, _ = x_sample.shape
    assert dec_seq == 1

    l = 0
    bf = lambda w: w.astype(BF16)
    w_in_b, w_pool_b, w_out_b, w_xq_b = bf(w_in[l]), bf(w_pool[l]), bf(w_out[l]), bf(w_xq[l])
    w_mem_kv_b, w_xo_b, w_ff1_b, w_ff2_b = bf(w_mem_kv[l]), bf(w_xo[l]), bf(w_ff1[l]), bf(w_ff2[l])
    scale = pool_scale[l][None, :]
    gain = lambda g: g[l][None, :]

    tm = 512
    xp = x_prompt.reshape(batch * seq, d)
    xs = x_sample.reshape(dec_batch, d)
    cos_p, sin_p = _rope_tables(jnp.arange(seq, dtype=jnp.int32))
    cos_s, sin_s = _rope_tables(jnp.full((dec_batch,), PAST_LEN, jnp.int32))
    ones, zeros = jnp.ones((tm, HEAD_DIM), F32), jnp.zeros((tm, HEAD_DIM), F32)

    mem = mem_prompt.reshape(batch * N_MEM, d)
    mk, mv, mk_h, mv_h = _norm_proj(
        mem, gain(g_mem_kv), w_mem_kv_b, ones, zeros, tm=tm, ncol=MEM_WIDTH, rope_blocks=(),
        outputs=((0, FLAT), (1, FLAT), (0, HEADS), (1, HEADS)), name="mem_kv")

    u, q, k, v, k_h, v_h = _norm_proj(
        xp, gain(g_mix_pre), w_in_b, cos_p, sin_p, tm=256, ncol=POOL_WIDTH, rope_blocks=(1, 2),
        outputs=((0, FLAT), (1, FLAT), (2, FLAT), (3, FLAT), (2, HEADS), (3, HEADS)), name="in_proj_prompt")
    pool_out = _pool_prompt(u, w_pool_b, scale, tm=tm, seq=seq)
    attn = _attn_prompt(q, k, v, seq=seq)
    x1, qm = _out_proj(pool_out, attn, xp, w_out_b, gain(g_mix_post), gain(g_mem_pre), w_xq_b,
                       tm=tm, qm_layout=FLAT, name="out_proj_prompt")
    x2 = _mem_attn_prompt(qm, mk, mv, x1, w_xo_b, gain(g_mem_post), tm=tm, seq=seq)
    yp = _ffn(x2, gain(g_ffn_pre), w_ff1_b, w_ff2_b, gain(g_ffn_post), tm=tm, tf=512, name="ffn_prompt")

    us, qs_h, ks_h, vs_h = _norm_proj(
        xs, gain(g_mix_pre), w_in_b, cos_s, sin_s, tm=dec_batch, ncol=POOL_WIDTH, rope_blocks=(1, 2),
        outputs=((0, FLAT), (1, HEADS), (2, HEADS), (3, HEADS)), name="in_proj_sample")
    heads3 = lambda a: a.reshape(dec_batch, -1, HEAD_DIM)
    pool_out_s, new_pool_s = _pool_sample(us, jnp.swapaxes(state_pool[l], 0, 1), w_pool_b, scale, pos=PAST_LEN)
    attn_s = _attn_sample(heads3(qs_h), heads3(ks_h), heads3(vs_h), cache_attn_k[l], cache_attn_v[l])
    x1s, qms_h = _out_proj(pool_out_s, attn_s.reshape(dec_batch, ATTN_WIDTH), xs, w_out_b, gain(g_mix_post),
                           gain(g_mem_pre), w_xq_b, tm=dec_batch, qm_layout=HEADS, name="out_proj_sample")
    mem_o_s = _mem_attn_sample(heads3(qms_h), cache_mem_k[l], cache_mem_v[l])
    x2s = _proj_norm_residual(mem_o_s.reshape(dec_batch, MEM_WIDTH), w_xo_b, gain(g_mem_post), x1s,
                              name="mem_out_sample")
    ys = _ffn(x2s, gain(g_ffn_pre), w_ff1_b, w_ff2_b, gain(g_ffn_post), tm=dec_batch, tf=2048,
              name="ffn_sample")

    keep = min(max(w for w, _ in DILATED), seq)
    return (
        yp.reshape(batch, seq, d),
        ys.reshape(dec_batch, 1, d),
        u.reshape(batch, seq, POOL_WIDTH)[:, seq - POOL_STATE:][None],
        jnp.swapaxes(new_pool_s, 0, 1)[None],
        k_h.reshape(batch, seq, N_HEADS, HEAD_DIM)[:, seq - keep:][None],
        v_h.reshape(batch, seq, N_HEADS, HEAD_DIM)[:, seq - keep:][None],
        ks_h.reshape(dec_batch, 1, N_HEADS, HEAD_DIM)[None],
        vs_h.reshape(dec_batch, 1, N_HEADS, HEAD_DIM)[None],
        mk_h.reshape(batch, N_MEM, MEM_HEADS, HEAD_DIM)[None],
        mv_h.reshape(batch, N_MEM, MEM_HEADS, HEAD_DIM)[None],
    )
```

```python
import functools

import jax
import jax.numpy as jnp
from jax import lax
from jax.experimental import pallas as pl
from jax.experimental.pallas import tpu as pltpu

D_MODEL = 2048
POOL_WIDTH = 1024
POOL_WINDOWS = (2, 4, 8, 16)
POOL_GROUP = POOL_WIDTH // len(POOL_WINDOWS)
POOL_STATE = max(POOL_WINDOWS) - 1
HEAD_DIM = 128
ATTN_WIDTH = 1024
N_HEADS = ATTN_WIDTH // HEAD_DIM
DILATED = ((128, 1), (512, 4), (2048, 16))
ROPE_DIM = HEAD_DIM // 4
ROPE_HALF = ROPE_DIM // 2
ROPE_THETA = 500000.0
N_MEM = 256
MEM_HEADS = 4
MEM_WIDTH = MEM_HEADS * HEAD_DIM
EPS = 1e-6
BLOCK = 128
NEG_INF = -1e30
LOG2_E = 1.4426950408889634
PAST_LEN = 8192
HALO = 16

F32 = jnp.float32
BF16 = jnp.bfloat16

VMEM_LIMIT_BYTES = 56 * 1024 * 1024

FLAT, HEADS = "flat", "heads"


def _params(n_grid_axes):
    return pltpu.CompilerParams(
        dimension_semantics=("arbitrary",) * n_grid_axes,
        vmem_limit_bytes=VMEM_LIMIT_BYTES,
    )


def _resident(block_shape, index_map):
    return pl.BlockSpec(block_shape, index_map, pipeline_mode=pl.Buffered(1))


def _rmsnorm(x, g):
    ms = jnp.mean(x * x, axis=-1, keepdims=True)
    return x * lax.rsqrt(ms + EPS) * g


def _dot(a, b):
    return jnp.dot(a, b, preferred_element_type=F32)


def _dot_nt(a, b):
    return lax.dot_general(a, b, (((1,), (1,)), ((), ())), preferred_element_type=F32)


def _store_heads(o_ref, layout, h, heads, rows, value):
    if layout == FLAT:
        o_ref[:, h * HEAD_DIM:(h + 1) * HEAD_DIM] = value
    else:
        o_ref[pl.ds(h, rows, stride=heads), :] = value


def _out_block(layout, tm, heads):
    return (tm, heads * HEAD_DIM) if layout == FLAT else (tm * heads, HEAD_DIM)


def _norm_proj_kernel(x_ref, g_ref, w_ref, cos_ref, sin_ref, *out_refs, tm, ncol, rope_blocks, outputs):
    heads = ncol // HEAD_DIM
    xn = _rmsnorm(x_ref[...], g_ref[...]).astype(BF16)
    for c in sorted({c for c, _ in outputs}):
        acc = _dot(xn, w_ref[:, c * ncol:(c + 1) * ncol])
        if c in rope_blocks:
            cos = cos_ref[...]
            sin = sin_ref[...]
            lane = lax.broadcasted_iota(jnp.int32, cos.shape, 1)
        for h in range(heads):
            a = acc[:, h * HEAD_DIM:(h + 1) * HEAD_DIM]
            if c in rope_blocks:
                partner = jnp.where(lane < ROPE_HALF,
                                    pltpu.roll(a, HEAD_DIM - ROPE_HALF, 1),
                                    pltpu.roll(a, ROPE_HALF, 1))
                a = a * cos + partner * sin
            for (oc, layout), o_ref in zip(outputs, out_refs):
                if oc == c:
                    _store_heads(o_ref, layout, h, heads, tm, a)


def _norm_proj(x, g, w, cos, sin, *, tm, ncol, rope_blocks, outputs, name):
    m, d = x.shape
    heads = ncol // HEAD_DIM
    assert w.shape[0] == d and w.shape[1] % ncol == 0 and m % tm == 0
    table_tiles = cos.shape[0] // tm
    kern = functools.partial(_norm_proj_kernel, tm=tm, ncol=ncol, rope_blocks=rope_blocks, outputs=outputs)
    blocks = [_out_block(layout, tm, heads) for _, layout in outputs]
    return pl.pallas_call(
        kern,
        grid=(m // tm,),
        in_specs=[
            pl.BlockSpec((tm, d), lambda i: (i, 0)),
            _resident((1, d), lambda i: (0, 0)),
            _resident(w.shape, lambda i: (0, 0)),
            pl.BlockSpec((tm, HEAD_DIM), lambda i: (i % table_tiles, 0)),
            pl.BlockSpec((tm, HEAD_DIM), lambda i: (i % table_tiles, 0)),
        ],
        out_specs=[pl.BlockSpec(b, lambda i: (i, 0)) for b in blocks],
        out_shape=[jax.ShapeDtypeStruct((m // tm * b[0], b[1]), F32) for b in blocks],
        compiler_params=_params(1),
        name=name,
    )(x, g, w, cos, sin)


def _rope_tables(pos):
    inv = jnp.power(jnp.float32(ROPE_THETA), -jnp.arange(ROPE_HALF, dtype=F32) * 2.0 / ROPE_DIM)
    ang = pos.astype(F32)[:, None] * inv[None, :]
    cos, sin = jnp.cos(ang), jnp.sin(ang)
    n = pos.shape[0]
    pad = HEAD_DIM - ROPE_DIM
    cos_t = jnp.concatenate([cos, cos, jnp.ones((n, pad), F32)], axis=1)
    sin_t = jnp.concatenate([-sin, sin, jnp.zeros((n, pad), F32)], axis=1)
    return cos_t, sin_t


def _pool_prompt_kernel(u_ref, halo_ref, wp_ref, scale_ref, o_ref, ext_ref, *, tm, tiles_per_seq):
    t_in_seq = pl.program_id(0) % tiles_per_seq
    keep = (t_in_seq > 0).astype(F32)
    ext_ref[0:HALO, :] = halo_ref[...] * keep
    ext_ref[HALO:HALO + tm, :] = u_ref[...]
    pos = t_in_seq * tm + lax.broadcasted_iota(jnp.int32, (tm, 1), 0)
    for g, w in enumerate(POOL_WINDOWS):
        cols = slice(g * POOL_GROUP, (g + 1) * POOL_GROUP)
        wsum = ext_ref[HALO:HALO + tm, cols]
        for back in range(1, w):
            wsum = wsum + ext_ref[HALO - back:HALO - back + tm, cols]
        cnt = jnp.minimum(w, pos + 1).astype(F32)
        pooled = (wsum / cnt - u_ref[:, cols]).astype(BF16)
        o_ref[:, cols] = (_dot(pooled, wp_ref[g]) * scale_ref[:, cols]).astype(o_ref.dtype)


def _pool_prompt(u, w_pool, pool_scale, *, tm, seq):
    m, c = u.shape
    tiles_per_seq = seq // tm
    halo_blocks = tm // HALO
    kern = functools.partial(_pool_prompt_kernel, tm=tm, tiles_per_seq=tiles_per_seq)
    return pl.pallas_call(
        kern,
        grid=(m // tm,),
        in_specs=[
            pl.BlockSpec((tm, c), lambda i: (i, 0)),
            pl.BlockSpec((HALO, c), lambda i: (jnp.maximum(i * halo_blocks - 1, 0), 0)),
            _resident(w_pool.shape, lambda i: (0, 0, 0)),
            _resident((1, c), lambda i: (0, 0)),
        ],
        out_specs=pl.BlockSpec((tm, c), lambda i: (i, 0)),
        out_shape=jax.ShapeDtypeStruct((m, c), BF16),
        scratch_shapes=[pltpu.VMEM((HALO + tm, c), F32)],
        compiler_params=_params(1),
        name="pool_prompt",
    )(u, u, w_pool, pool_scale)


def _pool_sample_kernel(u_ref, prev_ref, wp_ref, scale_ref, o_ref, state_ref, *, pos):
    for j in range(POOL_STATE - 1):
        state_ref[j] = prev_ref[j + 1]
    state_ref[POOL_STATE - 1] = u_ref[...]
    for g, w in enumerate(POOL_WINDOWS):
        cols = slice(g * POOL_GROUP, (g + 1) * POOL_GROUP)
        wsum = u_ref[:, cols]
        for back in range(1, w):
            wsum = wsum + prev_ref[POOL_STATE - back, :, cols]
        cnt = float(min(w, pos + 1))
        pooled = (wsum / cnt - u_ref[:, cols]).astype(BF16)
        o_ref[:, cols] = (_dot(pooled, wp_ref[g]) * scale_ref[:, cols]).astype(o_ref.dtype)


def _pool_sample(u, prev_t, w_pool, pool_scale, *, pos):
    n, c = u.shape
    kern = functools.partial(_pool_sample_kernel, pos=pos)
    return pl.pallas_call(
        kern,
        out_shape=[jax.ShapeDtypeStruct((n, c), BF16), jax.ShapeDtypeStruct(prev_t.shape, F32)],
        compiler_params=pltpu.CompilerParams(vmem_limit_bytes=VMEM_LIMIT_BYTES),
        name="pool_sample",
    )(u, prev_t, w_pool, pool_scale)


def _run_blocks(count, unroll, body):
    trips = count // unroll
    if trips > 1:
        def trip(t, carry):
            for j in range(unroll):
                body(t * unroll + j)
            return carry
        lax.fori_loop(0, trips, trip, 0)
        done = trips * unroll
    else:
        done = 0
    for idx in range(done, count):
        body(idx)


def _attn_prompt_kernel(q_ref, k_ref, v_ref, o_ref, *scratch, seq, unroll):
    n_br = len(DILATED)
    ob_refs, mb_refs, sb_refs = scratch[:n_br], scratch[n_br:2 * n_br], scratch[2 * n_br:3 * n_br]
    p_ref, bias_ref = scratch[3 * n_br:]
    exp2_scale = HEAD_DIM ** -0.5 * LOG2_E
    qi = lax.broadcasted_iota(jnp.int32, (BLOCK, 2 * BLOCK), 0)
    kj = lax.broadcasted_iota(jnp.int32, (BLOCK, 2 * BLOCK), 1)
    in_band = ((kj < BLOCK) & (kj >= qi)) | ((kj >= BLOCK) & (kj - BLOCK <= qi))
    bias_ref[...] = jnp.where(in_band, 0.0, NEG_INF)
    ones = jnp.ones((2 * BLOCK, HEAD_DIM), BF16)

    def rows(block, r, d):
        if d == 1:
            start = block * BLOCK
            return pl.ds(start if isinstance(start, int) else pl.multiple_of(start, BLOCK), BLOCK)
        return pl.ds(block * (BLOCK * d) + r, BLOCK, stride=d)

    def keys(ref, block, r, d, with_prev):
        cur = ref[rows(block, r, d), :]
        if not with_prev:
            return cur.astype(BF16)
        return jnp.concatenate([ref[rows(block - 1, r, d), :], cur], axis=0).astype(BF16)

    for g, (_, d) in enumerate(DILATED):
        n_blocks = seq // d // BLOCK
        for with_prev in (False, True):
            per_r = n_blocks - 1 if with_prev else 1
            count = d * per_r
            if count == 0:
                continue
            cols = slice(0, 2 * BLOCK) if with_prev else slice(BLOCK, 2 * BLOCK)
            width = 2 * BLOCK if with_prev else BLOCK

            def locate(idx, per_r=per_r, with_prev=with_prev):
                return (idx % per_r + 1, idx // per_r) if with_prev else (0, idx)

            def probabilities(idx, g=g, d=d, with_prev=with_prev, cols=cols, width=width, locate=locate):
                b, r = locate(idx)
                q = q_ref[rows(b, r, d), :].astype(BF16)
                s = _dot_nt(q, keys(k_ref, b, r, d, with_prev)) + bias_ref[:, cols]
                m = jnp.max(s, axis=-1, keepdims=True)
                p_ref[idx, :, 0:width] = jnp.exp2((s - m) * exp2_scale).astype(BF16)
                mb_refs[g][rows(b, r, d), :] = jnp.broadcast_to(m, (BLOCK, HEAD_DIM))

            def values(idx, g=g, d=d, with_prev=with_prev, width=width, locate=locate):
                b, r = locate(idx)
                v1 = jnp.concatenate([keys(v_ref, b, r, d, with_prev), ones[0:width]], axis=1)
                acc = _dot(p_ref[idx, :, 0:width], v1)
                ob_refs[g][rows(b, r, d), :] = acc[:, 0:HEAD_DIM]
                sb_refs[g][rows(b, r, d), :] = acc[:, HEAD_DIM:]

            _run_blocks(count, unroll, probabilities)
            _run_blocks(count, unroll, values)

    chunk = 256

    def combine(c, carry):
        rw = pl.ds(pl.multiple_of(c * chunk, chunk), chunk)
        ms = [mb[rw, :] for mb in mb_refs]
        m = functools.reduce(jnp.maximum, ms)
        ws = [jnp.exp2((mg - m) * exp2_scale) for mg in ms]
        num = functools.reduce(jnp.add, [w * ob[rw, :] for w, ob in zip(ws, ob_refs)])
        den = functools.reduce(jnp.add, [w * sb[rw, :] for w, sb in zip(ws, sb_refs)])
        o_ref[rw, :] = (num / den).astype(o_ref.dtype)
        return carry

    lax.fori_loop(0, seq // chunk, combine, 0)


def _attn_prompt(q, k, v, *, seq):
    m, width = q.shape
    n_seq = m // seq
    n_heads = width // HEAD_DIM
    spec = pl.BlockSpec((seq, HEAD_DIM), lambda n, h: (n, h))
    kern = functools.partial(_attn_prompt_kernel, seq=seq, unroll=8)
    return pl.pallas_call(
        kern,
        grid=(n_seq, n_heads),
        in_specs=[spec, spec, spec],
        out_specs=spec,
        out_shape=jax.ShapeDtypeStruct((m, width), BF16),
        scratch_shapes=(
            [pltpu.VMEM((seq, HEAD_DIM), F32) for _ in range(3 * len(DILATED))]
            + [pltpu.VMEM((seq // BLOCK, BLOCK, 2 * BLOCK), BF16),
               pltpu.VMEM((BLOCK, 2 * BLOCK), F32)]),
        compiler_params=_params(2),
        name="attn_prompt",
    )(q, k, v)


def _attn_sample_kernel(q_ref, kn_ref, vn_ref, *refs):
    n_br = len(DILATED)
    kc_refs, vc_refs, o_ref = refs[:n_br], refs[n_br:2 * n_br], refs[2 * n_br]
    scale = HEAD_DIM ** -0.5
    q = q_ref[...]
    s_new = jnp.sum(q * kn_ref[...], axis=-1, keepdims=True) * scale
    s_br = [jnp.sum(kc[...] * q[None], axis=-1, keepdims=True) * scale for kc in kc_refs]
    m = s_new
    for s in s_br:
        m = jnp.maximum(m, jnp.max(s, axis=0))
    p_new = jnp.exp(s_new - m) * float(n_br)
    den = p_new
    num = p_new * vn_ref[...]
    for s, vc in zip(s_br, vc_refs):
        p = jnp.exp(s - m[None])
        den = den + jnp.sum(p, axis=0)
        num = num + jnp.sum(p * vc[...], axis=0)
    o_ref[...] = num / den


def _attn_sample(q, k_new, v_new, k_cache, v_cache):
    n, heads, _ = q.shape
    win = k_cache.shape[1]
    row = pl.BlockSpec((None, heads, HEAD_DIM), lambda i: (i, 0, 0))
    cache_specs, k_views, v_views = [], [], []
    for w, d in DILATED:
        assert w == BLOCK * d and win % w == 0
        last = win // w - 1
        cache_specs.append(pl.BlockSpec((None, BLOCK, None, heads, HEAD_DIM),
                                        lambda i, last=last: (i, last, 0, 0, 0)))
        k_views.append(k_cache.reshape(n, win // d, d, heads, HEAD_DIM))
        v_views.append(v_cache.reshape(n, win // d, d, heads, HEAD_DIM))
    return pl.pallas_call(
        _attn_sample_kernel,
        grid=(n,),
        in_specs=[row, row, row] + cache_specs + cache_specs,
        out_specs=row,
        out_shape=jax.ShapeDtypeStruct((n, heads, HEAD_DIM), F32),
        compiler_params=_params(1),
        name="attn_sample",
    )(q, k_new, v_new, *k_views, *v_views)


def _out_proj_kernel(pool_ref, attn_ref, x_ref, wo_ref, gpost_ref, gpre_ref, wq_ref, x1_ref, qm_ref,
                     *, tm, qm_layout):
    y = _dot(pool_ref[...].astype(BF16), wo_ref[0:POOL_WIDTH, :])
    y = y + _dot(attn_ref[...].astype(BF16), wo_ref[POOL_WIDTH:, :])
    x1 = x_ref[...] + _rmsnorm(y, gpost_ref[...])
    x1_ref[...] = x1
    qm = _dot(_rmsnorm(x1, gpre_ref[...]).astype(BF16), wq_ref[...])
    for h in range(MEM_HEADS):
        _store_heads(qm_ref, qm_layout, h, MEM_HEADS, tm, qm[:, h * HEAD_DIM:(h + 1) * HEAD_DIM])


def _out_proj(pool_out, attn, x, w_out, g_post, g_pre, w_xq, *, tm, qm_layout, name):
    m, d = x.shape
    half = pool_out.shape[1]
    qm_block = _out_block(qm_layout, tm, MEM_HEADS)
    kern = functools.partial(_out_proj_kernel, tm=tm, qm_layout=qm_layout)
    return pl.pallas_call(
        kern,
        grid=(m // tm,),
        in_specs=[
            pl.BlockSpec((tm, half), lambda i: (i, 0)),
            pl.BlockSpec((tm, half), lambda i: (i, 0)),
            pl.BlockSpec((tm, d), lambda i: (i, 0)),
            _resident(w_out.shape, lambda i: (0, 0)),
            _resident((1, d), lambda i: (0, 0)),
            _resident((1, d), lambda i: (0, 0)),
            _resident(w_xq.shape, lambda i: (0, 0)),
        ],
        out_specs=[pl.BlockSpec((tm, d), lambda i: (i, 0)),
                   pl.BlockSpec(qm_block, lambda i: (i, 0))],
        out_shape=[jax.ShapeDtypeStruct((m, d), F32),
                   jax.ShapeDtypeStruct((m // tm * qm_block[0], qm_block[1]), F32)],
        compiler_params=_params(1),
        name=name,
    )(pool_out, attn, x, w_out, g_post, g_pre, w_xq)


def _mem_attn_prompt_kernel(qm_ref, mk_ref, mv_ref, x1_ref, wxo_ref, g_ref, x2_ref):
    scale = HEAD_DIM ** -0.5
    heads = []
    for h in range(MEM_HEADS):
        cols = slice(h * HEAD_DIM, (h + 1) * HEAD_DIM)
        s = _dot_nt(qm_ref[:, cols].astype(BF16), mk_ref[:, cols].astype(BF16)) * scale
        p = jnp.exp(s - jnp.max(s, axis=-1, keepdims=True))
        o = _dot(p.astype(BF16), mv_ref[:, cols].astype(BF16))
        heads.append((o / jnp.sum(p, axis=-1, keepdims=True)).astype(BF16))
    y = _dot(jnp.concatenate(heads, axis=-1), wxo_ref[...])
    x2_ref[...] = x1_ref[...] + _rmsnorm(y, g_ref[...])


def _mem_attn_prompt(qm, mk, mv, x1, w_xo, g_post, *, tm, seq):
    m, d = x1.shape
    tiles_per_seq = seq // tm
    mem_spec = pl.BlockSpec((N_MEM, MEM_WIDTH), lambda i: (i // tiles_per_seq, 0))
    return pl.pallas_call(
        _mem_attn_prompt_kernel,
        grid=(m // tm,),
        in_specs=[
            pl.BlockSpec((tm, MEM_WIDTH), lambda i: (i, 0)),
            mem_spec,
            mem_spec,
            pl.BlockSpec((tm, d), lambda i: (i, 0)),
            _resident(w_xo.shape, lambda i: (0, 0)),
            _resident((1, d), lambda i: (0, 0)),
        ],
        out_specs=pl.BlockSpec((tm, d), lambda i: (i, 0)),
        out_shape=jax.ShapeDtypeStruct((m, d), F32),
        compiler_params=_params(1),
        name="mem_attn_prompt",
    )(qm, mk, mv, x1, w_xo, g_post)


def _mem_attn_sample_kernel(qm_ref, mk_ref, mv_ref, o_ref):
    scale = HEAD_DIM ** -0.5
    s = jnp.sum(mk_ref[...] * qm_ref[...][None], axis=-1, keepdims=True) * scale
    p = jnp.exp(s - jnp.max(s, axis=0)[None])
    o_ref[...] = jnp.sum(p * mv_ref[...], axis=0) / jnp.sum(p, axis=0)


def _mem_attn_sample(qm, mem_k, mem_v):
    n, heads, _ = qm.shape
    row = pl.BlockSpec((None, heads, HEAD_DIM), lambda i: (i, 0, 0))
    mem = pl.BlockSpec((None, N_MEM, heads, HEAD_DIM), lambda i: (i, 0, 0, 0))
    return pl.pallas_call(
        _mem_attn_sample_kernel,
        grid=(n,),
        in_specs=[row, mem, mem],
        out_specs=row,
        out_shape=jax.ShapeDtypeStruct((n, heads, HEAD_DIM), F32),
        compiler_params=_params(1),
        name="mem_attn_sample",
    )(qm, mem_k, mem_v)


def _proj_norm_residual_kernel(a_ref, w_ref, g_ref, r_ref, o_ref):
    y = _dot(a_ref[...].astype(BF16), w_ref[...])
    o_ref[...] = r_ref[...] + _rmsnorm(y, g_ref[...])


def _proj_norm_residual(a, w, g, resid, *, name):
    return pl.pallas_call(
        _proj_norm_residual_kernel,
        out_shape=jax.ShapeDtypeStruct(resid.shape, F32),
        compiler_params=pltpu.CompilerParams(vmem_limit_bytes=VMEM_LIMIT_BYTES),
        name=name,
    )(a, w, g, resid)


def _ffn_kernel(x_ref, gpre_ref, w1_ref, w2_ref, gpost_ref, o_ref, xn_ref, acc_ref):
    f = pl.program_id(1)

    @pl.when(f == 0)
    def _():
        xn_ref[...] = _rmsnorm(x_ref[...], gpre_ref[...]).astype(BF16)
        acc_ref[...] = jnp.zeros_like(acc_ref)

    hidden = jnp.square(jnp.maximum(_dot(xn_ref[...], w1_ref[...]), 0.0)).astype(BF16)
    acc_ref[...] += _dot(hidden, w2_ref[...])

    @pl.when(f == pl.num_programs(1) - 1)
    def _():
        o_ref[...] = x_ref[...] + _rmsnorm(acc_ref[...], gpost_ref[...])


def _ffn(x, g_pre, w1, w2, g_post, *, tm, tf, name):
    m, d = x.shape
    dff = w1.shape[1]
    return pl.pallas_call(
        _ffn_kernel,
        grid=(m // tm, dff // tf),
        in_specs=[
            pl.BlockSpec((tm, d), lambda i, f: (i, 0)),
            _resident((1, d), lambda i, f: (0, 0)),
            pl.BlockSpec((d, tf), lambda i, f: (0, f)),
            pl.BlockSpec((tf, d), lambda i, f: (f, 0)),
            _resident((1, d), lambda i, f: (0, 0)),
        ],
        out_specs=pl.BlockSpec((tm, d), lambda i, f: (i, 0)),
        out_shape=jax.ShapeDtypeStruct((m, d), F32),
        scratch_shapes=[pltpu.VMEM((tm, d), BF16), pltpu.VMEM((tm, d), F32)],
        compiler_params=_params(2),
        name=name,
    )(x, g_pre, w1, w2, g_post)


def kernel(x_prompt, x_sample, state_pool, cache_attn_k, cache_attn_v, cache_mem_k, cache_mem_v, mem_prompt,
           g_mix_pre, g_mix_post, g_mem_pre, g_mem_post, g_ffn_pre, g_ffn_post, g_mem_kv,
           w_in, w_pool, pool_scale, w_out, w_xq, w_mem_kv, w_xo, w_ff1, w_ff2):
    depth = w_in.shape[0]
    assert depth == 1
    batch, seq, d = x_prompt.shape
    dec_batch, dec_seq, _ = x_sample.shape
    assert dec_seq == 1

    l = 0
    bf = lambda w: w.astype(BF16)
    w_in_b, w_pool_b, w_out_b, w_xq_b = bf(w_in[l]), bf(w_pool[l]), bf(w_out[l]), bf(w_xq[l])
    w_mem_kv_b, w_xo_b, w_ff1_b, w_ff2_b = bf(w_mem_kv[l]), bf(w_xo[l]), bf(w_ff1[l]), bf(w_ff2[l])
    scale = pool_scale[l][None, :]
    gain = lambda g: g[l][None, :]

    tm = 512
    xp = x_prompt.reshape(batch * seq, d)
    xs = x_sample.reshape(dec_batch, d)
    cos_p, sin_p = _rope_tables(jnp.arange(seq, dtype=jnp.int32))
    cos_s, sin_s = _rope_tables(jnp.full((dec_batch,), PAST_LEN, jnp.int32))
    ones, zeros = jnp.ones((tm, HEAD_DIM), F32), jnp.zeros((tm, HEAD_DIM), F32)

    mem = mem_prompt.reshape(batch * N_MEM, d)
    mk, mv, mk_h, mv_h = _norm_proj(
        mem, gain(g_mem_kv), w_mem_kv_b, ones, zeros, tm=tm, ncol=MEM_WIDTH, rope_blocks=(),
        outputs=((0, FLAT), (1, FLAT), (0, HEADS), (1, HEADS)), name="mem_kv")

    u, q, k, v, k_h, v_h = _norm_proj(
        xp, gain(g_mix_pre), w_in_b, cos_p, sin_p, tm=256, ncol=POOL_WIDTH, rope_blocks=(1, 2),
        outputs=((0, FLAT), (1, FLAT), (2, FLAT), (3, FLAT), (2, HEADS), (3, HEADS)), name="in_proj_prompt")
    pool_out = _pool_prompt(u, w_pool_b, scale, tm=tm, seq=seq)
    attn = _attn_prompt(q, k, v, seq=seq)
    x1, qm = _out_proj(pool_out, attn, xp, w_out_b, gain(g_mix_post), gain(g_mem_pre), w_xq_b,
                       tm=tm, qm_layout=FLAT, name="out_proj_prompt")
    x2 = _mem_attn_prompt(qm, mk, mv, x1, w_xo_b, gain(g_mem_post), tm=tm, seq=seq)
    yp = _ffn(x2, gain(g_ffn_pre), w_ff1_b, w_ff2_b, gain(g_ffn_post), tm=tm, tf=512, name="ffn_prompt")

    us, qs_h, ks_h, vs_h = _norm_proj(
        xs, gain(g_mix_pre), w_in_b, cos_s, sin_s, tm=dec_batch, ncol=POOL_WIDTH, rope_blocks=(1, 2),
        outputs=((0, FLAT), (1, HEADS), (2, HEADS), (3, HEADS)), name="in_proj_sample")
    heads3 = lambda a: a.reshape(dec_batch, -1, HEAD_DIM)
    pool_out_s, new_pool_s = _pool_sample(us, jnp.swapaxes(state_pool[l], 0, 1), w_pool_b, scale, pos=PAST_LEN)
    attn_s = _attn_sample(heads3(qs_h), heads3(ks_h), heads3(vs_h), cache_attn_k[l], cache_attn_v[l])
    x1s, qms_h = _out_proj(pool_out_s, attn_s.reshape(dec_batch, ATTN_WIDTH), xs, w_out_b, gain(g_mix_post),
                           gain(g_mem_pre), w_xq_b, tm=dec_batch, qm_layout=HEADS, name="out_proj_sample")
    mem_o_s = _mem_attn_sample(heads3(qms_h), cache_mem_k[l], cache_mem_v[l])
    x2s = _proj_norm_residual(mem_o_s.reshape(dec_batch, MEM_WIDTH), w_xo_b, gain(g_mem_post), x1s,
                              name="mem_out_sample")
    ys = _ffn(x2s, gain(g_ffn_pre), w_ff1_b, w_ff2_b, gain(g_ffn_post), tm=dec_batch, tf=2048,
              name="ffn_sample")

    keep = min(max(w for w, _ in DILATED), seq)
    return (
        yp.reshape(batch, seq, d),
        ys.reshape(dec_batch, 1, d),
        u.reshape(batch, seq, POOL_WIDTH)[:, seq - POOL_STATE:][None],
        jnp.swapaxes(new_pool_s, 0, 1)[None],
        k_h.reshape(batch, seq, N_HEADS, HEAD_DIM)[:, seq - keep:][None],
        v_h.reshape(batch, seq, N_HEADS, HEAD_DIM)[:, seq - keep:][None],
        ks_h.reshape(dec_batch, 1, N_HEADS, HEAD_DIM)[None],
        vs_h.reshape(dec_batch, 1, N_HEADS, HEAD_DIM)[None],
        mk_h.reshape(batch, N_MEM, MEM_HEADS, HEAD_DIM)[None],
        mv_h.reshape(batch, N_MEM, MEM_HEADS, HEAD_DIM)[None],
    )
```

```python
import functools

import jax
import jax.numpy as jnp
from jax import lax
from jax.experimental import pallas as pl
from jax.experimental.pallas import tpu as pltpu

D_MODEL = 2048
POOL_WIDTH = 1024
POOL_WINDOWS = (2, 4, 8, 16)
POOL_GROUP = POOL_WIDTH // len(POOL_WINDOWS)
POOL_STATE = max(POOL_WINDOWS) - 1
HEAD_DIM = 128
ATTN_WIDTH = 1024
N_HEADS = ATTN_WIDTH // HEAD_DIM
DILATED = ((128, 1), (512, 4), (2048, 16))
ROPE_DIM = HEAD_DIM // 4
ROPE_HALF = ROPE_DIM // 2
ROPE_THETA = 500000.0
N_MEM = 256
MEM_HEADS = 4
MEM_WIDTH = MEM_HEADS * HEAD_DIM
EPS = 1e-6
BLOCK = 128
NEG_INF = -1e30
LOG2_E = 1.4426950408889634
PAST_LEN = 8192
HALO = 32
STATE_ROWS = 16

F32 = jnp.float32
BF16 = jnp.bfloat16

VMEM_LIMIT_BYTES = 56 * 1024 * 1024

FLAT, HEADS = "flat", "heads"


def _params(n_grid_axes):
    return pltpu.CompilerParams(
        dimension_semantics=("arbitrary",) * n_grid_axes,
        vmem_limit_bytes=VMEM_LIMIT_BYTES,
    )


def _resident(block_shape, index_map):
    return pl.BlockSpec(block_shape, index_map, pipeline_mode=pl.Buffered(1))


def _rmsnorm(x, g):
    ms = jnp.mean(x * x, axis=-1, keepdims=True)
    return x * lax.rsqrt(ms + EPS) * g


def _dot(a, b):
    return jnp.dot(a, b, preferred_element_type=F32)


def _dot_nt(a, b):
    return lax.dot_general(a, b, (((1,), (1,)), ((), ())), preferred_element_type=F32)


def _call_with_casts(kernel_fn, *, grid, in_specs, out_specs, out_shape, scratch_shapes=(), args, casts=(),
                     step_of=None, name):
    n_in, n_out, n_casts = len(in_specs), len(out_specs), len(casts)
    n_steps = 1
    for g in grid:
        n_steps *= g
    if step_of is None:
        step_of = lambda i: i
    cast_specs = []
    for a in casts:
        rows, cols = a.shape
        assert rows % n_steps == 0
        cast_specs.append(pl.BlockSpec((rows // n_steps, cols), lambda *idx: (step_of(*idx), 0)))

    def body(*refs):
        ins, cast_in = refs[:n_in], refs[n_in:n_in + n_casts]
        outs = refs[n_in + n_casts:n_in + n_casts + n_out]
        cast_out = refs[n_in + n_casts + n_out:n_in + 2 * n_casts + n_out]
        for i_ref, o_ref in zip(cast_in, cast_out):
            o_ref[...] = i_ref[...].astype(BF16)
        kernel_fn(*ins, *outs, *refs[n_in + 2 * n_casts + n_out:])

    res = pl.pallas_call(
        body,
        grid=grid,
        in_specs=list(in_specs) + cast_specs,
        out_specs=list(out_specs) + cast_specs,
        out_shape=list(out_shape) + [jax.ShapeDtypeStruct(a.shape, BF16) for a in casts],
        scratch_shapes=list(scratch_shapes),
        compiler_params=_params(len(grid)),
        name=name,
    )(*args, *casts)
    return res[:n_out], res[n_out:]


def _store_heads(o_ref, layout, h, heads, rows, value):
    if layout == FLAT:
        o_ref[:, h * HEAD_DIM:(h + 1) * HEAD_DIM] = value
    else:
        o_ref[pl.ds(h, rows, stride=heads), :] = value


def _out_block(layout, tm, heads):
    return (tm, heads * HEAD_DIM) if layout == FLAT else (tm * heads, HEAD_DIM)


def _norm_proj_kernel(x_ref, g_ref, w_ref, cos_ref, sin_ref, *out_refs, tm, ncol, rope_blocks, outputs):
    heads = ncol // HEAD_DIM
    xn = _rmsnorm(x_ref[...], g_ref[...]).astype(BF16)
    for c in sorted({c for c, _ in outputs}):
        acc = _dot(xn, w_ref[:, c * ncol:(c + 1) * ncol].astype(BF16))
        if c in rope_blocks:
            cos = cos_ref[...]
            sin = sin_ref[...]
            lane = lax.broadcasted_iota(jnp.int32, cos.shape, 1)
        for h in range(heads):
            a = acc[:, h * HEAD_DIM:(h + 1) * HEAD_DIM]
            if c in rope_blocks:
                partner = jnp.where(lane < ROPE_HALF,
                                    pltpu.roll(a, HEAD_DIM - ROPE_HALF, 1),
                                    pltpu.roll(a, ROPE_HALF, 1))
                a = a * cos + partner * sin
            for (oc, layout), o_ref in zip(outputs, out_refs):
                if oc == c:
                    _store_heads(o_ref, layout, h, heads, tm, a)


def _norm_proj(x, g, w, cos, sin, *, tm, ncol, rope_blocks, outputs, name, casts=()):
    m, d = x.shape
    heads = ncol // HEAD_DIM
    assert w.shape[0] == d and w.shape[1] % ncol == 0 and m % tm == 0
    table_tiles = cos.shape[0] // tm
    kern = functools.partial(_norm_proj_kernel, tm=tm, ncol=ncol, rope_blocks=rope_blocks, outputs=outputs)
    blocks = [_out_block(layout, tm, heads) for _, layout in outputs]
    return _call_with_casts(
        kern,
        grid=(m // tm,),
        in_specs=[
            pl.BlockSpec((tm, d), lambda i: (i, 0)),
            _resident((1, d), lambda i: (0, 0)),
            _resident(w.shape, lambda i: (0, 0)),
            pl.BlockSpec((tm, HEAD_DIM), lambda i: (i % table_tiles, 0)),
            pl.BlockSpec((tm, HEAD_DIM), lambda i: (i % table_tiles, 0)),
        ],
        out_specs=[pl.BlockSpec(b, lambda i: (i, 0)) for b in blocks],
        out_shape=[jax.ShapeDtypeStruct((m // tm * b[0], b[1]), F32) for b in blocks],
        args=(x, g, w, cos, sin),
        casts=casts,
        name=name,
    )


def _rope_tables(pos):
    inv = jnp.power(jnp.float32(ROPE_THETA), -jnp.arange(ROPE_HALF, dtype=F32) * 2.0 / ROPE_DIM)
    ang = pos.astype(F32)[:, None] * inv[None, :]
    cos, sin = jnp.cos(ang), jnp.sin(ang)
    n = pos.shape[0]
    pad = HEAD_DIM - ROPE_DIM
    cos_t = jnp.concatenate([cos, cos, jnp.ones((n, pad), F32)], axis=1)
    sin_t = jnp.concatenate([-sin, sin, jnp.zeros((n, pad), F32)], axis=1)
    return cos_t, sin_t


def _window_sums(ext_ref, lvl_refs, g, w, rows):
    cols = slice(g * POOL_GROUP, (g + 1) * POOL_GROUP)
    levels = w.bit_length() - 1
    src, src_cols = ext_ref, cols
    for level in range(levels):
        shift = 1 << level
        last = level == levels - 1
        lo = HALO if last else 8 * (level + 1)
        assert lo - shift >= 8 * level
        total = src[lo:rows, src_cols] + src[lo - shift:rows - shift, src_cols]
        if last:
            return total
        lvl_refs[level % 2][lo:rows, :] = total
        src, src_cols = lvl_refs[level % 2], slice(None)


def _in_proj_pool_kernel(x_ref, g_ref, w_ref, cos_ref, sin_ref, wp_ref, scale_ref,
                         pool_ref, q_ref, k_ref, v_ref, kh_ref, vh_ref, state_ref,
                         ext_ref, lvl_a_ref, lvl_b_ref, *, tm, tiles_per_seq):
    t_in_seq = pl.program_id(0) % tiles_per_seq
    xn = _rmsnorm(x_ref[...], g_ref[...]).astype(BF16)

    @pl.when(t_in_seq == 0)
    def _():
        ext_ref[0:HALO, :] = jnp.zeros((HALO, POOL_WIDTH), F32)

    ext_ref[HALO:HALO + tm, :] = _dot(xn, w_ref[:, 0:POOL_WIDTH])
    pos = t_in_seq * tm + lax.broadcasted_iota(jnp.int32, (tm, 1), 0)
    for g, w in enumerate(POOL_WINDOWS):
        cols = slice(g * POOL_GROUP, (g + 1) * POOL_GROUP)
        wsum = _window_sums(ext_ref, (lvl_a_ref, lvl_b_ref), g, w, HALO + tm)
        cnt = jnp.minimum(w, pos + 1).astype(F32)
        pooled = (wsum / cnt - ext_ref[HALO:HALO + tm, cols]).astype(BF16)
        pool_ref[:, cols] = (_dot(pooled, wp_ref[g].astype(BF16)) * scale_ref[:, cols]).astype(pool_ref.dtype)
    state_ref[...] = ext_ref[HALO + tm - STATE_ROWS:HALO + tm, :]
    ext_ref[0:HALO, :] = ext_ref[tm:tm + HALO, :]

    cos = cos_ref[...]
    sin = sin_ref[...]
    lane = lax.broadcasted_iota(jnp.int32, cos.shape, 1)
    for c, (flat_ref, heads_ref) in enumerate(((q_ref, None), (k_ref, kh_ref), (v_ref, vh_ref)), start=1):
        acc = _dot(xn, w_ref[:, c * ATTN_WIDTH:(c + 1) * ATTN_WIDTH])
        for h in range(N_HEADS):
            a = acc[:, h * HEAD_DIM:(h + 1) * HEAD_DIM]
            if flat_ref is not v_ref:
                partner = jnp.where(lane < ROPE_HALF,
                                    pltpu.roll(a, HEAD_DIM - ROPE_HALF, 1),
                                    pltpu.roll(a, ROPE_HALF, 1))
                a = a * cos + partner * sin
            _store_heads(flat_ref, FLAT, h, N_HEADS, tm, a)
            if heads_ref is not None:
                _store_heads(heads_ref, HEADS, h, N_HEADS, tm, a)


def _in_proj_pool(x, g, w, cos, sin, w_pool, pool_scale, *, tm, seq, casts=()):
    m, d = x.shape
    tiles_per_seq = seq // tm
    n_seq = m // seq
    assert POOL_WIDTH == ATTN_WIDTH and w.shape == (d, POOL_WIDTH + 3 * ATTN_WIDTH)
    kern = functools.partial(_in_proj_pool_kernel, tm=tm, tiles_per_seq=tiles_per_seq)
    row_tile = lambda width: pl.BlockSpec((tm, width), lambda i: (i, 0))
    heads_tile = pl.BlockSpec((tm * N_HEADS, HEAD_DIM), lambda i: (i, 0))
    table = pl.BlockSpec((tm, HEAD_DIM), lambda i: (i % tiles_per_seq, 0))
    flat = jax.ShapeDtypeStruct((m, ATTN_WIDTH), F32)
    by_heads = jax.ShapeDtypeStruct((m * N_HEADS, HEAD_DIM), F32)
    return _call_with_casts(
        kern,
        grid=(m // tm,),
        in_specs=[
            row_tile(d),
            _resident((1, d), lambda i: (0, 0)),
            _resident(w.shape, lambda i: (0, 0)),
            table,
            table,
            _resident(w_pool.shape, lambda i: (0, 0, 0)),
            _resident((1, POOL_WIDTH), lambda i: (0, 0)),
        ],
        out_specs=[row_tile(POOL_WIDTH), row_tile(ATTN_WIDTH), row_tile(ATTN_WIDTH), row_tile(ATTN_WIDTH),
                   heads_tile, heads_tile,
                   pl.BlockSpec((STATE_ROWS, POOL_WIDTH), lambda i: (i // tiles_per_seq, 0))],
        out_shape=[jax.ShapeDtypeStruct((m, POOL_WIDTH), BF16), flat, flat, flat, by_heads, by_heads,
                   jax.ShapeDtypeStruct((n_seq * STATE_ROWS, POOL_WIDTH), F32)],
        scratch_shapes=[pltpu.VMEM((HALO + tm, POOL_WIDTH), F32),
                        pltpu.VMEM((HALO + tm, POOL_GROUP), F32),
                        pltpu.VMEM((HALO + tm, POOL_GROUP), F32)],
        args=(x, g, w, cos, sin, w_pool, pool_scale),
        casts=casts,
        name="in_proj_pool_prompt",
    )


def _pool_sample_kernel(u_ref, prev_ref, wp_ref, scale_ref, o_ref, state_ref, *, pos):
    for j in range(POOL_STATE - 1):
        state_ref[j] = prev_ref[j + 1]
    state_ref[POOL_STATE - 1] = u_ref[...]
    for g, w in enumerate(POOL_WINDOWS):
        cols = slice(g * POOL_GROUP, (g + 1) * POOL_GROUP)
        wsum = u_ref[:, cols]
        for back in range(1, w):
            wsum = wsum + prev_ref[POOL_STATE - back, :, cols]
        cnt = float(min(w, pos + 1))
        pooled = (wsum / cnt - u_ref[:, cols]).astype(BF16)
        o_ref[:, cols] = (_dot(pooled, wp_ref[g].astype(BF16)) * scale_ref[:, cols]).astype(o_ref.dtype)


def _pool_sample(u, prev_t, w_pool, pool_scale, *, pos):
    n, c = u.shape
    kern = functools.partial(_pool_sample_kernel, pos=pos)
    return pl.pallas_call(
        kern,
        out_shape=[jax.ShapeDtypeStruct((n, c), BF16), jax.ShapeDtypeStruct(prev_t.shape, F32)],
        compiler_params=pltpu.CompilerParams(vmem_limit_bytes=VMEM_LIMIT_BYTES),
        name="pool_sample",
    )(u, prev_t, w_pool, pool_scale)


def _run_blocks(count, unroll, body):
    trips = count // unroll
    if trips > 1:
        def trip(t, carry):
            for j in range(unroll):
                body(t * unroll + j)
            return carry
        lax.fori_loop(0, trips, trip, 0)
        done = trips * unroll
    else:
        done = 0
    for idx in range(done, count):
        body(idx)


def _attn_prompt_kernel(q_ref, k_ref, v_ref, o_ref, *scratch, seq, unroll):
    n_br = len(DILATED)
    ob_refs, mb_refs, sb_refs = scratch[:n_br], scratch[n_br:2 * n_br], scratch[2 * n_br:3 * n_br]
    p_ref, bias_ref = scratch[3 * n_br:]
    exp2_scale = HEAD_DIM ** -0.5 * LOG2_E
    qi = lax.broadcasted_iota(jnp.int32, (BLOCK, 2 * BLOCK), 0)
    kj = lax.broadcasted_iota(jnp.int32, (BLOCK, 2 * BLOCK), 1)
    in_band = ((kj < BLOCK) & (kj >= qi)) | ((kj >= BLOCK) & (kj - BLOCK <= qi))
    bias_ref[...] = jnp.where(in_band, 0.0, NEG_INF)
    ones = jnp.ones((2 * BLOCK, HEAD_DIM), BF16)

    def rows(block, r, d):
        if d == 1:
            start = block * BLOCK
            return pl.ds(start if isinstance(start, int) else pl.multiple_of(start, BLOCK), BLOCK)
        return pl.ds(block * (BLOCK * d) + r, BLOCK, stride=d)

    def keys(ref, block, r, d, with_prev):
        cur = ref[rows(block, r, d), :]
        if not with_prev:
            return cur.astype(BF16)
        return jnp.concatenate([ref[rows(block - 1, r, d), :], cur], axis=0).astype(BF16)

    for g, (_, d) in enumerate(DILATED):
        n_blocks = seq // d // BLOCK
        for with_prev in (False, True):
            per_r = n_blocks - 1 if with_prev else 1
            count = d * per_r
            if count == 0:
                continue
            cols = slice(0, 2 * BLOCK) if with_prev else slice(BLOCK, 2 * BLOCK)
            width = 2 * BLOCK if with_prev else BLOCK

            def locate(idx, per_r=per_r, with_prev=with_prev):
                return (idx % per_r + 1, idx // per_r) if with_prev else (0, idx)

            def probabilities(idx, g=g, d=d, with_prev=with_prev, cols=cols, width=width, locate=locate):
                b, r = locate(idx)
                q = q_ref[rows(b, r, d), :].astype(BF16)
                s = _dot_nt(q, keys(k_ref, b, r, d, with_prev)) + bias_ref[:, cols]
                m = jnp.max(s, axis=-1, keepdims=True)
                p_ref[idx, :, 0:width] = jnp.exp2((s - m) * exp2_scale).astype(BF16)
                mb_refs[g][rows(b, r, d), :] = jnp.broadcast_to(m, (BLOCK, HEAD_DIM))

            def values(idx, g=g, d=d, with_prev=with_prev, width=width, locate=locate):
                b, r = locate(idx)
                v1 = jnp.concatenate([keys(v_ref, b, r, d, with_prev), ones[0:width]], axis=1)
                acc = _dot(p_ref[idx, :, 0:width], v1)
                ob_refs[g][rows(b, r, d), :] = acc[:, 0:HEAD_DIM]
                sb_refs[g][rows(b, r, d), :] = acc[:, HEAD_DIM:]

            _run_blocks(count, unroll, probabilities)
            _run_blocks(count, unroll, values)

    chunk = 256

    def combine(c, carry):
        rw = pl.ds(pl.multiple_of(c * chunk, chunk), chunk)
        ms = [mb[rw, :] for mb in mb_refs]
        m = functools.reduce(jnp.maximum, ms)
        ws = [jnp.exp2((mg - m) * exp2_scale) for mg in ms]
        num = functools.reduce(jnp.add, [w * ob[rw, :] for w, ob in zip(ws, ob_refs)])
        den = functools.reduce(jnp.add, [w * sb[rw, :] for w, sb in zip(ws, sb_refs)])
        o_ref[rw, :] = (num / den).astype(o_ref.dtype)
        return carry

    lax.fori_loop(0, seq // chunk, combine, 0)


def _attn_prompt(q, k, v, *, seq, casts=()):
    m, width = q.shape
    n_seq = m // seq
    n_heads = width // HEAD_DIM
    spec = pl.BlockSpec((seq, HEAD_DIM), lambda n, h: (n, h))
    kern = functools.partial(_attn_prompt_kernel, seq=seq, unroll=8)
    (attn,), cast_out = _call_with_casts(
        kern,
        grid=(n_seq, n_heads),
        in_specs=[spec, spec, spec],
        out_specs=[spec],
        out_shape=[jax.ShapeDtypeStruct((m, width), BF16)],
        scratch_shapes=(
            [pltpu.VMEM((seq, HEAD_DIM), F32) for _ in range(3 * len(DILATED))]
            + [pltpu.VMEM((seq // BLOCK, BLOCK, 2 * BLOCK), BF16),
               pltpu.VMEM((BLOCK, 2 * BLOCK), F32)]),
        args=(q, k, v),
        casts=casts,
        step_of=lambda n, h: n * n_heads + h,
        name="attn_prompt",
    )
    return attn, cast_out


def _attn_sample_kernel(q_ref, kn_ref, vn_ref, *refs):
    n_br = len(DILATED)
    kc_refs, vc_refs, o_ref = refs[:n_br], refs[n_br:2 * n_br], refs[2 * n_br]
    scale = HEAD_DIM ** -0.5
    q = q_ref[...]
    s_new = jnp.sum(q * kn_ref[...], axis=-1, keepdims=True) * scale
    s_br = [jnp.sum(kc[...] * q[None], axis=-1, keepdims=True) * scale for kc in kc_refs]
    m = s_new
    for s in s_br:
        m = jnp.maximum(m, jnp.max(s, axis=0))
    p_new = jnp.exp(s_new - m) * float(n_br)
    den = p_new
    num = p_new * vn_ref[...]
    for s, vc in zip(s_br, vc_refs):
        p = jnp.exp(s - m[None])
        den = den + jnp.sum(p, axis=0)
        num = num + jnp.sum(p * vc[...], axis=0)
    o_ref[...] = num / den


def _attn_sample(q, k_new, v_new, k_cache, v_cache):
    n, heads, _ = q.shape
    win = k_cache.shape[1]
    row = pl.BlockSpec((None, heads, HEAD_DIM), lambda i: (i, 0, 0))
    cache_specs, k_views, v_views = [], [], []
    for w, d in DILATED:
        assert w == BLOCK * d and win % w == 0
        last = win // w - 1
        cache_specs.append(pl.BlockSpec((None, BLOCK, None, heads, HEAD_DIM),
                                        lambda i, last=last: (i, last, 0, 0, 0)))
        k_views.append(k_cache.reshape(n, win // d, d, heads, HEAD_DIM))
        v_views.append(v_cache.reshape(n, win // d, d, heads, HEAD_DIM))
    return pl.pallas_call(
        _attn_sample_kernel,
        grid=(n,),
        in_specs=[row, row, row] + cache_specs + cache_specs,
        out_specs=row,
        out_shape=jax.ShapeDtypeStruct((n, heads, HEAD_DIM), F32),
        compiler_params=_params(1),
        name="attn_sample",
    )(q, k_new, v_new, *k_views, *v_views)


def _out_proj_kernel(pool_ref, attn_ref, x_ref, wo_ref, gpost_ref, gpre_ref, wq_ref, x1_ref, qm_ref,
                     *, tm, qm_layout):
    y = _dot(pool_ref[...].astype(BF16), wo_ref[0:POOL_WIDTH, :])
    y = y + _dot(attn_ref[...].astype(BF16), wo_ref[POOL_WIDTH:, :])
    x1 = x_ref[...] + _rmsnorm(y, gpost_ref[...])
    x1_ref[...] = x1
    qm = _dot(_rmsnorm(x1, gpre_ref[...]).astype(BF16), wq_ref[...])
    for h in range(MEM_HEADS):
        _store_heads(qm_ref, qm_layout, h, MEM_HEADS, tm, qm[:, h * HEAD_DIM:(h + 1) * HEAD_DIM])


def _out_proj(pool_out, attn, x, w_out, g_post, g_pre, w_xq, *, tm, qm_layout, name):
    m, d = x.shape
    half = pool_out.shape[1]
    qm_block = _out_block(qm_layout, tm, MEM_HEADS)
    kern = functools.partial(_out_proj_kernel, tm=tm, qm_layout=qm_layout)
    return pl.pallas_call(
        kern,
        grid=(m // tm,),
        in_specs=[
            pl.BlockSpec((tm, half), lambda i: (i, 0)),
            pl.BlockSpec((tm, half), lambda i: (i, 0)),
            pl.BlockSpec((tm, d), lambda i: (i, 0)),
            _resident(w_out.shape, lambda i: (0, 0)),
            _resident((1, d), lambda i: (0, 0)),
            _resident((1, d), lambda i: (0, 0)),
            _resident(w_xq.shape, lambda i: (0, 0)),
        ],
        out_specs=[pl.BlockSpec((tm, d), lambda i: (i, 0)),
                   pl.BlockSpec(qm_block, lambda i: (i, 0))],
        out_shape=[jax.ShapeDtypeStruct((m, d), F32),
                   jax.ShapeDtypeStruct((m // tm * qm_block[0], qm_block[1]), F32)],
        compiler_params=_params(1),
        name=name,
    )(pool_out, attn, x, w_out, g_post, g_pre, w_xq)


def _mix_mem_prompt_kernel(pool_ref, attn_ref, x_ref, mk_ref, mv_ref, wo_ref, gmix_ref, gpre_ref, wq_ref,
                           wxo_ref, gmem_ref, x2_ref, *, tm, row_split):
    scale = HEAD_DIM ** -0.5
    for part in range(row_split):
        rows = slice(part * (tm // row_split), (part + 1) * (tm // row_split))
        mixed = jnp.concatenate([pool_ref[rows, :], attn_ref[rows, :]], axis=-1)
        x1 = x_ref[rows, :] + _rmsnorm(_dot(mixed, wo_ref[...]), gmix_ref[...])
        qm = _dot(_rmsnorm(x1, gpre_ref[...]).astype(BF16), wq_ref[...]).astype(BF16)
        heads = []
        for h in range(MEM_HEADS):
            cols = slice(h * HEAD_DIM, (h + 1) * HEAD_DIM)
            s = _dot_nt(qm[:, cols], mk_ref[:, cols].astype(BF16)) * scale
            p = jnp.exp(s - jnp.max(s, axis=-1, keepdims=True))
            o = _dot(p.astype(BF16), mv_ref[:, cols].astype(BF16))
            heads.append((o / jnp.sum(p, axis=-1, keepdims=True)).astype(BF16))
        y = _dot(jnp.concatenate(heads, axis=-1), wxo_ref[...])
        x2_ref[rows, :] = x1 + _rmsnorm(y, gmem_ref[...])


def _mix_mem_prompt(pool_out, attn, x, mk, mv, w_out, g_mix_post, g_mem_pre, w_xq, w_xo, g_mem_post, *, tm, seq):
    m, d = x.shape
    tiles_per_seq = seq // tm
    half = pool_out.shape[1]
    mem_spec = pl.BlockSpec((N_MEM, MEM_WIDTH), lambda i: (i // tiles_per_seq, 0))
    gain_spec = _resident((1, d), lambda i: (0, 0))
    return pl.pallas_call(
        functools.partial(_mix_mem_prompt_kernel, tm=tm, row_split=1),
        grid=(m // tm,),
        in_specs=[
            pl.BlockSpec((tm, half), lambda i: (i, 0)),
            pl.BlockSpec((tm, half), lambda i: (i, 0)),
            pl.BlockSpec((tm, d), lambda i: (i, 0)),
            mem_spec,
            mem_spec,
            _resident(w_out.shape, lambda i: (0, 0)),
            gain_spec,
            gain_spec,
            _resident(w_xq.shape, lambda i: (0, 0)),
            _resident(w_xo.shape, lambda i: (0, 0)),
            gain_spec,
        ],
        out_specs=pl.BlockSpec((tm, d), lambda i: (i, 0)),
        out_shape=jax.ShapeDtypeStruct((m, d), F32),
        compiler_params=_params(1),
        name="mix_mem_prompt",
    )(pool_out, attn, x, mk, mv, w_out, g_mix_post, g_mem_pre, w_xq, w_xo, g_mem_post)


def _mem_attn_sample_kernel(qm_ref, mk_ref, mv_ref, o_ref):
    scale = HEAD_DIM ** -0.5
    s = jnp.sum(mk_ref[...] * qm_ref[...][None], axis=-1, keepdims=True) * scale
    p = jnp.exp(s - jnp.max(s, axis=0)[None])
    o_ref[...] = jnp.sum(p * mv_ref[...], axis=0) / jnp.sum(p, axis=0)


def _mem_attn_sample(qm, mem_k, mem_v):
    n, heads, _ = qm.shape
    row = pl.BlockSpec((None, heads, HEAD_DIM), lambda i: (i, 0, 0))
    mem = pl.BlockSpec((None, N_MEM, heads, HEAD_DIM), lambda i: (i, 0, 0, 0))
    return pl.pallas_call(
        _mem_attn_sample_kernel,
        grid=(n,),
        in_specs=[row, mem, mem],
        out_specs=row,
        out_shape=jax.ShapeDtypeStruct((n, heads, HEAD_DIM), F32),
        compiler_params=_params(1),
        name="mem_attn_sample",
    )(qm, mem_k, mem_v)


def _proj_norm_residual_kernel(a_ref, w_ref, g_ref, r_ref, o_ref):
    y = _dot(a_ref[...].astype(BF16), w_ref[...])
    o_ref[...] = r_ref[...] + _rmsnorm(y, g_ref[...])


def _proj_norm_residual(a, w, g, resid, *, name):
    return pl.pallas_call(
        _proj_norm_residual_kernel,
        out_shape=jax.ShapeDtypeStruct(resid.shape, F32),
        compiler_params=pltpu.CompilerParams(vmem_limit_bytes=VMEM_LIMIT_BYTES),
        name=name,
    )(a, w, g, resid)


def _ffn_kernel(x_ref, gpre_ref, w1_ref, w2_ref, gpost_ref, o_ref, xn_ref, acc_ref):
    f = pl.program_id(1)

    @pl.when(f == 0)
    def _():
        xn_ref[...] = _rmsnorm(x_ref[...], gpre_ref[...]).astype(BF16)
        acc_ref[...] = jnp.zeros_like(acc_ref)

    hidden = jnp.square(jnp.maximum(_dot(xn_ref[...], w1_ref[...]), 0.0)).astype(BF16)
    acc_ref[...] += _dot(hidden, w2_ref[...])

    @pl.when(f == pl.num_programs(1) - 1)
    def _():
        o_ref[...] = x_ref[...] + _rmsnorm(acc_ref[...], gpost_ref[...])


def _ffn(x, g_pre, w1, w2, g_post, *, tm, tf, name):
    m, d = x.shape
    dff = w1.shape[1]
    return pl.pallas_call(
        _ffn_kernel,
        grid=(m // tm, dff // tf),
        in_specs=[
            pl.BlockSpec((tm, d), lambda i, f: (i, 0)),
            _resident((1, d), lambda i, f: (0, 0)),
            pl.BlockSpec((d, tf), lambda i, f: (0, f)),
            pl.BlockSpec((tf, d), lambda i, f: (f, 0)),
            _resident((1, d), lambda i, f: (0, 0)),
        ],
        out_specs=pl.BlockSpec((tm, d), lambda i, f: (i, 0)),
        out_shape=jax.ShapeDtypeStruct((m, d), F32),
        scratch_shapes=[pltpu.VMEM((tm, d), BF16), pltpu.VMEM((tm, d), F32)],
        compiler_params=_params(2),
        name=name,
    )(x, g_pre, w1, w2, g_post)


def kernel(x_prompt, x_sample, state_pool, cache_attn_k, cache_attn_v, cache_mem_k, cache_mem_v, mem_prompt,
           g_mix_pre, g_mix_post, g_mem_pre, g_mem_post, g_ffn_pre, g_ffn_post, g_mem_kv,
           w_in, w_pool, pool_scale, w_out, w_xq, w_mem_kv, w_xo, w_ff1, w_ff2):
    depth = w_in.shape[0]
    assert depth == 1
    batch, seq, d = x_prompt.shape
    dec_batch, dec_seq, _ = x_sample.shape
    assert dec_seq == 1

    l = 0
    scale = pool_scale[l][None, :]
    gain = lambda g: g[l][None, :]

    tm = 512
    xp = x_prompt.reshape(batch * seq, d)
    xs = x_sample.reshape(dec_batch, d)
    cos_p, sin_p = _rope_tables(jnp.arange(seq, dtype=jnp.int32))
    cos_s, sin_s = _rope_tables(jnp.full((dec_batch,), PAST_LEN, jnp.int32))
    tm_mem = 128
    ones, zeros = jnp.ones((tm_mem, HEAD_DIM), F32), jnp.zeros((tm_mem, HEAD_DIM), F32)
    w_pool_f = w_pool[l]


    mem = mem_prompt.reshape(batch * N_MEM, d)
    (mk, mv, mk_h, mv_h), (w_in_b,) = _norm_proj(
        mem, gain(g_mem_kv), w_mem_kv[l], ones, zeros, tm=tm_mem, ncol=MEM_WIDTH, rope_blocks=(),
        outputs=((0, FLAT), (1, FLAT), (0, HEADS), (1, HEADS)), name="mem_kv", casts=(w_in[l],))

    (pool_out, q, k, v, k_h, v_h, pool_state), (w_ff1_b, w_out_b, w_xq_b, w_xo_b) = _in_proj_pool(
        xp, gain(g_mix_pre), w_in_b, cos_p, sin_p, w_pool_f, scale, tm=256, seq=seq,
        casts=(w_ff1[l], w_out[l], w_xq[l], w_xo[l]))
    attn, (w_ff2_b,) = _attn_prompt(q, k, v, seq=seq, casts=(w_ff2[l],))
    x2 = _mix_mem_prompt(pool_out, attn, xp, mk, mv, w_out_b, gain(g_mix_post), gain(g_mem_pre), w_xq_b,
                         w_xo_b, gain(g_mem_post), tm=tm, seq=seq)
    yp = _ffn(x2, gain(g_ffn_pre), w_ff1_b, w_ff2_b, gain(g_ffn_post), tm=tm, tf=1024, name="ffn_prompt")

    (us, qs_h, ks_h, vs_h), _ = _norm_proj(
        xs, gain(g_mix_pre), w_in_b, cos_s, sin_s, tm=dec_batch, ncol=POOL_WIDTH, rope_blocks=(1, 2),
        outputs=((0, FLAT), (1, HEADS), (2, HEADS), (3, HEADS)), name="in_proj_sample")
    heads3 = lambda a: a.reshape(dec_batch, -1, HEAD_DIM)
    pool_out_s, new_pool_s = _pool_sample(us, jnp.swapaxes(state_pool[l], 0, 1), w_pool_f, scale, pos=PAST_LEN)
    attn_s = _attn_sample(heads3(qs_h), heads3(ks_h), heads3(vs_h), cache_attn_k[l], cache_attn_v[l])
    x1s, qms_h = _out_proj(pool_out_s, attn_s.reshape(dec_batch, ATTN_WIDTH), xs, w_out_b, gain(g_mix_post),
                           gain(g_mem_pre), w_xq_b, tm=dec_batch, qm_layout=HEADS, name="out_proj_sample")
    mem_o_s = _mem_attn_sample(heads3(qms_h), cache_mem_k[l], cache_mem_v[l])
    x2s = _proj_norm_residual(mem_o_s.reshape(dec_batch, MEM_WIDTH), w_xo_b, gain(g_mem_post), x1s,
                              name="mem_out_sample")
    ys = _ffn(x2s, gain(g_ffn_pre), w_ff1_b, w_ff2_b, gain(g_ffn_post), tm=dec_batch, tf=2048,
              name="ffn_sample")

    keep = min(max(w for w, _ in DILATED), seq)
    return (
        yp.reshape(batch, seq, d),
        ys.reshape(dec_batch, 1, d),
        pool_state.reshape(batch, STATE_ROWS, POOL_WIDTH)[:, STATE_ROWS - POOL_STATE:][None],
        jnp.swapaxes(new_pool_s, 0, 1)[None],
        k_h.reshape(batch, seq, N_HEADS, HEAD_DIM)[:, seq - keep:][None],
        v_h.reshape(batch, seq, N_HEADS, HEAD_DIM)[:, seq - keep:][None],
        ks_h.reshape(dec_batch, 1, N_HEADS, HEAD_DIM)[None],
        vs_h.reshape(dec_batch, 1, N_HEADS, HEAD_DIM)[None],
        mk_h.reshape(batch, N_MEM, MEM_HEADS, HEAD_DIM)[None],
        mv_h.reshape(batch, N_MEM, MEM_HEADS, HEAD_DIM)[None],
    )
```

```python
import functools

import jax
import jax.numpy as jnp
from jax import lax
from jax.experimental import pallas as pl
from jax.experimental.pallas import tpu as pltpu

D_MODEL = 2048
POOL_WIDTH = 1024
POOL_WINDOWS = (2, 4, 8, 16)
POOL_GROUP = POOL_WIDTH // len(POOL_WINDOWS)
POOL_STATE = max(POOL_WINDOWS) - 1
HEAD_DIM = 128
ATTN_WIDTH = 1024
N_HEADS = ATTN_WIDTH // HEAD_DIM
DILATED = ((128, 1), (512, 4), (2048, 16))
ROPE_DIM = HEAD_DIM // 4
ROPE_HALF = ROPE_DIM // 2
ROPE_THETA = 500000.0
N_MEM = 256
MEM_HEADS = 4
MEM_WIDTH = MEM_HEADS * HEAD_DIM
EPS = 1e-6
BLOCK = 128
NEG_INF = -1e30
LOG2_E = 1.4426950408889634
PAST_LEN = 8192
HALO = 32
STATE_ROWS = 16

F32 = jnp.float32
BF16 = jnp.bfloat16

VMEM_LIMIT_BYTES = 56 * 1024 * 1024

FLAT, HEADS = "flat", "heads"


def _params(n_grid_axes):
    return pltpu.CompilerParams(
        dimension_semantics=("arbitrary",) * n_grid_axes,
        vmem_limit_bytes=VMEM_LIMIT_BYTES,
    )


def _resident(block_shape, index_map):
    return pl.BlockSpec(block_shape, index_map, pipeline_mode=pl.Buffered(1))


def _rmsnorm(x, g):
    ms = jnp.mean(x * x, axis=-1, keepdims=True)
    return x * lax.rsqrt(ms + EPS) * g


def _dot(a, b):
    return jnp.dot(a, b, preferred_element_type=F32)


def _dot_nt(a, b):
    return lax.dot_general(a, b, (((1,), (1,)), ((), ())), preferred_element_type=F32)


def _call_with_casts(kernel_fn, *, grid, in_specs, out_specs, out_shape, scratch_shapes=(), args, casts=(),
                     step_of=None, name):
    n_in, n_out, n_casts = len(in_specs), len(out_specs), len(casts)
    n_steps = 1
    for g in grid:
        n_steps *= g
    if step_of is None:
        step_of = lambda i: i
    cast_specs = []
    for a in casts:
        rows, cols = a.shape
        assert rows % n_steps == 0
        cast_specs.append(pl.BlockSpec((rows // n_steps, cols), lambda *idx: (step_of(*idx), 0)))

    def body(*refs):
        ins, cast_in = refs[:n_in], refs[n_in:n_in + n_casts]
        outs = refs[n_in + n_casts:n_in + n_casts + n_out]
        cast_out = refs[n_in + n_casts + n_out:n_in + 2 * n_casts + n_out]
        for i_ref, o_ref in zip(cast_in, cast_out):
            o_ref[...] = i_ref[...].astype(BF16)
        kernel_fn(*ins, *outs, *refs[n_in + 2 * n_casts + n_out:])

    res = pl.pallas_call(
        body,
        grid=grid,
        in_specs=list(in_specs) + cast_specs,
        out_specs=list(out_specs) + cast_specs,
        out_shape=list(out_shape) + [jax.ShapeDtypeStruct(a.shape, BF16) for a in casts],
        scratch_shapes=list(scratch_shapes),
        compiler_params=_params(len(grid)),
        name=name,
    )(*args, *casts)
    return res[:n_out], res[n_out:]


def _store_heads(o_ref, layout, h, heads, rows, value):
    if layout == FLAT:
        o_ref[:, h * HEAD_DIM:(h + 1) * HEAD_DIM] = value
    else:
        o_ref[pl.ds(h, rows, stride=heads), :] = value


def _out_block(layout, tm, heads):
    return (tm, heads * HEAD_DIM) if layout == FLAT else (tm * heads, HEAD_DIM)


def _norm_proj_kernel(x_ref, g_ref, w_ref, cos_ref, sin_ref, *out_refs, tm, ncol, rope_blocks, outputs):
    heads = ncol // HEAD_DIM
    xn = _rmsnorm(x_ref[...], g_ref[...]).astype(BF16)
    for c in sorted({c for c, _ in outputs}):
        acc = _dot(xn, w_ref[:, c * ncol:(c + 1) * ncol].astype(BF16))
        if c in rope_blocks:
            cos = cos_ref[...]
            sin = sin_ref[...]
            lane = lax.broadcasted_iota(jnp.int32, cos.shape, 1)
        for h in range(heads):
            a = acc[:, h * HEAD_DIM:(h + 1) * HEAD_DIM]
            if c in rope_blocks:
                partner = jnp.where(lane < ROPE_HALF,
                                    pltpu.roll(a, HEAD_DIM - ROPE_HALF, 1),
                                    pltpu.roll(a, ROPE_HALF, 1))
                a = a * cos + partner * sin
            for (oc, layout), o_ref in zip(outputs, out_refs):
                if oc == c:
                    _store_heads(o_ref, layout, h, heads, tm, a)


def _norm_proj(x, g, w, cos, sin, *, tm, ncol, rope_blocks, outputs, name, casts=()):
    m, d = x.shape
    heads = ncol // HEAD_DIM
    assert w.shape[0] == d and w.shape[1] % ncol == 0 and m % tm == 0
    table_tiles = cos.shape[0] // tm
    kern = functools.partial(_norm_proj_kernel, tm=tm, ncol=ncol, rope_blocks=rope_blocks, outputs=outputs)
    blocks = [_out_block(layout, tm, heads) for _, layout in outputs]
    return _call_with_casts(
        kern,
        grid=(m // tm,),
        in_specs=[
            pl.BlockSpec((tm, d), lambda i: (i, 0)),
            _resident((1, d), lambda i: (0, 0)),
            _resident(w.shape, lambda i: (0, 0)),
            pl.BlockSpec((tm, HEAD_DIM), lambda i: (i % table_tiles, 0)),
            pl.BlockSpec((tm, HEAD_DIM), lambda i: (i % table_tiles, 0)),
        ],
        out_specs=[pl.BlockSpec(b, lambda i: (i, 0)) for b in blocks],
        out_shape=[jax.ShapeDtypeStruct((m // tm * b[0], b[1]), F32) for b in blocks],
        args=(x, g, w, cos, sin),
        casts=casts,
        name=name,
    )


def _rope_tables(pos):
    inv = jnp.power(jnp.float32(ROPE_THETA), -jnp.arange(ROPE_HALF, dtype=F32) * 2.0 / ROPE_DIM)
    ang = pos.astype(F32)[:, None] * inv[None, :]
    cos, sin = jnp.cos(ang), jnp.sin(ang)
    n = pos.shape[0]
    pad = HEAD_DIM - ROPE_DIM
    cos_t = jnp.concatenate([cos, cos, jnp.ones((n, pad), F32)], axis=1)
    sin_t = jnp.concatenate([-sin, sin, jnp.zeros((n, pad), F32)], axis=1)
    return cos_t, sin_t


def _window_sums(ext_ref, lvl_refs, g, w, rows):
    cols = slice(g * POOL_GROUP, (g + 1) * POOL_GROUP)
    levels = w.bit_length() - 1
    src, src_cols = ext_ref, cols
    for level in range(levels):
        shift = 1 << level
        last = level == levels - 1
        lo = HALO if last else 8 * (level + 1)
        assert lo - shift >= 8 * level
        total = src[lo:rows, src_cols] + src[lo - shift:rows - shift, src_cols]
        if last:
            return total
        lvl_refs[level % 2][lo:rows, :] = total
        src, src_cols = lvl_refs[level % 2], slice(None)


def _in_proj_pool_kernel(x_ref, xnext_ref, g_ref, w_ref, cos_ref, sin_ref, wp_ref, scale_ref,
                         pool_ref, q_ref, k_ref, v_ref, kh_ref, vh_ref, state_ref,
                         xn_ref, ext_ref, lvl_a_ref, lvl_b_ref, *, tm, tiles_per_seq):
    step = pl.program_id(0)
    t_in_seq = step % tiles_per_seq

    @pl.when(step == 0)
    def _():
        xn_ref[0] = _rmsnorm(x_ref[...], g_ref[...]).astype(BF16)

    @pl.when(t_in_seq == 0)
    def _():
        ext_ref[0:HALO, :] = jnp.zeros((HALO, POOL_WIDTH), F32)

    xn = xn_ref[step % 2]

    ext_ref[HALO:HALO + tm, :] = _dot(xn, w_ref[:, 0:POOL_WIDTH])

    cos = cos_ref[...]
    sin = sin_ref[...]
    lane = lax.broadcasted_iota(jnp.int32, cos.shape, 1)
    for c, (flat_ref, heads_ref) in enumerate(((q_ref, None), (k_ref, kh_ref), (v_ref, vh_ref)), start=1):
        acc = _dot(xn, w_ref[:, c * ATTN_WIDTH:(c + 1) * ATTN_WIDTH])
        for h in range(N_HEADS):
            a = acc[:, h * HEAD_DIM:(h + 1) * HEAD_DIM]
            if flat_ref is not v_ref:
                partner = jnp.where(lane < ROPE_HALF,
                                    pltpu.roll(a, HEAD_DIM - ROPE_HALF, 1),
                                    pltpu.roll(a, ROPE_HALF, 1))
                a = a * cos + partner * sin
            _store_heads(flat_ref, FLAT, h, N_HEADS, tm, a)
            if heads_ref is not None:
                _store_heads(heads_ref, HEADS, h, N_HEADS, tm, a)

    pos = t_in_seq * tm + lax.broadcasted_iota(jnp.int32, (tm, 1), 0)
    for g, w in enumerate(POOL_WINDOWS):
        cols = slice(g * POOL_GROUP, (g + 1) * POOL_GROUP)
        wsum = _window_sums(ext_ref, (lvl_a_ref, lvl_b_ref), g, w, HALO + tm)
        cnt = jnp.minimum(w, pos + 1).astype(F32)
        pooled = (wsum / cnt - ext_ref[HALO:HALO + tm, cols]).astype(BF16)
        pool_ref[:, cols] = (_dot(pooled, wp_ref[g].astype(BF16)) * scale_ref[:, cols]).astype(pool_ref.dtype)
    state_ref[...] = ext_ref[HALO + tm - STATE_ROWS:HALO + tm, :]
    ext_ref[0:HALO, :] = ext_ref[tm:tm + HALO, :]

    xn_ref[(step + 1) % 2] = _rmsnorm(xnext_ref[...], g_ref[...]).astype(BF16)


def _in_proj_pool(x, g, w, cos, sin, w_pool, pool_scale, *, tm, seq, casts=()):
    m, d = x.shape
    tiles_per_seq = seq // tm
    n_seq = m // seq
    assert POOL_WIDTH == ATTN_WIDTH and w.shape == (d, POOL_WIDTH + 3 * ATTN_WIDTH)
    kern = functools.partial(_in_proj_pool_kernel, tm=tm, tiles_per_seq=tiles_per_seq)
    row_tile = lambda width: pl.BlockSpec((tm, width), lambda i: (i, 0))
    heads_tile = pl.BlockSpec((tm * N_HEADS, HEAD_DIM), lambda i: (i, 0))
    table = pl.BlockSpec((tm, HEAD_DIM), lambda i: (i % tiles_per_seq, 0))
    flat = jax.ShapeDtypeStruct((m, ATTN_WIDTH), F32)
    by_heads = jax.ShapeDtypeStruct((m * N_HEADS, HEAD_DIM), F32)
    return _call_with_casts(
        kern,
        grid=(m // tm,),
        in_specs=[
            row_tile(d),
            pl.BlockSpec((tm, d), lambda i: (jnp.minimum(i + 1, m // tm - 1), 0)),
            _resident((1, d), lambda i: (0, 0)),
            _resident(w.shape, lambda i: (0, 0)),
            table,
            table,
            _resident(w_pool.shape, lambda i: (0, 0, 0)),
            _resident((1, POOL_WIDTH), lambda i: (0, 0)),
        ],
        out_specs=[row_tile(POOL_WIDTH), row_tile(ATTN_WIDTH), row_tile(ATTN_WIDTH), row_tile(ATTN_WIDTH),
                   heads_tile, heads_tile,
                   pl.BlockSpec((STATE_ROWS, POOL_WIDTH), lambda i: (i // tiles_per_seq, 0))],
        out_shape=[jax.ShapeDtypeStruct((m, POOL_WIDTH), BF16), flat, flat, flat, by_heads, by_heads,
                   jax.ShapeDtypeStruct((n_seq * STATE_ROWS, POOL_WIDTH), F32)],
        scratch_shapes=[pltpu.VMEM((2, tm, d), BF16),
                        pltpu.VMEM((HALO + tm, POOL_WIDTH), F32),
                        pltpu.VMEM((HALO + tm, POOL_GROUP), F32),
                        pltpu.VMEM((HALO + tm, POOL_GROUP), F32)],
        args=(x, x, g, w, cos, sin, w_pool, pool_scale),
        casts=casts,
        name="in_proj_pool_prompt",
    )


def _pool_sample_kernel(u_ref, prev_ref, wp_ref, scale_ref, o_ref, state_ref, *, pos):
    for j in range(POOL_STATE - 1):
        state_ref[j] = prev_ref[j + 1]
    state_ref[POOL_STATE - 1] = u_ref[...]
    for g, w in enumerate(POOL_WINDOWS):
        cols = slice(g * POOL_GROUP, (g + 1) * POOL_GROUP)
        wsum = u_ref[:, cols]
        for back in range(1, w):
            wsum = wsum + prev_ref[POOL_STATE - back, :, cols]
        cnt = float(min(w, pos + 1))
        pooled = (wsum / cnt - u_ref[:, cols]).astype(BF16)
        o_ref[:, cols] = (_dot(pooled, wp_ref[g].astype(BF16)) * scale_ref[:, cols]).astype(o_ref.dtype)


def _pool_sample(u, prev_t, w_pool, pool_scale, *, pos):
    n, c = u.shape
    kern = functools.partial(_pool_sample_kernel, pos=pos)
    return pl.pallas_call(
        kern,
        out_shape=[jax.ShapeDtypeStruct((n, c), BF16), jax.ShapeDtypeStruct(prev_t.shape, F32)],
        compiler_params=pltpu.CompilerParams(vmem_limit_bytes=VMEM_LIMIT_BYTES),
        name="pool_sample",
    )(u, prev_t, w_pool, pool_scale)


def _run_blocks(count, unroll, body):
    trips = count // unroll
    if trips > 1:
        def trip(t, carry):
            for j in range(unroll):
                body(t * unroll + j)
            return carry
        lax.fori_loop(0, trips, trip, 0)
        done = trips * unroll
    else:
        done = 0
    for idx in range(done, count):
        body(idx)


def _attn_prompt_kernel(q_ref, k_ref, v_ref, o_ref, *scratch, seq, unroll):
    n_br = len(DILATED)
    ob_refs, mb_refs, sb_refs = scratch[:n_br], scratch[n_br:2 * n_br], scratch[2 * n_br:3 * n_br]
    p_ref, bias_ref = scratch[3 * n_br:]
    exp2_scale = HEAD_DIM ** -0.5 * LOG2_E
    qi = lax.broadcasted_iota(jnp.int32, (BLOCK, 2 * BLOCK), 0)
    kj = lax.broadcasted_iota(jnp.int32, (BLOCK, 2 * BLOCK), 1)
    in_band = ((kj < BLOCK) & (kj >= qi)) | ((kj >= BLOCK) & (kj - BLOCK <= qi))
    bias_ref[...] = jnp.where(in_band, 0.0, NEG_INF)
    ones = jnp.ones((2 * BLOCK, HEAD_DIM), BF16)

    def rows(block, r, d):
        if d == 1:
            start = block * BLOCK
            return pl.ds(start if isinstance(start, int) else pl.multiple_of(start, BLOCK), BLOCK)
        return pl.ds(block * (BLOCK * d) + r, BLOCK, stride=d)

    def keys(ref, block, r, d, with_prev):
        cur = ref[rows(block, r, d), :]
        if not with_prev:
            return cur.astype(BF16)
        return jnp.concatenate([ref[rows(block - 1, r, d), :], cur], axis=0).astype(BF16)

    for g, (_, d) in enumerate(DILATED):
        n_blocks = seq // d // BLOCK
        for with_prev in (False, True):
            per_r = n_blocks - 1 if with_prev else 1
            count = d * per_r
            if count == 0:
                continue
            cols = slice(0, 2 * BLOCK) if with_prev else slice(BLOCK, 2 * BLOCK)
            width = 2 * BLOCK if with_prev else BLOCK

            def locate(idx, per_r=per_r, with_prev=with_prev):
                return (idx % per_r + 1, idx // per_r) if with_prev else (0, idx)

            def probabilities(idx, g=g, d=d, with_prev=with_prev, cols=cols, width=width, locate=locate):
                b, r = locate(idx)
                q = q_ref[rows(b, r, d), :].astype(BF16)
                s = _dot_nt(q, keys(k_ref, b, r, d, with_prev)) + bias_ref[:, cols]
                m = jnp.max(s, axis=-1, keepdims=True)
                p_ref[idx, :, 0:width] = jnp.exp2((s - m) * exp2_scale).astype(BF16)
                mb_refs[g][rows(b, r, d), :] = jnp.broadcast_to(m, (BLOCK, HEAD_DIM))

            def values(idx, g=g, d=d, with_prev=with_prev, width=width, locate=locate):
                b, r = locate(idx)
                v1 = jnp.concatenate([keys(v_ref, b, r, d, with_prev), ones[0:width]], axis=1)
                acc = _dot(p_ref[idx, :, 0:width], v1)
                ob_refs[g][rows(b, r, d), :] = acc[:, 0:HEAD_DIM]
                sb_refs[g][rows(b, r, d), :] = acc[:, HEAD_DIM:]

            _run_blocks(count, unroll, probabilities)
            _run_blocks(count, unroll, values)

    chunk = 256

    def combine(c, carry):
        rw = pl.ds(pl.multiple_of(c * chunk, chunk), chunk)
        ms = [mb[rw, :] for mb in mb_refs]
        m = functools.reduce(jnp.maximum, ms)
        ws = [jnp.exp2((mg - m) * exp2_scale) for mg in ms]
        num = functools.reduce(jnp.add, [w * ob[rw, :] for w, ob in zip(ws, ob_refs)])
        den = functools.reduce(jnp.add, [w * sb[rw, :] for w, sb in zip(ws, sb_refs)])
        o_ref[rw, :] = (num / den).astype(o_ref.dtype)
        return carry

    lax.fori_loop(0, seq // chunk, combine, 0)


def _attn_prompt(q, k, v, *, seq, casts=()):
    m, width = q.shape
    n_seq = m // seq
    n_heads = width // HEAD_DIM
    spec = pl.BlockSpec((seq, HEAD_DIM), lambda n, h: (n, h))
    kern = functools.partial(_attn_prompt_kernel, seq=seq, unroll=8)
    (attn,), cast_out = _call_with_casts(
        kern,
        grid=(n_seq, n_heads),
        in_specs=[spec, spec, spec],
        out_specs=[spec],
        out_shape=[jax.ShapeDtypeStruct((m, width), BF16)],
        scratch_shapes=(
            [pltpu.VMEM((seq, HEAD_DIM), F32) for _ in range(3 * len(DILATED))]
            + [pltpu.VMEM((seq // BLOCK, BLOCK, 2 * BLOCK), BF16),
               pltpu.VMEM((BLOCK, 2 * BLOCK), F32)]),
        args=(q, k, v),
        casts=casts,
        step_of=lambda n, h: n * n_heads + h,
        name="attn_prompt",
    )
    return attn, cast_out


def _attn_sample_kernel(q_ref, kn_ref, vn_ref, *refs):
    n_br = len(DILATED)
    kc_refs, vc_refs, o_ref = refs[:n_br], refs[n_br:2 * n_br], refs[2 * n_br]
    q = q_ref[...] * (HEAD_DIM ** -0.5 * LOG2_E)
    t_new = jnp.sum(q * kn_ref[...], axis=-1, keepdims=True)
    t_br = [jnp.sum(kc[...] * q[None], axis=-1, keepdims=True) for kc in kc_refs]
    m = t_new
    for t in t_br:
        m = jnp.maximum(m, jnp.max(t, axis=0))
    p_new = jnp.exp2(t_new - m) * float(n_br)
    den = p_new
    num = p_new * vn_ref[...]
    for t, vc in zip(t_br, vc_refs):
        p = jnp.exp2(t - m[None])
        den = den + jnp.sum(p, axis=0)
        num = num + jnp.sum(p * vc[...], axis=0)
    o_ref[...] = num / den


def _attn_sample(q, k_new, v_new, k_cache, v_cache):
    n, heads, _ = q.shape
    win = k_cache.shape[1]
    row = pl.BlockSpec((None, heads, HEAD_DIM), lambda i: (i, 0, 0))
    cache_specs, k_views, v_views = [], [], []
    for w, d in DILATED:
        assert w == BLOCK * d and win % w == 0
        last = win // w - 1
        cache_specs.append(pl.BlockSpec((None, BLOCK, None, heads, HEAD_DIM),
                                        lambda i, last=last: (i, last, 0, 0, 0)))
        k_views.append(k_cache.reshape(n, win // d, d, heads, HEAD_DIM))
        v_views.append(v_cache.reshape(n, win // d, d, heads, HEAD_DIM))
    return pl.pallas_call(
        _attn_sample_kernel,
        grid=(n,),
        in_specs=[row, row, row] + cache_specs + cache_specs,
        out_specs=row,
        out_shape=jax.ShapeDtypeStruct((n, heads, HEAD_DIM), F32),
        compiler_params=_params(1),
        name="attn_sample",
    )(q, k_new, v_new, *k_views, *v_views)


def _out_proj_kernel(pool_ref, attn_ref, x_ref, wo_ref, gpost_ref, gpre_ref, wq_ref, x1_ref, qm_ref,
                     *, tm, qm_layout):
    y = _dot(pool_ref[...].astype(BF16), wo_ref[0:POOL_WIDTH, :])
    y = y + _dot(attn_ref[...].astype(BF16), wo_ref[POOL_WIDTH:, :])
    x1 = x_ref[...] + _rmsnorm(y, gpost_ref[...])
    x1_ref[...] = x1
    qm = _dot(_rmsnorm(x1, gpre_ref[...]).astype(BF16), wq_ref[...])
    for h in range(MEM_HEADS):
        _store_heads(qm_ref, qm_layout, h, MEM_HEADS, tm, qm[:, h * HEAD_DIM:(h + 1) * HEAD_DIM])


def _out_proj(pool_out, attn, x, w_out, g_post, g_pre, w_xq, *, tm, qm_layout, name):
    m, d = x.shape
    half = pool_out.shape[1]
    qm_block = _out_block(qm_layout, tm, MEM_HEADS)
    kern = functools.partial(_out_proj_kernel, tm=tm, qm_layout=qm_layout)
    return pl.pallas_call(
        kern,
        grid=(m // tm,),
        in_specs=[
            pl.BlockSpec((tm, half), lambda i: (i, 0)),
            pl.BlockSpec((tm, half), lambda i: (i, 0)),
            pl.BlockSpec((tm, d), lambda i: (i, 0)),
            _resident(w_out.shape, lambda i: (0, 0)),
            _resident((1, d), lambda i: (0, 0)),
            _resident((1, d), lambda i: (0, 0)),
            _resident(w_xq.shape, lambda i: (0, 0)),
        ],
        out_specs=[pl.BlockSpec((tm, d), lambda i: (i, 0)),
                   pl.BlockSpec(qm_block, lambda i: (i, 0))],
        out_shape=[jax.ShapeDtypeStruct((m, d), F32),
                   jax.ShapeDtypeStruct((m // tm * qm_block[0], qm_block[1]), F32)],
        compiler_params=_params(1),
        name=name,
    )(pool_out, attn, x, w_out, g_post, g_pre, w_xq)


def _mix_mem_prompt_kernel(pool_ref, attn_ref, x_ref, mk_ref, mv_ref, wo_ref, gmix_ref, gpre_ref, wq_ref,
                           wxo_ref, gmem_ref, x2_ref, *, tm, row_split):
    scale = HEAD_DIM ** -0.5
    for part in range(row_split):
        rows = slice(part * (tm // row_split), (part + 1) * (tm // row_split))
        mixed = jnp.concatenate([pool_ref[rows, :], attn_ref[rows, :]], axis=-1)
        x1 = x_ref[rows, :] + _rmsnorm(_dot(mixed, wo_ref[...]), gmix_ref[...])
        qm = _dot(_rmsnorm(x1, gpre_ref[...]).astype(BF16), wq_ref[...]).astype(BF16)
        heads = []
        for h in range(MEM_HEADS):
            cols = slice(h * HEAD_DIM, (h + 1) * HEAD_DIM)
            s = _dot_nt(qm[:, cols], mk_ref[:, cols].astype(BF16)) * scale
            p = jnp.exp(s - jnp.max(s, axis=-1, keepdims=True))
            o = _dot(p.astype(BF16), mv_ref[:, cols].astype(BF16))
            heads.append((o / jnp.sum(p, axis=-1, keepdims=True)).astype(BF16))
        y = _dot(jnp.concatenate(heads, axis=-1), wxo_ref[...])
        x2_ref[rows, :] = x1 + _rmsnorm(y, gmem_ref[...])


def _mix_mem_prompt(pool_out, attn, x, mk, mv, w_out, g_mix_post, g_mem_pre, w_xq, w_xo, g_mem_post, *, tm, seq):
    m, d = x.shape
    tiles_per_seq = seq // tm
    half = pool_out.shape[1]
    mem_spec = pl.BlockSpec((N_MEM, MEM_WIDTH), lambda i: (i // tiles_per_seq, 0))
    gain_spec = _resident((1, d), lambda i: (0, 0))
    return pl.pallas_call(
        functools.partial(_mix_mem_prompt_kernel, tm=tm, row_split=1),
        grid=(m // tm,),
        in_specs=[
            pl.BlockSpec((tm, half), lambda i: (i, 0)),
            pl.BlockSpec((tm, half), lambda i: (i, 0)),
            pl.BlockSpec((tm, d), lambda i: (i, 0)),
            mem_spec,
            mem_spec,
            _resident(w_out.shape, lambda i: (0, 0)),
            gain_spec,
            gain_spec,
            _resident(w_xq.shape, lambda i: (0, 0)),
            _resident(w_xo.shape, lambda i: (0, 0)),
            gain_spec,
        ],
        out_specs=pl.BlockSpec((tm, d), lambda i: (i, 0)),
        out_shape=jax.ShapeDtypeStruct((m, d), F32),
        compiler_params=_params(1),
        name="mix_mem_prompt",
    )(pool_out, attn, x, mk, mv, w_out, g_mix_post, g_mem_pre, w_xq, w_xo, g_mem_post)


def _mem_attn_sample_kernel(qm_ref, mk_ref, mv_ref, o_ref):
    heads = qm_ref.shape[0]
    both = lambda a: jnp.concatenate([a, a], axis=0)
    fold = lambda a: a[0:heads] + a[heads:]
    q = both(qm_ref[...] * (HEAD_DIM ** -0.5 * LOG2_E))
    t = jnp.sum(mk_ref[...] * q[None], axis=-1, keepdims=True)
    m = jnp.max(t, axis=0)
    m = both(jnp.maximum(m[0:heads], m[heads:]))
    p = jnp.exp2(t - m[None])
    o_ref[...] = fold(jnp.sum(p * mv_ref[...], axis=0)) / fold(jnp.sum(p, axis=0))


def _mem_attn_sample(qm, mem_k, mem_v):
    n, heads, _ = qm.shape
    row = pl.BlockSpec((None, heads, HEAD_DIM), lambda i: (i, 0, 0))
    mem = pl.BlockSpec((None, N_MEM // 2, 2 * heads, HEAD_DIM), lambda i: (i, 0, 0, 0))
    pairs = lambda a: a.reshape(n, N_MEM // 2, 2 * heads, HEAD_DIM)
    return pl.pallas_call(
        _mem_attn_sample_kernel,
        grid=(n,),
        in_specs=[row, mem, mem],
        out_specs=row,
        out_shape=jax.ShapeDtypeStruct((n, heads, HEAD_DIM), F32),
        compiler_params=_params(1),
        name="mem_attn_sample",
    )(qm, pairs(mem_k), pairs(mem_v))


def _proj_norm_residual_kernel(a_ref, w_ref, g_ref, r_ref, o_ref):
    y = _dot(a_ref[...].astype(BF16), w_ref[...])
    o_ref[...] = r_ref[...] + _rmsnorm(y, g_ref[...])


def _proj_norm_residual(a, w, g, resid, *, name):
    return pl.pallas_call(
        _proj_norm_residual_kernel,
        out_shape=jax.ShapeDtypeStruct(resid.shape, F32),
        compiler_params=pltpu.CompilerParams(vmem_limit_bytes=VMEM_LIMIT_BYTES),
        name=name,
    )(a, w, g, resid)


def _ffn_kernel(x_ref, gpre_ref, w1_ref, w2_ref, gpost_ref, o_ref, xn_ref, acc_ref):
    f = pl.program_id(1)

    @pl.when(f == 0)
    def _():
        xn_ref[...] = _rmsnorm(x_ref[...], gpre_ref[...]).astype(BF16)
        acc_ref[...] = jnp.zeros_like(acc_ref)

    hidden = jnp.square(jnp.maximum(_dot(xn_ref[...], w1_ref[...]), 0.0)).astype(BF16)
    acc_ref[...] += _dot(hidden, w2_ref[...])

    @pl.when(f == pl.num_programs(1) - 1)
    def _():
        o_ref[...] = x_ref[...] + _rmsnorm(acc_ref[...], gpost_ref[...])


def _ffn(x, g_pre, w1, w2, g_post, *, tm, tf, name):
    m, d = x.shape
    dff = w1.shape[1]
    return pl.pallas_call(
        _ffn_kernel,
        grid=(m // tm, dff // tf),
        in_specs=[
            pl.BlockSpec((tm, d), lambda i, f: (i, 0)),
            _resident((1, d), lambda i, f: (0, 0)),
            pl.BlockSpec((d, tf), lambda i, f: (0, f)),
            pl.BlockSpec((tf, d), lambda i, f: (f, 0)),
            _resident((1, d), lambda i, f: (0, 0)),
        ],
        out_specs=pl.BlockSpec((tm, d), lambda i, f: (i, 0)),
        out_shape=jax.ShapeDtypeStruct((m, d), F32),
        scratch_shapes=[pltpu.VMEM((tm, d), BF16), pltpu.VMEM((tm, d), F32)],
        compiler_params=_params(2),
        name=name,
    )(x, g_pre, w1, w2, g_post)


def kernel(x_prompt, x_sample, state_pool, cache_attn_k, cache_attn_v, cache_mem_k, cache_mem_v, mem_prompt,
           g_mix_pre, g_mix_post, g_mem_pre, g_mem_post, g_ffn_pre, g_ffn_post, g_mem_kv,
           w_in, w_pool, pool_scale, w_out, w_xq, w_mem_kv, w_xo, w_ff1, w_ff2):
    depth = w_in.shape[0]
    assert depth == 1
    batch, seq, d = x_prompt.shape
    dec_batch, dec_seq, _ = x_sample.shape
    assert dec_seq == 1

    l = 0
    scale = pool_scale[l][None, :]
    gain = lambda g: g[l][None, :]

    tm = 512
    xp = x_prompt.reshape(batch * seq, d)
    xs = x_sample.reshape(dec_batch, d)
    cos_p, sin_p = _rope_tables(jnp.arange(seq, dtype=jnp.int32))
    cos_s, sin_s = _rope_tables(jnp.full((dec_batch,), PAST_LEN, jnp.int32))
    tm_mem = 128
    ones, zeros = jnp.ones((tm_mem, HEAD_DIM), F32), jnp.zeros((tm_mem, HEAD_DIM), F32)
    w_pool_f = w_pool[l]


    mem = mem_prompt.reshape(batch * N_MEM, d)
    (mk, mv, mk_h, mv_h), (w_in_b,) = _norm_proj(
        mem, gain(g_mem_kv), w_mem_kv[l], ones, zeros, tm=tm_mem, ncol=MEM_WIDTH, rope_blocks=(),
        outputs=((0, FLAT), (1, FLAT), (0, HEADS), (1, HEADS)), name="mem_kv", casts=(w_in[l],))

    (pool_out, q, k, v, k_h, v_h, pool_state), (w_ff1_b, w_out_b, w_xq_b, w_xo_b) = _in_proj_pool(
        xp, gain(g_mix_pre), w_in_b, cos_p, sin_p, w_pool_f, scale, tm=256, seq=seq,
        casts=(w_ff1[l], w_out[l], w_xq[l], w_xo[l]))
    attn, (w_ff2_b,) = _attn_prompt(q, k, v, seq=seq, casts=(w_ff2[l],))
    x2 = _mix_mem_prompt(pool_out, attn, xp, mk, mv, w_out_b, gain(g_mix_post), gain(g_mem_pre), w_xq_b,
                         w_xo_b, gain(g_mem_post), tm=tm, seq=seq)
    yp = _ffn(x2, gain(g_ffn_pre), w_ff1_b, w_ff2_b, gain(g_ffn_post), tm=tm, tf=1024, name="ffn_prompt")

    (us, qs_h, ks_h, vs_h), _ = _norm_proj(
        xs, gain(g_mix_pre), w_in_b, cos_s, sin_s, tm=dec_batch, ncol=POOL_WIDTH, rope_blocks=(1, 2),
        outputs=((0, FLAT), (1, HEADS), (2, HEADS), (3, HEADS)), name="in_proj_sample")
    heads3 = lambda a: a.reshape(dec_batch, -1, HEAD_DIM)
    pool_out_s, new_pool_s = _pool_sample(us, jnp.swapaxes(state_pool[l], 0, 1), w_pool_f, scale, pos=PAST_LEN)
    attn_s = _attn_sample(heads3(qs_h), heads3(ks_h), heads3(vs_h), cache_attn_k[l], cache_attn_v[l])
    x1s, qms_h = _out_proj(pool_out_s, attn_s.reshape(dec_batch, ATTN_WIDTH), xs, w_out_b, gain(g_mix_post),
                           gain(g_mem_pre), w_xq_b, tm=dec_batch, qm_layout=HEADS, name="out_proj_sample")
    mem_o_s = _mem_attn_sample(heads3(qms_h), cache_mem_k[l], cache_mem_v[l])
    x2s = _proj_norm_residual(mem_o_s.reshape(dec_batch, MEM_WIDTH), w_xo_b, gain(g_mem_post), x1s,
                              name="mem_out_sample")
    ys = _ffn(x2s, gain(g_ffn_pre), w_ff1_b, w_ff2_b, gain(g_ffn_post), tm=dec_batch, tf=2048,
              name="ffn_sample")

    keep = min(max(w for w, _ in DILATED), seq)
    return (
        yp.reshape(batch, seq, d),
        ys.reshape(dec_batch, 1, d),
        pool_state.reshape(batch, STATE_ROWS, POOL_WIDTH)[:, STATE_ROWS - POOL_STATE:][None],
        jnp.swapaxes(new_pool_s, 0, 1)[None],
        k_h.reshape(batch, seq, N_HEADS, HEAD_DIM)[:, seq - keep:][None],
        v_h.reshape(batch, seq, N_HEADS, HEAD_DIM)[:, seq - keep:][None],
        ks_h.reshape(dec_batch, 1, N_HEADS, HEAD_DIM)[None],
        vs_h.reshape(dec_batch, 1, N_HEADS, HEAD_DIM)[None],
        mk_h.reshape(batch, N_MEM, MEM_HEADS, HEAD_DIM)[None],
        mv_h.reshape(batch, N_MEM, MEM_HEADS, HEAD_DIM)[None],
    )
```

```python
import functools

import jax
import jax.numpy as jnp
from jax import lax
from jax.experimental import pallas as pl
from jax.experimental.pallas import tpu as pltpu

D_MODEL = 2048
POOL_WIDTH = 1024
POOL_WINDOWS = (2, 4, 8, 16)
POOL_GROUP = POOL_WIDTH // len(POOL_WINDOWS)
POOL_STATE = max(POOL_WINDOWS) - 1
HEAD_DIM = 128
ATTN_WIDTH = 1024
N_HEADS = ATTN_WIDTH // HEAD_DIM
DILATED = ((128, 1), (512, 4), (2048, 16))
ROPE_DIM = HEAD_DIM // 4
ROPE_HALF = ROPE_DIM // 2
ROPE_THETA = 500000.0
N_MEM = 256
MEM_HEADS = 4
MEM_WIDTH = MEM_HEADS * HEAD_DIM
EPS = 1e-6
BLOCK = 128
NEG_INF = -1e30
LOG2_E = 1.4426950408889634
PAST_LEN = 8192
HALO = 32
STATE_ROWS = 16
SAMPLE_GROUP = 4

F32 = jnp.float32
BF16 = jnp.bfloat16

VMEM_LIMIT_BYTES = 56 * 1024 * 1024

FLAT, HEADS = "flat", "heads"


def _params(n_grid_axes):
    return pltpu.CompilerParams(
        dimension_semantics=("arbitrary",) * n_grid_axes,
        vmem_limit_bytes=VMEM_LIMIT_BYTES,
    )


def _resident(block_shape, index_map):
    return pl.BlockSpec(block_shape, index_map, pipeline_mode=pl.Buffered(1))


def _rmsnorm(x, g):
    ms = jnp.mean(x * x, axis=-1, keepdims=True)
    return x * lax.rsqrt(ms + EPS) * g


def _dot(a, b):
    return jnp.dot(a, b, preferred_element_type=F32)


def _dot_nt(a, b):
    return lax.dot_general(a, b, (((1,), (1,)), ((), ())), preferred_element_type=F32)


def _call_with_casts(kernel_fn, *, grid, in_specs, out_specs, out_shape, scratch_shapes=(), args, casts=(),
                     step_of=None, name):
    n_in, n_out, n_casts = len(in_specs), len(out_specs), len(casts)
    n_steps = 1
    for g in grid:
        n_steps *= g
    if step_of is None:
        step_of = lambda i: i
    cast_specs = []
    for a in casts:
        rows, cols = a.shape
        assert rows % n_steps == 0
        cast_specs.append(pl.BlockSpec((rows // n_steps, cols), lambda *idx: (step_of(*idx), 0)))

    def body(*refs):
        ins, cast_in = refs[:n_in], refs[n_in:n_in + n_casts]
        outs = refs[n_in + n_casts:n_in + n_casts + n_out]
        cast_out = refs[n_in + n_casts + n_out:n_in + 2 * n_casts + n_out]
        for i_ref, o_ref in zip(cast_in, cast_out):
            o_ref[...] = i_ref[...].astype(BF16)
        kernel_fn(*ins, *outs, *refs[n_in + 2 * n_casts + n_out:])

    res = pl.pallas_call(
        body,
        grid=grid,
        in_specs=list(in_specs) + cast_specs,
        out_specs=list(out_specs) + cast_specs,
        out_shape=list(out_shape) + [jax.ShapeDtypeStruct(a.shape, BF16) for a in casts],
        scratch_shapes=list(scratch_shapes),
        compiler_params=_params(len(grid)),
        name=name,
    )(*args, *casts)
    return res[:n_out], res[n_out:]


def _store_heads(o_ref, layout, h, heads, rows, value):
    if layout == FLAT:
        o_ref[:, h * HEAD_DIM:(h + 1) * HEAD_DIM] = value
    else:
        o_ref[pl.ds(h, rows, stride=heads), :] = value


def _out_block(layout, tm, heads):
    return (tm, heads * HEAD_DIM) if layout == FLAT else (tm * heads, HEAD_DIM)


def _norm_proj_kernel(x_ref, g_ref, w_ref, cos_ref, sin_ref, *out_refs, tm, ncol, rope_blocks, outputs):
    heads = ncol // HEAD_DIM
    xn = _rmsnorm(x_ref[...], g_ref[...]).astype(BF16)
    for c in sorted({c for c, _ in outputs}):
        acc = _dot(xn, w_ref[:, c * ncol:(c + 1) * ncol].astype(BF16))
        if c in rope_blocks:
            cos = cos_ref[...]
            sin = sin_ref[...]
            lane = lax.broadcasted_iota(jnp.int32, cos.shape, 1)
        for h in range(heads):
            a = acc[:, h * HEAD_DIM:(h + 1) * HEAD_DIM]
            if c in rope_blocks:
                partner = jnp.where(lane < ROPE_HALF,
                                    pltpu.roll(a, HEAD_DIM - ROPE_HALF, 1),
                                    pltpu.roll(a, ROPE_HALF, 1))
                a = a * cos + partner * sin
            for (oc, layout), o_ref in zip(outputs, out_refs):
                if oc == c:
                    _store_heads(o_ref, layout, h, heads, tm, a)


def _norm_proj(x, g, w, cos, sin, *, tm, ncol, rope_blocks, outputs, name, casts=()):
    m, d = x.shape
    heads = ncol // HEAD_DIM
    assert w.shape[0] == d and w.shape[1] % ncol == 0 and m % tm == 0
    table_tiles = cos.shape[0] // tm
    kern = functools.partial(_norm_proj_kernel, tm=tm, ncol=ncol, rope_blocks=rope_blocks, outputs=outputs)
    blocks = [_out_block(layout, tm, heads) for _, layout in outputs]
    return _call_with_casts(
        kern,
        grid=(m // tm,),
        in_specs=[
            pl.BlockSpec((tm, d), lambda i: (i, 0)),
            _resident((1, d), lambda i: (0, 0)),
            _resident(w.shape, lambda i: (0, 0)),
            pl.BlockSpec((tm, HEAD_DIM), lambda i: (i % table_tiles, 0)),
            pl.BlockSpec((tm, HEAD_DIM), lambda i: (i % table_tiles, 0)),
        ],
        out_specs=[pl.BlockSpec(b, lambda i: (i, 0)) for b in blocks],
        out_shape=[jax.ShapeDtypeStruct((m // tm * b[0], b[1]), F32) for b in blocks],
        args=(x, g, w, cos, sin),
        casts=casts,
        name=name,
    )


def _rope_tables(pos):
    inv = jnp.power(jnp.float32(ROPE_THETA), -jnp.arange(ROPE_HALF, dtype=F32) * 2.0 / ROPE_DIM)
    ang = pos.astype(F32)[:, None] * inv[None, :]
    cos, sin = jnp.cos(ang), jnp.sin(ang)
    n = pos.shape[0]
    pad = HEAD_DIM - ROPE_DIM
    cos_t = jnp.concatenate([cos, cos, jnp.ones((n, pad), F32)], axis=1)
    sin_t = jnp.concatenate([-sin, sin, jnp.zeros((n, pad), F32)], axis=1)
    return cos_t, sin_t


def _window_sums(ext_ref, lvl_refs, g, w, rows):
    cols = slice(g * POOL_GROUP, (g + 1) * POOL_GROUP)
    levels = w.bit_length() - 1
    src, src_cols = ext_ref, cols
    for level in range(levels):
        shift = 1 << level
        last = level == levels - 1
        lo = HALO if last else 8 * (level + 1)
        assert lo - shift >= 8 * level
        total = src[lo:rows, src_cols] + src[lo - shift:rows - shift, src_cols]
        if last:
            return total
        lvl_refs[level % 2][lo:rows, :] = total
        src, src_cols = lvl_refs[level % 2], slice(None)


def _rope(a, cos, sin):
    lane = lax.broadcasted_iota(jnp.int32, a.shape, 1)
    partner = jnp.where(lane < ROPE_HALF, pltpu.roll(a, HEAD_DIM - ROPE_HALF, 1), pltpu.roll(a, ROPE_HALF, 1))
    return a * cos + partner * sin


def _in_proj_pool_kernel(x_ref, xnext_ref, xs_ref, g_ref, w_ref, cos_ref, sin_ref, coss_ref, sins_ref,
                         wp_ref, scale_ref,
                         pool_ref, q_ref, k_ref, v_ref, kh_ref, vh_ref, state_ref, us_ref, qsh_ref, ksh_ref, vsh_ref,
                         xn_ref, ext_ref, lvl_a_ref, lvl_b_ref, *, tm, tiles_per_seq):
    step = pl.program_id(0)
    t_in_seq = step % tiles_per_seq
    n_sample = xs_ref.shape[0]

    def tile(xn, with_sample):
        u = _dot(xn, w_ref[:, 0:POOL_WIDTH])
        ext_ref[HALO:HALO + tm, :] = u[0:tm]
        if with_sample:
            us_ref[...] = u[tm:]
        cos, sin = cos_ref[...], sin_ref[...]
        outs = ((q_ref, None, qsh_ref), (k_ref, kh_ref, ksh_ref), (v_ref, vh_ref, vsh_ref))
        for c, (flat_ref, heads_ref, sample_ref) in enumerate(outs, start=1):
            acc = _dot(xn, w_ref[:, c * ATTN_WIDTH:(c + 1) * ATTN_WIDTH])
            for h in range(N_HEADS):
                a = acc[0:tm, h * HEAD_DIM:(h + 1) * HEAD_DIM]
                if flat_ref is not v_ref:
                    a = _rope(a, cos, sin)
                _store_heads(flat_ref, FLAT, h, N_HEADS, tm, a)
                if heads_ref is not None:
                    _store_heads(heads_ref, HEADS, h, N_HEADS, tm, a)
                if with_sample:
                    a = acc[tm:, h * HEAD_DIM:(h + 1) * HEAD_DIM]
                    if flat_ref is not v_ref:
                        a = _rope(a, coss_ref[...], sins_ref[...])
                    _store_heads(sample_ref, HEADS, h, N_HEADS, n_sample, a)

        pos = t_in_seq * tm + lax.broadcasted_iota(jnp.int32, (tm, 1), 0)
        for g, w in enumerate(POOL_WINDOWS):
            cols = slice(g * POOL_GROUP, (g + 1) * POOL_GROUP)
            wsum = _window_sums(ext_ref, (lvl_a_ref, lvl_b_ref), g, w, HALO + tm)
            cnt = jnp.minimum(w, pos + 1).astype(F32)
            pooled = (wsum / cnt - ext_ref[HALO:HALO + tm, cols]).astype(BF16)
            pool_ref[:, cols] = (_dot(pooled, wp_ref[g].astype(BF16)) * scale_ref[:, cols]).astype(pool_ref.dtype)
        state_ref[...] = ext_ref[HALO + tm - STATE_ROWS:HALO + tm, :]
        ext_ref[0:HALO, :] = ext_ref[tm:tm + HALO, :]

        xn_ref[(step + 1) % 2] = _rmsnorm(xnext_ref[...], g_ref[...]).astype(BF16)

    @pl.when(t_in_seq == 0)
    def _():
        ext_ref[0:HALO, :] = jnp.zeros((HALO, POOL_WIDTH), F32)

    @pl.when(step == 0)
    def _():
        rows = jnp.concatenate([x_ref[...], xs_ref[...]], axis=0)
        tile(_rmsnorm(rows, g_ref[...]).astype(BF16), True)

    @pl.when(step > 0)
    def _():
        tile(xn_ref[step % 2], False)


def _in_proj_pool(x, xs, g, w, cos, sin, cos_s, sin_s, w_pool, pool_scale, *, tm, seq, casts=()):
    m, d = x.shape
    ms = xs.shape[0]
    tiles_per_seq = seq // tm
    n_seq = m // seq
    assert POOL_WIDTH == ATTN_WIDTH and w.shape == (d, POOL_WIDTH + 3 * ATTN_WIDTH)
    kern = functools.partial(_in_proj_pool_kernel, tm=tm, tiles_per_seq=tiles_per_seq)
    row_tile = lambda width: pl.BlockSpec((tm, width), lambda i: (i, 0))
    heads_tile = pl.BlockSpec((tm * N_HEADS, HEAD_DIM), lambda i: (i, 0))
    table = pl.BlockSpec((tm, HEAD_DIM), lambda i: (i % tiles_per_seq, 0))
    whole = lambda shape: _resident(shape, lambda i: (0,) * len(shape))
    flat = jax.ShapeDtypeStruct((m, ATTN_WIDTH), F32)
    by_heads = jax.ShapeDtypeStruct((m * N_HEADS, HEAD_DIM), F32)
    sample_heads = jax.ShapeDtypeStruct((ms * N_HEADS, HEAD_DIM), F32)
    sample_heads_spec = pl.BlockSpec((ms * N_HEADS, HEAD_DIM), lambda i: (0, 0))
    return _call_with_casts(
        kern,
        grid=(m // tm,),
        in_specs=[
            row_tile(d),
            pl.BlockSpec((tm, d), lambda i: (jnp.minimum(i + 1, m // tm - 1), 0)),
            whole((ms, d)),
            whole((1, d)),
            whole(w.shape),
            table,
            table,
            whole((ms, HEAD_DIM)),
            whole((ms, HEAD_DIM)),
            whole(w_pool.shape),
            whole((1, POOL_WIDTH)),
        ],
        out_specs=[row_tile(POOL_WIDTH), row_tile(ATTN_WIDTH), row_tile(ATTN_WIDTH), row_tile(ATTN_WIDTH),
                   heads_tile, heads_tile,
                   pl.BlockSpec((STATE_ROWS, POOL_WIDTH), lambda i: (i // tiles_per_seq, 0)),
                   pl.BlockSpec((ms, POOL_WIDTH), lambda i: (0, 0)),
                   sample_heads_spec, sample_heads_spec, sample_heads_spec],
        out_shape=[jax.ShapeDtypeStruct((m, POOL_WIDTH), BF16), flat, flat, flat, by_heads, by_heads,
                   jax.ShapeDtypeStruct((n_seq * STATE_ROWS, POOL_WIDTH), F32),
                   jax.ShapeDtypeStruct((ms, POOL_WIDTH), F32), sample_heads, sample_heads, sample_heads],
        scratch_shapes=[pltpu.VMEM((2, tm, d), BF16),
                        pltpu.VMEM((HALO + tm, POOL_WIDTH), F32),
                        pltpu.VMEM((HALO + tm, POOL_GROUP), F32),
                        pltpu.VMEM((HALO + tm, POOL_GROUP), F32)],
        args=(x, x, xs, g, w, cos, sin, cos_s, sin_s, w_pool, pool_scale),
        casts=casts,
        name="in_proj_pool",
    )


def _pool_sample_kernel(u_ref, prev_ref, wp_ref, scale_ref, o_ref, state_ref, *, pos):
    for j in range(POOL_STATE - 1):
        state_ref[:, j, :] = prev_ref[:, j + 1, :]
    state_ref[:, POOL_STATE - 1, :] = u_ref[...]
    for g, w in enumerate(POOL_WINDOWS):
        cols = slice(g * POOL_GROUP, (g + 1) * POOL_GROUP)
        wsum = u_ref[:, cols]
        for back in range(1, w):
            wsum = wsum + prev_ref[:, POOL_STATE - back, cols]
        cnt = float(min(w, pos + 1))
        pooled = (wsum / cnt - u_ref[:, cols]).astype(BF16)
        o_ref[:, cols] = (_dot(pooled, wp_ref[g].astype(BF16)) * scale_ref[:, cols]).astype(o_ref.dtype)


def _pool_sample(u, prev, w_pool, pool_scale, *, pos):
    n, c = u.shape
    kern = functools.partial(_pool_sample_kernel, pos=pos)
    return pl.pallas_call(
        kern,
        out_shape=[jax.ShapeDtypeStruct((n, c), BF16), jax.ShapeDtypeStruct(prev.shape, F32)],
        compiler_params=pltpu.CompilerParams(vmem_limit_bytes=VMEM_LIMIT_BYTES),
        name="pool_sample",
    )(u, prev, w_pool, pool_scale)


def _run_blocks(count, unroll, body):
    trips = count // unroll
    if trips > 1:
        def trip(t, carry):
            for j in range(unroll):
                body(t * unroll + j)
            return carry
        lax.fori_loop(0, trips, trip, 0)
        done = trips * unroll
    else:
        done = 0
    for idx in range(done, count):
        body(idx)


def _attn_prompt_kernel(q_ref, k_ref, v_ref, o_ref, *scratch, seq, unroll):
    n_br = len(DILATED)
    ob_refs, mb_refs, sb_refs = scratch[:n_br], scratch[n_br:2 * n_br], scratch[2 * n_br:3 * n_br]
    p_ref, bias_ref = scratch[3 * n_br:]
    exp2_scale = HEAD_DIM ** -0.5 * LOG2_E
    qi = lax.broadcasted_iota(jnp.int32, (BLOCK, 2 * BLOCK), 0)
    kj = lax.broadcasted_iota(jnp.int32, (BLOCK, 2 * BLOCK), 1)
    in_band = ((kj < BLOCK) & (kj >= qi)) | ((kj >= BLOCK) & (kj - BLOCK <= qi))
    bias_ref[...] = jnp.where(in_band, 0.0, NEG_INF)
    ones = jnp.ones((2 * BLOCK, HEAD_DIM), BF16)

    def rows(block, r, d):
        if d == 1:
            start = block * BLOCK
            return pl.ds(start if isinstance(start, int) else pl.multiple_of(start, BLOCK), BLOCK)
        return pl.ds(block * (BLOCK * d) + r, BLOCK, stride=d)

    def keys(ref, block, r, d, with_prev):
        cur = ref[rows(block, r, d), :]
        if not with_prev:
            return cur.astype(BF16)
        return jnp.concatenate([ref[rows(block - 1, r, d), :], cur], axis=0).astype(BF16)

    for g, (_, d) in enumerate(DILATED):
        n_blocks = seq // d // BLOCK
        for with_prev in (False, True):
            per_r = n_blocks - 1 if with_prev else 1
            count = d * per_r
            if count == 0:
                continue
            cols = slice(0, 2 * BLOCK) if with_prev else slice(BLOCK, 2 * BLOCK)
            width = 2 * BLOCK if with_prev else BLOCK

            def locate(idx, per_r=per_r, with_prev=with_prev):
                return (idx % per_r + 1, idx // per_r) if with_prev else (0, idx)

            def probabilities(idx, g=g, d=d, with_prev=with_prev, cols=cols, width=width, locate=locate):
                b, r = locate(idx)
                q = q_ref[rows(b, r, d), :].astype(BF16)
                s = _dot_nt(q, keys(k_ref, b, r, d, with_prev)) + bias_ref[:, cols]
                m = jnp.max(s, axis=-1, keepdims=True)
                p_ref[idx, :, 0:width] = jnp.exp2((s - m) * exp2_scale).astype(BF16)
                mb_refs[g][rows(b, r, d), :] = jnp.broadcast_to(m, (BLOCK, HEAD_DIM))

            def values(idx, g=g, d=d, with_prev=with_prev, width=width, locate=locate):
                b, r = locate(idx)
                v1 = jnp.concatenate([keys(v_ref, b, r, d, with_prev), ones[0:width]], axis=1)
                acc = _dot(p_ref[idx, :, 0:width], v1)
                ob_refs[g][rows(b, r, d), :] = acc[:, 0:HEAD_DIM]
                sb_refs[g][rows(b, r, d), :] = acc[:, HEAD_DIM:]

            _run_blocks(count, unroll, probabilities)
            _run_blocks(count, unroll, values)

    chunk = 256

    def combine(c, carry):
        rw = pl.ds(pl.multiple_of(c * chunk, chunk), chunk)
        ms = [mb[rw, :] for mb in mb_refs]
        m = functools.reduce(jnp.maximum, ms)
        ws = [jnp.exp2((mg - m) * exp2_scale) for mg in ms]
        num = functools.reduce(jnp.add, [w * ob[rw, :] for w, ob in zip(ws, ob_refs)])
        den = functools.reduce(jnp.add, [w * sb[rw, :] for w, sb in zip(ws, sb_refs)])
        o_ref[rw, :] = (num / den).astype(o_ref.dtype)
        return carry

    lax.fori_loop(0, seq // chunk, combine, 0)


def _attn_prompt(q, k, v, *, seq, casts=()):
    m, width = q.shape
    n_seq = m // seq
    n_heads = width // HEAD_DIM
    spec = pl.BlockSpec((seq, HEAD_DIM), lambda n, h: (n, h))
    kern = functools.partial(_attn_prompt_kernel, seq=seq, unroll=8)
    (attn,), cast_out = _call_with_casts(
        kern,
        grid=(n_seq, n_heads),
        in_specs=[spec, spec, spec],
        out_specs=[spec],
        out_shape=[jax.ShapeDtypeStruct((m, width), BF16)],
        scratch_shapes=(
            [pltpu.VMEM((seq, HEAD_DIM), F32) for _ in range(3 * len(DILATED))]
            + [pltpu.VMEM((seq // BLOCK, BLOCK, 2 * BLOCK), BF16),
               pltpu.VMEM((BLOCK, 2 * BLOCK), F32)]),
        args=(q, k, v),
        casts=casts,
        step_of=lambda n, h: n * n_heads + h,
        name="attn_prompt",
    )
    return attn, cast_out


def _attn_sample_kernel(q_ref, kn_ref, vn_ref, *refs, group):
    n_br = len(DILATED)
    kc_refs, vc_refs, o_ref = refs[:n_br], refs[n_br:2 * n_br], refs[2 * n_br]
    for j in range(group):
        q = q_ref[j] * (HEAD_DIM ** -0.5 * LOG2_E)
        t_new = jnp.sum(q * kn_ref[j], axis=-1, keepdims=True)
        t_br = [jnp.sum(kc[j] * q[None], axis=-1, keepdims=True) for kc in kc_refs]
        m = t_new
        for t in t_br:
            m = jnp.maximum(m, jnp.max(t, axis=0))
        p_new = jnp.exp2(t_new - m) * float(n_br)
        den = p_new
        num = p_new * vn_ref[j]
        for t, vc in zip(t_br, vc_refs):
            p = jnp.exp2(t - m[None])
            den = den + jnp.sum(p, axis=0)
            num = num + jnp.sum(p * vc[j], axis=0)
        o_ref[j] = num / den


def _attn_sample(q, k_new, v_new, k_cache, v_cache):
    n, heads, _ = q.shape
    win = k_cache.shape[1]
    group = SAMPLE_GROUP
    row = pl.BlockSpec((group, heads, HEAD_DIM), lambda i: (i, 0, 0))
    cache_specs, k_views, v_views = [], [], []
    for w, d in DILATED:
        assert w == BLOCK * d and win % w == 0
        last = win // w - 1
        cache_specs.append(pl.BlockSpec((group, BLOCK, None, heads, HEAD_DIM),
                                        lambda i, last=last: (i, last, 0, 0, 0)))
        k_views.append(k_cache.reshape(n, win // d, d, heads, HEAD_DIM))
        v_views.append(v_cache.reshape(n, win // d, d, heads, HEAD_DIM))
    return pl.pallas_call(
        functools.partial(_attn_sample_kernel, group=group),
        grid=(n // group,),
        in_specs=[row, row, row] + cache_specs + cache_specs,
        out_specs=row,
        out_shape=jax.ShapeDtypeStruct((n, heads, HEAD_DIM), F32),
        compiler_params=_params(1),
        name="attn_sample",
    )(q, k_new, v_new, *k_views, *v_views)


def _out_proj_kernel(pool_ref, attn_ref, x_ref, wo_ref, gpost_ref, gpre_ref, wq_ref, x1_ref, qm_ref,
                     *, tm, qm_layout):
    y = _dot(pool_ref[...].astype(BF16), wo_ref[0:POOL_WIDTH, :])
    y = y + _dot(attn_ref[...].astype(BF16), wo_ref[POOL_WIDTH:, :])
    x1 = x_ref[...] + _rmsnorm(y, gpost_ref[...])
    x1_ref[...] = x1
    qm = _dot(_rmsnorm(x1, gpre_ref[...]).astype(BF16), wq_ref[...])
    for h in range(MEM_HEADS):
        _store_heads(qm_ref, qm_layout, h, MEM_HEADS, tm, qm[:, h * HEAD_DIM:(h + 1) * HEAD_DIM])


def _out_proj(pool_out, attn, x, w_out, g_post, g_pre, w_xq, *, tm, qm_layout, name):
    m, d = x.shape
    half = pool_out.shape[1]
    qm_block = _out_block(qm_layout, tm, MEM_HEADS)
    kern = functools.partial(_out_proj_kernel, tm=tm, qm_layout=qm_layout)
    return pl.pallas_call(
        kern,
        grid=(m // tm,),
        in_specs=[
            pl.BlockSpec((tm, half), lambda i: (i, 0)),
            pl.BlockSpec((tm, half), lambda i: (i, 0)),
            pl.BlockSpec((tm, d), lambda i: (i, 0)),
            _resident(w_out.shape, lambda i: (0, 0)),
            _resident((1, d), lambda i: (0, 0)),
            _resident((1, d), lambda i: (0, 0)),
            _resident(w_xq.shape, lambda i: (0, 0)),
        ],
        out_specs=[pl.BlockSpec((tm, d), lambda i: (i, 0)),
                   pl.BlockSpec(qm_block, lambda i: (i, 0))],
        out_shape=[jax.ShapeDtypeStruct((m, d), F32),
                   jax.ShapeDtypeStruct((m // tm * qm_block[0], qm_block[1]), F32)],
        compiler_params=_params(1),
        name=name,
    )(pool_out, attn, x, w_out, g_post, g_pre, w_xq)


def _mix_mem_prompt_kernel(pool_ref, attn_ref, x_ref, mk_ref, mv_ref, wo_ref,
                           gmix_ref, gpre_ref, wq_ref, wxo_ref, gmem_ref, x2_ref):
    scale = HEAD_DIM ** -0.5
    mixed = jnp.concatenate([pool_ref[...], attn_ref[...]], axis=-1)
    x1 = x_ref[...] + _rmsnorm(_dot(mixed, wo_ref[...]), gmix_ref[...])
    qm = _dot(_rmsnorm(x1, gpre_ref[...]).astype(BF16), wq_ref[...]).astype(BF16)
    heads = []
    for h in range(MEM_HEADS):
        cols = slice(h * HEAD_DIM, (h + 1) * HEAD_DIM)
        s = _dot_nt(qm[:, cols], mk_ref[:, cols].astype(BF16)) * scale
        p = jnp.exp(s - jnp.max(s, axis=-1, keepdims=True))
        o = _dot(p.astype(BF16), mv_ref[:, cols].astype(BF16))
        heads.append((o / jnp.sum(p, axis=-1, keepdims=True)).astype(BF16))
    y = _dot(jnp.concatenate(heads, axis=-1), wxo_ref[...])
    x2_ref[...] = x1 + _rmsnorm(y, gmem_ref[...])


def _mix_mem_prompt(pool_out, attn, x, mk, mv, w_out, g_mix_post, g_mem_pre, w_xq, w_xo, g_mem_post, *, tm, seq):
    m, d = x.shape
    n_tiles = m // tm
    tiles_per_seq = seq // tm
    half = pool_out.shape[1]
    mem_spec = pl.BlockSpec((N_MEM, MEM_WIDTH), lambda i: (i // tiles_per_seq, 0))
    gain_spec = _resident((1, d), lambda i: (0, 0))
    cur = pl.BlockSpec((tm, half), lambda i: (i, 0))
    return pl.pallas_call(
        _mix_mem_prompt_kernel,
        grid=(n_tiles,),
        in_specs=[
            cur,
            cur,
            pl.BlockSpec((tm, d), lambda i: (i, 0)),
            mem_spec,
            mem_spec,
            _resident(w_out.shape, lambda i: (0, 0)),
            gain_spec,
            gain_spec,
            _resident(w_xq.shape, lambda i: (0, 0)),
            _resident(w_xo.shape, lambda i: (0, 0)),
            gain_spec,
        ],
        out_specs=pl.BlockSpec((tm, d), lambda i: (i, 0)),
        out_shape=jax.ShapeDtypeStruct((m, d), F32),
        compiler_params=_params(1),
        name="mix_mem_prompt",
    )(pool_out, attn, x, mk, mv, w_out, g_mix_post, g_mem_pre, w_xq, w_xo, g_mem_post)


def _mem_attn_sample_kernel(qm_ref, mk_ref, mv_ref, o_ref, *, group):
    heads = qm_ref.shape[1]
    both = lambda a: jnp.concatenate([a, a], axis=0)
    fold = lambda a: a[0:heads] + a[heads:]
    for j in range(group):
        q = both(qm_ref[j] * (HEAD_DIM ** -0.5 * LOG2_E))
        t = jnp.sum(mk_ref[j] * q[None], axis=-1, keepdims=True)
        m = jnp.max(t, axis=0)
        m = both(jnp.maximum(m[0:heads], m[heads:]))
        p = jnp.exp2(t - m[None])
        o_ref[j] = fold(jnp.sum(p * mv_ref[j], axis=0)) / fold(jnp.sum(p, axis=0))


def _mem_attn_sample(qm, mem_k, mem_v):
    n, heads, _ = qm.shape
    group = SAMPLE_GROUP
    row = pl.BlockSpec((group, heads, HEAD_DIM), lambda i: (i, 0, 0))
    mem = pl.BlockSpec((group, N_MEM // 2, 2 * heads, HEAD_DIM), lambda i: (i, 0, 0, 0))
    pairs = lambda a: a.reshape(n, N_MEM // 2, 2 * heads, HEAD_DIM)
    return pl.pallas_call(
        functools.partial(_mem_attn_sample_kernel, group=group),
        grid=(n // group,),
        in_specs=[row, mem, mem],
        out_specs=row,
        out_shape=jax.ShapeDtypeStruct((n, heads, HEAD_DIM), F32),
        compiler_params=_params(1),
        name="mem_attn_sample",
    )(qm, pairs(mem_k), pairs(mem_v))


def _proj_norm_residual_kernel(a_ref, w_ref, g_ref, r_ref, o_ref):
    y = _dot(a_ref[...].astype(BF16), w_ref[...])
    o_ref[...] = r_ref[...] + _rmsnorm(y, g_ref[...])


def _proj_norm_residual(a, w, g, resid, *, name):
    return pl.pallas_call(
        _proj_norm_residual_kernel,
        out_shape=jax.ShapeDtypeStruct(resid.shape, F32),
        compiler_params=pltpu.CompilerParams(vmem_limit_bytes=VMEM_LIMIT_BYTES),
        name=name,
    )(a, w, g, resid)


def _ffn_kernel(x_ref, xnext_ref, xs_ref, gpre_ref, w1_ref, w2_ref, gpost_ref, o_ref, os_ref,
                xn_ref, xns_ref, acc_ref, accs_ref, *, tm, nf):
    i, f = pl.program_id(0), pl.program_id(1)
    last_f = nf - 1
    chunk = tm // nf

    @pl.when((i == 0) & (f == 0))
    def _():
        xn_ref[0] = _rmsnorm(x_ref[...], gpre_ref[...]).astype(BF16)
        xns_ref[...] = _rmsnorm(xs_ref[...], gpre_ref[...]).astype(BF16)
        accs_ref[...] = jnp.zeros_like(accs_ref)

    @pl.when(f == 0)
    def _():
        acc_ref[...] = jnp.zeros_like(acc_ref)

    def mlp(lhs):
        hidden = jnp.square(jnp.maximum(_dot(lhs, w1_ref[...]), 0.0)).astype(BF16)
        return _dot(hidden, w2_ref[...])

    def norm_next_rows():
        rows = pl.ds(pl.multiple_of(f * chunk, chunk), chunk)
        xn_ref[(i + 1) % 2, rows, :] = _rmsnorm(xnext_ref[rows, :], gpre_ref[...]).astype(BF16)

    @pl.when(i == 0)
    def _():
        both = mlp(jnp.concatenate([xn_ref[0], xns_ref[...]], axis=0))
        acc_ref[...] += both[0:tm]
        accs_ref[...] += both[tm:]
        norm_next_rows()

    @pl.when(i > 0)
    def _():
        acc_ref[...] += mlp(xn_ref[i % 2])
        norm_next_rows()

    @pl.when(f == last_f)
    def _():
        o_ref[...] = x_ref[...] + _rmsnorm(acc_ref[...], gpost_ref[...])

    @pl.when((f == last_f) & (i == 0))
    def _():
        os_ref[...] = xs_ref[...] + _rmsnorm(accs_ref[...], gpost_ref[...])


def _ffn(x, xs, g_pre, w1, w2, g_post, *, tm, tf):
    m, d = x.shape
    ms = xs.shape[0]
    dff = w1.shape[1]
    n_tiles, nf = m // tm, dff // tf
    assert tm % nf == 0 and (tm // nf) % 16 == 0
    sample_spec = _resident((ms, d), lambda i, f: (0, 0))
    return pl.pallas_call(
        functools.partial(_ffn_kernel, tm=tm, nf=nf),
        grid=(n_tiles, nf),
        in_specs=[
            pl.BlockSpec((tm, d), lambda i, f: (i, 0)),
            pl.BlockSpec((tm, d), lambda i, f: (jnp.minimum(i + 1, n_tiles - 1), 0)),
            sample_spec,
            _resident((1, d), lambda i, f: (0, 0)),
            pl.BlockSpec((d, tf), lambda i, f: (0, f)),
            pl.BlockSpec((tf, d), lambda i, f: (f, 0)),
            _resident((1, d), lambda i, f: (0, 0)),
        ],
        out_specs=[pl.BlockSpec((tm, d), lambda i, f: (i, 0)),
                   pl.BlockSpec((ms, d), lambda i, f: (0, 0))],
        out_shape=[jax.ShapeDtypeStruct((m, d), F32), jax.ShapeDtypeStruct((ms, d), F32)],
        scratch_shapes=[pltpu.VMEM((2, tm, d), BF16), pltpu.VMEM((ms, d), BF16),
                        pltpu.VMEM((tm, d), F32), pltpu.VMEM((ms, d), F32)],
        compiler_params=_params(2),
        name="ffn",
    )(x, x, xs, g_pre, w1, w2, g_post)


def kernel(x_prompt, x_sample, state_pool, cache_attn_k, cache_attn_v, cache_mem_k, cache_mem_v, mem_prompt,
           g_mix_pre, g_mix_post, g_mem_pre, g_mem_post, g_ffn_pre, g_ffn_post, g_mem_kv,
           w_in, w_pool, pool_scale, w_out, w_xq, w_mem_kv, w_xo, w_ff1, w_ff2):
    depth = w_in.shape[0]
    assert depth == 1
    batch, seq, d = x_prompt.shape
    dec_batch, dec_seq, _ = x_sample.shape
    assert dec_seq == 1

    l = 0
    scale = pool_scale[l][None, :]
    gain = lambda g: g[l][None, :]

    tm = 512
    xp = x_prompt.reshape(batch * seq, d)
    xs = x_sample.reshape(dec_batch, d)
    cos_p, sin_p = _rope_tables(jnp.arange(seq, dtype=jnp.int32))
    cos_s, sin_s = _rope_tables(jnp.full((dec_batch,), PAST_LEN, jnp.int32))
    tm_mem = 128
    ones, zeros = jnp.ones((tm_mem, HEAD_DIM), F32), jnp.zeros((tm_mem, HEAD_DIM), F32)
    w_pool_f = w_pool[l]


    mem = mem_prompt.reshape(batch * N_MEM, d)
    (mk, mv, mk_h, mv_h), (w_in_b,) = _norm_proj(
        mem, gain(g_mem_kv), w_mem_kv[l], ones, zeros, tm=tm_mem, ncol=MEM_WIDTH, rope_blocks=(),
        outputs=((0, FLAT), (1, FLAT), (0, HEADS), (1, HEADS)), name="mem_kv", casts=(w_in[l],))

    ((pool_out, q, k, v, k_h, v_h, pool_state, us, qs_h, ks_h, vs_h),
     (w_ff1_b, w_out_b, w_xq_b, w_xo_b)) = _in_proj_pool(
        xp, xs, gain(g_mix_pre), w_in_b, cos_p, sin_p, cos_s, sin_s, w_pool_f, scale, tm=256, seq=seq,
        casts=(w_ff1[l], w_out[l], w_xq[l], w_xo[l]))
    attn, (w_ff2_b,) = _attn_prompt(q, k, v, seq=seq, casts=(w_ff2[l],))
    x2 = _mix_mem_prompt(pool_out, attn, xp, mk, mv, w_out_b, gain(g_mix_post), gain(g_mem_pre), w_xq_b,
                         w_xo_b, gain(g_mem_post), tm=tm, seq=seq)

    heads3 = lambda a: a.reshape(dec_batch, -1, HEAD_DIM)
    pool_out_s, new_pool_s = _pool_sample(us, state_pool[l], w_pool_f, scale, pos=PAST_LEN)
    attn_s = _attn_sample(heads3(qs_h), heads3(ks_h), heads3(vs_h), cache_attn_k[l], cache_attn_v[l])
    x1s, qms_h = _out_proj(pool_out_s, attn_s.reshape(dec_batch, ATTN_WIDTH), xs, w_out_b, gain(g_mix_post),
                           gain(g_mem_pre), w_xq_b, tm=dec_batch, qm_layout=HEADS, name="out_proj_sample")
    mem_o_s = _mem_attn_sample(heads3(qms_h), cache_mem_k[l], cache_mem_v[l])
    x2s = _proj_norm_residual(mem_o_s.reshape(dec_batch, MEM_WIDTH), w_xo_b, gain(g_mem_post), x1s,
                              name="mem_out_sample")

    yp, ys = _ffn(x2, x2s, gain(g_ffn_pre), w_ff1_b, w_ff2_b, gain(g_ffn_post), tm=tm, tf=1024)

    keep = min(max(w for w, _ in DILATED), seq)
    return (
        yp.reshape(batch, seq, d),
        ys.reshape(dec_batch, 1, d),
        pool_state.reshape(batch, STATE_ROWS, POOL_WIDTH)[:, STATE_ROWS - POOL_STATE:][None],
        new_pool_s[None],
        k_h.reshape(batch, seq, N_HEADS, HEAD_DIM)[:, seq - keep:][None],
        v_h.reshape(batch, seq, N_HEADS, HEAD_DIM)[:, seq - keep:][None],
        ks_h.reshape(dec_batch, 1, N_HEADS, HEAD_DIM)[None],
        vs_h.reshape(dec_batch, 1, N_HEADS, HEAD_DIM)[None],
        mk_h.reshape(batch, N_MEM, MEM_HEADS, HEAD_DIM)[None],
        mv_h.reshape(batch, N_MEM, MEM_HEADS, HEAD_DIM)[None],
    )
```

```python
import functools

import jax
import jax.numpy as jnp
from jax import lax
from jax.experimental import pallas as pl
from jax.experimental.pallas import tpu as pltpu

D_MODEL = 2048
POOL_WIDTH = 1024
POOL_WINDOWS = (2, 4, 8, 16)
POOL_GROUP = POOL_WIDTH // len(POOL_WINDOWS)
POOL_STATE = max(POOL_WINDOWS) - 1
HEAD_DIM = 128
ATTN_WIDTH = 1024
N_HEADS = ATTN_WIDTH // HEAD_DIM
DILATED = ((128, 1), (512, 4), (2048, 16))
ROPE_DIM = HEAD_DIM // 4
ROPE_HALF = ROPE_DIM // 2
ROPE_THETA = 500000.0
N_MEM = 256
MEM_HEADS = 4
MEM_WIDTH = MEM_HEADS * HEAD_DIM
EPS = 1e-6
BLOCK = 128
NEG_INF = -1e30
LOG2_E = 1.4426950408889634
PAST_LEN = 8192
HALO = 32
STATE_ROWS = 16
SAMPLE_GROUP = 4

F32 = jnp.float32
BF16 = jnp.bfloat16

VMEM_LIMIT_BYTES = 56 * 1024 * 1024
FFN_VMEM_LIMIT_BYTES = 60 * 1024 * 1024

FLAT, HEADS = "flat", "heads"


def _params(n_grid_axes):
    return pltpu.CompilerParams(
        dimension_semantics=("arbitrary",) * n_grid_axes,
        vmem_limit_bytes=VMEM_LIMIT_BYTES,
    )


def _resident(block_shape, index_map):
    return pl.BlockSpec(block_shape, index_map, pipeline_mode=pl.Buffered(1))


def _rmsnorm(x, g):
    ms = jnp.mean(x * x, axis=-1, keepdims=True)
    return x * lax.rsqrt(ms + EPS) * g


def _dot(a, b):
    return jnp.dot(a, b, preferred_element_type=F32)


def _dot_nt(a, b):
    return lax.dot_general(a, b, (((1,), (1,)), ((), ())), preferred_element_type=F32)


def _call_with_casts(kernel_fn, *, grid, in_specs, out_specs, out_shape, scratch_shapes=(), args, casts=(),
                     step_of=None, name):
    n_in, n_out, n_casts = len(in_specs), len(out_specs), len(casts)
    n_steps = 1
    for g in grid:
        n_steps *= g
    if step_of is None:
        step_of = lambda i: i
    cast_specs = []
    for a in casts:
        rows, cols = a.shape
        assert rows % n_steps == 0
        cast_specs.append(pl.BlockSpec((rows // n_steps, cols), lambda *idx: (step_of(*idx), 0)))

    def body(*refs):
        ins, cast_in = refs[:n_in], refs[n_in:n_in + n_casts]
        outs = refs[n_in + n_casts:n_in + n_casts + n_out]
        cast_out = refs[n_in + n_casts + n_out:n_in + 2 * n_casts + n_out]

        def run_casts():
            for i_ref, o_ref in zip(cast_in, cast_out):
                o_ref[...] = i_ref[...].astype(BF16)

        kernel_fn(run_casts, *ins, *outs, *refs[n_in + 2 * n_casts + n_out:])

    res = pl.pallas_call(
        body,
        grid=grid,
        in_specs=list(in_specs) + cast_specs,
        out_specs=list(out_specs) + cast_specs,
        out_shape=list(out_shape) + [jax.ShapeDtypeStruct(a.shape, BF16) for a in casts],
        scratch_shapes=list(scratch_shapes),
        compiler_params=_params(len(grid)),
        name=name,
    )(*args, *casts)
    return res[:n_out], res[n_out:]


def _store_heads(o_ref, layout, h, heads, rows, value):
    if layout == FLAT:
        o_ref[:, h * HEAD_DIM:(h + 1) * HEAD_DIM] = value
    else:
        o_ref[pl.ds(h, rows, stride=heads), :] = value


def _out_block(layout, tm, heads):
    return (tm, heads * HEAD_DIM) if layout == FLAT else (tm * heads, HEAD_DIM)


def _norm_proj_kernel(run_casts, x_ref, g_ref, w_ref, cos_ref, sin_ref, *out_refs, tm, ncol, rope_blocks, outputs):
    run_casts()
    heads = ncol // HEAD_DIM
    xn = _rmsnorm(x_ref[...], g_ref[...]).astype(BF16)
    for c in sorted({c for c, _ in outputs}):
        acc = _dot(xn, w_ref[:, c * ncol:(c + 1) * ncol].astype(BF16))
        if c in rope_blocks:
            cos = cos_ref[...]
            sin = sin_ref[...]
            lane = lax.broadcasted_iota(jnp.int32, cos.shape, 1)
        for h in range(heads):
            a = acc[:, h * HEAD_DIM:(h + 1) * HEAD_DIM]
            if c in rope_blocks:
                partner = jnp.where(lane < ROPE_HALF,
                                    pltpu.roll(a, HEAD_DIM - ROPE_HALF, 1),
                                    pltpu.roll(a, ROPE_HALF, 1))
                a = a * cos + partner * sin
            for (oc, layout), o_ref in zip(outputs, out_refs):
                if oc == c:
                    _store_heads(o_ref, layout, h, heads, tm, a)


def _norm_proj(x, g, w, cos, sin, *, tm, ncol, rope_blocks, outputs, name, casts=()):
    m, d = x.shape
    heads = ncol // HEAD_DIM
    assert w.shape[0] == d and w.shape[1] % ncol == 0 and m % tm == 0
    table_tiles = cos.shape[0] // tm
    kern = functools.partial(_norm_proj_kernel, tm=tm, ncol=ncol, rope_blocks=rope_blocks, outputs=outputs)
    blocks = [_out_block(layout, tm, heads) for _, layout in outputs]
    return _call_with_casts(
        kern,
        grid=(m // tm,),
        in_specs=[
            pl.BlockSpec((tm, d), lambda i: (i, 0)),
            _resident((1, d), lambda i: (0, 0)),
            _resident(w.shape, lambda i: (0, 0)),
            pl.BlockSpec((tm, HEAD_DIM), lambda i: (i % table_tiles, 0)),
            pl.BlockSpec((tm, HEAD_DIM), lambda i: (i % table_tiles, 0)),
        ],
        out_specs=[pl.BlockSpec(b, lambda i: (i, 0)) for b in blocks],
        out_shape=[jax.ShapeDtypeStruct((m // tm * b[0], b[1]), F32) for b in blocks],
        args=(x, g, w, cos, sin),
        casts=casts,
        name=name,
    )


def _rope_tables(pos):
    inv = jnp.power(jnp.float32(ROPE_THETA), -jnp.arange(ROPE_HALF, dtype=F32) * 2.0 / ROPE_DIM)
    ang = pos.astype(F32)[:, None] * inv[None, :]
    cos, sin = jnp.cos(ang), jnp.sin(ang)
    n = pos.shape[0]
    pad = HEAD_DIM - ROPE_DIM
    cos_t = jnp.concatenate([cos, cos, jnp.ones((n, pad), F32)], axis=1)
    sin_t = jnp.concatenate([-sin, sin, jnp.zeros((n, pad), F32)], axis=1)
    return cos_t, sin_t


def _window_sums(ext_ref, lvl_refs, g, w, rows):
    cols = slice(g * POOL_GROUP, (g + 1) * POOL_GROUP)
    levels = w.bit_length() - 1
    src, src_cols = ext_ref, cols
    for level in range(levels):
        shift = 1 << level
        last = level == levels - 1
        lo = HALO if last else 8 * (level + 1)
        assert lo - shift >= 8 * level
        total = src[lo:rows, src_cols] + src[lo - shift:rows - shift, src_cols]
        if last:
            return total
        lvl_refs[level % 2][lo:rows, :] = total
        src, src_cols = lvl_refs[level % 2], slice(None)


def _rope(a, cos, sin):
    lane = lax.broadcasted_iota(jnp.int32, a.shape, 1)
    partner = jnp.where(lane < ROPE_HALF, pltpu.roll(a, HEAD_DIM - ROPE_HALF, 1), pltpu.roll(a, ROPE_HALF, 1))
    return a * cos + partner * sin


def _in_proj_pool_kernel(run_casts, x_ref, xnext_ref, xs_ref, g_ref, w_ref, cos_ref, sin_ref, coss_ref, sins_ref,
                         wp_ref, scale_ref,
                         pool_ref, q_ref, k_ref, v_ref, kh_ref, vh_ref, state_ref, us_ref, qsh_ref, ksh_ref, vsh_ref,
                         xn_ref, ext_ref, lvl_a_ref, lvl_b_ref, *, tm, tiles_per_seq):
    step = pl.program_id(0)
    t_in_seq = step % tiles_per_seq
    n_sample = xs_ref.shape[0]

    def tile(xn, with_sample):
        run_casts()
        u = _dot(xn, w_ref[:, 0:POOL_WIDTH])
        ext_ref[HALO:HALO + tm, :] = u[0:tm]
        if with_sample:
            us_ref[...] = u[tm:]
        cos, sin = cos_ref[...], sin_ref[...]
        outs = ((q_ref, None, qsh_ref), (k_ref, kh_ref, ksh_ref), (v_ref, vh_ref, vsh_ref))
        for c, (flat_ref, heads_ref, sample_ref) in enumerate(outs, start=1):
            acc = _dot(xn, w_ref[:, c * ATTN_WIDTH:(c + 1) * ATTN_WIDTH])
            for h in range(N_HEADS):
                a = acc[0:tm, h * HEAD_DIM:(h + 1) * HEAD_DIM]
                if flat_ref is not v_ref:
                    a = _rope(a, cos, sin)
                _store_heads(flat_ref, FLAT, h, N_HEADS, tm, a)
                if heads_ref is not None:
                    _store_heads(heads_ref, HEADS, h, N_HEADS, tm, a)
                if with_sample:
                    a = acc[tm:, h * HEAD_DIM:(h + 1) * HEAD_DIM]
                    if flat_ref is not v_ref:
                        a = _rope(a, coss_ref[...], sins_ref[...])
                    _store_heads(sample_ref, HEADS, h, N_HEADS, n_sample, a)

        pos = t_in_seq * tm + lax.broadcasted_iota(jnp.int32, (tm, 1), 0)
        for g, w in enumerate(POOL_WINDOWS):
            cols = slice(g * POOL_GROUP, (g + 1) * POOL_GROUP)
            wsum = _window_sums(ext_ref, (lvl_a_ref, lvl_b_ref), g, w, HALO + tm)
            cnt = jnp.minimum(w, pos + 1).astype(F32)
            pooled = (wsum / cnt - ext_ref[HALO:HALO + tm, cols]).astype(BF16)
            pool_ref[:, cols] = (_dot(pooled, wp_ref[g].astype(BF16)) * scale_ref[:, cols]).astype(pool_ref.dtype)
        state_ref[...] = ext_ref[HALO + tm - STATE_ROWS:HALO + tm, :]
        ext_ref[0:HALO, :] = ext_ref[tm:tm + HALO, :]

        xn_ref[(step + 1) % 2] = _rmsnorm(xnext_ref[...], g_ref[...]).astype(BF16)

    @pl.when(t_in_seq == 0)
    def _():
        ext_ref[0:HALO, :] = jnp.zeros((HALO, POOL_WIDTH), F32)

    @pl.when(step == 0)
    def _():
        rows = jnp.concatenate([x_ref[...], xs_ref[...]], axis=0)
        tile(_rmsnorm(rows, g_ref[...]).astype(BF16), True)

    @pl.when(step > 0)
    def _():
        tile(xn_ref[step % 2], False)


def _in_proj_pool(x, xs, g, w, cos, sin, cos_s, sin_s, w_pool, pool_scale, *, tm, seq, casts=()):
    m, d = x.shape
    ms = xs.shape[0]
    tiles_per_seq = seq // tm
    n_seq = m // seq
    assert POOL_WIDTH == ATTN_WIDTH and w.shape == (d, POOL_WIDTH + 3 * ATTN_WIDTH)
    kern = functools.partial(_in_proj_pool_kernel, tm=tm, tiles_per_seq=tiles_per_seq)
    row_tile = lambda width: pl.BlockSpec((tm, width), lambda i: (i, 0))
    heads_tile = pl.BlockSpec((tm * N_HEADS, HEAD_DIM), lambda i: (i, 0))
    table = pl.BlockSpec((tm, HEAD_DIM), lambda i: (i % tiles_per_seq, 0))
    whole = lambda shape: _resident(shape, lambda i: (0,) * len(shape))
    flat = jax.ShapeDtypeStruct((m, ATTN_WIDTH), F32)
    by_heads = jax.ShapeDtypeStruct((m * N_HEADS, HEAD_DIM), F32)
    sample_heads = jax.ShapeDtypeStruct((ms * N_HEADS, HEAD_DIM), F32)
    sample_heads_spec = pl.BlockSpec((ms * N_HEADS, HEAD_DIM), lambda i: (0, 0))
    return _call_with_casts(
        kern,
        grid=(m // tm,),
        in_specs=[
            row_tile(d),
            pl.BlockSpec((tm, d), lambda i: (jnp.minimum(i + 1, m // tm - 1), 0)),
            whole((ms, d)),
            whole((1, d)),
            whole(w.shape),
            table,
            table,
            whole((ms, HEAD_DIM)),
            whole((ms, HEAD_DIM)),
            whole(w_pool.shape),
            whole((1, POOL_WIDTH)),
        ],
        out_specs=[row_tile(POOL_WIDTH), row_tile(ATTN_WIDTH), row_tile(ATTN_WIDTH), row_tile(ATTN_WIDTH),
                   heads_tile, heads_tile,
                   pl.BlockSpec((STATE_ROWS, POOL_WIDTH), lambda i: (i // tiles_per_seq, 0)),
                   pl.BlockSpec((ms, POOL_WIDTH), lambda i: (0, 0)),
                   sample_heads_spec, sample_heads_spec, sample_heads_spec],
        out_shape=[jax.ShapeDtypeStruct((m, POOL_WIDTH), BF16), flat, flat, flat, by_heads, by_heads,
                   jax.ShapeDtypeStruct((n_seq * STATE_ROWS, POOL_WIDTH), F32),
                   jax.ShapeDtypeStruct((ms, POOL_WIDTH), F32), sample_heads, sample_heads, sample_heads],
        scratch_shapes=[pltpu.VMEM((2, tm, d), BF16),
                        pltpu.VMEM((HALO + tm, POOL_WIDTH), F32),
                        pltpu.VMEM((HALO + tm, POOL_GROUP), F32),
                        pltpu.VMEM((HALO + tm, POOL_GROUP), F32)],
        args=(x, x, xs, g, w, cos, sin, cos_s, sin_s, w_pool, pool_scale),
        casts=casts,
        name="in_proj_pool",
    )


def _pool_sample_kernel(u_ref, prev_ref, wp_ref, scale_ref, o_ref, state_ref, *, pos):
    for j in range(POOL_STATE - 1):
        state_ref[:, j, :] = prev_ref[:, j + 1, :]
    state_ref[:, POOL_STATE - 1, :] = u_ref[...]
    for g, w in enumerate(POOL_WINDOWS):
        cols = slice(g * POOL_GROUP, (g + 1) * POOL_GROUP)
        wsum = u_ref[:, cols]
        for back in range(1, w):
            wsum = wsum + prev_ref[:, POOL_STATE - back, cols]
        cnt = float(min(w, pos + 1))
        pooled = (wsum / cnt - u_ref[:, cols]).astype(BF16)
        o_ref[:, cols] = (_dot(pooled, wp_ref[g].astype(BF16)) * scale_ref[:, cols]).astype(o_ref.dtype)


def _pool_sample(u, prev, w_pool, pool_scale, *, pos):
    n, c = u.shape
    kern = functools.partial(_pool_sample_kernel, pos=pos)
    return pl.pallas_call(
        kern,
        out_shape=[jax.ShapeDtypeStruct((n, c), BF16), jax.ShapeDtypeStruct(prev.shape, F32)],
        compiler_params=pltpu.CompilerParams(vmem_limit_bytes=VMEM_LIMIT_BYTES),
        name="pool_sample",
    )(u, prev, w_pool, pool_scale)


def _run_blocks(count, unroll, body):
    trips = count // unroll
    if trips > 1:
        def trip(t, carry):
            for j in range(unroll):
                body(t * unroll + j)
            return carry
        lax.fori_loop(0, trips, trip, 0)
        done = trips * unroll
    else:
        done = 0
    for idx in range(done, count):
        body(idx)


def _attn_prompt_kernel(run_casts, q_ref, k_ref, v_ref, o_ref, *scratch, seq, unroll):
    run_casts()
    n_br = len(DILATED)
    ob_refs, mb_refs, sb_refs = scratch[:n_br], scratch[n_br:2 * n_br], scratch[2 * n_br:3 * n_br]
    p_ref, bias_ref = scratch[3 * n_br:]
    exp2_scale = HEAD_DIM ** -0.5 * LOG2_E
    qi = lax.broadcasted_iota(jnp.int32, (BLOCK, 2 * BLOCK), 0)
    kj = lax.broadcasted_iota(jnp.int32, (BLOCK, 2 * BLOCK), 1)
    in_band = ((kj < BLOCK) & (kj >= qi)) | ((kj >= BLOCK) & (kj - BLOCK <= qi))
    bias_ref[...] = jnp.where(in_band, 0.0, NEG_INF)
    ones = jnp.ones((2 * BLOCK, HEAD_DIM), BF16)

    def rows(block, r, d):
        if d == 1:
            start = block * BLOCK
            return pl.ds(start if isinstance(start, int) else pl.multiple_of(start, BLOCK), BLOCK)
        return pl.ds(block * (BLOCK * d) + r, BLOCK, stride=d)

    def keys(ref, block, r, d, with_prev):
        cur = ref[rows(block, r, d), :]
        if not with_prev:
            return cur.astype(BF16)
        return jnp.concatenate([ref[rows(block - 1, r, d), :], cur], axis=0).astype(BF16)

    for g, (_, d) in enumerate(DILATED):
        n_blocks = seq // d // BLOCK
        for with_prev in (False, True):
            per_r = n_blocks - 1 if with_prev else 1
            count = d * per_r
            if count == 0:
                continue
            cols = slice(0, 2 * BLOCK) if with_prev else slice(BLOCK, 2 * BLOCK)
            width = 2 * BLOCK if with_prev else BLOCK

            def locate(idx, per_r=per_r, with_prev=with_prev):
                return (idx % per_r + 1, idx // per_r) if with_prev else (0, idx)

            def probabilities(idx, g=g, d=d, with_prev=with_prev, cols=cols, width=width, locate=locate):
                b, r = locate(idx)
                q = q_ref[rows(b, r, d), :].astype(BF16)
                s = _dot_nt(q, keys(k_ref, b, r, d, with_prev)) + bias_ref[:, cols]
                m = jnp.max(s, axis=-1, keepdims=True)
                p_ref[idx, :, 0:width] = jnp.exp2((s - m) * exp2_scale).astype(BF16)
                mb_refs[g][rows(b, r, d), :] = jnp.broadcast_to(m, (BLOCK, HEAD_DIM))

            def values(idx, g=g, d=d, with_prev=with_prev, width=width, locate=locate):
                b, r = locate(idx)
                v1 = jnp.concatenate([keys(v_ref, b, r, d, with_prev), ones[0:width]], axis=1)
                acc = _dot(p_ref[idx, :, 0:width], v1)
                ob_refs[g][rows(b, r, d), :] = acc[:, 0:HEAD_DIM]
                sb_refs[g][rows(b, r, d), :] = acc[:, HEAD_DIM:]

            _run_blocks(count, unroll, probabilities)
            _run_blocks(count, unroll, values)

    chunk = 256

    def combine(c, carry):
        rw = pl.ds(pl.multiple_of(c * chunk, chunk), chunk)
        ms = [mb[rw, :] for mb in mb_refs]
        m = functools.reduce(jnp.maximum, ms)
        ws = [jnp.exp2((mg - m) * exp2_scale) for mg in ms]
        num = functools.reduce(jnp.add, [w * ob[rw, :] for w, ob in zip(ws, ob_refs)])
        den = functools.reduce(jnp.add, [w * sb[rw, :] for w, sb in zip(ws, sb_refs)])
        o_ref[rw, :] = (num / den).astype(o_ref.dtype)
        return carry

    lax.fori_loop(0, seq // chunk, combine, 0)


def _attn_prompt(q, k, v, *, seq, casts=()):
    m, width = q.shape
    n_seq = m // seq
    n_heads = width // HEAD_DIM
    spec = pl.BlockSpec((seq, HEAD_DIM), lambda n, h: (n, h))
    kern = functools.partial(_attn_prompt_kernel, seq=seq, unroll=8)
    (attn,), cast_out = _call_with_casts(
        kern,
        grid=(n_seq, n_heads),
        in_specs=[spec, spec, spec],
        out_specs=[spec],
        out_shape=[jax.ShapeDtypeStruct((m, width), BF16)],
        scratch_shapes=(
            [pltpu.VMEM((seq, HEAD_DIM), F32) for _ in range(3 * len(DILATED))]
            + [pltpu.VMEM((seq // BLOCK, BLOCK, 2 * BLOCK), BF16),
               pltpu.VMEM((BLOCK, 2 * BLOCK), F32)]),
        args=(q, k, v),
        casts=casts,
        step_of=lambda n, h: n * n_heads + h,
        name="attn_prompt",
    )
    return attn, cast_out


def _attn_sample_kernel(q_ref, kn_ref, vn_ref, *refs, group):
    n_br = len(DILATED)
    kc_refs, vc_refs, o_ref = refs[:n_br], refs[n_br:2 * n_br], refs[2 * n_br]
    for j in range(group):
        q = q_ref[j] * (HEAD_DIM ** -0.5 * LOG2_E)
        t_new = jnp.sum(q * kn_ref[j], axis=-1, keepdims=True)
        t_br = [jnp.sum(kc[j] * q[None], axis=-1, keepdims=True) for kc in kc_refs]
        m = t_new
        for t in t_br:
            m = jnp.maximum(m, jnp.max(t, axis=0))
        p_new = jnp.exp2(t_new - m) * float(n_br)
        den = p_new
        num = p_new * vn_ref[j]
        for t, vc in zip(t_br, vc_refs):
            p = jnp.exp2(t - m[None])
            den = den + jnp.sum(p, axis=0)
            num = num + jnp.sum(p * vc[j], axis=0)
        o_ref[j] = num / den


def _attn_sample(q, k_new, v_new, k_cache, v_cache):
    n, heads, _ = q.shape
    win = k_cache.shape[1]
    group = SAMPLE_GROUP
    row = pl.BlockSpec((group, heads, HEAD_DIM), lambda i: (i, 0, 0))
    cache_specs, k_views, v_views = [], [], []
    for w, d in DILATED:
        assert w == BLOCK * d and win % w == 0
        last = win // w - 1
        cache_specs.append(pl.BlockSpec((group, BLOCK, None, heads, HEAD_DIM),
                                        lambda i, last=last: (i, last, 0, 0, 0)))
        k_views.append(k_cache.reshape(n, win // d, d, heads, HEAD_DIM))
        v_views.append(v_cache.reshape(n, win // d, d, heads, HEAD_DIM))
    return pl.pallas_call(
        functools.partial(_attn_sample_kernel, group=group),
        grid=(n // group,),
        in_specs=[row, row, row] + cache_specs + cache_specs,
        out_specs=row,
        out_shape=jax.ShapeDtypeStruct((n, heads, HEAD_DIM), F32),
        compiler_params=_params(1),
        name="attn_sample",
    )(q, k_new, v_new, *k_views, *v_views)


def _out_proj_kernel(pool_ref, attn_ref, x_ref, wo_ref, gpost_ref, gpre_ref, wq_ref, x1_ref, qm_ref,
                     *, tm, qm_layout):
    y = _dot(pool_ref[...].astype(BF16), wo_ref[0:POOL_WIDTH, :])
    y = y + _dot(attn_ref[...].astype(BF16), wo_ref[POOL_WIDTH:, :])
    x1 = x_ref[...] + _rmsnorm(y, gpost_ref[...])
    x1_ref[...] = x1
    qm = _dot(_rmsnorm(x1, gpre_ref[...]).astype(BF16), wq_ref[...])
    for h in range(MEM_HEADS):
        _store_heads(qm_ref, qm_layout, h, MEM_HEADS, tm, qm[:, h * HEAD_DIM:(h + 1) * HEAD_DIM])


def _out_proj(pool_out, attn, x, w_out, g_post, g_pre, w_xq, *, tm, qm_layout, name):
    m, d = x.shape
    half = pool_out.shape[1]
    qm_block = _out_block(qm_layout, tm, MEM_HEADS)
    kern = functools.partial(_out_proj_kernel, tm=tm, qm_layout=qm_layout)
    return pl.pallas_call(
        kern,
        grid=(m // tm,),
        in_specs=[
            pl.BlockSpec((tm, half), lambda i: (i, 0)),
            pl.BlockSpec((tm, half), lambda i: (i, 0)),
            pl.BlockSpec((tm, d), lambda i: (i, 0)),
            _resident(w_out.shape, lambda i: (0, 0)),
            _resident((1, d), lambda i: (0, 0)),
            _resident((1, d), lambda i: (0, 0)),
            _resident(w_xq.shape, lambda i: (0, 0)),
        ],
        out_specs=[pl.BlockSpec((tm, d), lambda i: (i, 0)),
                   pl.BlockSpec(qm_block, lambda i: (i, 0))],
        out_shape=[jax.ShapeDtypeStruct((m, d), F32),
                   jax.ShapeDtypeStruct((m // tm * qm_block[0], qm_block[1]), F32)],
        compiler_params=_params(1),
        name=name,
    )(pool_out, attn, x, w_out, g_post, g_pre, w_xq)


def _mix_mem_prompt_kernel(run_casts, pool_ref, attn_ref, x_ref, mk_ref, mv_ref, wo_ref,
                           gmix_ref, gpre_ref, wq_ref, wxo_ref, gmem_ref, x2_ref):
    run_casts()
    scale = HEAD_DIM ** -0.5
    mixed = jnp.concatenate([pool_ref[...], attn_ref[...]], axis=-1)
    x1 = x_ref[...] + _rmsnorm(_dot(mixed, wo_ref[...]), gmix_ref[...])
    qm = _dot(_rmsnorm(x1, gpre_ref[...]).astype(BF16), wq_ref[...]).astype(BF16)
    heads = []
    for h in range(MEM_HEADS):
        cols = slice(h * HEAD_DIM, (h + 1) * HEAD_DIM)
        s = _dot_nt(qm[:, cols], mk_ref[:, cols].astype(BF16)) * scale
        p = jnp.exp(s - jnp.max(s, axis=-1, keepdims=True))
        o = _dot(p.astype(BF16), mv_ref[:, cols].astype(BF16))
        heads.append((o / jnp.sum(p, axis=-1, keepdims=True)).astype(BF16))
    y = _dot(jnp.concatenate(heads, axis=-1), wxo_ref[...])
    x2_ref[...] = x1 + _rmsnorm(y, gmem_ref[...])


def _mix_mem_prompt(pool_out, attn, x, mk, mv, w_out, g_mix_post, g_mem_pre, w_xq, w_xo, g_mem_post, *, tm, seq,
                    casts=()):
    m, d = x.shape
    n_tiles = m // tm
    tiles_per_seq = seq // tm
    half = pool_out.shape[1]
    mem_spec = pl.BlockSpec((N_MEM, MEM_WIDTH), lambda i: (i // tiles_per_seq, 0))
    gain_spec = _resident((1, d), lambda i: (0, 0))
    cur = pl.BlockSpec((tm, half), lambda i: (i, 0))
    (x2,), cast_out = _call_with_casts(
        _mix_mem_prompt_kernel,
        grid=(n_tiles,),
        in_specs=[
            cur,
            cur,
            pl.BlockSpec((tm, d), lambda i: (i, 0)),
            mem_spec,
            mem_spec,
            _resident(w_out.shape, lambda i: (0, 0)),
            gain_spec,
            gain_spec,
            _resident(w_xq.shape, lambda i: (0, 0)),
            _resident(w_xo.shape, lambda i: (0, 0)),
            gain_spec,
        ],
        out_specs=[pl.BlockSpec((tm, d), lambda i: (i, 0))],
        out_shape=[jax.ShapeDtypeStruct((m, d), F32)],
        args=(pool_out, attn, x, mk, mv, w_out, g_mix_post, g_mem_pre, w_xq, w_xo, g_mem_post),
        casts=casts,
        name="mix_mem_prompt",
    )
    return x2, cast_out


def _mem_attn_sample_kernel(qm_ref, mk_ref, mv_ref, o_ref, *, group):
    heads = qm_ref.shape[1]
    both = lambda a: jnp.concatenate([a, a], axis=0)
    fold = lambda a: a[0:heads] + a[heads:]
    for j in range(group):
        q = both(qm_ref[j] * (HEAD_DIM ** -0.5 * LOG2_E))
        t = jnp.sum(mk_ref[j] * q[None], axis=-1, keepdims=True)
        m = jnp.max(t, axis=0)
        m = both(jnp.maximum(m[0:heads], m[heads:]))
        p = jnp.exp2(t - m[None])
        o_ref[j] = fold(jnp.sum(p * mv_ref[j], axis=0)) / fold(jnp.sum(p, axis=0))


def _mem_attn_sample(qm, mem_k, mem_v):
    n, heads, _ = qm.shape
    group = SAMPLE_GROUP
    row = pl.BlockSpec((group, heads, HEAD_DIM), lambda i: (i, 0, 0))
    mem = pl.BlockSpec((group, N_MEM // 2, 2 * heads, HEAD_DIM), lambda i: (i, 0, 0, 0))
    pairs = lambda a: a.reshape(n, N_MEM // 2, 2 * heads, HEAD_DIM)
    return pl.pallas_call(
        functools.partial(_mem_attn_sample_kernel, group=group),
        grid=(n // group,),
        in_specs=[row, mem, mem],
        out_specs=row,
        out_shape=jax.ShapeDtypeStruct((n, heads, HEAD_DIM), F32),
        compiler_params=_params(1),
        name="mem_attn_sample",
    )(qm, pairs(mem_k), pairs(mem_v))


def _proj_norm_residual_kernel(a_ref, w_ref, g_ref, r_ref, o_ref):
    y = _dot(a_ref[...].astype(BF16), w_ref[...])
    o_ref[...] = r_ref[...] + _rmsnorm(y, g_ref[...])


def _proj_norm_residual(a, w, g, resid, *, name):
    return pl.pallas_call(
        _proj_norm_residual_kernel,
        out_shape=jax.ShapeDtypeStruct(resid.shape, F32),
        compiler_params=pltpu.CompilerParams(vmem_limit_bytes=VMEM_LIMIT_BYTES),
        name=name,
    )(a, w, g, resid)


def _ffn_kernel(xprev_ref, xnext_ref, xs_ref, gpre_ref, w1_ref, w2_ref, gpost_ref, o_ref, os_ref,
                xn_ref, xns_ref, acc_ref, accs_ref, *, tm, nf, n_tiles):
    i, f = pl.program_id(0), pl.program_id(1)
    last_f = nf - 1
    chunk = tm // nf
    rows = pl.ds(pl.multiple_of(f * chunk, chunk), chunk)

    @pl.when((i == 0) & (f == 0))
    def _():
        xn_ref[0] = _rmsnorm(xprev_ref[...], gpre_ref[...]).astype(BF16)
        xns_ref[...] = _rmsnorm(xs_ref[...], gpre_ref[...]).astype(BF16)
        accs_ref[...] = jnp.zeros_like(accs_ref)

    @pl.when((f == 0) & (i < n_tiles))
    def _():
        acc_ref[i % 2] = jnp.zeros((tm, acc_ref.shape[2]), F32)

    def mlp(lhs):
        hidden = jnp.square(jnp.maximum(_dot(lhs, w1_ref[...]), 0.0)).astype(BF16)
        return _dot(hidden, w2_ref[...])

    def finish_prev_rows():
        o_ref[rows, :] = xprev_ref[rows, :] + _rmsnorm(acc_ref[(i + 1) % 2, rows, :], gpost_ref[...])

    def norm_next_rows():
        xn_ref[(i + 1) % 2, rows, :] = _rmsnorm(xnext_ref[rows, :], gpre_ref[...]).astype(BF16)

    @pl.when(i == 0)
    def _():
        both = mlp(jnp.concatenate([xn_ref[0], xns_ref[...]], axis=0))
        acc_ref[0] += both[0:tm]
        accs_ref[...] += both[tm:]
        norm_next_rows()

        @pl.when(f == last_f)
        def _():
            os_ref[...] = xs_ref[...] + _rmsnorm(accs_ref[...], gpost_ref[...])

    @pl.when((i > 0) & (i < n_tiles))
    def _():
        prev_acc, prev_x = acc_ref[(i + 1) % 2, rows, :], xprev_ref[rows, :]
        acc_ref[i % 2] += mlp(xn_ref[i % 2])
        norm_next_rows()
        o_ref[rows, :] = prev_x + _rmsnorm(prev_acc, gpost_ref[...])

    @pl.when(i == n_tiles)
    def _():
        finish_prev_rows()


def _ffn(x, xs, g_pre, w1, w2, g_post, *, tm, tf):
    m, d = x.shape
    ms = xs.shape[0]
    dff = w1.shape[1]
    n_tiles, nf = m // tm, dff // tf
    assert tm % nf == 0 and (tm // nf) % 16 == 0
    sample_spec = _resident((ms, d), lambda i, f: (0, 0))
    chunk_of = lambda i, f: jnp.where(i == n_tiles, nf - 1, f)
    return pl.pallas_call(
        functools.partial(_ffn_kernel, tm=tm, nf=nf, n_tiles=n_tiles),
        grid=(n_tiles + 1, nf),
        in_specs=[
            pl.BlockSpec((tm, d), lambda i, f: (jnp.maximum(i - 1, 0), 0)),
            pl.BlockSpec((tm, d), lambda i, f: (jnp.minimum(i + 1, n_tiles - 1), 0)),
            sample_spec,
            _resident((1, d), lambda i, f: (0, 0)),
            pl.BlockSpec((d, tf), lambda i, f: (0, chunk_of(i, f))),
            pl.BlockSpec((tf, d), lambda i, f: (chunk_of(i, f), 0)),
            _resident((1, d), lambda i, f: (0, 0)),
        ],
        out_specs=[pl.BlockSpec((tm, d), lambda i, f: (jnp.maximum(i - 1, 0), 0)),
                   pl.BlockSpec((ms, d), lambda i, f: (0, 0))],
        out_shape=[jax.ShapeDtypeStruct((m, d), F32), jax.ShapeDtypeStruct((ms, d), F32)],
        scratch_shapes=[pltpu.VMEM((2, tm, d), BF16), pltpu.VMEM((ms, d), BF16),
                        pltpu.VMEM((2, tm, d), F32), pltpu.VMEM((ms, d), F32)],
        compiler_params=pltpu.CompilerParams(dimension_semantics=("arbitrary", "arbitrary"),
                                             vmem_limit_bytes=FFN_VMEM_LIMIT_BYTES),
        name="ffn",
    )(x, x, xs, g_pre, w1, w2, g_post)


def kernel(x_prompt, x_sample, state_pool, cache_attn_k, cache_attn_v, cache_mem_k, cache_mem_v, mem_prompt,
           g_mix_pre, g_mix_post, g_mem_pre, g_mem_post, g_ffn_pre, g_ffn_post, g_mem_kv,
           w_in, w_pool, pool_scale, w_out, w_xq, w_mem_kv, w_xo, w_ff1, w_ff2):
    depth = w_in.shape[0]
    assert depth == 1
    batch, seq, d = x_prompt.shape
    dec_batch, dec_seq, _ = x_sample.shape
    assert dec_seq == 1

    l = 0
    scale = pool_scale[l][None, :]
    gain = lambda g: g[l][None, :]

    tm = 512
    xp = x_prompt.reshape(batch * seq, d)
    xs = x_sample.reshape(dec_batch, d)
    cos_p, sin_p = _rope_tables(jnp.arange(seq, dtype=jnp.int32))
    cos_s, sin_s = _rope_tables(jnp.full((dec_batch,), PAST_LEN, jnp.int32))
    tm_mem = 128
    ones, zeros = jnp.ones((tm_mem, HEAD_DIM), F32), jnp.zeros((tm_mem, HEAD_DIM), F32)
    w_pool_f = w_pool[l]


    mem = mem_prompt.reshape(batch * N_MEM, d)
    (mk, mv, mk_h, mv_h), (w_in_b,) = _norm_proj(
        mem, gain(g_mem_kv), w_mem_kv[l], ones, zeros, tm=tm_mem, ncol=MEM_WIDTH, rope_blocks=(),
        outputs=((0, FLAT), (1, FLAT), (0, HEADS), (1, HEADS)), name="mem_kv", casts=(w_in[l],))

    ((pool_out, q, k, v, k_h, v_h, pool_state, us, qs_h, ks_h, vs_h),
     (w_out_b, w_xq_b, w_xo_b)) = _in_proj_pool(
        xp, xs, gain(g_mix_pre), w_in_b, cos_p, sin_p, cos_s, sin_s, w_pool_f, scale, tm=256, seq=seq,
        casts=(w_out[l], w_xq[l], w_xo[l]))
    attn, (w_ff2_b,) = _attn_prompt(q, k, v, seq=seq, casts=(w_ff2[l],))
    x2, (w_ff1_b,) = _mix_mem_prompt(pool_out, attn, xp, mk, mv, w_out_b, gain(g_mix_post), gain(g_mem_pre),
                                     w_xq_b, w_xo_b, gain(g_mem_post), tm=tm, seq=seq, casts=(w_ff1[l],))

    heads3 = lambda a: a.reshape(dec_batch, -1, HEAD_DIM)
    pool_out_s, new_pool_s = _pool_sample(us, state_pool[l], w_pool_f, scale, pos=PAST_LEN)
    attn_s = _attn_sample(heads3(qs_h), heads3(ks_h), heads3(vs_h), cache_attn_k[l], cache_attn_v[l])
    x1s, qms_h = _out_proj(pool_out_s, attn_s.reshape(dec_batch, ATTN_WIDTH), xs, w_out_b, gain(g_mix_post),
                           gain(g_mem_pre), w_xq_b, tm=dec_batch, qm_layout=HEADS, name="out_proj_sample")
    mem_o_s = _mem_attn_sample(heads3(qms_h), cache_mem_k[l], cache_mem_v[l])
    x2s = _proj_norm_residual(mem_o_s.reshape(dec_batch, MEM_WIDTH), w_xo_b, gain(g_mem_post), x1s,
                              name="mem_out_sample")

    yp, ys = _ffn(x2, x2s, gain(g_ffn_pre), w_ff1_b, w_ff2_b, gain(g_ffn_post), tm=tm, tf=1024)

    keep = min(max(w for w, _ in DILATED), seq)
    return (
        yp.reshape(batch, seq, d),
        ys.reshape(dec_batch, 1, d),
        pool_state.reshape(batch, STATE_ROWS, POOL_WIDTH)[:, STATE_ROWS - POOL_STATE:][None],
        new_pool_s[None],
        k_h.reshape(batch, seq, N_HEADS, HEAD_DIM)[:, seq - keep:][None],
        v_h.reshape(batch, seq, N_HEADS, HEAD_DIM)[:, seq - keep:][None],
        ks_h.reshape(dec_batch, 1, N_HEADS, HEAD_DIM)[None],
        vs_h.reshape(dec_batch, 1, N_HEADS, HEAD_DIM)[None],
        mk_h.reshape(batch, N_MEM, MEM_HEADS, HEAD_DIM)[None],
        mv_h.reshape(batch, N_MEM, MEM_HEADS, HEAD_DIM)[None],
    )
```

```python
import functools

import jax
import jax.numpy as jnp
from jax import lax
from jax.experimental import pallas as pl
from jax.experimental.pallas import tpu as pltpu

D_MODEL = 2048
POOL_WIDTH = 1024
POOL_WINDOWS = (2, 4, 8, 16)
POOL_GROUP = POOL_WIDTH // len(POOL_WINDOWS)
POOL_STATE = max(POOL_WINDOWS) - 1
HEAD_DIM = 128
ATTN_WIDTH = 1024
N_HEADS = ATTN_WIDTH // HEAD_DIM
DILATED = ((128, 1), (512, 4), (2048, 16))
ROPE_DIM = HEAD_DIM // 4
ROPE_HALF = ROPE_DIM // 2
ROPE_THETA = 500000.0
N_MEM = 256
MEM_HEADS = 4
MEM_WIDTH = MEM_HEADS * HEAD_DIM
EPS = 1e-6
BLOCK = 128
NEG_INF = -1e30
LOG2_E = 1.4426950408889634
PAST_LEN = 8192
HALO = 32
STATE_ROWS = 16
SAMPLE_GROUP = 4

F32 = jnp.float32
BF16 = jnp.bfloat16

VMEM_LIMIT_BYTES = 56 * 1024 * 1024
FFN_CHUNK = 1024

FLAT, HEADS = "flat", "heads"


def _params(n_grid_axes):
    return pltpu.CompilerParams(
        dimension_semantics=("arbitrary",) * n_grid_axes,
        vmem_limit_bytes=VMEM_LIMIT_BYTES,
    )


def _resident(block_shape, index_map):
    return pl.BlockSpec(block_shape, index_map, pipeline_mode=pl.Buffered(1))


def _rmsnorm(x, g):
    ms = jnp.mean(x * x, axis=-1, keepdims=True)
    return x * lax.rsqrt(ms + EPS) * g


def _dot(a, b):
    return jnp.dot(a, b, preferred_element_type=F32)


def _dot_nt(a, b):
    return lax.dot_general(a, b, (((1,), (1,)), ((), ())), preferred_element_type=F32)


def _call_with_casts(kernel_fn, *, grid, in_specs, out_specs, out_shape, scratch_shapes=(), args, casts=(),
                     step_of=None, name):
    casts = [c if isinstance(c, tuple) else (c, 1) for c in casts]
    n_in, n_out, n_casts = len(in_specs), len(out_specs), len(casts)
    n_steps = 1
    for g in grid:
        n_steps *= g
    if step_of is None:
        step_of = lambda i: i
    cast_in_specs, cast_out_specs, cast_shapes = [], [], []
    for a, n_chunks in casts:
        rows, cols = a.shape
        assert rows % n_steps == 0 and cols % n_chunks == 0
        cast_in_specs.append(pl.BlockSpec((rows // n_steps, cols), lambda *idx: (step_of(*idx), 0)))
        if n_chunks == 1:
            cast_out_specs.append(cast_in_specs[-1])
            cast_shapes.append(jax.ShapeDtypeStruct(a.shape, BF16))
        else:
            cast_out_specs.append(pl.BlockSpec((n_chunks, rows // n_steps, cols // n_chunks),
                                               lambda *idx: (0, step_of(*idx), 0)))
            cast_shapes.append(jax.ShapeDtypeStruct((n_chunks, rows, cols // n_chunks), BF16))

    def body(*refs):
        ins, cast_in = refs[:n_in], refs[n_in:n_in + n_casts]
        outs = refs[n_in + n_casts:n_in + n_casts + n_out]
        cast_out = refs[n_in + n_casts + n_out:n_in + 2 * n_casts + n_out]

        def run_casts():
            for i_ref, o_ref, (_, n_chunks) in zip(cast_in, cast_out, casts):
                if n_chunks == 1:
                    o_ref[...] = i_ref[...].astype(BF16)
                else:
                    width = i_ref.shape[1] // n_chunks
                    for c in range(n_chunks):
                        o_ref[c] = i_ref[:, c * width:(c + 1) * width].astype(BF16)

        kernel_fn(run_casts, *ins, *outs, *refs[n_in + 2 * n_casts + n_out:])

    res = pl.pallas_call(
        body,
        grid=grid,
        in_specs=list(in_specs) + cast_in_specs,
        out_specs=list(out_specs) + cast_out_specs,
        out_shape=list(out_shape) + cast_shapes,
        scratch_shapes=list(scratch_shapes),
        compiler_params=_params(len(grid)),
        name=name,
    )(*args, *[a for a, _ in casts])
    return res[:n_out], res[n_out:]


def _store_heads(o_ref, layout, h, heads, rows, value):
    if layout == FLAT:
        o_ref[:, h * HEAD_DIM:(h + 1) * HEAD_DIM] = value
    else:
        o_ref[pl.ds(h, rows, stride=heads), :] = value


def _out_block(layout, tm, heads):
    return (tm, heads * HEAD_DIM) if layout == FLAT else (tm * heads, HEAD_DIM)


def _norm_proj_kernel(run_casts, x_ref, g_ref, w_ref, cos_ref, sin_ref, *out_refs, tm, ncol, rope_blocks, outputs):
    run_casts()
    heads = ncol // HEAD_DIM
    xn = _rmsnorm(x_ref[...], g_ref[...]).astype(BF16)
    for c in sorted({c for c, _ in outputs}):
        acc = _dot(xn, w_ref[:, c * ncol:(c + 1) * ncol].astype(BF16))
        if c in rope_blocks:
            cos = cos_ref[...]
            sin = sin_ref[...]
            lane = lax.broadcasted_iota(jnp.int32, cos.shape, 1)
        for h in range(heads):
            a = acc[:, h * HEAD_DIM:(h + 1) * HEAD_DIM]
            if c in rope_blocks:
                partner = jnp.where(lane < ROPE_HALF,
                                    pltpu.roll(a, HEAD_DIM - ROPE_HALF, 1),
                                    pltpu.roll(a, ROPE_HALF, 1))
                a = a * cos + partner * sin
            for (oc, layout), o_ref in zip(outputs, out_refs):
                if oc == c:
                    _store_heads(o_ref, layout, h, heads, tm, a)


def _norm_proj(x, g, w, cos, sin, *, tm, ncol, rope_blocks, outputs, name, casts=()):
    m, d = x.shape
    heads = ncol // HEAD_DIM
    assert w.shape[0] == d and w.shape[1] % ncol == 0 and m % tm == 0
    table_tiles = cos.shape[0] // tm
    kern = functools.partial(_norm_proj_kernel, tm=tm, ncol=ncol, rope_blocks=rope_blocks, outputs=outputs)
    blocks = [_out_block(layout, tm, heads) for _, layout in outputs]
    return _call_with_casts(
        kern,
        grid=(m // tm,),
        in_specs=[
            pl.BlockSpec((tm, d), lambda i: (i, 0)),
            _resident((1, d), lambda i: (0, 0)),
            _resident(w.shape, lambda i: (0, 0)),
            pl.BlockSpec((tm, HEAD_DIM), lambda i: (i % table_tiles, 0)),
            pl.BlockSpec((tm, HEAD_DIM), lambda i: (i % table_tiles, 0)),
        ],
        out_specs=[pl.BlockSpec(b, lambda i: (i, 0)) for b in blocks],
        out_shape=[jax.ShapeDtypeStruct((m // tm * b[0], b[1]), F32) for b in blocks],
        args=(x, g, w, cos, sin),
        casts=casts,
        name=name,
    )


def _rope_tables(pos):
    inv = jnp.power(jnp.float32(ROPE_THETA), -jnp.arange(ROPE_HALF, dtype=F32) * 2.0 / ROPE_DIM)
    ang = pos.astype(F32)[:, None] * inv[None, :]
    cos, sin = jnp.cos(ang), jnp.sin(ang)
    n = pos.shape[0]
    pad = HEAD_DIM - ROPE_DIM
    cos_t = jnp.concatenate([cos, cos, jnp.ones((n, pad), F32)], axis=1)
    sin_t = jnp.concatenate([-sin, sin, jnp.zeros((n, pad), F32)], axis=1)
    return cos_t, sin_t


def _window_sums(ext_ref, lvl_refs, g, w, rows):
    cols = slice(g * POOL_GROUP, (g + 1) * POOL_GROUP)
    levels = w.bit_length() - 1
    src, src_cols = ext_ref, cols
    for level in range(levels):
        shift = 1 << level
        last = level == levels - 1
        lo = HALO if last else 8 * (level + 1)
        assert lo - shift >= 8 * level
        total = src[lo:rows, src_cols] + src[lo - shift:rows - shift, src_cols]
        if last:
            return total
        lvl_refs[level % 2][lo:rows, :] = total
        src, src_cols = lvl_refs[level % 2], slice(None)


def _rope(a, cos, sin):
    lane = lax.broadcasted_iota(jnp.int32, a.shape, 1)
    partner = jnp.where(lane < ROPE_HALF, pltpu.roll(a, HEAD_DIM - ROPE_HALF, 1), pltpu.roll(a, ROPE_HALF, 1))
    return a * cos + partner * sin


def _in_proj_pool_kernel(run_casts, x_ref, xnext_ref, xs_ref, g_ref, w_ref, cos_ref, sin_ref, coss_ref, sins_ref,
                         wp_ref, scale_ref,
                         pool_ref, q_ref, k_ref, v_ref, kh_ref, vh_ref, state_ref, us_ref, qsh_ref, ksh_ref, vsh_ref,
                         xn_ref, ext_ref, lvl_a_ref, lvl_b_ref, *, tm, tiles_per_seq):
    step = pl.program_id(0)
    t_in_seq = step % tiles_per_seq
    n_sample = xs_ref.shape[0]

    def tile(xn, with_sample):
        run_casts()
        u = _dot(xn, w_ref[:, 0:POOL_WIDTH])
        ext_ref[HALO:HALO + tm, :] = u[0:tm]
        if with_sample:
            us_ref[...] = u[tm:]
        cos, sin = cos_ref[...], sin_ref[...]
        outs = ((q_ref, None, qsh_ref), (k_ref, kh_ref, ksh_ref), (v_ref, vh_ref, vsh_ref))
        for c, (major_ref, heads_ref, sample_ref) in enumerate(outs, start=1):
            acc = _dot(xn, w_ref[:, c * ATTN_WIDTH:(c + 1) * ATTN_WIDTH])
            for h in range(N_HEADS):
                a = acc[0:tm, h * HEAD_DIM:(h + 1) * HEAD_DIM]
                if major_ref is not v_ref:
                    a = _rope(a, cos, sin)
                major_ref[h] = a
                if heads_ref is not None:
                    _store_heads(heads_ref, HEADS, h, N_HEADS, tm, a)
                if with_sample:
                    a = acc[tm:, h * HEAD_DIM:(h + 1) * HEAD_DIM]
                    if major_ref is not v_ref:
                        a = _rope(a, coss_ref[...], sins_ref[...])
                    _store_heads(sample_ref, HEADS, h, N_HEADS, n_sample, a)

        pos = t_in_seq * tm + lax.broadcasted_iota(jnp.int32, (tm, 1), 0)
        for g, w in enumerate(POOL_WINDOWS):
            cols = slice(g * POOL_GROUP, (g + 1) * POOL_GROUP)
            wsum = _window_sums(ext_ref, (lvl_a_ref, lvl_b_ref), g, w, HALO + tm)
            cnt = jnp.minimum(w, pos + 1).astype(F32)
            pooled = (wsum / cnt - ext_ref[HALO:HALO + tm, cols]).astype(BF16)
            pool_ref[:, cols] = (_dot(pooled, wp_ref[g].astype(BF16)) * scale_ref[:, cols]).astype(pool_ref.dtype)
        state_ref[...] = ext_ref[HALO + tm - STATE_ROWS:HALO + tm, :]
        ext_ref[0:HALO, :] = ext_ref[tm:tm + HALO, :]

        xn_ref[(step + 1) % 2] = _rmsnorm(xnext_ref[...], g_ref[...]).astype(BF16)

    @pl.when(t_in_seq == 0)
    def _():
        ext_ref[0:HALO, :] = jnp.zeros((HALO, POOL_WIDTH), F32)

    @pl.when(step == 0)
    def _():
        rows = jnp.concatenate([x_ref[...], xs_ref[...]], axis=0)
        tile(_rmsnorm(rows, g_ref[...]).astype(BF16), True)

    @pl.when(step > 0)
    def _():
        tile(xn_ref[step % 2], False)


def _in_proj_pool(x, xs, g, w, cos, sin, cos_s, sin_s, w_pool, pool_scale, *, tm, seq, casts=()):
    m, d = x.shape
    ms = xs.shape[0]
    tiles_per_seq = seq // tm
    n_seq = m // seq
    assert POOL_WIDTH == ATTN_WIDTH and w.shape == (d, POOL_WIDTH + 3 * ATTN_WIDTH)
    kern = functools.partial(_in_proj_pool_kernel, tm=tm, tiles_per_seq=tiles_per_seq)
    row_tile = lambda width: pl.BlockSpec((tm, width), lambda i: (i, 0))
    heads_tile = pl.BlockSpec((tm * N_HEADS, HEAD_DIM), lambda i: (i, 0))
    table = pl.BlockSpec((tm, HEAD_DIM), lambda i: (i % tiles_per_seq, 0))
    whole = lambda shape: _resident(shape, lambda i: (0,) * len(shape))
    head_major = jax.ShapeDtypeStruct((N_HEADS, m, HEAD_DIM), F32)
    head_major_tile = pl.BlockSpec((N_HEADS, tm, HEAD_DIM), lambda i: (0, i, 0))
    by_heads = jax.ShapeDtypeStruct((m * N_HEADS, HEAD_DIM), F32)
    sample_heads = jax.ShapeDtypeStruct((ms * N_HEADS, HEAD_DIM), F32)
    sample_heads_spec = pl.BlockSpec((ms * N_HEADS, HEAD_DIM), lambda i: (0, 0))
    return _call_with_casts(
        kern,
        grid=(m // tm,),
        in_specs=[
            row_tile(d),
            pl.BlockSpec((tm, d), lambda i: (jnp.minimum(i + 1, m // tm - 1), 0)),
            whole((ms, d)),
            whole((1, d)),
            whole(w.shape),
            table,
            table,
            whole((ms, HEAD_DIM)),
            whole((ms, HEAD_DIM)),
            whole(w_pool.shape),
            whole((1, POOL_WIDTH)),
        ],
        out_specs=[row_tile(POOL_WIDTH), head_major_tile, head_major_tile, head_major_tile,
                   heads_tile, heads_tile,
                   pl.BlockSpec((STATE_ROWS, POOL_WIDTH), lambda i: (i // tiles_per_seq, 0)),
                   pl.BlockSpec((ms, POOL_WIDTH), lambda i: (0, 0)),
                   sample_heads_spec, sample_heads_spec, sample_heads_spec],
        out_shape=[jax.ShapeDtypeStruct((m, POOL_WIDTH), BF16), head_major, head_major, head_major, by_heads, by_heads,
                   jax.ShapeDtypeStruct((n_seq * STATE_ROWS, POOL_WIDTH), F32),
                   jax.ShapeDtypeStruct((ms, POOL_WIDTH), F32), sample_heads, sample_heads, sample_heads],
        scratch_shapes=[pltpu.VMEM((2, tm, d), BF16),
                        pltpu.VMEM((HALO + tm, POOL_WIDTH), F32),
                        pltpu.VMEM((HALO + tm, POOL_GROUP), F32),
                        pltpu.VMEM((HALO + tm, POOL_GROUP), F32)],
        args=(x, x, xs, g, w, cos, sin, cos_s, sin_s, w_pool, pool_scale),
        casts=casts,
        name="in_proj_pool",
    )


def _pool_sample_kernel(u_ref, prev_ref, wp_ref, scale_ref, o_ref, state_ref, *, pos):
    for j in range(POOL_STATE - 1):
        state_ref[:, j, :] = prev_ref[:, j + 1, :]
    state_ref[:, POOL_STATE - 1, :] = u_ref[...]
    for g, w in enumerate(POOL_WINDOWS):
        cols = slice(g * POOL_GROUP, (g + 1) * POOL_GROUP)
        wsum = u_ref[:, cols]
        for back in range(1, w):
            wsum = wsum + prev_ref[:, POOL_STATE - back, cols]
        cnt = float(min(w, pos + 1))
        pooled = (wsum / cnt - u_ref[:, cols]).astype(BF16)
        o_ref[:, cols] = (_dot(pooled, wp_ref[g].astype(BF16)) * scale_ref[:, cols]).astype(o_ref.dtype)


def _pool_sample(u, prev, w_pool, pool_scale, *, pos):
    n, c = u.shape
    kern = functools.partial(_pool_sample_kernel, pos=pos)
    return pl.pallas_call(
        kern,
        out_shape=[jax.ShapeDtypeStruct((n, c), BF16), jax.ShapeDtypeStruct(prev.shape, F32)],
        compiler_params=pltpu.CompilerParams(vmem_limit_bytes=VMEM_LIMIT_BYTES),
        name="pool_sample",
    )(u, prev, w_pool, pool_scale)


def _run_blocks(count, unroll, body):
    trips = count // unroll
    if trips > 1:
        def trip(t, carry):
            for j in range(unroll):
                body(t * unroll + j)
            return carry
        lax.fori_loop(0, trips, trip, 0)
        done = trips * unroll
    else:
        done = 0
    for idx in range(done, count):
        body(idx)


def _attn_prompt_kernel(run_casts, q_ref, k_ref, v_ref, o_ref, *scratch, seq, unroll):
    run_casts()
    n_br = len(DILATED)
    ob_refs, mb_refs, sb_refs = scratch[:n_br], scratch[n_br:2 * n_br], scratch[2 * n_br:3 * n_br]
    p_ref, bias_ref = scratch[3 * n_br:]
    exp2_scale = HEAD_DIM ** -0.5 * LOG2_E
    qi = lax.broadcasted_iota(jnp.int32, (BLOCK, 2 * BLOCK), 0)
    kj = lax.broadcasted_iota(jnp.int32, (BLOCK, 2 * BLOCK), 1)
    in_band = ((kj < BLOCK) & (kj >= qi)) | ((kj >= BLOCK) & (kj - BLOCK <= qi))
    bias_ref[...] = jnp.where(in_band, 0.0, NEG_INF)
    ones = jnp.ones((2 * BLOCK, HEAD_DIM), BF16)

    def rows(block, r, d):
        if d == 1:
            start = block * BLOCK
            return pl.ds(start if isinstance(start, int) else pl.multiple_of(start, BLOCK), BLOCK)
        return pl.ds(block * (BLOCK * d) + r, BLOCK, stride=d)

    def keys(ref, block, r, d, with_prev):
        cur = ref[rows(block, r, d), :]
        if not with_prev:
            return cur.astype(BF16)
        return jnp.concatenate([ref[rows(block - 1, r, d), :], cur], axis=0).astype(BF16)

    for g, (_, d) in enumerate(DILATED):
        n_blocks = seq // d // BLOCK
        for with_prev in (False, True):
            per_r = n_blocks - 1 if with_prev else 1
            count = d * per_r
            if count == 0:
                continue
            cols = slice(0, 2 * BLOCK) if with_prev else slice(BLOCK, 2 * BLOCK)
            width = 2 * BLOCK if with_prev else BLOCK

            def locate(idx, per_r=per_r, with_prev=with_prev):
                return (idx % per_r + 1, idx // per_r) if with_prev else (0, idx)

            def probabilities(idx, g=g, d=d, with_prev=with_prev, cols=cols, width=width, locate=locate):
                b, r = locate(idx)
                q = q_ref[rows(b, r, d), :].astype(BF16)
                s = _dot_nt(q, keys(k_ref, b, r, d, with_prev)) + bias_ref[:, cols]
                m = jnp.max(s, axis=-1, keepdims=True)
                p_ref[idx, :, 0:width] = jnp.exp2((s - m) * exp2_scale).astype(BF16)
                mb_refs[g][rows(b, r, d), :] = jnp.broadcast_to(m, (BLOCK, HEAD_DIM))

            def values(idx, g=g, d=d, with_prev=with_prev, width=width, locate=locate):
                b, r = locate(idx)
                v1 = jnp.concatenate([keys(v_ref, b, r, d, with_prev), ones[0:width]], axis=1)
                acc = _dot(p_ref[idx, :, 0:width], v1)
                ob_refs[g][rows(b, r, d), :] = acc[:, 0:HEAD_DIM]
                sb_refs[g][rows(b, r, d), :] = acc[:, HEAD_DIM:]

            _run_blocks(count, unroll, probabilities)
            _run_blocks(count, unroll, values)

    chunk = 256

    def combine(c, carry):
        rw = pl.ds(pl.multiple_of(c * chunk, chunk), chunk)
        ms = [mb[rw, :] for mb in mb_refs]
        m = functools.reduce(jnp.maximum, ms)
        ws = [jnp.exp2((mg - m) * exp2_scale) for mg in ms]
        num = functools.reduce(jnp.add, [w * ob[rw, :] for w, ob in zip(ws, ob_refs)])
        den = functools.reduce(jnp.add, [w * sb[rw, :] for w, sb in zip(ws, sb_refs)])
        o_ref[rw, :] = (num / den).astype(o_ref.dtype)
        return carry

    lax.fori_loop(0, seq // chunk, combine, 0)


def _attn_prompt(q, k, v, *, seq, casts=()):
    n_heads, m, _ = q.shape
    n_seq = m // seq
    spec = pl.BlockSpec((None, seq, HEAD_DIM), lambda n, h: (h, n, 0))
    kern = functools.partial(_attn_prompt_kernel, seq=seq, unroll=8)
    (attn,), cast_out = _call_with_casts(
        kern,
        grid=(n_seq, n_heads),
        in_specs=[spec, spec, spec],
        out_specs=[spec],
        out_shape=[jax.ShapeDtypeStruct(q.shape, BF16)],
        scratch_shapes=(
            [pltpu.VMEM((seq, HEAD_DIM), F32) for _ in range(3 * len(DILATED))]
            + [pltpu.VMEM((seq // BLOCK, BLOCK, 2 * BLOCK), BF16),
               pltpu.VMEM((BLOCK, 2 * BLOCK), F32)]),
        args=(q, k, v),
        casts=casts,
        step_of=lambda n, h: n * n_heads + h,
        name="attn_prompt",
    )
    return attn, cast_out


def _attn_sample_kernel(q_ref, kn_ref, vn_ref, *refs, group):
    n_br = len(DILATED)
    kc_refs, vc_refs, o_ref = refs[:n_br], refs[n_br:2 * n_br], refs[2 * n_br]
    for j in range(group):
        q = q_ref[j] * (HEAD_DIM ** -0.5 * LOG2_E)
        t_new = jnp.sum(q * kn_ref[j], axis=-1, keepdims=True)
        t_br = [jnp.sum(kc[j] * q[None], axis=-1, keepdims=True) for kc in kc_refs]
        m = t_new
        for t in t_br:
            m = jnp.maximum(m, jnp.max(t, axis=0))
        p_new = jnp.exp2(t_new - m) * float(n_br)
        den = p_new
        num = p_new * vn_ref[j]
        for t, vc in zip(t_br, vc_refs):
            p = jnp.exp2(t - m[None])
            den = den + jnp.sum(p, axis=0)
            num = num + jnp.sum(p * vc[j], axis=0)
        o_ref[j] = num / den


def _attn_sample_operands(q, k_new, v_new, k_cache, v_cache, *, group):
    n, heads, _ = q.shape
    win = k_cache.shape[1]
    row = pl.BlockSpec((group, heads, HEAD_DIM), lambda i: (i, 0, 0))
    cache_specs, k_views, v_views = [], [], []
    for w, d in DILATED:
        assert w == BLOCK * d and win % w == 0
        last = win // w - 1
        cache_specs.append(pl.BlockSpec((group, BLOCK, None, heads, HEAD_DIM),
                                        lambda i, last=last: (i, last, 0, 0, 0)))
        k_views.append(k_cache.reshape(n, win // d, d, heads, HEAD_DIM))
        v_views.append(v_cache.reshape(n, win // d, d, heads, HEAD_DIM))
    return ([row, row, row] + cache_specs + cache_specs, row, jax.ShapeDtypeStruct((n, heads, HEAD_DIM), F32),
            (q, k_new, v_new, *k_views, *v_views))


def _out_proj_kernel(pool_ref, attn_ref, x_ref, wo_ref, gpost_ref, gpre_ref, wq_ref, x1_ref, qm_ref,
                     *, tm, qm_layout):
    y = _dot(pool_ref[...].astype(BF16), wo_ref[0:POOL_WIDTH, :])
    y = y + _dot(attn_ref[...].astype(BF16), wo_ref[POOL_WIDTH:, :])
    x1 = x_ref[...] + _rmsnorm(y, gpost_ref[...])
    x1_ref[...] = x1
    qm = _dot(_rmsnorm(x1, gpre_ref[...]).astype(BF16), wq_ref[...])
    for h in range(MEM_HEADS):
        _store_heads(qm_ref, qm_layout, h, MEM_HEADS, tm, qm[:, h * HEAD_DIM:(h + 1) * HEAD_DIM])


def _out_proj(pool_out, attn, x, w_out, g_post, g_pre, w_xq, *, tm, qm_layout, name):
    m, d = x.shape
    half = pool_out.shape[1]
    qm_block = _out_block(qm_layout, tm, MEM_HEADS)
    kern = functools.partial(_out_proj_kernel, tm=tm, qm_layout=qm_layout)
    return pl.pallas_call(
        kern,
        grid=(m // tm,),
        in_specs=[
            pl.BlockSpec((tm, half), lambda i: (i, 0)),
            pl.BlockSpec((tm, half), lambda i: (i, 0)),
            pl.BlockSpec((tm, d), lambda i: (i, 0)),
            _resident(w_out.shape, lambda i: (0, 0)),
            _resident((1, d), lambda i: (0, 0)),
            _resident((1, d), lambda i: (0, 0)),
            _resident(w_xq.shape, lambda i: (0, 0)),
        ],
        out_specs=[pl.BlockSpec((tm, d), lambda i: (i, 0)),
                   pl.BlockSpec(qm_block, lambda i: (i, 0))],
        out_shape=[jax.ShapeDtypeStruct((m, d), F32),
                   jax.ShapeDtypeStruct((m // tm * qm_block[0], qm_block[1]), F32)],
        compiler_params=_params(1),
        name=name,
    )(pool_out, attn, x, w_out, g_post, g_pre, w_xq)


def _mix_mem_prompt_kernel(run_casts, pool_ref, attn_ref, x_ref, mk_ref, mv_ref, wo_ref,
                           gmix_ref, gpre_ref, wq_ref, wxo_ref, gmem_ref, *refs, sample_group):
    n_sample_in = 3 + 2 * len(DILATED)
    sample_in, (x2_ref, attn_s_ref) = refs[:n_sample_in], refs[n_sample_in:]
    run_casts()
    scale = HEAD_DIM ** -0.5
    mixed = jnp.concatenate([pool_ref[...]] + [attn_ref[h] for h in range(N_HEADS)], axis=-1)
    x1 = x_ref[...] + _rmsnorm(_dot(mixed, wo_ref[...]), gmix_ref[...])
    qm = _dot(_rmsnorm(x1, gpre_ref[...]).astype(BF16), wq_ref[...]).astype(BF16)
    heads = []
    for h in range(MEM_HEADS):
        cols = slice(h * HEAD_DIM, (h + 1) * HEAD_DIM)
        s = _dot_nt(qm[:, cols], mk_ref[:, cols].astype(BF16)) * scale
        p = jnp.exp(s - jnp.max(s, axis=-1, keepdims=True))
        o = _dot(p.astype(BF16), mv_ref[:, cols].astype(BF16))
        heads.append((o / jnp.sum(p, axis=-1, keepdims=True)).astype(BF16))
    y = _dot(jnp.concatenate(heads, axis=-1), wxo_ref[...])
    x2_ref[...] = x1 + _rmsnorm(y, gmem_ref[...])

    _attn_sample_kernel(*sample_in, attn_s_ref, group=sample_group)


def _mix_mem_prompt(pool_out, attn, x, mk, mv, w_out, g_mix_post, g_mem_pre, w_xq, w_xo, g_mem_post,
                    sample_attn, *, tm, seq, casts=()):
    m, d = x.shape
    n_tiles = m // tm
    tiles_per_seq = seq // tm
    half = pool_out.shape[1]
    mem_spec = pl.BlockSpec((N_MEM, MEM_WIDTH), lambda i: (i // tiles_per_seq, 0))
    gain_spec = _resident((1, d), lambda i: (0, 0))
    n_sample = sample_attn[0].shape[0]
    assert n_sample % n_tiles == 0
    sample_group = n_sample // n_tiles
    s_in_specs, s_out_spec, s_out_shape, s_args = _attn_sample_operands(*sample_attn, group=sample_group)
    (x2, attn_s), cast_out = _call_with_casts(
        functools.partial(_mix_mem_prompt_kernel, sample_group=sample_group),
        grid=(n_tiles,),
        in_specs=[
            pl.BlockSpec((tm, half), lambda i: (i, 0)),
            pl.BlockSpec((N_HEADS, tm, HEAD_DIM), lambda i: (0, i, 0)),
            pl.BlockSpec((tm, d), lambda i: (i, 0)),
            mem_spec,
            mem_spec,
            _resident(w_out.shape, lambda i: (0, 0)),
            gain_spec,
            gain_spec,
            _resident(w_xq.shape, lambda i: (0, 0)),
            _resident(w_xo.shape, lambda i: (0, 0)),
            gain_spec,
        ] + s_in_specs,
        out_specs=[pl.BlockSpec((tm, d), lambda i: (i, 0)), s_out_spec],
        out_shape=[jax.ShapeDtypeStruct((m, d), F32), s_out_shape],
        args=(pool_out, attn, x, mk, mv, w_out, g_mix_post, g_mem_pre, w_xq, w_xo, g_mem_post) + tuple(s_args),
        casts=casts,
        name="mix_mem_prompt",
    )
    return x2, attn_s, cast_out


def _mem_attn_sample_kernel(qm_ref, mk_ref, mv_ref, o_ref, *, group):
    heads = qm_ref.shape[1]
    both = lambda a: jnp.concatenate([a, a], axis=0)
    fold = lambda a: a[0:heads] + a[heads:]
    for j in range(group):
        q = both(qm_ref[j] * (HEAD_DIM ** -0.5 * LOG2_E))
        t = jnp.sum(mk_ref[j] * q[None], axis=-1, keepdims=True)
        m = jnp.max(t, axis=0)
        m = both(jnp.maximum(m[0:heads], m[heads:]))
        p = jnp.exp2(t - m[None])
        o_ref[j] = fold(jnp.sum(p * mv_ref[j], axis=0)) / fold(jnp.sum(p, axis=0))


def _mem_attn_sample(qm, mem_k, mem_v):
    n, heads, _ = qm.shape
    group = SAMPLE_GROUP
    row = pl.BlockSpec((group, heads, HEAD_DIM), lambda i: (i, 0, 0))
    mem = pl.BlockSpec((group, N_MEM // 2, 2 * heads, HEAD_DIM), lambda i: (i, 0, 0, 0))
    pairs = lambda a: a.reshape(n, N_MEM // 2, 2 * heads, HEAD_DIM)
    return pl.pallas_call(
        functools.partial(_mem_attn_sample_kernel, group=group),
        grid=(n // group,),
        in_specs=[row, mem, mem],
        out_specs=row,
        out_shape=jax.ShapeDtypeStruct((n, heads, HEAD_DIM), F32),
        compiler_params=_params(1),
        name="mem_attn_sample",
    )(qm, pairs(mem_k), pairs(mem_v))


def _proj_norm_residual_kernel(a_ref, w_ref, g_ref, r_ref, o_ref):
    y = _dot(a_ref[...].astype(BF16), w_ref[...])
    o_ref[...] = r_ref[...] + _rmsnorm(y, g_ref[...])


def _proj_norm_residual(a, w, g, resid, *, name):
    return pl.pallas_call(
        _proj_norm_residual_kernel,
        out_shape=jax.ShapeDtypeStruct(resid.shape, F32),
        compiler_params=pltpu.CompilerParams(vmem_limit_bytes=VMEM_LIMIT_BYTES),
        name=name,
    )(a, w, g, resid)


def _ffn_kernel(x_ref, xnext_ref, xs_ref, gpre_ref, w1_ref, w2_ref, gpost_ref, o_ref, os_ref,
                xn_ref, xns_ref, acc_ref, accs_ref, *, tm, nf):
    i, f = pl.program_id(0), pl.program_id(1)
    last_f = nf - 1
    chunk = tm // nf

    @pl.when((i == 0) & (f == 0))
    def _():
        xn_ref[0] = _rmsnorm(x_ref[...], gpre_ref[...]).astype(BF16)
        xns_ref[...] = _rmsnorm(xs_ref[...], gpre_ref[...]).astype(BF16)
        accs_ref[...] = jnp.zeros_like(accs_ref)

    @pl.when(f == 0)
    def _():
        acc_ref[...] = jnp.zeros_like(acc_ref)

    def mlp(lhs):
        hidden = jnp.square(jnp.maximum(_dot(lhs, w1_ref[...]), 0.0)).astype(BF16)
        return _dot(hidden, w2_ref[...])

    def norm_next_rows():
        rows = pl.ds(pl.multiple_of(f * chunk, chunk), chunk)
        xn_ref[(i + 1) % 2, rows, :] = _rmsnorm(xnext_ref[rows, :], gpre_ref[...]).astype(BF16)

    @pl.when(i == 0)
    def _():
        both = mlp(jnp.concatenate([xn_ref[0], xns_ref[...]], axis=0))
        acc_ref[...] += both[0:tm]
        accs_ref[...] += both[tm:]
        norm_next_rows()

    @pl.when(i > 0)
    def _():
        acc_ref[...] += mlp(xn_ref[i % 2])
        norm_next_rows()

    @pl.when(f == last_f)
    def _():
        o_ref[...] = x_ref[...] + _rmsnorm(acc_ref[...], gpost_ref[...])

    @pl.when((f == last_f) & (i == 0))
    def _():
        os_ref[...] = xs_ref[...] + _rmsnorm(accs_ref[...], gpost_ref[...])


def _ffn(x, xs, g_pre, w1_chunks, w2, g_post, *, tm):
    m, d = x.shape
    ms = xs.shape[0]
    nf, _, tf = w1_chunks.shape
    n_tiles = m // tm
    assert tm % nf == 0 and (tm // nf) % 16 == 0
    sample_spec = _resident((ms, d), lambda i, f: (0, 0))
    return pl.pallas_call(
        functools.partial(_ffn_kernel, tm=tm, nf=nf),
        grid=(n_tiles, nf),
        in_specs=[
            pl.BlockSpec((tm, d), lambda i, f: (i, 0)),
            pl.BlockSpec((tm, d), lambda i, f: (jnp.minimum(i + 1, n_tiles - 1), 0)),
            sample_spec,
            _resident((1, d), lambda i, f: (0, 0)),
            pl.BlockSpec((None, d, tf), lambda i, f: (f, 0, 0)),
            pl.BlockSpec((tf, d), lambda i, f: (f, 0)),
            _resident((1, d), lambda i, f: (0, 0)),
        ],
        out_specs=[pl.BlockSpec((tm, d), lambda i, f: (i, 0)),
                   pl.BlockSpec((ms, d), lambda i, f: (0, 0))],
        out_shape=[jax.ShapeDtypeStruct((m, d), F32), jax.ShapeDtypeStruct((ms, d), F32)],
        scratch_shapes=[pltpu.VMEM((2, tm, d), BF16), pltpu.VMEM((ms, d), BF16),
                        pltpu.VMEM((tm, d), F32), pltpu.VMEM((ms, d), F32)],
        compiler_params=_params(2),
        name="ffn",
    )(x, x, xs, g_pre, w1_chunks, w2, g_post)


def kernel(x_prompt, x_sample, state_pool, cache_attn_k, cache_attn_v, cache_mem_k, cache_mem_v, mem_prompt,
           g_mix_pre, g_mix_post, g_mem_pre, g_mem_post, g_ffn_pre, g_ffn_post, g_mem_kv,
           w_in, w_pool, pool_scale, w_out, w_xq, w_mem_kv, w_xo, w_ff1, w_ff2):
    depth = w_in.shape[0]
    assert depth == 1
    batch, seq, d = x_prompt.shape
    dec_batch, dec_seq, _ = x_sample.shape
    assert dec_seq == 1

    l = 0
    scale = pool_scale[l][None, :]
    gain = lambda g: g[l][None, :]

    tm = 512
    xp = x_prompt.reshape(batch * seq, d)
    xs = x_sample.reshape(dec_batch, d)
    cos_p, sin_p = _rope_tables(jnp.arange(seq, dtype=jnp.int32))
    cos_s, sin_s = _rope_tables(jnp.full((dec_batch,), PAST_LEN, jnp.int32))
    tm_mem = 128
    ones, zeros = jnp.ones((tm_mem, HEAD_DIM), F32), jnp.zeros((tm_mem, HEAD_DIM), F32)
    w_pool_f = w_pool[l]


    mem = mem_prompt.reshape(batch * N_MEM, d)
    (mk, mv, mk_h, mv_h), (w_in_b,) = _norm_proj(
        mem, gain(g_mem_kv), w_mem_kv[l], ones, zeros, tm=tm_mem, ncol=MEM_WIDTH, rope_blocks=(),
        outputs=((0, FLAT), (1, FLAT), (0, HEADS), (1, HEADS)), name="mem_kv", casts=(w_in[l],))

    ((pool_out, q, k, v, k_h, v_h, pool_state, us, qs_h, ks_h, vs_h),
     (w_out_b, w_xq_b, w_xo_b, w_ff1_b)) = _in_proj_pool(
        xp, xs, gain(g_mix_pre), w_in_b, cos_p, sin_p, cos_s, sin_s, w_pool_f, scale, tm=256, seq=seq,
        casts=(w_out[l], w_xq[l], w_xo[l], (w_ff1[l], w_ff1.shape[2] // FFN_CHUNK)))
    attn, (w_ff2_b,) = _attn_prompt(q, k, v, seq=seq, casts=(w_ff2[l],))
    heads3 = lambda a: a.reshape(dec_batch, -1, HEAD_DIM)
    x2, attn_s, _ = _mix_mem_prompt(
        pool_out, attn, xp, mk, mv, w_out_b, gain(g_mix_post), gain(g_mem_pre), w_xq_b, w_xo_b, gain(g_mem_post),
        (heads3(qs_h), heads3(ks_h), heads3(vs_h), cache_attn_k[l], cache_attn_v[l]), tm=tm, seq=seq)

    pool_out_s, new_pool_s = _pool_sample(us, state_pool[l], w_pool_f, scale, pos=PAST_LEN)
    x1s, qms_h = _out_proj(pool_out_s, attn_s.reshape(dec_batch, ATTN_WIDTH), xs, w_out_b, gain(g_mix_post),
                           gain(g_mem_pre), w_xq_b, tm=dec_batch, qm_layout=HEADS, name="out_proj_sample")
    mem_o_s = _mem_attn_sample(heads3(qms_h), cache_mem_k[l], cache_mem_v[l])
    x2s = _proj_norm_residual(mem_o_s.reshape(dec_batch, MEM_WIDTH), w_xo_b, gain(g_mem_post), x1s,
                              name="mem_out_sample")

    yp, ys = _ffn(x2, x2s, gain(g_ffn_pre), w_ff1_b, w_ff2_b, gain(g_ffn_post), tm=tm)

    keep = min(max(w for w, _ in DILATED), seq)
    return (
        yp.reshape(batch, seq, d),
        ys.reshape(dec_batch, 1, d),
        pool_state.reshape(batch, STATE_ROWS, POOL_WIDTH)[:, STATE_ROWS - POOL_STATE:][None],
        new_pool_s[None],
        k_h.reshape(batch, seq, N_HEADS, HEAD_DIM)[:, seq - keep:][None],
        v_h.reshape(batch, seq, N_HEADS, HEAD_DIM)[:, seq - keep:][None],
        ks_h.reshape(dec_batch, 1, N_HEADS, HEAD_DIM)[None],
        vs_h.reshape(dec_batch, 1, N_HEADS, HEAD_DIM)[None],
        mk_h.reshape(batch, N_MEM, MEM_HEADS, HEAD_DIM)[None],
        mv_h.reshape(batch, N_MEM, MEM_HEADS, HEAD_DIM)[None],
    )
```

```python
import functools

import jax
import jax.numpy as jnp
from jax import lax
from jax.experimental import pallas as pl
from jax.experimental.pallas import tpu as pltpu

D_MODEL = 2048
POOL_WIDTH = 1024
POOL_WINDOWS = (2, 4, 8, 16)
POOL_GROUP = POOL_WIDTH // len(POOL_WINDOWS)
POOL_STATE = max(POOL_WINDOWS) - 1
HEAD_DIM = 128
ATTN_WIDTH = 1024
N_HEADS = ATTN_WIDTH // HEAD_DIM
DILATED = ((128, 1), (512, 4), (2048, 16))
ROPE_DIM = HEAD_DIM // 4
ROPE_HALF = ROPE_DIM // 2
ROPE_THETA = 500000.0
N_MEM = 256
MEM_HEADS = 4
MEM_WIDTH = MEM_HEADS * HEAD_DIM
EPS = 1e-6
BLOCK = 128
NEG_INF = -1e30
LOG2_E = 1.4426950408889634
PAST_LEN = 8192
HALO = 32
STATE_ROWS = 16
SAMPLE_GROUP = 8

F32 = jnp.float32
BF16 = jnp.bfloat16

VMEM_LIMIT_BYTES = 56 * 1024 * 1024
FFN_CHUNK = 1024

FLAT, HEADS = "flat", "heads"


def _params(n_grid_axes):
    return pltpu.CompilerParams(
        dimension_semantics=("arbitrary",) * n_grid_axes,
        vmem_limit_bytes=VMEM_LIMIT_BYTES,
    )


def _resident(block_shape, index_map):
    return pl.BlockSpec(block_shape, index_map, pipeline_mode=pl.Buffered(1))


def _rmsnorm(x, g):
    ms = jnp.mean(x * x, axis=-1, keepdims=True)
    return x * lax.rsqrt(ms + EPS) * g


def _dot(a, b):
    return jnp.dot(a, b, preferred_element_type=F32)


def _dot_nt(a, b):
    return lax.dot_general(a, b, (((1,), (1,)), ((), ())), preferred_element_type=F32)


def _call_with_casts(kernel_fn, *, grid, in_specs, out_specs, out_shape, scratch_shapes=(), args, casts=(),
                     step_of=None, name):
    casts = [c if isinstance(c, tuple) else (c, 1) for c in casts]
    n_in, n_out, n_casts = len(in_specs), len(out_specs), len(casts)
    n_steps = 1
    for g in grid:
        n_steps *= g
    if step_of is None:
        step_of = lambda i: i
    cast_in_specs, cast_out_specs, cast_shapes = [], [], []
    for a, n_chunks in casts:
        rows, cols = a.shape
        assert rows % n_steps == 0 and cols % n_chunks == 0
        cast_in_specs.append(pl.BlockSpec((rows // n_steps, cols), lambda *idx: (step_of(*idx), 0)))
        if n_chunks == 1:
            cast_out_specs.append(cast_in_specs[-1])
            cast_shapes.append(jax.ShapeDtypeStruct(a.shape, BF16))
        else:
            cast_out_specs.append(pl.BlockSpec((n_chunks, rows // n_steps, cols // n_chunks),
                                               lambda *idx: (0, step_of(*idx), 0)))
            cast_shapes.append(jax.ShapeDtypeStruct((n_chunks, rows, cols // n_chunks), BF16))

    def body(*refs):
        ins, cast_in = refs[:n_in], refs[n_in:n_in + n_casts]
        outs = refs[n_in + n_casts:n_in + n_casts + n_out]
        cast_out = refs[n_in + n_casts + n_out:n_in + 2 * n_casts + n_out]

        def run_casts():
            for i_ref, o_ref, (_, n_chunks) in zip(cast_in, cast_out, casts):
                if n_chunks == 1:
                    o_ref[...] = i_ref[...].astype(BF16)
                else:
                    width = i_ref.shape[1] // n_chunks
                    for c in range(n_chunks):
                        o_ref[c] = i_ref[:, c * width:(c + 1) * width].astype(BF16)

        kernel_fn(run_casts, *ins, *outs, *refs[n_in + 2 * n_casts + n_out:])

    res = pl.pallas_call(
        body,
        grid=grid,
        in_specs=list(in_specs) + cast_in_specs,
        out_specs=list(out_specs) + cast_out_specs,
        out_shape=list(out_shape) + cast_shapes,
        scratch_shapes=list(scratch_shapes),
        compiler_params=_params(len(grid)),
        name=name,
    )(*args, *[a for a, _ in casts])
    return res[:n_out], res[n_out:]


def _store_heads(o_ref, layout, h, heads, rows, value):
    if layout == FLAT:
        o_ref[:, h * HEAD_DIM:(h + 1) * HEAD_DIM] = value
    else:
        o_ref[pl.ds(h, rows, stride=heads), :] = value


def _out_block(layout, tm, heads):
    return (tm, heads * HEAD_DIM) if layout == FLAT else (tm * heads, HEAD_DIM)


def _norm_proj_kernel(run_casts, x_ref, g_ref, w_ref, cos_ref, sin_ref, *out_refs, tm, ncol, rope_blocks, outputs):
    run_casts()
    heads = ncol // HEAD_DIM
    xn = _rmsnorm(x_ref[...], g_ref[...]).astype(BF16)
    for c in sorted({c for c, _ in outputs}):
        acc = _dot(xn, w_ref[:, c * ncol:(c + 1) * ncol].astype(BF16))
        if c in rope_blocks:
            cos = cos_ref[...]
            sin = sin_ref[...]
            lane = lax.broadcasted_iota(jnp.int32, cos.shape, 1)
        for h in range(heads):
            a = acc[:, h * HEAD_DIM:(h + 1) * HEAD_DIM]
            if c in rope_blocks:
                partner = jnp.where(lane < ROPE_HALF,
                                    pltpu.roll(a, HEAD_DIM - ROPE_HALF, 1),
                                    pltpu.roll(a, ROPE_HALF, 1))
                a = a * cos + partner * sin
            for (oc, layout), o_ref in zip(outputs, out_refs):
                if oc == c:
                    _store_heads(o_ref, layout, h, heads, tm, a)


def _norm_proj(x, g, w, cos, sin, *, tm, ncol, rope_blocks, outputs, name, casts=()):
    m, d = x.shape
    heads = ncol // HEAD_DIM
    assert w.shape[0] == d and w.shape[1] % ncol == 0 and m % tm == 0
    table_tiles = cos.shape[0] // tm
    kern = functools.partial(_norm_proj_kernel, tm=tm, ncol=ncol, rope_blocks=rope_blocks, outputs=outputs)
    blocks = [_out_block(layout, tm, heads) for _, layout in outputs]
    return _call_with_casts(
        kern,
        grid=(m // tm,),
        in_specs=[
            pl.BlockSpec((tm, d), lambda i: (i, 0)),
            _resident((1, d), lambda i: (0, 0)),
            _resident(w.shape, lambda i: (0, 0)),
            pl.BlockSpec((tm, HEAD_DIM), lambda i: (i % table_tiles, 0)),
            pl.BlockSpec((tm, HEAD_DIM), lambda i: (i % table_tiles, 0)),
        ],
        out_specs=[pl.BlockSpec(b, lambda i: (i, 0)) for b in blocks],
        out_shape=[jax.ShapeDtypeStruct((m // tm * b[0], b[1]), F32) for b in blocks],
        args=(x, g, w, cos, sin),
        casts=casts,
        name=name,
    )


def _rope_tables(pos):
    inv = jnp.power(jnp.float32(ROPE_THETA), -jnp.arange(ROPE_HALF, dtype=F32) * 2.0 / ROPE_DIM)
    ang = pos.astype(F32)[:, None] * inv[None, :]
    cos, sin = jnp.cos(ang), jnp.sin(ang)
    n = pos.shape[0]
    pad = HEAD_DIM - ROPE_DIM
    cos_t = jnp.concatenate([cos, cos, jnp.ones((n, pad), F32)], axis=1)
    sin_t = jnp.concatenate([-sin, sin, jnp.zeros((n, pad), F32)], axis=1)
    return cos_t, sin_t


def _window_sums(ext_ref, lvl_refs, g, w, rows):
    cols = slice(g * POOL_GROUP, (g + 1) * POOL_GROUP)
    levels = w.bit_length() - 1
    src, src_cols = ext_ref, cols
    for level in range(levels):
        shift = 1 << level
        last = level == levels - 1
        lo = HALO if last else 8 * (level + 1)
        assert lo - shift >= 8 * level
        total = src[lo:rows, src_cols] + src[lo - shift:rows - shift, src_cols]
        if last:
            return total
        lvl_refs[level % 2][lo:rows, :] = total
        src, src_cols = lvl_refs[level % 2], slice(None)


def _rope(a, cos, sin):
    lane = lax.broadcasted_iota(jnp.int32, a.shape, 1)
    partner = jnp.where(lane < ROPE_HALF, pltpu.roll(a, HEAD_DIM - ROPE_HALF, 1), pltpu.roll(a, ROPE_HALF, 1))
    return a * cos + partner * sin


def _in_proj_pool_kernel(run_casts, x_ref, xnext_ref, xs_ref, g_ref, w_ref, cos_ref, sin_ref, coss_ref, sins_ref,
                         wp_ref, scale_ref,
                         pool_ref, q_ref, k_ref, v_ref, kh_ref, vh_ref, state_ref, us_ref, qsh_ref, ksh_ref, vsh_ref,
                         xn_ref, ext_ref, lvl_a_ref, lvl_b_ref, *, tm, tiles_per_seq):
    step = pl.program_id(0)
    t_in_seq = step % tiles_per_seq
    n_sample = xs_ref.shape[0]

    def tile(xn, with_sample):
        run_casts()
        u = _dot(xn, w_ref[:, 0:POOL_WIDTH])
        ext_ref[HALO:HALO + tm, :] = u[0:tm]
        if with_sample:
            us_ref[...] = u[tm:]
        cos, sin = cos_ref[...], sin_ref[...]
        outs = ((q_ref, None, qsh_ref), (k_ref, kh_ref, ksh_ref), (v_ref, vh_ref, vsh_ref))
        for c, (major_ref, heads_ref, sample_ref) in enumerate(outs, start=1):
            acc = _dot(xn, w_ref[:, c * ATTN_WIDTH:(c + 1) * ATTN_WIDTH])
            for h in range(N_HEADS):
                a = acc[0:tm, h * HEAD_DIM:(h + 1) * HEAD_DIM]
                if major_ref is not v_ref:
                    a = _rope(a, cos, sin)
                major_ref[h] = a
                if heads_ref is not None:
                    _store_heads(heads_ref, HEADS, h, N_HEADS, tm, a)
                if with_sample:
                    a = acc[tm:, h * HEAD_DIM:(h + 1) * HEAD_DIM]
                    if major_ref is not v_ref:
                        a = _rope(a, coss_ref[...], sins_ref[...])
                    _store_heads(sample_ref, HEADS, h, N_HEADS, n_sample, a)

        pos = t_in_seq * tm + lax.broadcasted_iota(jnp.int32, (tm, 1), 0)
        for g, w in enumerate(POOL_WINDOWS):
            cols = slice(g * POOL_GROUP, (g + 1) * POOL_GROUP)
            wsum = _window_sums(ext_ref, (lvl_a_ref, lvl_b_ref), g, w, HALO + tm)
            cnt = jnp.minimum(w, pos + 1).astype(F32)
            pooled = (wsum / cnt - ext_ref[HALO:HALO + tm, cols]).astype(BF16)
            pool_ref[:, cols] = (_dot(pooled, wp_ref[g].astype(BF16)) * scale_ref[:, cols]).astype(pool_ref.dtype)
        state_ref[...] = ext_ref[HALO + tm - STATE_ROWS:HALO + tm, :]
        ext_ref[0:HALO, :] = ext_ref[tm:tm + HALO, :]

        xn_ref[(step + 1) % 2] = _rmsnorm(xnext_ref[...], g_ref[...]).astype(BF16)

    @pl.when(t_in_seq == 0)
    def _():
        ext_ref[0:HALO, :] = jnp.zeros((HALO, POOL_WIDTH), F32)

    @pl.when(step == 0)
    def _():
        rows = jnp.concatenate([x_ref[...], xs_ref[...]], axis=0)
        tile(_rmsnorm(rows, g_ref[...]).astype(BF16), True)

    @pl.when(step > 0)
    def _():
        tile(xn_ref[step % 2], False)


def _in_proj_pool(x, xs, g, w, cos, sin, cos_s, sin_s, w_pool, pool_scale, *, tm, seq, casts=()):
    m, d = x.shape
    ms = xs.shape[0]
    tiles_per_seq = seq // tm
    n_seq = m // seq
    assert POOL_WIDTH == ATTN_WIDTH and w.shape == (d, POOL_WIDTH + 3 * ATTN_WIDTH)
    kern = functools.partial(_in_proj_pool_kernel, tm=tm, tiles_per_seq=tiles_per_seq)
    row_tile = lambda width: pl.BlockSpec((tm, width), lambda i: (i, 0))
    heads_tile = pl.BlockSpec((tm * N_HEADS, HEAD_DIM), lambda i: (i, 0))
    table = pl.BlockSpec((tm, HEAD_DIM), lambda i: (i % tiles_per_seq, 0))
    whole = lambda shape: _resident(shape, lambda i: (0,) * len(shape))
    head_major = jax.ShapeDtypeStruct((N_HEADS, m, HEAD_DIM), F32)
    head_major_tile = pl.BlockSpec((N_HEADS, tm, HEAD_DIM), lambda i: (0, i, 0))
    by_heads = jax.ShapeDtypeStruct((m * N_HEADS, HEAD_DIM), F32)
    sample_heads = jax.ShapeDtypeStruct((ms * N_HEADS, HEAD_DIM), F32)
    sample_heads_spec = pl.BlockSpec((ms * N_HEADS, HEAD_DIM), lambda i: (0, 0))
    return _call_with_casts(
        kern,
        grid=(m // tm,),
        in_specs=[
            row_tile(d),
            pl.BlockSpec((tm, d), lambda i: (jnp.minimum(i + 1, m // tm - 1), 0)),
            whole((ms, d)),
            whole((1, d)),
            whole(w.shape),
            table,
            table,
            whole((ms, HEAD_DIM)),
            whole((ms, HEAD_DIM)),
            whole(w_pool.shape),
            whole((1, POOL_WIDTH)),
        ],
        out_specs=[row_tile(POOL_WIDTH), head_major_tile, head_major_tile, head_major_tile,
                   heads_tile, heads_tile,
                   pl.BlockSpec((STATE_ROWS, POOL_WIDTH), lambda i: (i // tiles_per_seq, 0)),
                   pl.BlockSpec((ms, POOL_WIDTH), lambda i: (0, 0)),
                   sample_heads_spec, sample_heads_spec, sample_heads_spec],
        out_shape=[jax.ShapeDtypeStruct((m, POOL_WIDTH), BF16), head_major, head_major, head_major, by_heads, by_heads,
                   jax.ShapeDtypeStruct((n_seq * STATE_ROWS, POOL_WIDTH), F32),
                   jax.ShapeDtypeStruct((ms, POOL_WIDTH), F32), sample_heads, sample_heads, sample_heads],
        scratch_shapes=[pltpu.VMEM((2, tm, d), BF16),
                        pltpu.VMEM((HALO + tm, POOL_WIDTH), F32),
                        pltpu.VMEM((HALO + tm, POOL_GROUP), F32),
                        pltpu.VMEM((HALO + tm, POOL_GROUP), F32)],
        args=(x, x, xs, g, w, cos, sin, cos_s, sin_s, w_pool, pool_scale),
        casts=casts,
        name="in_proj_pool",
    )


def _pool_sample_kernel(u_ref, prev_ref, wp_ref, scale_ref, o_ref, state_ref, *, pos):
    for j in range(POOL_STATE - 1):
        state_ref[:, j, :] = prev_ref[:, j + 1, :]
    state_ref[:, POOL_STATE - 1, :] = u_ref[...]
    for g, w in enumerate(POOL_WINDOWS):
        cols = slice(g * POOL_GROUP, (g + 1) * POOL_GROUP)
        wsum = u_ref[:, cols]
        for back in range(1, w):
            wsum = wsum + prev_ref[:, POOL_STATE - back, cols]
        cnt = float(min(w, pos + 1))
        pooled = (wsum / cnt - u_ref[:, cols]).astype(BF16)
        o_ref[:, cols] = (_dot(pooled, wp_ref[g].astype(BF16)) * scale_ref[:, cols]).astype(o_ref.dtype)


def _pool_sample(u, prev, w_pool, pool_scale, *, pos):
    n, c = u.shape
    kern = functools.partial(_pool_sample_kernel, pos=pos)
    return pl.pallas_call(
        kern,
        out_shape=[jax.ShapeDtypeStruct((n, c), BF16), jax.ShapeDtypeStruct(prev.shape, F32)],
        compiler_params=pltpu.CompilerParams(vmem_limit_bytes=VMEM_LIMIT_BYTES),
        name="pool_sample",
    )(u, prev, w_pool, pool_scale)


def _run_blocks(count, unroll, body):
    trips = count // unroll
    if trips > 1:
        def trip(t, carry):
            for j in range(unroll):
                body(t * unroll + j)
            return carry
        lax.fori_loop(0, trips, trip, 0)
        done = trips * unroll
    else:
        done = 0
    for idx in range(done, count):
        body(idx)


def _attn_prompt_kernel(run_casts, q_ref, k_ref, v_ref, o_ref, *scratch, seq, unroll):
    run_casts()
    n_br = len(DILATED)
    ob_refs, mb_refs, sb_refs = scratch[:n_br], scratch[n_br:2 * n_br], scratch[2 * n_br:3 * n_br]
    p_ref, bias_ref = scratch[3 * n_br:]
    exp2_scale = HEAD_DIM ** -0.5 * LOG2_E
    qi = lax.broadcasted_iota(jnp.int32, (BLOCK, 2 * BLOCK), 0)
    kj = lax.broadcasted_iota(jnp.int32, (BLOCK, 2 * BLOCK), 1)
    in_band = ((kj < BLOCK) & (kj >= qi)) | ((kj >= BLOCK) & (kj - BLOCK <= qi))
    bias_ref[...] = jnp.where(in_band, 0.0, NEG_INF)
    ones = jnp.ones((2 * BLOCK, HEAD_DIM), BF16)

    def rows(block, r, d):
        if d == 1:
            start = block * BLOCK
            return pl.ds(start if isinstance(start, int) else pl.multiple_of(start, BLOCK), BLOCK)
        return pl.ds(block * (BLOCK * d) + r, BLOCK, stride=d)

    def keys(ref, block, r, d, with_prev):
        cur = ref[rows(block, r, d), :]
        if not with_prev:
            return cur.astype(BF16)
        return jnp.concatenate([ref[rows(block - 1, r, d), :], cur], axis=0).astype(BF16)

    for g, (_, d) in enumerate(DILATED):
        n_blocks = seq // d // BLOCK
        for with_prev in (False, True):
            per_r = n_blocks - 1 if with_prev else 1
            count = d * per_r
            if count == 0:
                continue
            cols = slice(0, 2 * BLOCK) if with_prev else slice(BLOCK, 2 * BLOCK)
            width = 2 * BLOCK if with_prev else BLOCK

            def locate(idx, per_r=per_r, with_prev=with_prev):
                return (idx % per_r + 1, idx // per_r) if with_prev else (0, idx)

            def probabilities(idx, g=g, d=d, with_prev=with_prev, cols=cols, width=width, locate=locate):
                b, r = locate(idx)
                q = q_ref[rows(b, r, d), :].astype(BF16)
                s = _dot_nt(q, keys(k_ref, b, r, d, with_prev)) + bias_ref[:, cols]
                m = jnp.max(s, axis=-1, keepdims=True)
                p_ref[idx, :, 0:width] = jnp.exp2((s - m) * exp2_scale).astype(BF16)
                mb_refs[g][rows(b, r, d), :] = jnp.broadcast_to(m, (BLOCK, HEAD_DIM))

            def values(idx, g=g, d=d, with_prev=with_prev, width=width, locate=locate):
                b, r = locate(idx)
                v1 = jnp.concatenate([keys(v_ref, b, r, d, with_prev), ones[0:width]], axis=1)
                acc = _dot(p_ref[idx, :, 0:width], v1)
                ob_refs[g][rows(b, r, d), :] = acc[:, 0:HEAD_DIM]
                sb_refs[g][rows(b, r, d), :] = acc[:, HEAD_DIM:]

            _run_blocks(count, unroll, probabilities)
            _run_blocks(count, unroll, values)

    chunk = 256

    def combine(c, carry):
        rw = pl.ds(pl.multiple_of(c * chunk, chunk), chunk)
        ms = [mb[rw, :] for mb in mb_refs]
        m = functools.reduce(jnp.maximum, ms)
        ws = [jnp.exp2((mg - m) * exp2_scale) for mg in ms]
        num = functools.reduce(jnp.add, [w * ob[rw, :] for w, ob in zip(ws, ob_refs)])
        den = functools.reduce(jnp.add, [w * sb[rw, :] for w, sb in zip(ws, sb_refs)])
        o_ref[rw, :] = (num / den).astype(o_ref.dtype)
        return carry

    lax.fori_loop(0, seq // chunk, combine, 0)


def _attn_prompt(q, k, v, *, seq, casts=()):
    n_heads, m, _ = q.shape
    n_seq = m // seq
    spec = pl.BlockSpec((None, seq, HEAD_DIM), lambda n, h: (h, n, 0))
    kern = functools.partial(_attn_prompt_kernel, seq=seq, unroll=16)
    (attn,), cast_out = _call_with_casts(
        kern,
        grid=(n_seq, n_heads),
        in_specs=[spec, spec, spec],
        out_specs=[spec],
        out_shape=[jax.ShapeDtypeStruct(q.shape, BF16)],
        scratch_shapes=(
            [pltpu.VMEM((seq, HEAD_DIM), F32) for _ in range(3 * len(DILATED))]
            + [pltpu.VMEM((seq // BLOCK, BLOCK, 2 * BLOCK), BF16),
               pltpu.VMEM((BLOCK, 2 * BLOCK), F32)]),
        args=(q, k, v),
        casts=casts,
        step_of=lambda n, h: n * n_heads + h,
        name="attn_prompt",
    )
    return attn, cast_out


def _attn_sample_kernel(q_ref, kn_ref, vn_ref, *refs, group):
    n_br = len(DILATED)
    kc_refs, vc_refs, o_ref = refs[:n_br], refs[n_br:2 * n_br], refs[2 * n_br]
    for j in range(group):
        q = q_ref[j] * (HEAD_DIM ** -0.5 * LOG2_E)
        t_new = jnp.sum(q * kn_ref[j], axis=-1, keepdims=True)
        t_br = [jnp.sum(kc[j] * q[None], axis=-1, keepdims=True) for kc in kc_refs]
        m = t_new
        for t in t_br:
            m = jnp.maximum(m, jnp.max(t, axis=0))
        p_new = jnp.exp2(t_new - m) * float(n_br)
        den = p_new
        num = p_new * vn_ref[j]
        for t, vc in zip(t_br, vc_refs):
            p = jnp.exp2(t - m[None])
            den = den + jnp.sum(p, axis=0)
            num = num + jnp.sum(p * vc[j], axis=0)
        o_ref[j] = num / den


def _attn_sample_operands(q, k_new, v_new, k_cache, v_cache, *, group):
    n, heads, _ = q.shape
    win = k_cache.shape[1]
    row = pl.BlockSpec((group, heads, HEAD_DIM), lambda i: (i, 0, 0))
    cache_specs, k_views, v_views = [], [], []
    for w, d in DILATED:
        assert w == BLOCK * d and win % w == 0
        last = win // w - 1
        cache_specs.append(pl.BlockSpec((group, BLOCK, None, heads, HEAD_DIM),
                                        lambda i, last=last: (i, last, 0, 0, 0)))
        k_views.append(k_cache.reshape(n, win // d, d, heads, HEAD_DIM))
        v_views.append(v_cache.reshape(n, win // d, d, heads, HEAD_DIM))
    return ([row, row, row] + cache_specs + cache_specs, row, jax.ShapeDtypeStruct((n, heads, HEAD_DIM), F32),
            (q, k_new, v_new, *k_views, *v_views))


def _out_proj_kernel(pool_ref, attn_ref, x_ref, wo_ref, gpost_ref, gpre_ref, wq_ref, x1_ref, qm_ref,
                     *, tm, qm_layout):
    y = _dot(pool_ref[...].astype(BF16), wo_ref[0:POOL_WIDTH, :])
    y = y + _dot(attn_ref[...].astype(BF16), wo_ref[POOL_WIDTH:, :])
    x1 = x_ref[...] + _rmsnorm(y, gpost_ref[...])
    x1_ref[...] = x1
    qm = _dot(_rmsnorm(x1, gpre_ref[...]).astype(BF16), wq_ref[...])
    for h in range(MEM_HEADS):
        _store_heads(qm_ref, qm_layout, h, MEM_HEADS, tm, qm[:, h * HEAD_DIM:(h + 1) * HEAD_DIM])


def _out_proj(pool_out, attn, x, w_out, g_post, g_pre, w_xq, *, tm, qm_layout, name):
    m, d = x.shape
    half = pool_out.shape[1]
    qm_block = _out_block(qm_layout, tm, MEM_HEADS)
    kern = functools.partial(_out_proj_kernel, tm=tm, qm_layout=qm_layout)
    return pl.pallas_call(
        kern,
        grid=(m // tm,),
        in_specs=[
            pl.BlockSpec((tm, half), lambda i: (i, 0)),
            pl.BlockSpec((tm, half), lambda i: (i, 0)),
            pl.BlockSpec((tm, d), lambda i: (i, 0)),
            _resident(w_out.shape, lambda i: (0, 0)),
            _resident((1, d), lambda i: (0, 0)),
            _resident((1, d), lambda i: (0, 0)),
            _resident(w_xq.shape, lambda i: (0, 0)),
        ],
        out_specs=[pl.BlockSpec((tm, d), lambda i: (i, 0)),
                   pl.BlockSpec(qm_block, lambda i: (i, 0))],
        out_shape=[jax.ShapeDtypeStruct((m, d), F32),
                   jax.ShapeDtypeStruct((m // tm * qm_block[0], qm_block[1]), F32)],
        compiler_params=_params(1),
        name=name,
    )(pool_out, attn, x, w_out, g_post, g_pre, w_xq)


def _mix_mem_prompt_kernel(run_casts, pool_ref, attn_ref, x_ref, mk_ref, mv_ref, wo_ref,
                           gmix_ref, gpre_ref, wq_ref, wxo_ref, gmem_ref, *refs, sample_group):
    n_sample_in = 3 + 2 * len(DILATED)
    sample_in, (x2_ref, attn_s_ref) = refs[:n_sample_in], refs[n_sample_in:]
    run_casts()
    scale = HEAD_DIM ** -0.5
    mixed = jnp.concatenate([pool_ref[...]] + [attn_ref[h] for h in range(N_HEADS)], axis=-1)
    x1 = x_ref[...] + _rmsnorm(_dot(mixed, wo_ref[...]), gmix_ref[...])
    qm = _dot(_rmsnorm(x1, gpre_ref[...]).astype(BF16), wq_ref[...]).astype(BF16)
    heads = []
    for h in range(MEM_HEADS):
        cols = slice(h * HEAD_DIM, (h + 1) * HEAD_DIM)
        s = _dot_nt(qm[:, cols], mk_ref[:, cols].astype(BF16)) * scale
        p = jnp.exp(s - jnp.max(s, axis=-1, keepdims=True))
        o = _dot(p.astype(BF16), mv_ref[:, cols].astype(BF16))
        heads.append((o / jnp.sum(p, axis=-1, keepdims=True)).astype(BF16))
    y = _dot(jnp.concatenate(heads, axis=-1), wxo_ref[...])
    x2_ref[...] = x1 + _rmsnorm(y, gmem_ref[...])

    _attn_sample_kernel(*sample_in, attn_s_ref, group=sample_group)


def _mix_mem_prompt(pool_out, attn, x, mk, mv, w_out, g_mix_post, g_mem_pre, w_xq, w_xo, g_mem_post,
                    sample_attn, *, tm, seq, casts=()):
    m, d = x.shape
    n_tiles = m // tm
    tiles_per_seq = seq // tm
    half = pool_out.shape[1]
    mem_spec = pl.BlockSpec((N_MEM, MEM_WIDTH), lambda i: (i // tiles_per_seq, 0))
    gain_spec = _resident((1, d), lambda i: (0, 0))
    n_sample = sample_attn[0].shape[0]
    assert n_sample % n_tiles == 0
    sample_group = n_sample // n_tiles
    s_in_specs, s_out_spec, s_out_shape, s_args = _attn_sample_operands(*sample_attn, group=sample_group)
    (x2, attn_s), cast_out = _call_with_casts(
        functools.partial(_mix_mem_prompt_kernel, sample_group=sample_group),
        grid=(n_tiles,),
        in_specs=[
            pl.BlockSpec((tm, half), lambda i: (i, 0)),
            pl.BlockSpec((N_HEADS, tm, HEAD_DIM), lambda i: (0, i, 0)),
            pl.BlockSpec((tm, d), lambda i: (i, 0)),
            mem_spec,
            mem_spec,
            _resident(w_out.shape, lambda i: (0, 0)),
            gain_spec,
            gain_spec,
            _resident(w_xq.shape, lambda i: (0, 0)),
            _resident(w_xo.shape, lambda i: (0, 0)),
            gain_spec,
        ] + s_in_specs,
        out_specs=[pl.BlockSpec((tm, d), lambda i: (i, 0)), s_out_spec],
        out_shape=[jax.ShapeDtypeStruct((m, d), F32), s_out_shape],
        args=(pool_out, attn, x, mk, mv, w_out, g_mix_post, g_mem_pre, w_xq, w_xo, g_mem_post) + tuple(s_args),
        casts=casts,
        name="mix_mem_prompt",
    )
    return x2, attn_s, cast_out


def _mem_attn_sample_kernel(qm_ref, mk_ref, mv_ref, o_ref, *, group):
    heads = qm_ref.shape[1]
    both = lambda a: jnp.concatenate([a, a], axis=0)
    fold = lambda a: a[0:heads] + a[heads:]
    for j in range(group):
        q = both(qm_ref[j] * (HEAD_DIM ** -0.5 * LOG2_E))
        t = jnp.sum(mk_ref[j] * q[None], axis=-1, keepdims=True)
        m = jnp.max(t, axis=0)
        m = both(jnp.maximum(m[0:heads], m[heads:]))
        p = jnp.exp2(t - m[None])
        o_ref[j] = fold(jnp.sum(p * mv_ref[j], axis=0)) / fold(jnp.sum(p, axis=0))


def _mem_attn_sample(qm, mem_k, mem_v):
    n, heads, _ = qm.shape
    group = SAMPLE_GROUP
    row = pl.BlockSpec((group, heads, HEAD_DIM), lambda i: (i, 0, 0))
    mem = pl.BlockSpec((group, N_MEM // 2, 2 * heads, HEAD_DIM), lambda i: (i, 0, 0, 0))
    pairs = lambda a: a.reshape(n, N_MEM // 2, 2 * heads, HEAD_DIM)
    return pl.pallas_call(
        functools.partial(_mem_attn_sample_kernel, group=group),
        grid=(n // group,),
        in_specs=[row, mem, mem],
        out_specs=row,
        out_shape=jax.ShapeDtypeStruct((n, heads, HEAD_DIM), F32),
        compiler_params=_params(1),
        name="mem_attn_sample",
    )(qm, pairs(mem_k), pairs(mem_v))


def _proj_norm_residual_kernel(a_ref, w_ref, g_ref, r_ref, o_ref):
    y = _dot(a_ref[...].astype(BF16), w_ref[...])
    o_ref[...] = r_ref[...] + _rmsnorm(y, g_ref[...])


def _proj_norm_residual(a, w, g, resid, *, name):
    return pl.pallas_call(
        _proj_norm_residual_kernel,
        out_shape=jax.ShapeDtypeStruct(resid.shape, F32),
        compiler_params=pltpu.CompilerParams(vmem_limit_bytes=VMEM_LIMIT_BYTES),
        name=name,
    )(a, w, g, resid)


def _ffn_kernel(x_ref, xnext_ref, xs_ref, gpre_ref, w1_ref, w2_ref, gpost_ref, o_ref, os_ref,
                xn_ref, xns_ref, acc_ref, accs_ref, *, tm, nf):
    i, f = pl.program_id(0), pl.program_id(1)
    last_f = nf - 1
    chunk = tm // nf

    @pl.when((i == 0) & (f == 0))
    def _():
        xn_ref[0] = _rmsnorm(x_ref[...], gpre_ref[...]).astype(BF16)
        xns_ref[...] = _rmsnorm(xs_ref[...], gpre_ref[...]).astype(BF16)
        accs_ref[...] = jnp.zeros_like(accs_ref)

    @pl.when(f == 0)
    def _():
        acc_ref[...] = jnp.zeros_like(acc_ref)

    def mlp(lhs):
        hidden = jnp.square(jnp.maximum(_dot(lhs, w1_ref[...]), 0.0)).astype(BF16)
        return _dot(hidden, w2_ref[...])

    def norm_next_rows():
        rows = pl.ds(pl.multiple_of(f * chunk, chunk), chunk)
        xn_ref[(i + 1) % 2, rows, :] = _rmsnorm(xnext_ref[rows, :], gpre_ref[...]).astype(BF16)

    @pl.when(i == 0)
    def _():
        both = mlp(jnp.concatenate([xn_ref[0], xns_ref[...]], axis=0))
        acc_ref[...] += both[0:tm]
        accs_ref[...] += both[tm:]
        norm_next_rows()

    @pl.when(i > 0)
    def _():
        acc_ref[...] += mlp(xn_ref[i % 2])
        norm_next_rows()

    @pl.when(f == last_f)
    def _():
        o_ref[...] = x_ref[...] + _rmsnorm(acc_ref[...], gpost_ref[...])

    @pl.when((f == last_f) & (i == 0))
    def _():
        os_ref[...] = xs_ref[...] + _rmsnorm(accs_ref[...], gpost_ref[...])


def _ffn(x, xs, g_pre, w1_chunks, w2, g_post, *, tm):
    m, d = x.shape
    ms = xs.shape[0]
    nf, _, tf = w1_chunks.shape
    n_tiles = m // tm
    assert tm % nf == 0 and (tm // nf) % 16 == 0
    sample_spec = _resident((ms, d), lambda i, f: (0, 0))
    return pl.pallas_call(
        functools.partial(_ffn_kernel, tm=tm, nf=nf),
        grid=(n_tiles, nf),
        in_specs=[
            pl.BlockSpec((tm, d), lambda i, f: (i, 0)),
            pl.BlockSpec((tm, d), lambda i, f: (jnp.minimum(i + 1, n_tiles - 1), 0)),
            sample_spec,
            _resident((1, d), lambda i, f: (0, 0)),
            pl.BlockSpec((None, d, tf), lambda i, f: (f, 0, 0)),
            pl.BlockSpec((tf, d), lambda i, f: (f, 0)),
            _resident((1, d), lambda i, f: (0, 0)),
        ],
        out_specs=[pl.BlockSpec((tm, d), lambda i, f: (i, 0)),
                   pl.BlockSpec((ms, d), lambda i, f: (0, 0))],
        out_shape=[jax.ShapeDtypeStruct((m, d), F32), jax.ShapeDtypeStruct((ms, d), F32)],
        scratch_shapes=[pltpu.VMEM((2, tm, d), BF16), pltpu.VMEM((ms, d), BF16),
                        pltpu.VMEM((tm, d), F32), pltpu.VMEM((ms, d), F32)],
        compiler_params=_params(2),
        name="ffn",
    )(x, x, xs, g_pre, w1_chunks, w2, g_post)


def kernel(x_prompt, x_sample, state_pool, cache_attn_k, cache_attn_v, cache_mem_k, cache_mem_v, mem_prompt,
           g_mix_pre, g_mix_post, g_mem_pre, g_mem_post, g_ffn_pre, g_ffn_post, g_mem_kv,
           w_in, w_pool, pool_scale, w_out, w_xq, w_mem_kv, w_xo, w_ff1, w_ff2):
    depth = w_in.shape[0]
    assert depth == 1
    batch, seq, d = x_prompt.shape
    dec_batch, dec_seq, _ = x_sample.shape
    assert dec_seq == 1

    l = 0
    scale = pool_scale[l][None, :]
    gain = lambda g: g[l][None, :]

    tm = 512
    xp = x_prompt.reshape(batch * seq, d)
    xs = x_sample.reshape(dec_batch, d)
    cos_p, sin_p = _rope_tables(jnp.arange(seq, dtype=jnp.int32))
    cos_s, sin_s = _rope_tables(jnp.full((dec_batch,), PAST_LEN, jnp.int32))
    tm_mem = 128
    ones, zeros = jnp.ones((tm_mem, HEAD_DIM), F32), jnp.zeros((tm_mem, HEAD_DIM), F32)
    w_pool_f = w_pool[l]


    mem = mem_prompt.reshape(batch * N_MEM, d)
    (mk, mv, mk_h, mv_h), (w_in_b,) = _norm_proj(
        mem, gain(g_mem_kv), w_mem_kv[l], ones, zeros, tm=tm_mem, ncol=MEM_WIDTH, rope_blocks=(),
        outputs=((0, FLAT), (1, FLAT), (0, HEADS), (1, HEADS)), name="mem_kv", casts=(w_in[l],))

    ((pool_out, q, k, v, k_h, v_h, pool_state, us, qs_h, ks_h, vs_h),
     (w_out_b, w_xq_b, w_xo_b)) = _in_proj_pool(
        xp, xs, gain(g_mix_pre), w_in_b, cos_p, sin_p, cos_s, sin_s, w_pool_f, scale, tm=256, seq=seq,
        casts=(w_out[l], w_xq[l], w_xo[l]))
    attn, (w_ff1_b, w_ff2_b) = _attn_prompt(
        q, k, v, seq=seq, casts=((w_ff1[l], w_ff1.shape[2] // FFN_CHUNK), w_ff2[l]))
    heads3 = lambda a: a.reshape(dec_batch, -1, HEAD_DIM)
    x2, attn_s, _ = _mix_mem_prompt(
        pool_out, attn, xp, mk, mv, w_out_b, gain(g_mix_post), gain(g_mem_pre), w_xq_b, w_xo_b, gain(g_mem_post),
        (heads3(qs_h), heads3(ks_h), heads3(vs_h), cache_attn_k[l], cache_attn_v[l]), tm=tm, seq=seq)

    pool_out_s, new_pool_s = _pool_sample(us, state_pool[l], w_pool_f, scale, pos=PAST_LEN)
    x1s, qms_h = _out_proj(pool_out_s, attn_s.reshape(dec_batch, ATTN_WIDTH), xs, w_out_b, gain(g_mix_post),
                           gain(g_mem_pre), w_xq_b, tm=dec_batch, qm_layout=HEADS, name="out_proj_sample")
    mem_o_s = _mem_attn_sample(heads3(qms_h), cache_mem_k[l], cache_mem_v[l])
    x2s = _proj_norm_residual(mem_o_s.reshape(dec_batch, MEM_WIDTH), w_xo_b, gain(g_mem_post), x1s,
                              name="mem_out_sample")

    yp, ys = _ffn(x2, x2s, gain(g_ffn_pre), w_ff1_b, w_ff2_b, gain(g_ffn_post), tm=tm)

    keep = min(max(w for w, _ in DILATED), seq)
    return (
        yp.reshape(batch, seq, d),
        ys.reshape(dec_batch, 1, d),
        pool_state.reshape(batch, STATE_ROWS, POOL_WIDTH)[:, STATE_ROWS - POOL_STATE:][None],
        new_pool_s[None],
        k_h.reshape(batch, seq, N_HEADS, HEAD_DIM)[:, seq - keep:][None],
        v_h.reshape(batch, seq, N_HEADS, HEAD_DIM)[:, seq - keep:][None],
        ks_h.reshape(dec_batch, 1, N_HEADS, HEAD_DIM)[None],
        vs_h.reshape(dec_batch, 1, N_HEADS, HEAD_DIM)[None],
        mk_h.reshape(batch, N_MEM, MEM_HEADS, HEAD_DIM)[None],
        mv_h.reshape(batch, N_MEM, MEM_HEADS, HEAD_DIM)[None],
    )
```

```python
import functools

import jax
import jax.numpy as jnp
from jax import lax
from jax.experimental import pallas as pl
from jax.experimental.pallas import tpu as pltpu

D_MODEL = 2048
POOL_WIDTH = 1024
POOL_WINDOWS = (2, 4, 8, 16)
POOL_GROUP = POOL_WIDTH // len(POOL_WINDOWS)
POOL_STATE = max(POOL_WINDOWS) - 1
HEAD_DIM = 128
ATTN_WIDTH = 1024
N_HEADS = ATTN_WIDTH // HEAD_DIM
DILATED = ((128, 1), (512, 4), (2048, 16))
ROPE_DIM = HEAD_DIM // 4
ROPE_HALF = ROPE_DIM // 2
ROPE_THETA = 500000.0
N_MEM = 256
MEM_HEADS = 4
MEM_WIDTH = MEM_HEADS * HEAD_DIM
EPS = 1e-6
BLOCK = 128
NEG_INF = -1e30
LOG2_E = 1.4426950408889634
PAST_LEN = 8192
HALO = 32
STATE_ROWS = 16
SAMPLE_GROUP = 8

F32 = jnp.float32
BF16 = jnp.bfloat16

VMEM_LIMIT_BYTES = 56 * 1024 * 1024
FFN_CHUNK = 1024

FLAT, HEADS = "flat", "heads"


def _params(n_grid_axes):
    return pltpu.CompilerParams(
        dimension_semantics=("arbitrary",) * n_grid_axes,
        vmem_limit_bytes=VMEM_LIMIT_BYTES,
    )


def _resident(block_shape, index_map):
    return pl.BlockSpec(block_shape, index_map, pipeline_mode=pl.Buffered(1))


def _rmsnorm(x, g):
    ms = jnp.mean(x * x, axis=-1, keepdims=True)
    return x * lax.rsqrt(ms + EPS) * g


def _dot(a, b):
    return jnp.dot(a, b, preferred_element_type=F32)


def _dot_nt(a, b):
    return lax.dot_general(a, b, (((1,), (1,)), ((), ())), preferred_element_type=F32)


def _call_with_casts(kernel_fn, *, grid, in_specs, out_specs, out_shape, scratch_shapes=(), args, casts=(),
                     step_of=None, name):
    casts = [c if isinstance(c, tuple) else (c, 1) for c in casts]
    n_in, n_out, n_casts = len(in_specs), len(out_specs), len(casts)
    n_steps = 1
    for g in grid:
        n_steps *= g
    if step_of is None:
        step_of = lambda i: i
    cast_in_specs, cast_out_specs, cast_shapes = [], [], []
    for a, n_chunks in casts:
        rows, cols = a.shape
        assert rows % n_steps == 0 and cols % n_chunks == 0
        cast_in_specs.append(pl.BlockSpec((rows // n_steps, cols), lambda *idx: (step_of(*idx), 0)))
        if n_chunks == 1:
            cast_out_specs.append(cast_in_specs[-1])
            cast_shapes.append(jax.ShapeDtypeStruct(a.shape, BF16))
        else:
            cast_out_specs.append(pl.BlockSpec((n_chunks, rows // n_steps, cols // n_chunks),
                                               lambda *idx: (0, step_of(*idx), 0)))
            cast_shapes.append(jax.ShapeDtypeStruct((n_chunks, rows, cols // n_chunks), BF16))

    def body(*refs):
        ins, cast_in = refs[:n_in], refs[n_in:n_in + n_casts]
        outs = refs[n_in + n_casts:n_in + n_casts + n_out]
        cast_out = refs[n_in + n_casts + n_out:n_in + 2 * n_casts + n_out]

        def run_casts():
            for i_ref, o_ref, (_, n_chunks) in zip(cast_in, cast_out, casts):
                if n_chunks == 1:
                    o_ref[...] = i_ref[...].astype(BF16)
                else:
                    width = i_ref.shape[1] // n_chunks
                    for c in range(n_chunks):
                        o_ref[c] = i_ref[:, c * width:(c + 1) * width].astype(BF16)

        kernel_fn(run_casts, *ins, *outs, *refs[n_in + 2 * n_casts + n_out:])

    res = pl.pallas_call(
        body,
        grid=grid,
        in_specs=list(in_specs) + cast_in_specs,
        out_specs=list(out_specs) + cast_out_specs,
        out_shape=list(out_shape) + cast_shapes,
        scratch_shapes=list(scratch_shapes),
        compiler_params=_params(len(grid)),
        name=name,
    )(*args, *[a for a, _ in casts])
    return res[:n_out], res[n_out:]


def _store_heads(o_ref, layout, h, heads, rows, value):
    if layout == FLAT:
        o_ref[:, h * HEAD_DIM:(h + 1) * HEAD_DIM] = value
    else:
        o_ref[pl.ds(h, rows, stride=heads), :] = value


def _out_block(layout, tm, heads):
    return (tm, heads * HEAD_DIM) if layout == FLAT else (tm * heads, HEAD_DIM)


def _norm_proj_kernel(run_casts, x_ref, g_ref, w_ref, cos_ref, sin_ref, *out_refs, tm, ncol, rope_blocks, outputs):
    run_casts()
    heads = ncol // HEAD_DIM
    xn = _rmsnorm(x_ref[...], g_ref[...]).astype(BF16)
    for c in sorted({c for c, _ in outputs}):
        acc = _dot(xn, w_ref[:, c * ncol:(c + 1) * ncol].astype(BF16))
        if c in rope_blocks:
            cos = cos_ref[...]
            sin = sin_ref[...]
            lane = lax.broadcasted_iota(jnp.int32, cos.shape, 1)
        for h in range(heads):
            a = acc[:, h * HEAD_DIM:(h + 1) * HEAD_DIM]
            if c in rope_blocks:
                partner = jnp.where(lane < ROPE_HALF,
                                    pltpu.roll(a, HEAD_DIM - ROPE_HALF, 1),
                                    pltpu.roll(a, ROPE_HALF, 1))
                a = a * cos + partner * sin
            for (oc, layout), o_ref in zip(outputs, out_refs):
                if oc == c:
                    _store_heads(o_ref, layout, h, heads, tm, a)


def _norm_proj(x, g, w, cos, sin, *, tm, ncol, rope_blocks, outputs, name, casts=()):
    m, d = x.shape
    heads = ncol // HEAD_DIM
    assert w.shape[0] == d and w.shape[1] % ncol == 0 and m % tm == 0
    table_tiles = cos.shape[0] // tm
    kern = functools.partial(_norm_proj_kernel, tm=tm, ncol=ncol, rope_blocks=rope_blocks, outputs=outputs)
    blocks = [_out_block(layout, tm, heads) for _, layout in outputs]
    return _call_with_casts(
        kern,
        grid=(m // tm,),
        in_specs=[
            pl.BlockSpec((tm, d), lambda i: (i, 0)),
            _resident((1, d), lambda i: (0, 0)),
            _resident(w.shape, lambda i: (0, 0)),
            pl.BlockSpec((tm, HEAD_DIM), lambda i: (i % table_tiles, 0)),
            pl.BlockSpec((tm, HEAD_DIM), lambda i: (i % table_tiles, 0)),
        ],
        out_specs=[pl.BlockSpec(b, lambda i: (i, 0)) for b in blocks],
        out_shape=[jax.ShapeDtypeStruct((m // tm * b[0], b[1]), F32) for b in blocks],
        args=(x, g, w, cos, sin),
        casts=casts,
        name=name,
    )


def _rope_tables(pos):
    inv = jnp.power(jnp.float32(ROPE_THETA), -jnp.arange(ROPE_HALF, dtype=F32) * 2.0 / ROPE_DIM)
    ang = pos.astype(F32)[:, None] * inv[None, :]
    cos, sin = jnp.cos(ang), jnp.sin(ang)
    n = pos.shape[0]
    pad = HEAD_DIM - ROPE_DIM
    cos_t = jnp.concatenate([cos, cos, jnp.ones((n, pad), F32)], axis=1)
    sin_t = jnp.concatenate([-sin, sin, jnp.zeros((n, pad), F32)], axis=1)
    return cos_t, sin_t


def _window_sums(ext_ref, lvl_refs, g, w, rows):
    cols = slice(g * POOL_GROUP, (g + 1) * POOL_GROUP)
    levels = w.bit_length() - 1
    src, src_cols = ext_ref, cols
    for level in range(levels):
        shift = 1 << level
        last = level == levels - 1
        lo = HALO if last else 8 * (level + 1)
        assert lo - shift >= 8 * level
        total = src[lo:rows, src_cols] + src[lo - shift:rows - shift, src_cols]
        if last:
            return total
        lvl_refs[level % 2][lo:rows, :] = total
        src, src_cols = lvl_refs[level % 2], slice(None)


def _rope(a, cos, sin):
    lane = lax.broadcasted_iota(jnp.int32, a.shape, 1)
    partner = jnp.where(lane < ROPE_HALF, pltpu.roll(a, HEAD_DIM - ROPE_HALF, 1), pltpu.roll(a, ROPE_HALF, 1))
    return a * cos + partner * sin


def _in_proj_pool_kernel(run_casts, x_ref, xnext_ref, xs_ref, g_ref, w_ref, cos_ref, sin_ref, coss_ref, sins_ref,
                         wp_ref, scale_ref,
                         pool_ref, q_ref, k_ref, v_ref, kh_ref, vh_ref, state_ref, us_ref, qsh_ref, ksh_ref, vsh_ref,
                         xn_ref, ext_ref, lvl_a_ref, lvl_b_ref, *, tm, tiles_per_seq):
    step = pl.program_id(0)
    t_in_seq = step % tiles_per_seq
    n_sample = xs_ref.shape[0]

    def tile(xn, with_sample):
        run_casts()
        u = _dot(xn, w_ref[:, 0:POOL_WIDTH])
        ext_ref[HALO:HALO + tm, :] = u[0:tm]
        if with_sample:
            us_ref[...] = u[tm:]
        cos, sin = cos_ref[...], sin_ref[...]
        outs = ((q_ref, None, qsh_ref), (k_ref, kh_ref, ksh_ref), (v_ref, vh_ref, vsh_ref))
        for c, (major_ref, heads_ref, sample_ref) in enumerate(outs, start=1):
            acc = _dot(xn, w_ref[:, c * ATTN_WIDTH:(c + 1) * ATTN_WIDTH])
            for h in range(N_HEADS):
                a = acc[0:tm, h * HEAD_DIM:(h + 1) * HEAD_DIM]
                if major_ref is not v_ref:
                    a = _rope(a, cos, sin)
                major_ref[h] = a
                if heads_ref is not None:
                    _store_heads(heads_ref, HEADS, h, N_HEADS, tm, a)
                if with_sample:
                    a = acc[tm:, h * HEAD_DIM:(h + 1) * HEAD_DIM]
                    if major_ref is not v_ref:
                        a = _rope(a, coss_ref[...], sins_ref[...])
                    _store_heads(sample_ref, HEADS, h, N_HEADS, n_sample, a)

        pos = t_in_seq * tm + lax.broadcasted_iota(jnp.int32, (tm, 1), 0)
        for g, w in enumerate(POOL_WINDOWS):
            cols = slice(g * POOL_GROUP, (g + 1) * POOL_GROUP)
            wsum = _window_sums(ext_ref, (lvl_a_ref, lvl_b_ref), g, w, HALO + tm)
            cnt = jnp.minimum(w, pos + 1).astype(F32)
            pooled = (wsum / cnt - ext_ref[HALO:HALO + tm, cols]).astype(BF16)
            pool_ref[:, cols] = (_dot(pooled, wp_ref[g].astype(BF16)) * scale_ref[:, cols]).astype(pool_ref.dtype)
        state_ref[...] = ext_ref[HALO + tm - STATE_ROWS:HALO + tm, :]
        ext_ref[0:HALO, :] = ext_ref[tm:tm + HALO, :]

        xn_ref[(step + 1) % 2] = _rmsnorm(xnext_ref[...], g_ref[...]).astype(BF16)

    @pl.when(t_in_seq == 0)
    def _():
        ext_ref[0:HALO, :] = jnp.zeros((HALO, POOL_WIDTH), F32)

    @pl.when(step == 0)
    def _():
        rows = jnp.concatenate([x_ref[...], xs_ref[...]], axis=0)
        tile(_rmsnorm(rows, g_ref[...]).astype(BF16), True)

    @pl.when(step > 0)
    def _():
        tile(xn_ref[step % 2], False)


def _in_proj_pool(x, xs, g, w, cos, sin, cos_s, sin_s, w_pool, pool_scale, *, tm, seq, casts=()):
    m, d = x.shape
    ms = xs.shape[0]
    tiles_per_seq = seq // tm
    n_seq = m // seq
    assert POOL_WIDTH == ATTN_WIDTH and w.shape == (d, POOL_WIDTH + 3 * ATTN_WIDTH)
    kern = functools.partial(_in_proj_pool_kernel, tm=tm, tiles_per_seq=tiles_per_seq)
    row_tile = lambda width: pl.BlockSpec((tm, width), lambda i: (i, 0))
    heads_tile = pl.BlockSpec((tm * N_HEADS, HEAD_DIM), lambda i: (i, 0))
    table = pl.BlockSpec((tm, HEAD_DIM), lambda i: (i % tiles_per_seq, 0))
    whole = lambda shape: _resident(shape, lambda i: (0,) * len(shape))
    head_major = jax.ShapeDtypeStruct((N_HEADS, m, HEAD_DIM), F32)
    head_major_tile = pl.BlockSpec((N_HEADS, tm, HEAD_DIM), lambda i: (0, i, 0))
    by_heads = jax.ShapeDtypeStruct((m * N_HEADS, HEAD_DIM), F32)
    sample_heads = jax.ShapeDtypeStruct((ms * N_HEADS, HEAD_DIM), F32)
    sample_heads_spec = pl.BlockSpec((ms * N_HEADS, HEAD_DIM), lambda i: (0, 0))
    return _call_with_casts(
        kern,
        grid=(m // tm,),
        in_specs=[
            row_tile(d),
            pl.BlockSpec((tm, d), lambda i: (jnp.minimum(i + 1, m // tm - 1), 0)),
            whole((ms, d)),
            whole((1, d)),
            whole(w.shape),
            table,
            table,
            whole((ms, HEAD_DIM)),
            whole((ms, HEAD_DIM)),
            whole(w_pool.shape),
            whole((1, POOL_WIDTH)),
        ],
        out_specs=[row_tile(POOL_WIDTH), head_major_tile, head_major_tile, head_major_tile,
                   heads_tile, heads_tile,
                   pl.BlockSpec((STATE_ROWS, POOL_WIDTH), lambda i: (i // tiles_per_seq, 0)),
                   pl.BlockSpec((ms, POOL_WIDTH), lambda i: (0, 0)),
                   sample_heads_spec, sample_heads_spec, sample_heads_spec],
        out_shape=[jax.ShapeDtypeStruct((m, POOL_WIDTH), BF16), head_major, head_major, head_major, by_heads, by_heads,
                   jax.ShapeDtypeStruct((n_seq * STATE_ROWS, POOL_WIDTH), F32),
                   jax.ShapeDtypeStruct((ms, POOL_WIDTH), F32), sample_heads, sample_heads, sample_heads],
        scratch_shapes=[pltpu.VMEM((2, tm, d), BF16),
                        pltpu.VMEM((HALO + tm, POOL_WIDTH), F32),
                        pltpu.VMEM((HALO + tm, POOL_GROUP), F32),
                        pltpu.VMEM((HALO + tm, POOL_GROUP), F32)],
        args=(x, x, xs, g, w, cos, sin, cos_s, sin_s, w_pool, pool_scale),
        casts=casts,
        name="in_proj_pool",
    )


def _mix_sample_kernel(u_ref, prev_ref, wp_ref, scale_ref, attn_ref, x_ref, wo_ref, gpost_ref, gpre_ref, wq_ref,
                       state_ref, x1_ref, qm_ref, *, pos):
    n = u_ref.shape[0]
    for j in range(POOL_STATE - 1):
        state_ref[:, j, :] = prev_ref[:, j + 1, :]
    state_ref[:, POOL_STATE - 1, :] = u_ref[...]
    y = jnp.zeros(x_ref.shape, F32)
    for g, w in enumerate(POOL_WINDOWS):
        cols = slice(g * POOL_GROUP, (g + 1) * POOL_GROUP)
        wsum = u_ref[:, cols]
        for back in range(1, w):
            wsum = wsum + prev_ref[:, POOL_STATE - back, cols]
        cnt = float(min(w, pos + 1))
        pooled = (wsum / cnt - u_ref[:, cols]).astype(BF16)
        pool_out = (_dot(pooled, wp_ref[g].astype(BF16)) * scale_ref[:, cols]).astype(BF16)
        y = y + _dot(pool_out, wo_ref[cols, :])
    for h in range(N_HEADS):
        rows = slice(POOL_WIDTH + h * HEAD_DIM, POOL_WIDTH + (h + 1) * HEAD_DIM)
        y = y + _dot(attn_ref[:, h, :].astype(BF16), wo_ref[rows, :])
    x1 = x_ref[...] + _rmsnorm(y, gpost_ref[...])
    x1_ref[...] = x1
    qm = _dot(_rmsnorm(x1, gpre_ref[...]).astype(BF16), wq_ref[...])
    for h in range(MEM_HEADS):
        _store_heads(qm_ref, HEADS, h, MEM_HEADS, n, qm[:, h * HEAD_DIM:(h + 1) * HEAD_DIM])


def _mix_sample(u, prev, w_pool, pool_scale, attn, x, w_out, g_post, g_pre, w_xq, *, pos):
    n, d = x.shape
    kern = functools.partial(_mix_sample_kernel, pos=pos)
    return pl.pallas_call(
        kern,
        out_shape=[jax.ShapeDtypeStruct(prev.shape, F32), jax.ShapeDtypeStruct((n, d), F32),
                   jax.ShapeDtypeStruct((n * MEM_HEADS, HEAD_DIM), F32)],
        compiler_params=pltpu.CompilerParams(vmem_limit_bytes=VMEM_LIMIT_BYTES),
        name="mix_sample",
    )(u, prev, w_pool, pool_scale, attn, x, w_out, g_post, g_pre, w_xq)


def _run_blocks(count, unroll, body):
    trips = count // unroll
    if trips > 1:
        def trip(t, carry):
            for j in range(unroll):
                body(t * unroll + j)
            return carry
        lax.fori_loop(0, trips, trip, 0)
        done = trips * unroll
    else:
        done = 0
    for idx in range(done, count):
        body(idx)


def _attn_prompt_kernel(run_casts, q_ref, k_ref, v_ref, o_ref, *scratch, seq, unroll):
    run_casts()
    n_br = len(DILATED)
    ob_refs, mb_refs, sb_refs = scratch[:n_br], scratch[n_br:2 * n_br], scratch[2 * n_br:3 * n_br]
    p_ref, bias_ref = scratch[3 * n_br:]
    exp2_scale = HEAD_DIM ** -0.5 * LOG2_E
    qi = lax.broadcasted_iota(jnp.int32, (BLOCK, 2 * BLOCK), 0)
    kj = lax.broadcasted_iota(jnp.int32, (BLOCK, 2 * BLOCK), 1)
    in_band = ((kj < BLOCK) & (kj >= qi)) | ((kj >= BLOCK) & (kj - BLOCK <= qi))
    bias_ref[...] = jnp.where(in_band, 0.0, NEG_INF)
    ones = jnp.ones((2 * BLOCK, HEAD_DIM), BF16)

    def rows(block, r, d):
        if d == 1:
            start = block * BLOCK
            return pl.ds(start if isinstance(start, int) else pl.multiple_of(start, BLOCK), BLOCK)
        return pl.ds(block * (BLOCK * d) + r, BLOCK, stride=d)

    def keys(ref, block, r, d, with_prev):
        cur = ref[rows(block, r, d), :]
        if not with_prev:
            return cur.astype(BF16)
        return jnp.concatenate([ref[rows(block - 1, r, d), :], cur], axis=0).astype(BF16)

    for g, (_, d) in enumerate(DILATED):
        n_blocks = seq // d // BLOCK
        for with_prev in (False, True):
            per_r = n_blocks - 1 if with_prev else 1
            count = d * per_r
            if count == 0:
                continue
            cols = slice(0, 2 * BLOCK) if with_prev else slice(BLOCK, 2 * BLOCK)
            width = 2 * BLOCK if with_prev else BLOCK

            def locate(idx, per_r=per_r, with_prev=with_prev):
                return (idx % per_r + 1, idx // per_r) if with_prev else (0, idx)

            def probabilities(idx, g=g, d=d, with_prev=with_prev, cols=cols, width=width, locate=locate):
                b, r = locate(idx)
                q = q_ref[rows(b, r, d), :].astype(BF16)
                s = _dot_nt(q, keys(k_ref, b, r, d, with_prev)) + bias_ref[:, cols]
                m = jnp.max(s, axis=-1, keepdims=True)
                p_ref[idx, :, 0:width] = jnp.exp2((s - m) * exp2_scale).astype(BF16)
                mb_refs[g][rows(b, r, d), :] = jnp.broadcast_to(m, (BLOCK, HEAD_DIM))

            def values(idx, g=g, d=d, with_prev=with_prev, width=width, locate=locate):
                b, r = locate(idx)
                v1 = jnp.concatenate([keys(v_ref, b, r, d, with_prev), ones[0:width]], axis=1)
                acc = _dot(p_ref[idx, :, 0:width], v1)
                ob_refs[g][rows(b, r, d), :] = acc[:, 0:HEAD_DIM]
                sb_refs[g][rows(b, r, d), :] = acc[:, HEAD_DIM:]

            _run_blocks(count, unroll, probabilities)
            _run_blocks(count, unroll, values)

    chunk = 256

    def combine(c, carry):
        rw = pl.ds(pl.multiple_of(c * chunk, chunk), chunk)
        ms = [mb[rw, :] for mb in mb_refs]
        m = functools.reduce(jnp.maximum, ms)
        ws = [jnp.exp2((mg - m) * exp2_scale) for mg in ms]
        num = functools.reduce(jnp.add, [w * ob[rw, :] for w, ob in zip(ws, ob_refs)])
        den = functools.reduce(jnp.add, [w * sb[rw, :] for w, sb in zip(ws, sb_refs)])
        o_ref[rw, :] = (num / den).astype(o_ref.dtype)
        return carry

    lax.fori_loop(0, seq // chunk, combine, 0)


def _attn_prompt(q, k, v, *, seq, casts=()):
    n_heads, m, _ = q.shape
    n_seq = m // seq
    spec = pl.BlockSpec((None, seq, HEAD_DIM), lambda n, h: (h, n, 0))
    kern = functools.partial(_attn_prompt_kernel, seq=seq, unroll=16)
    (attn,), cast_out = _call_with_casts(
        kern,
        grid=(n_seq, n_heads),
        in_specs=[spec, spec, spec],
        out_specs=[spec],
        out_shape=[jax.ShapeDtypeStruct(q.shape, BF16)],
        scratch_shapes=(
            [pltpu.VMEM((seq, HEAD_DIM), F32) for _ in range(3 * len(DILATED))]
            + [pltpu.VMEM((seq // BLOCK, BLOCK, 2 * BLOCK), BF16),
               pltpu.VMEM((BLOCK, 2 * BLOCK), F32)]),
        args=(q, k, v),
        casts=casts,
        step_of=lambda n, h: n * n_heads + h,
        name="attn_prompt",
    )
    return attn, cast_out


def _attn_sample_kernel(q_ref, kn_ref, vn_ref, *refs, group):
    n_br = len(DILATED)
    kc_refs, vc_refs, o_ref = refs[:n_br], refs[n_br:2 * n_br], refs[2 * n_br]
    for j in range(group):
        q = q_ref[j] * (HEAD_DIM ** -0.5 * LOG2_E)
        t_new = jnp.sum(q * kn_ref[j], axis=-1, keepdims=True)
        t_br = [jnp.sum(kc[j] * q[None], axis=-1, keepdims=True) for kc in kc_refs]
        m = t_new
        for t in t_br:
            m = jnp.maximum(m, jnp.max(t, axis=0))
        p_new = jnp.exp2(t_new - m) * float(n_br)
        den = p_new
        num = p_new * vn_ref[j]
        for t, vc in zip(t_br, vc_refs):
            p = jnp.exp2(t - m[None])
            den = den + jnp.sum(p, axis=0)
            num = num + jnp.sum(p * vc[j], axis=0)
        o_ref[j] = num / den


def _attn_sample_operands(q, k_new, v_new, k_cache, v_cache, *, group):
    n, heads, _ = q.shape
    win = k_cache.shape[1]
    row = pl.BlockSpec((group, heads, HEAD_DIM), lambda i: (i, 0, 0))
    cache_specs, k_views, v_views = [], [], []
    for w, d in DILATED:
        assert w == BLOCK * d and win % w == 0
        last = win // w - 1
        cache_specs.append(pl.BlockSpec((group, BLOCK, None, heads, HEAD_DIM),
                                        lambda i, last=last: (i, last, 0, 0, 0)))
        k_views.append(k_cache.reshape(n, win // d, d, heads, HEAD_DIM))
        v_views.append(v_cache.reshape(n, win // d, d, heads, HEAD_DIM))
    return ([row, row, row] + cache_specs + cache_specs, row, jax.ShapeDtypeStruct((n, heads, HEAD_DIM), F32),
            (q, k_new, v_new, *k_views, *v_views))


def _mix_mem_prompt_kernel(run_casts, pool_ref, attn_ref, x_ref, mk_ref, mv_ref, wo_ref,
                           gmix_ref, gpre_ref, wq_ref, wxo_ref, gmem_ref, *refs, sample_group):
    n_sample_in = 3 + 2 * len(DILATED)
    sample_in, (x2_ref, attn_s_ref) = refs[:n_sample_in], refs[n_sample_in:]
    run_casts()
    scale = HEAD_DIM ** -0.5
    mixed = jnp.concatenate([pool_ref[...]] + [attn_ref[h] for h in range(N_HEADS)], axis=-1)
    x1 = x_ref[...] + _rmsnorm(_dot(mixed, wo_ref[...]), gmix_ref[...])
    qm = _dot(_rmsnorm(x1, gpre_ref[...]).astype(BF16), wq_ref[...]).astype(BF16)
    heads = []
    for h in range(MEM_HEADS):
        cols = slice(h * HEAD_DIM, (h + 1) * HEAD_DIM)
        s = _dot_nt(qm[:, cols], mk_ref[:, cols].astype(BF16)) * scale
        p = jnp.exp(s - jnp.max(s, axis=-1, keepdims=True))
        o = _dot(p.astype(BF16), mv_ref[:, cols].astype(BF16))
        heads.append((o / jnp.sum(p, axis=-1, keepdims=True)).astype(BF16))
    y = _dot(jnp.concatenate(heads, axis=-1), wxo_ref[...])
    x2_ref[...] = x1 + _rmsnorm(y, gmem_ref[...])

    _attn_sample_kernel(*sample_in, attn_s_ref, group=sample_group)


def _mix_mem_prompt(pool_out, attn, x, mk, mv, w_out, g_mix_post, g_mem_pre, w_xq, w_xo, g_mem_post,
                    sample_attn, *, tm, seq, casts=()):
    m, d = x.shape
    n_tiles = m // tm
    tiles_per_seq = seq // tm
    half = pool_out.shape[1]
    mem_spec = pl.BlockSpec((N_MEM, MEM_WIDTH), lambda i: (i // tiles_per_seq, 0))
    gain_spec = _resident((1, d), lambda i: (0, 0))
    n_sample = sample_attn[0].shape[0]
    assert n_sample % n_tiles == 0
    sample_group = n_sample // n_tiles
    s_in_specs, s_out_spec, s_out_shape, s_args = _attn_sample_operands(*sample_attn, group=sample_group)
    (x2, attn_s), cast_out = _call_with_casts(
        functools.partial(_mix_mem_prompt_kernel, sample_group=sample_group),
        grid=(n_tiles,),
        in_specs=[
            pl.BlockSpec((tm, half), lambda i: (i, 0)),
            pl.BlockSpec((N_HEADS, tm, HEAD_DIM), lambda i: (0, i, 0)),
            pl.BlockSpec((tm, d), lambda i: (i, 0)),
            mem_spec,
            mem_spec,
            _resident(w_out.shape, lambda i: (0, 0)),
            gain_spec,
            gain_spec,
            _resident(w_xq.shape, lambda i: (0, 0)),
            _resident(w_xo.shape, lambda i: (0, 0)),
            gain_spec,
        ] + s_in_specs,
        out_specs=[pl.BlockSpec((tm, d), lambda i: (i, 0)), s_out_spec],
        out_shape=[jax.ShapeDtypeStruct((m, d), F32), s_out_shape],
        args=(pool_out, attn, x, mk, mv, w_out, g_mix_post, g_mem_pre, w_xq, w_xo, g_mem_post) + tuple(s_args),
        casts=casts,
        name="mix_mem_prompt",
    )
    return x2, attn_s, cast_out


def _mem_attn_sample_kernel(qm_ref, mk_ref, mv_ref, o_ref, *, group):
    heads = qm_ref.shape[1]
    both = lambda a: jnp.concatenate([a, a], axis=0)
    fold = lambda a: a[0:heads] + a[heads:]
    for j in range(group):
        q = both(qm_ref[j] * (HEAD_DIM ** -0.5 * LOG2_E))
        t = jnp.sum(mk_ref[j] * q[None], axis=-1, keepdims=True)
        m = jnp.max(t, axis=0)
        m = both(jnp.maximum(m[0:heads], m[heads:]))
        p = jnp.exp2(t - m[None])
        o_ref[j] = fold(jnp.sum(p * mv_ref[j], axis=0)) / fold(jnp.sum(p, axis=0))


def _mem_attn_sample(qm, mem_k, mem_v):
    n, heads, _ = qm.shape
    group = SAMPLE_GROUP
    row = pl.BlockSpec((group, heads, HEAD_DIM), lambda i: (i, 0, 0))
    mem = pl.BlockSpec((group, N_MEM // 2, 2 * heads, HEAD_DIM), lambda i: (i, 0, 0, 0))
    pairs = lambda a: a.reshape(n, N_MEM // 2, 2 * heads, HEAD_DIM)
    return pl.pallas_call(
        functools.partial(_mem_attn_sample_kernel, group=group),
        grid=(n // group,),
        in_specs=[row, mem, mem],
        out_specs=row,
        out_shape=jax.ShapeDtypeStruct((n, heads, HEAD_DIM), F32),
        compiler_params=_params(1),
        name="mem_attn_sample",
    )(qm, pairs(mem_k), pairs(mem_v))


def _mem_out_sample_kernel(a_ref, w_ref, g_ref, r_ref, o_ref):
    y = jnp.zeros(r_ref.shape, F32)
    for h in range(a_ref.shape[1]):
        y = y + _dot(a_ref[:, h, :].astype(BF16), w_ref[h * HEAD_DIM:(h + 1) * HEAD_DIM, :])
    o_ref[...] = r_ref[...] + _rmsnorm(y, g_ref[...])


def _mem_out_sample(a, w, g, resid):
    return pl.pallas_call(
        _mem_out_sample_kernel,
        out_shape=jax.ShapeDtypeStruct(resid.shape, F32),
        compiler_params=pltpu.CompilerParams(vmem_limit_bytes=VMEM_LIMIT_BYTES),
        name="mem_out_sample",
    )(a, w, g, resid)


def _ffn_kernel(x_ref, xnext_ref, xs_ref, gpre_ref, w1_ref, w2_ref, gpost_ref, o_ref, os_ref,
                xn_ref, xns_ref, acc_ref, accs_ref, *, tm, nf):
    i, f = pl.program_id(0), pl.program_id(1)
    last_f = nf - 1
    chunk = tm // nf

    @pl.when((i == 0) & (f == 0))
    def _():
        xn_ref[0] = _rmsnorm(x_ref[...], gpre_ref[...]).astype(BF16)
        xns_ref[...] = _rmsnorm(xs_ref[...], gpre_ref[...]).astype(BF16)
        accs_ref[...] = jnp.zeros_like(accs_ref)

    @pl.when(f == 0)
    def _():
        acc_ref[...] = jnp.zeros_like(acc_ref)

    def mlp(lhs):
        hidden = jnp.square(jnp.maximum(_dot(lhs, w1_ref[...]), 0.0)).astype(BF16)
        return _dot(hidden, w2_ref[...])

    def norm_next_rows():
        rows = pl.ds(pl.multiple_of(f * chunk, chunk), chunk)
        xn_ref[(i + 1) % 2, rows, :] = _rmsnorm(xnext_ref[rows, :], gpre_ref[...]).astype(BF16)

    @pl.when(i == 0)
    def _():
        both = mlp(jnp.concatenate([xn_ref[0], xns_ref[...]], axis=0))
        acc_ref[...] += both[0:tm]
        accs_ref[...] += both[tm:]
        norm_next_rows()

    @pl.when(i > 0)
    def _():
        acc_ref[...] += mlp(xn_ref[i % 2])
        norm_next_rows()

    @pl.when(f == last_f)
    def _():
        o_ref[...] = x_ref[...] + _rmsnorm(acc_ref[...], gpost_ref[...])

    @pl.when((f == last_f) & (i == 0))
    def _():
        os_ref[...] = xs_ref[...] + _rmsnorm(accs_ref[...], gpost_ref[...])


def _ffn(x, xs, g_pre, w1_chunks, w2, g_post, *, tm):
    m, d = x.shape
    ms = xs.shape[0]
    nf, _, tf = w1_chunks.shape
    n_tiles = m // tm
    assert tm % nf == 0 and (tm // nf) % 16 == 0
    sample_spec = _resident((ms, d), lambda i, f: (0, 0))
    return pl.pallas_call(
        functools.partial(_ffn_kernel, tm=tm, nf=nf),
        grid=(n_tiles, nf),
        in_specs=[
            pl.BlockSpec((tm, d), lambda i, f: (i, 0)),
            pl.BlockSpec((tm, d), lambda i, f: (jnp.minimum(i + 1, n_tiles - 1), 0)),
            sample_spec,
            _resident((1, d), lambda i, f: (0, 0)),
            pl.BlockSpec((None, d, tf), lambda i, f: (f, 0, 0)),
            pl.BlockSpec((tf, d), lambda i, f: (f, 0)),
            _resident((1, d), lambda i, f: (0, 0)),
        ],
        out_specs=[pl.BlockSpec((tm, d), lambda i, f: (i, 0)),
                   pl.BlockSpec((ms, d), lambda i, f: (0, 0))],
        out_shape=[jax.ShapeDtypeStruct((m, d), F32), jax.ShapeDtypeStruct((ms, d), F32)],
        scratch_shapes=[pltpu.VMEM((2, tm, d), BF16), pltpu.VMEM((ms, d), BF16),
                        pltpu.VMEM((tm, d), F32), pltpu.VMEM((ms, d), F32)],
        compiler_params=_params(2),
        name="ffn",
    )(x, x, xs, g_pre, w1_chunks, w2, g_post)


def kernel(x_prompt, x_sample, state_pool, cache_attn_k, cache_attn_v, cache_mem_k, cache_mem_v, mem_prompt,
           g_mix_pre, g_mix_post, g_mem_pre, g_mem_post, g_ffn_pre, g_ffn_post, g_mem_kv,
           w_in, w_pool, pool_scale, w_out, w_xq, w_mem_kv, w_xo, w_ff1, w_ff2):
    depth = w_in.shape[0]
    assert depth == 1
    batch, seq, d = x_prompt.shape
    dec_batch, dec_seq, _ = x_sample.shape
    assert dec_seq == 1

    l = 0
    scale = pool_scale[l][None, :]
    gain = lambda g: g[l][None, :]

    tm = 512
    xp = x_prompt.reshape(batch * seq, d)
    xs = x_sample.reshape(dec_batch, d)
    cos_p, sin_p = _rope_tables(jnp.arange(seq, dtype=jnp.int32))
    cos_s, sin_s = _rope_tables(jnp.full((dec_batch,), PAST_LEN, jnp.int32))
    tm_mem = 256
    ones, zeros = jnp.ones((tm_mem, HEAD_DIM), F32), jnp.zeros((tm_mem, HEAD_DIM), F32)
    w_pool_f = w_pool[l]


    mem = mem_prompt.reshape(batch * N_MEM, d)
    (mk, mv, mk_h, mv_h), (w_in_b,) = _norm_proj(
        mem, gain(g_mem_kv), w_mem_kv[l], ones, zeros, tm=tm_mem, ncol=MEM_WIDTH, rope_blocks=(),
        outputs=((0, FLAT), (1, FLAT), (0, HEADS), (1, HEADS)), name="mem_kv", casts=(w_in[l],))

    ((pool_out, q, k, v, k_h, v_h, pool_state, us, qs_h, ks_h, vs_h),
     (w_out_b, w_xq_b, w_xo_b)) = _in_proj_pool(
        xp, xs, gain(g_mix_pre), w_in_b, cos_p, sin_p, cos_s, sin_s, w_pool_f, scale, tm=256, seq=seq,
        casts=(w_out[l], w_xq[l], w_xo[l]))
    attn, (w_ff1_b, w_ff2_b) = _attn_prompt(
        q, k, v, seq=seq, casts=((w_ff1[l], w_ff1.shape[2] // FFN_CHUNK), w_ff2[l]))
    heads3 = lambda a: a.reshape(dec_batch, -1, HEAD_DIM)
    x2, attn_s, _ = _mix_mem_prompt(
        pool_out, attn, xp, mk, mv, w_out_b, gain(g_mix_post), gain(g_mem_pre), w_xq_b, w_xo_b, gain(g_mem_post),
        (heads3(qs_h), heads3(ks_h), heads3(vs_h), cache_attn_k[l], cache_attn_v[l]), tm=tm, seq=seq)

    new_pool_s, x1s, qms_h = _mix_sample(us, state_pool[l], w_pool_f, scale, attn_s, xs, w_out_b,
                                         gain(g_mix_post), gain(g_mem_pre), w_xq_b, pos=PAST_LEN)
    mem_o_s = _mem_attn_sample(heads3(qms_h), cache_mem_k[l], cache_mem_v[l])
    x2s = _mem_out_sample(mem_o_s, w_xo_b, gain(g_mem_post), x1s)

    yp, ys = _ffn(x2, x2s, gain(g_ffn_pre), w_ff1_b, w_ff2_b, gain(g_ffn_post), tm=tm)

    keep = min(max(w for w, _ in DILATED), seq)
    return (
        yp.reshape(batch, seq, d),
        ys.reshape(dec_batch, 1, d),
        pool_state.reshape(batch, STATE_ROWS, POOL_WIDTH)[:, STATE_ROWS - POOL_STATE:][None],
        new_pool_s[None],
        k_h.reshape(batch, seq, N_HEADS, HEAD_DIM)[:, seq - keep:][None],
        v_h.reshape(batch, seq, N_HEADS, HEAD_DIM)[:, seq - keep:][None],
        ks_h.reshape(dec_batch, 1, N_HEADS, HEAD_DIM)[None],
        vs_h.reshape(dec_batch, 1, N_HEADS, HEAD_DIM)[None],
        mk_h.reshape(batch, N_MEM, MEM_HEADS, HEAD_DIM)[None],
        mv_h.reshape(batch, N_MEM, MEM_HEADS, HEAD_DIM)[None],
    )
```

```python
import functools

import jax
import jax.numpy as jnp
from jax import lax
from jax.experimental import pallas as pl
from jax.experimental.pallas import tpu as pltpu

D_MODEL = 2048
POOL_WIDTH = 1024
POOL_WINDOWS = (2, 4, 8, 16)
POOL_GROUP = POOL_WIDTH // len(POOL_WINDOWS)
POOL_STATE = max(POOL_WINDOWS) - 1
HEAD_DIM = 128
ATTN_WIDTH = 1024
N_HEADS = ATTN_WIDTH // HEAD_DIM
DILATED = ((128, 1), (512, 4), (2048, 16))
PITCHED_DILATION = 16
PITCH = 128 + 8
ROPE_DIM = HEAD_DIM // 4
ROPE_HALF = ROPE_DIM // 2
ROPE_THETA = 500000.0
N_MEM = 256
MEM_HEADS = 4
MEM_WIDTH = MEM_HEADS * HEAD_DIM
EPS = 1e-6
BLOCK = 128
NEG_INF = -1e30
LOG2_E = 1.4426950408889634
PAST_LEN = 8192
HALO = 32
STATE_ROWS = 16

F32 = jnp.float32
BF16 = jnp.bfloat16

VMEM_LIMIT_BYTES = 56 * 1024 * 1024

ROW_TILE = 512
IN_PROJ_TILE = 256
MEM_KV_TILE = 256
FFN_CHUNK = 1024
SAMPLE_GROUP = 8

FLAT, HEADS = "flat", "heads"


def _params(n_grid_axes):
    return pltpu.CompilerParams(
        dimension_semantics=("arbitrary",) * n_grid_axes,
        vmem_limit_bytes=VMEM_LIMIT_BYTES,
    )


def _resident(block_shape, index_map):
    return pl.BlockSpec(block_shape, index_map, pipeline_mode=pl.Buffered(1))


def _rmsnorm(x, g):
    ms = jnp.mean(x * x, axis=-1, keepdims=True)
    return x * lax.rsqrt(ms + EPS) * g


def _dot(a, b):
    return jnp.dot(a, b, preferred_element_type=F32)


def _dot_nt(a, b):
    return lax.dot_general(a, b, (((1,), (1,)), ((), ())), preferred_element_type=F32)


def _call_with_casts(kernel_fn, *, grid, in_specs, out_specs, out_shape, scratch_shapes=(), args, casts=(),
                     step_of=None, name):
    casts = [c if isinstance(c, tuple) else (c, 1) for c in casts]
    n_in, n_out, n_casts = len(in_specs), len(out_specs), len(casts)
    n_steps = 1
    for g in grid:
        n_steps *= g
    if step_of is None:
        step_of = lambda i: i
    cast_in_specs, cast_out_specs, cast_shapes = [], [], []
    for a, n_chunks in casts:
        rows, cols = a.shape
        assert rows % n_steps == 0 and cols % n_chunks == 0
        cast_in_specs.append(pl.BlockSpec((rows // n_steps, cols), lambda *idx: (step_of(*idx), 0)))
        if n_chunks == 1:
            cast_out_specs.append(cast_in_specs[-1])
            cast_shapes.append(jax.ShapeDtypeStruct(a.shape, BF16))
        else:
            cast_out_specs.append(pl.BlockSpec((n_chunks, rows // n_steps, cols // n_chunks),
                                               lambda *idx: (0, step_of(*idx), 0)))
            cast_shapes.append(jax.ShapeDtypeStruct((n_chunks, rows, cols // n_chunks), BF16))

    def body(*refs):
        ins, cast_in = refs[:n_in], refs[n_in:n_in + n_casts]
        outs = refs[n_in + n_casts:n_in + n_casts + n_out]
        cast_out = refs[n_in + n_casts + n_out:n_in + 2 * n_casts + n_out]

        def run_casts():
            for i_ref, o_ref, (_, n_chunks) in zip(cast_in, cast_out, casts):
                if n_chunks == 1:
                    o_ref[...] = i_ref[...].astype(BF16)
                else:
                    width = i_ref.shape[1] // n_chunks
                    for c in range(n_chunks):
                        o_ref[c] = i_ref[:, c * width:(c + 1) * width].astype(BF16)

        kernel_fn(run_casts, *ins, *outs, *refs[n_in + 2 * n_casts + n_out:])

    res = pl.pallas_call(
        body,
        grid=grid,
        in_specs=list(in_specs) + cast_in_specs,
        out_specs=list(out_specs) + cast_out_specs,
        out_shape=list(out_shape) + cast_shapes,
        scratch_shapes=list(scratch_shapes),
        compiler_params=_params(len(grid)),
        name=name,
    )(*args, *[a for a, _ in casts])
    return res[:n_out], res[n_out:]


def _store_heads(o_ref, layout, h, heads, rows, value):
    if layout == FLAT:
        o_ref[:, h * HEAD_DIM:(h + 1) * HEAD_DIM] = value
    else:
        o_ref[pl.ds(h, rows, stride=heads), :] = value


def _out_block(layout, tm, heads):
    return (tm, heads * HEAD_DIM) if layout == FLAT else (tm * heads, HEAD_DIM)


def _norm_proj_kernel(run_casts, x_ref, g_ref, w_ref, *out_refs, tm, ncol, outputs):
    run_casts()
    heads = ncol // HEAD_DIM
    xn = _rmsnorm(x_ref[...], g_ref[...]).astype(BF16)
    for c in sorted({c for c, _ in outputs}):
        acc = _dot(xn, w_ref[:, c * ncol:(c + 1) * ncol].astype(BF16))
        for h in range(heads):
            for (oc, layout), o_ref in zip(outputs, out_refs):
                if oc == c:
                    _store_heads(o_ref, layout, h, heads, tm, acc[:, h * HEAD_DIM:(h + 1) * HEAD_DIM])


def _norm_proj(x, g, w, *, tm, ncol, outputs, name, casts=()):
    m, d = x.shape
    heads = ncol // HEAD_DIM
    assert w.shape[0] == d and w.shape[1] % ncol == 0 and m % tm == 0
    kern = functools.partial(_norm_proj_kernel, tm=tm, ncol=ncol, outputs=outputs)
    blocks = [_out_block(layout, tm, heads) for _, layout in outputs]
    return _call_with_casts(
        kern,
        grid=(m // tm,),
        in_specs=[
            pl.BlockSpec((tm, d), lambda i: (i, 0)),
            _resident((1, d), lambda i: (0, 0)),
            _resident(w.shape, lambda i: (0, 0)),
        ],
        out_specs=[pl.BlockSpec(b, lambda i: (i, 0)) for b in blocks],
        out_shape=[jax.ShapeDtypeStruct((m // tm * b[0], b[1]), F32) for b in blocks],
        args=(x, g, w),
        casts=casts,
        name=name,
    )


def _rope_tables(pos):
    inv = jnp.power(jnp.float32(ROPE_THETA), -jnp.arange(ROPE_HALF, dtype=F32) * 2.0 / ROPE_DIM)
    ang = pos.astype(F32)[:, None] * inv[None, :]
    cos, sin = jnp.cos(ang), jnp.sin(ang)
    n = pos.shape[0]
    pad = HEAD_DIM - ROPE_DIM
    cos_t = jnp.concatenate([cos, cos, jnp.ones((n, pad), F32)], axis=1)
    sin_t = jnp.concatenate([-sin, sin, jnp.zeros((n, pad), F32)], axis=1)
    return cos_t, sin_t


def _window_sums(ext_ref, lvl_refs, g, w, rows):
    cols = slice(g * POOL_GROUP, (g + 1) * POOL_GROUP)
    levels = w.bit_length() - 1
    src, src_cols = ext_ref, cols
    for level in range(levels):
        shift = 1 << level
        last = level == levels - 1
        lo = HALO if last else 8 * (level + 1)
        assert lo - shift >= 8 * level
        total = src[lo:rows, src_cols] + src[lo - shift:rows - shift, src_cols]
        if last:
            return total
        lvl_refs[level % 2][lo:rows, :] = total
        src, src_cols = lvl_refs[level % 2], slice(None)


def _rope(a, cos, sin):
    lane = lax.broadcasted_iota(jnp.int32, a.shape, 1)
    partner = jnp.where(lane < ROPE_HALF, pltpu.roll(a, HEAD_DIM - ROPE_HALF, 1), pltpu.roll(a, ROPE_HALF, 1))
    return a * cos + partner * sin


def _in_proj_pool_kernel(run_casts, x_ref, xnext_ref, xs_ref, g_ref, w_ref, cos_ref, sin_ref, coss_ref, sins_ref,
                         wp_ref, scale_ref,
                         pool_ref, q_ref, k_ref, v_ref, kh_ref, vh_ref, state_ref, us_ref, qsh_ref, ksh_ref, vsh_ref,
                         xn_ref, ext_ref, lvl_a_ref, lvl_b_ref, *, tm, tiles_per_seq):
    step = pl.program_id(0)
    t_in_seq = step % tiles_per_seq
    n_sample = xs_ref.shape[0]

    def tile(xn, with_sample):
        run_casts()
        u = _dot(xn, w_ref[:, 0:POOL_WIDTH])
        ext_ref[HALO:HALO + tm, :] = u[0:tm]
        if with_sample:
            us_ref[...] = u[tm:]
        cos, sin = cos_ref[...], sin_ref[...]
        outs = ((q_ref, None, qsh_ref), (k_ref, kh_ref, ksh_ref), (v_ref, vh_ref, vsh_ref))
        for c, (major_ref, heads_ref, sample_ref) in enumerate(outs, start=1):
            acc = _dot(xn, w_ref[:, c * ATTN_WIDTH:(c + 1) * ATTN_WIDTH])
            for h in range(N_HEADS):
                a = acc[0:tm, h * HEAD_DIM:(h + 1) * HEAD_DIM]
                if major_ref is not v_ref:
                    a = _rope(a, cos, sin)
                major_ref[h] = a
                if heads_ref is not None:
                    _store_heads(heads_ref, HEADS, h, N_HEADS, tm, a)
                if with_sample:
                    a = acc[tm:, h * HEAD_DIM:(h + 1) * HEAD_DIM]
                    if major_ref is not v_ref:
                        a = _rope(a, coss_ref[...], sins_ref[...])
                    _store_heads(sample_ref, HEADS, h, N_HEADS, n_sample, a)

        pos = t_in_seq * tm + lax.broadcasted_iota(jnp.int32, (tm, 1), 0)
        for g, w in enumerate(POOL_WINDOWS):
            cols = slice(g * POOL_GROUP, (g + 1) * POOL_GROUP)
            wsum = _window_sums(ext_ref, (lvl_a_ref, lvl_b_ref), g, w, HALO + tm)
            cnt = jnp.minimum(w, pos + 1).astype(F32)
            pooled = (wsum / cnt - ext_ref[HALO:HALO + tm, cols]).astype(BF16)
            pool_ref[:, cols] = (_dot(pooled, wp_ref[g].astype(BF16)) * scale_ref[:, cols]).astype(pool_ref.dtype)
        state_ref[...] = ext_ref[HALO + tm - STATE_ROWS:HALO + tm, :]
        ext_ref[0:HALO, :] = ext_ref[tm:tm + HALO, :]

        xn_ref[(step + 1) % 2] = _rmsnorm(xnext_ref[...], g_ref[...]).astype(BF16)

    @pl.when(t_in_seq == 0)
    def _():
        ext_ref[0:HALO, :] = jnp.zeros((HALO, POOL_WIDTH), F32)

    @pl.when(step == 0)
    def _():
        rows = jnp.concatenate([x_ref[...], xs_ref[...]], axis=0)
        tile(_rmsnorm(rows, g_ref[...]).astype(BF16), True)

    @pl.when(step > 0)
    def _():
        tile(xn_ref[step % 2], False)


def _in_proj_pool(x, xs, g, w, cos, sin, cos_s, sin_s, w_pool, pool_scale, *, tm, seq, casts=()):
    m, d = x.shape
    ms = xs.shape[0]
    tiles_per_seq = seq // tm
    n_seq = m // seq
    assert POOL_WIDTH == ATTN_WIDTH and w.shape == (d, POOL_WIDTH + 3 * ATTN_WIDTH)
    kern = functools.partial(_in_proj_pool_kernel, tm=tm, tiles_per_seq=tiles_per_seq)
    row_tile = lambda width: pl.BlockSpec((tm, width), lambda i: (i, 0))
    heads_tile = pl.BlockSpec((tm * N_HEADS, HEAD_DIM), lambda i: (i, 0))
    table = pl.BlockSpec((tm, HEAD_DIM), lambda i: (i % tiles_per_seq, 0))
    whole = lambda shape: _resident(shape, lambda i: (0,) * len(shape))
    head_major = jax.ShapeDtypeStruct((N_HEADS, m, HEAD_DIM), F32)
    head_major_tile = pl.BlockSpec((N_HEADS, tm, HEAD_DIM), lambda i: (0, i, 0))
    by_heads = jax.ShapeDtypeStruct((m * N_HEADS, HEAD_DIM), F32)
    sample_heads = jax.ShapeDtypeStruct((ms * N_HEADS, HEAD_DIM), F32)
    sample_heads_spec = pl.BlockSpec((ms * N_HEADS, HEAD_DIM), lambda i: (0, 0))
    return _call_with_casts(
        kern,
        grid=(m // tm,),
        in_specs=[
            row_tile(d),
            pl.BlockSpec((tm, d), lambda i: (jnp.minimum(i + 1, m // tm - 1), 0)),
            whole((ms, d)),
            whole((1, d)),
            whole(w.shape),
            table,
            table,
            whole((ms, HEAD_DIM)),
            whole((ms, HEAD_DIM)),
            whole(w_pool.shape),
            whole((1, POOL_WIDTH)),
        ],
        out_specs=[row_tile(POOL_WIDTH), head_major_tile, head_major_tile, head_major_tile,
                   heads_tile, heads_tile,
                   pl.BlockSpec((STATE_ROWS, POOL_WIDTH), lambda i: (i // tiles_per_seq, 0)),
                   pl.BlockSpec((ms, POOL_WIDTH), lambda i: (0, 0)),
                   sample_heads_spec, sample_heads_spec, sample_heads_spec],
        out_shape=[jax.ShapeDtypeStruct((m, POOL_WIDTH), BF16), head_major, head_major, head_major, by_heads, by_heads,
                   jax.ShapeDtypeStruct((n_seq * STATE_ROWS, POOL_WIDTH), F32),
                   jax.ShapeDtypeStruct((ms, POOL_WIDTH), F32), sample_heads, sample_heads, sample_heads],
        scratch_shapes=[pltpu.VMEM((2, tm, d), BF16),
                        pltpu.VMEM((HALO + tm, POOL_WIDTH), F32),
                        pltpu.VMEM((HALO + tm, POOL_GROUP), F32),
                        pltpu.VMEM((HALO + tm, POOL_GROUP), F32)],
        args=(x, x, xs, g, w, cos, sin, cos_s, sin_s, w_pool, pool_scale),
        casts=casts,
        name="in_proj_pool",
    )


def _mix_sample_kernel(u_ref, prev_ref, wp_ref, scale_ref, attn_ref, x_ref, wo_ref, gpost_ref, gpre_ref, wq_ref,
                       state_ref, x1_ref, qm_ref, *, pos):
    n = u_ref.shape[0]
    for j in range(POOL_STATE - 1):
        state_ref[:, j, :] = prev_ref[:, j + 1, :]
    state_ref[:, POOL_STATE - 1, :] = u_ref[...]
    y = jnp.zeros(x_ref.shape, F32)
    for g, w in enumerate(POOL_WINDOWS):
        cols = slice(g * POOL_GROUP, (g + 1) * POOL_GROUP)
        wsum = u_ref[:, cols]
        for back in range(1, w):
            wsum = wsum + prev_ref[:, POOL_STATE - back, cols]
        cnt = float(min(w, pos + 1))
        pooled = (wsum / cnt - u_ref[:, cols]).astype(BF16)
        pool_out = (_dot(pooled, wp_ref[g].astype(BF16)) * scale_ref[:, cols]).astype(BF16)
        y = y + _dot(pool_out, wo_ref[cols, :])
    for h in range(N_HEADS):
        rows = slice(POOL_WIDTH + h * HEAD_DIM, POOL_WIDTH + (h + 1) * HEAD_DIM)
        y = y + _dot(attn_ref[:, h, :].astype(BF16), wo_ref[rows, :])
    x1 = x_ref[...] + _rmsnorm(y, gpost_ref[...])
    x1_ref[...] = x1
    qm = _dot(_rmsnorm(x1, gpre_ref[...]).astype(BF16), wq_ref[...])
    for h in range(MEM_HEADS):
        _store_heads(qm_ref, HEADS, h, MEM_HEADS, n, qm[:, h * HEAD_DIM:(h + 1) * HEAD_DIM])


def _mix_sample(u, prev, w_pool, pool_scale, attn, x, w_out, g_post, g_pre, w_xq, *, pos):
    n, d = x.shape
    kern = functools.partial(_mix_sample_kernel, pos=pos)
    return pl.pallas_call(
        kern,
        out_shape=[jax.ShapeDtypeStruct(prev.shape, F32), jax.ShapeDtypeStruct((n, d), F32),
                   jax.ShapeDtypeStruct((n * MEM_HEADS, HEAD_DIM), F32)],
        compiler_params=pltpu.CompilerParams(vmem_limit_bytes=VMEM_LIMIT_BYTES),
        name="mix_sample",
    )(u, prev, w_pool, pool_scale, attn, x, w_out, g_post, g_pre, w_xq)


def _run_blocks(count, unroll, body):
    trips = count // unroll
    if trips > 1:
        def trip(t, carry):
            for j in range(unroll):
                body(t * unroll + j)
            return carry
        lax.fori_loop(0, trips, trip, 0)
        done = trips * unroll
    else:
        done = 0
    for idx in range(done, count):
        body(idx)


def _attn_prompt_kernel(run_casts, q_ref, k_ref, v_ref, o_ref, *scratch, seq, unroll):
    run_casts()
    n_br = len(DILATED)
    ob_refs, mb_refs, sb_refs = scratch[:n_br], scratch[n_br:2 * n_br], scratch[2 * n_br:3 * n_br]
    p_ref, bias_ref = scratch[3 * n_br:]
    exp2_scale = HEAD_DIM ** -0.5 * LOG2_E
    qi = lax.broadcasted_iota(jnp.int32, (BLOCK, 2 * BLOCK), 0)
    kj = lax.broadcasted_iota(jnp.int32, (BLOCK, 2 * BLOCK), 1)
    in_band = ((kj < BLOCK) & (kj >= qi)) | ((kj >= BLOCK) & (kj - BLOCK <= qi))
    bias_ref[...] = jnp.where(in_band, 0.0, NEG_INF)
    ones = jnp.ones((2 * BLOCK, HEAD_DIM), BF16)

    def rows(block, r, d):
        if d == 1:
            start = block * BLOCK
            return pl.ds(start if isinstance(start, int) else pl.multiple_of(start, BLOCK), BLOCK)
        return pl.ds(block * (BLOCK * d) + r, BLOCK, stride=d)

    def out_rows(block, r, d):
        if d == PITCHED_DILATION:
            start = r * PITCH
            return pl.ds(start if isinstance(start, int) else pl.multiple_of(start, 8), BLOCK)
        return rows(block, r, d)

    def keys(ref, block, r, d, with_prev):
        cur = ref[rows(block, r, d), :]
        if not with_prev:
            return cur.astype(BF16)
        return jnp.concatenate([ref[rows(block - 1, r, d), :], cur], axis=0).astype(BF16)

    for g, (_, d) in enumerate(DILATED):
        n_blocks = seq // d // BLOCK
        for with_prev in (False, True):
            per_r = n_blocks - 1 if with_prev else 1
            count = d * per_r
            if count == 0:
                continue
            cols = slice(0, 2 * BLOCK) if with_prev else slice(BLOCK, 2 * BLOCK)
            width = 2 * BLOCK if with_prev else BLOCK

            def locate(idx, per_r=per_r, with_prev=with_prev):
                return (idx % per_r + 1, idx // per_r) if with_prev else (0, idx)

            def probabilities(idx, g=g, d=d, with_prev=with_prev, cols=cols, width=width, locate=locate):
                b, r = locate(idx)
                q = q_ref[rows(b, r, d), :].astype(BF16)
                s = _dot_nt(q, keys(k_ref, b, r, d, with_prev)) + bias_ref[:, cols]
                m = jnp.max(s, axis=-1, keepdims=True)
                p_ref[idx, :, 0:width] = jnp.exp2((s - m) * exp2_scale).astype(BF16)
                mb_refs[g][out_rows(b, r, d), :] = jnp.broadcast_to(m, (BLOCK, HEAD_DIM))

            def values(idx, g=g, d=d, with_prev=with_prev, width=width, locate=locate):
                b, r = locate(idx)
                v1 = jnp.concatenate([keys(v_ref, b, r, d, with_prev), ones[0:width]], axis=1)
                acc = _dot(p_ref[idx, :, 0:width], v1)
                ob_refs[g][out_rows(b, r, d), :] = acc[:, 0:HEAD_DIM]
                sb_refs[g][out_rows(b, r, d), :] = acc[:, HEAD_DIM:]

            _run_blocks(count, unroll, probabilities)
            _run_blocks(count, unroll, values)

    chunk = 256

    def combine(c, carry):
        rw = pl.ds(pl.multiple_of(c * chunk, chunk), chunk)

        def read(ref, g):
            if DILATED[g][1] != PITCHED_DILATION:
                return ref[rw, :]
            pieces = [ref[pl.ds((j % 2) * 8 * PITCH + c * (chunk // PITCHED_DILATION) + j // 2, 8, stride=PITCH), :]
                      for j in range(chunk // 8)]
            return jnp.concatenate(pieces, axis=0)

        ms = [read(mb, g) for g, mb in enumerate(mb_refs)]
        m = functools.reduce(jnp.maximum, ms)
        ws = [jnp.exp2((mg - m) * exp2_scale) for mg in ms]
        num = functools.reduce(jnp.add, [w * read(ob, g) for g, (w, ob) in enumerate(zip(ws, ob_refs))])
        den = functools.reduce(jnp.add, [w * read(sb, g) for g, (w, sb) in enumerate(zip(ws, sb_refs))])
        o_ref[rw, :] = (num / den).astype(o_ref.dtype)
        return carry

    lax.fori_loop(0, seq // chunk, combine, 0)


def _attn_prompt(q, k, v, *, seq, casts=()):
    n_heads, m, _ = q.shape
    n_seq = m // seq
    spec = pl.BlockSpec((None, seq, HEAD_DIM), lambda n, h: (h, n, 0))
    kern = functools.partial(_attn_prompt_kernel, seq=seq, unroll=16)
    (attn,), cast_out = _call_with_casts(
        kern,
        grid=(n_seq, n_heads),
        in_specs=[spec, spec, spec],
        out_specs=[spec],
        out_shape=[jax.ShapeDtypeStruct(q.shape, BF16)],
        scratch_shapes=(
            [pltpu.VMEM((PITCHED_DILATION * PITCH if d == PITCHED_DILATION else seq, HEAD_DIM), F32)
             for _ in range(3) for _, d in DILATED]
            + [pltpu.VMEM((seq // BLOCK, BLOCK, 2 * BLOCK), BF16),
               pltpu.VMEM((BLOCK, 2 * BLOCK), F32)]),
        args=(q, k, v),
        casts=casts,
        step_of=lambda n, h: n * n_heads + h,
        name="attn_prompt",
    )
    return attn, cast_out


def _attn_sample_kernel(q_ref, kn_ref, vn_ref, *refs, group):
    n_br = len(DILATED)
    kc_refs, vc_refs, o_ref = refs[:n_br], refs[n_br:2 * n_br], refs[2 * n_br]
    for j in range(group):
        q = q_ref[j] * (HEAD_DIM ** -0.5 * LOG2_E)
        t_new = jnp.sum(q * kn_ref[j], axis=-1, keepdims=True)
        t_br = [jnp.sum(kc[j] * q[None], axis=-1, keepdims=True) for kc in kc_refs]
        m = t_new
        for t in t_br:
            m = jnp.maximum(m, jnp.max(t, axis=0))
        p_new = jnp.exp2(t_new - m) * float(n_br)
        den = p_new
        num = p_new * vn_ref[j]
        for t, vc in zip(t_br, vc_refs):
            p = jnp.exp2(t - m[None])
            den = den + jnp.sum(p, axis=0)
            num = num + jnp.sum(p * vc[j], axis=0)
        o_ref[j] = num / den


def _attn_sample_operands(q, k_new, v_new, k_cache, v_cache, *, group):
    n, heads, _ = q.shape
    win = k_cache.shape[1]
    row = pl.BlockSpec((group, heads, HEAD_DIM), lambda i: (i, 0, 0))
    cache_specs, k_views, v_views = [], [], []
    for w, d in DILATED:
        assert w == BLOCK * d and win % w == 0
        last = win // w - 1
        cache_specs.append(pl.BlockSpec((group, BLOCK, None, heads, HEAD_DIM),
                                        lambda i, last=last: (i, last, 0, 0, 0)))
        k_views.append(k_cache.reshape(n, win // d, d, heads, HEAD_DIM))
        v_views.append(v_cache.reshape(n, win // d, d, heads, HEAD_DIM))
    return ([row, row, row] + cache_specs + cache_specs, row, jax.ShapeDtypeStruct((n, heads, HEAD_DIM), F32),
            (q, k_new, v_new, *k_views, *v_views))


def _mix_mem_prompt_kernel(run_casts, pool_ref, attn_ref, x_ref, mk_ref, mv_ref, wo_ref,
                           gmix_ref, gpre_ref, wq_ref, wxo_ref, gmem_ref, *refs, sample_group):
    n_sample_in = 3 + 2 * len(DILATED)
    sample_in, (x2_ref, attn_s_ref) = refs[:n_sample_in], refs[n_sample_in:]
    run_casts()
    scale = HEAD_DIM ** -0.5
    mixed = jnp.concatenate([pool_ref[...]] + [attn_ref[h] for h in range(N_HEADS)], axis=-1)
    x1 = x_ref[...] + _rmsnorm(_dot(mixed, wo_ref[...]), gmix_ref[...])
    qm = _dot(_rmsnorm(x1, gpre_ref[...]).astype(BF16), wq_ref[...]).astype(BF16)
    heads = []
    for h in range(MEM_HEADS):
        cols = slice(h * HEAD_DIM, (h + 1) * HEAD_DIM)
        s = _dot_nt(qm[:, cols], mk_ref[:, cols].astype(BF16)) * scale
        p = jnp.exp(s - jnp.max(s, axis=-1, keepdims=True))
        o = _dot(p.astype(BF16), mv_ref[:, cols].astype(BF16))
        heads.append((o / jnp.sum(p, axis=-1, keepdims=True)).astype(BF16))
    y = _dot(jnp.concatenate(heads, axis=-1), wxo_ref[...])
    x2_ref[...] = x1 + _rmsnorm(y, gmem_ref[...])

    _attn_sample_kernel(*sample_in, attn_s_ref, group=sample_group)


def _mix_mem_prompt(pool_out, attn, x, mk, mv, w_out, g_mix_post, g_mem_pre, w_xq, w_xo, g_mem_post,
                    sample_attn, *, tm, seq, casts=()):
    m, d = x.shape
    n_tiles = m // tm
    tiles_per_seq = seq // tm
    half = pool_out.shape[1]
    mem_spec = pl.BlockSpec((N_MEM, MEM_WIDTH), lambda i: (i // tiles_per_seq, 0))
    gain_spec = _resident((1, d), lambda i: (0, 0))
    n_sample = sample_attn[0].shape[0]
    assert n_sample % n_tiles == 0
    sample_group = n_sample // n_tiles
    s_in_specs, s_out_spec, s_out_shape, s_args = _attn_sample_operands(*sample_attn, group=sample_group)
    (x2, attn_s), cast_out = _call_with_casts(
        functools.partial(_mix_mem_prompt_kernel, sample_group=sample_group),
        grid=(n_tiles,),
        in_specs=[
            pl.BlockSpec((tm, half), lambda i: (i, 0)),
            pl.BlockSpec((N_HEADS, tm, HEAD_DIM), lambda i: (0, i, 0)),
            pl.BlockSpec((tm, d), lambda i: (i, 0)),
            mem_spec,
            mem_spec,
            _resident(w_out.shape, lambda i: (0, 0)),
            gain_spec,
            gain_spec,
            _resident(w_xq.shape, lambda i: (0, 0)),
            _resident(w_xo.shape, lambda i: (0, 0)),
            gain_spec,
        ] + s_in_specs,
        out_specs=[pl.BlockSpec((tm, d), lambda i: (i, 0)), s_out_spec],
        out_shape=[jax.ShapeDtypeStruct((m, d), F32), s_out_shape],
        args=(pool_out, attn, x, mk, mv, w_out, g_mix_post, g_mem_pre, w_xq, w_xo, g_mem_post) + tuple(s_args),
        casts=casts,
        name="mix_mem_prompt",
    )
    return x2, attn_s, cast_out


def _mem_attn_sample_kernel(qm_ref, mk_ref, mv_ref, o_ref, *, group):
    heads = qm_ref.shape[1]
    both = lambda a: jnp.concatenate([a, a], axis=0)
    fold = lambda a: a[0:heads] + a[heads:]
    for j in range(group):
        q = both(qm_ref[j] * (HEAD_DIM ** -0.5 * LOG2_E))
        t = jnp.sum(mk_ref[j] * q[None], axis=-1, keepdims=True)
        m = jnp.max(t, axis=0)
        m = both(jnp.maximum(m[0:heads], m[heads:]))
        p = jnp.exp2(t - m[None])
        o_ref[j] = fold(jnp.sum(p * mv_ref[j], axis=0)) / fold(jnp.sum(p, axis=0))


def _mem_attn_sample(qm, mem_k, mem_v):
    n, heads, _ = qm.shape
    group = SAMPLE_GROUP
    row = pl.BlockSpec((group, heads, HEAD_DIM), lambda i: (i, 0, 0))
    mem = pl.BlockSpec((group, N_MEM // 2, 2 * heads, HEAD_DIM), lambda i: (i, 0, 0, 0))
    pairs = lambda a: a.reshape(n, N_MEM // 2, 2 * heads, HEAD_DIM)
    return pl.pallas_call(
        functools.partial(_mem_attn_sample_kernel, group=group),
        grid=(n // group,),
        in_specs=[row, mem, mem],
        out_specs=row,
        out_shape=jax.ShapeDtypeStruct((n, heads, HEAD_DIM), F32),
        compiler_params=_params(1),
        name="mem_attn_sample",
    )(qm, pairs(mem_k), pairs(mem_v))


def _mem_out_sample_kernel(a_ref, w_ref, g_ref, r_ref, o_ref):
    y = jnp.zeros(r_ref.shape, F32)
    for h in range(a_ref.shape[1]):
        y = y + _dot(a_ref[:, h, :].astype(BF16), w_ref[h * HEAD_DIM:(h + 1) * HEAD_DIM, :])
    o_ref[...] = r_ref[...] + _rmsnorm(y, g_ref[...])


def _mem_out_sample(a, w, g, resid):
    return pl.pallas_call(
        _mem_out_sample_kernel,
        out_shape=jax.ShapeDtypeStruct(resid.shape, F32),
        compiler_params=pltpu.CompilerParams(vmem_limit_bytes=VMEM_LIMIT_BYTES),
        name="mem_out_sample",
    )(a, w, g, resid)


def _ffn_kernel(x_ref, xnext_ref, xs_ref, gpre_ref, w1_ref, w2_ref, gpost_ref, o_ref, os_ref,
                xn_ref, xns_ref, acc_ref, accs_ref, *, tm, nf):
    i, f = pl.program_id(0), pl.program_id(1)
    last_f = nf - 1
    chunk = tm // nf

    @pl.when((i == 0) & (f == 0))
    def _():
        xn_ref[0] = _rmsnorm(x_ref[...], gpre_ref[...]).astype(BF16)
        xns_ref[...] = _rmsnorm(xs_ref[...], gpre_ref[...]).astype(BF16)
        accs_ref[...] = jnp.zeros_like(accs_ref)

    @pl.when(f == 0)
    def _():
        acc_ref[...] = jnp.zeros_like(acc_ref)

    def mlp(lhs):
        hidden = jnp.square(jnp.maximum(_dot(lhs, w1_ref[...]), 0.0)).astype(BF16)
        return _dot(hidden, w2_ref[...])

    def norm_next_rows():
        rows = pl.ds(pl.multiple_of(f * chunk, chunk), chunk)
        xn_ref[(i + 1) % 2, rows, :] = _rmsnorm(xnext_ref[rows, :], gpre_ref[...]).astype(BF16)

    @pl.when(i == 0)
    def _():
        both = mlp(jnp.concatenate([xn_ref[0], xns_ref[...]], axis=0))
        acc_ref[...] += both[0:tm]
        accs_ref[...] += both[tm:]
        norm_next_rows()

    @pl.when(i > 0)
    def _():
        acc_ref[...] += mlp(xn_ref[i % 2])
        norm_next_rows()

    @pl.when(f == last_f)
    def _():
        o_ref[...] = x_ref[...] + _rmsnorm(acc_ref[...], gpost_ref[...])

    @pl.when((f == last_f) & (i == 0))
    def _():
        os_ref[...] = xs_ref[...] + _rmsnorm(accs_ref[...], gpost_ref[...])


def _ffn(x, xs, g_pre, w1_chunks, w2, g_post, *, tm):
    m, d = x.shape
    ms = xs.shape[0]
    nf, _, tf = w1_chunks.shape
    n_tiles = m // tm
    assert tm % nf == 0 and (tm // nf) % 16 == 0
    sample_spec = _resident((ms, d), lambda i, f: (0, 0))
    return pl.pallas_call(
        functools.partial(_ffn_kernel, tm=tm, nf=nf),
        grid=(n_tiles, nf),
        in_specs=[
            pl.BlockSpec((tm, d), lambda i, f: (i, 0)),
            pl.BlockSpec((tm, d), lambda i, f: (jnp.minimum(i + 1, n_tiles - 1), 0)),
            sample_spec,
            _resident((1, d), lambda i, f: (0, 0)),
            pl.BlockSpec((None, d, tf), lambda i, f: (f, 0, 0)),
            pl.BlockSpec((tf, d), lambda i, f: (f, 0)),
            _resident((1, d), lambda i, f: (0, 0)),
        ],
        out_specs=[pl.BlockSpec((tm, d), lambda i, f: (i, 0)),
                   pl.BlockSpec((ms, d), lambda i, f: (0, 0))],
        out_shape=[jax.ShapeDtypeStruct((m, d), F32), jax.ShapeDtypeStruct((ms, d), F32)],
        scratch_shapes=[pltpu.VMEM((2, tm, d), BF16), pltpu.VMEM((ms, d), BF16),
                        pltpu.VMEM((tm, d), F32), pltpu.VMEM((ms, d), F32)],
        compiler_params=_params(2),
        name="ffn",
    )(x, x, xs, g_pre, w1_chunks, w2, g_post)


def kernel(x_prompt, x_sample, state_pool, cache_attn_k, cache_attn_v, cache_mem_k, cache_mem_v, mem_prompt,
           g_mix_pre, g_mix_post, g_mem_pre, g_mem_post, g_ffn_pre, g_ffn_post, g_mem_kv,
           w_in, w_pool, pool_scale, w_out, w_xq, w_mem_kv, w_xo, w_ff1, w_ff2):
    depth = w_in.shape[0]
    assert depth == 1
    batch, seq, d = x_prompt.shape
    dec_batch, dec_seq, _ = x_sample.shape
    assert dec_seq == 1

    l = 0
    scale = pool_scale[l][None, :]
    gain = lambda g: g[l][None, :]

    xp = x_prompt.reshape(batch * seq, d)
    xs = x_sample.reshape(dec_batch, d)
    cos_p, sin_p = _rope_tables(jnp.arange(seq, dtype=jnp.int32))
    cos_s, sin_s = _rope_tables(jnp.full((dec_batch,), PAST_LEN, jnp.int32))
    w_pool_f = w_pool[l]
    tm = ROW_TILE


    mem = mem_prompt.reshape(batch * N_MEM, d)
    (mk, mv, mk_h, mv_h), (w_in_b,) = _norm_proj(
        mem, gain(g_mem_kv), w_mem_kv[l], tm=MEM_KV_TILE, ncol=MEM_WIDTH,
        outputs=((0, FLAT), (1, FLAT), (0, HEADS), (1, HEADS)), name="mem_kv", casts=(w_in[l],))

    ((pool_out, q, k, v, k_h, v_h, pool_state, us, qs_h, ks_h, vs_h),
     (w_out_b, w_xq_b, w_xo_b)) = _in_proj_pool(
        xp, xs, gain(g_mix_pre), w_in_b, cos_p, sin_p, cos_s, sin_s, w_pool_f, scale, tm=IN_PROJ_TILE, seq=seq,
        casts=(w_out[l], w_xq[l], w_xo[l]))
    attn, (w_ff1_b, w_ff2_b) = _attn_prompt(
        q, k, v, seq=seq, casts=((w_ff1[l], w_ff1.shape[2] // FFN_CHUNK), w_ff2[l]))
    heads3 = lambda a: a.reshape(dec_batch, -1, HEAD_DIM)
    x2, attn_s, _ = _mix_mem_prompt(
        pool_out, attn, xp, mk, mv, w_out_b, gain(g_mix_post), gain(g_mem_pre), w_xq_b, w_xo_b, gain(g_mem_post),
        (heads3(qs_h), heads3(ks_h), heads3(vs_h), cache_attn_k[l], cache_attn_v[l]), tm=tm, seq=seq)

    new_pool_s, x1s, qms_h = _mix_sample(us, state_pool[l], w_pool_f, scale, attn_s, xs, w_out_b,
                                         gain(g_mix_post), gain(g_mem_pre), w_xq_b, pos=PAST_LEN)
    mem_o_s = _mem_attn_sample(heads3(qms_h), cache_mem_k[l], cache_mem_v[l])
    x2s = _mem_out_sample(mem_o_s, w_xo_b, gain(g_mem_post), x1s)

    yp, ys = _ffn(x2, x2s, gain(g_ffn_pre), w_ff1_b, w_ff2_b, gain(g_ffn_post), tm=tm)

    keep = min(max(w for w, _ in DILATED), seq)
    return (
        yp.reshape(batch, seq, d),
        ys.reshape(dec_batch, 1, d),
        pool_state.reshape(batch, STATE_ROWS, POOL_WIDTH)[:, STATE_ROWS - POOL_STATE:][None],
        new_pool_s[None],
        k_h.reshape(batch, seq, N_HEADS, HEAD_DIM)[:, seq - keep:][None],
        v_h.reshape(batch, seq, N_HEADS, HEAD_DIM)[:, seq - keep:][None],
        ks_h.reshape(dec_batch, 1, N_HEADS, HEAD_DIM)[None],
        vs_h.reshape(dec_batch, 1, N_HEADS, HEAD_DIM)[None],
        mk_h.reshape(batch, N_MEM, MEM_HEADS, HEAD_DIM)[None],
        mv_h.reshape(batch, N_MEM, MEM_HEADS, HEAD_DIM)[None],
    )
```

```python
import functools

import jax
import jax.numpy as jnp
from jax import lax
from jax.experimental import pallas as pl
from jax.experimental.pallas import tpu as pltpu

D_MODEL = 2048
POOL_WIDTH = 1024
POOL_WINDOWS = (2, 4, 8, 16)
POOL_GROUP = POOL_WIDTH // len(POOL_WINDOWS)
POOL_STATE = max(POOL_WINDOWS) - 1
HEAD_DIM = 128
ATTN_WIDTH = 1024
N_HEADS = ATTN_WIDTH // HEAD_DIM
DILATED = ((128, 1), (512, 4), (2048, 16))
PITCHED_DILATION = 16
PITCH = 128 + 8
ROPE_DIM = HEAD_DIM // 4
ROPE_HALF = ROPE_DIM // 2
ROPE_THETA = 500000.0
N_MEM = 256
MEM_HEADS = 4
MEM_WIDTH = MEM_HEADS * HEAD_DIM
EPS = 1e-6
BLOCK = 128
NEG_INF = -1e30
LOG2_E = 1.4426950408889634
PAST_LEN = 8192
HALO = 32
STATE_ROWS = 16

F32 = jnp.float32
BF16 = jnp.bfloat16

VMEM_LIMIT_BYTES = 56 * 1024 * 1024

ROW_TILE = 512
IN_PROJ_TILE = 256
MEM_KV_TILE = 256
FFN_CHUNK = 1024
SAMPLE_GROUP = 8

FLAT, HEADS = "flat", "heads"


def _params(n_grid_axes):
    return pltpu.CompilerParams(
        dimension_semantics=("arbitrary",) * n_grid_axes,
        vmem_limit_bytes=VMEM_LIMIT_BYTES,
    )


def _resident(block_shape, index_map):
    return pl.BlockSpec(block_shape, index_map, pipeline_mode=pl.Buffered(1))


def _rmsnorm(x, g):
    ms = jnp.mean(x * x, axis=-1, keepdims=True)
    return x * lax.rsqrt(ms + EPS) * g


def _dot(a, b):
    return jnp.dot(a, b, preferred_element_type=F32)


def _dot_nt(a, b):
    return lax.dot_general(a, b, (((1,), (1,)), ((), ())), preferred_element_type=F32)


def _call_with_casts(kernel_fn, *, grid, in_specs, out_specs, out_shape, scratch_shapes=(), args, casts=(),
                     step_of=None, name):
    casts = [c if isinstance(c, tuple) else (c, 1) for c in casts]
    n_in, n_out, n_casts = len(in_specs), len(out_specs), len(casts)
    n_steps = 1
    for g in grid:
        n_steps *= g
    if step_of is None:
        step_of = lambda i: i
    cast_in_specs, cast_out_specs, cast_shapes = [], [], []
    for a, n_chunks in casts:
        rows, cols = a.shape
        assert rows % n_steps == 0 and cols % n_chunks == 0
        cast_in_specs.append(pl.BlockSpec((rows // n_steps, cols), lambda *idx: (step_of(*idx), 0)))
        if n_chunks == 1:
            cast_out_specs.append(cast_in_specs[-1])
            cast_shapes.append(jax.ShapeDtypeStruct(a.shape, BF16))
        else:
            cast_out_specs.append(pl.BlockSpec((n_chunks, rows // n_steps, cols // n_chunks),
                                               lambda *idx: (0, step_of(*idx), 0)))
            cast_shapes.append(jax.ShapeDtypeStruct((n_chunks, rows, cols // n_chunks), BF16))

    def body(*refs):
        ins, cast_in = refs[:n_in], refs[n_in:n_in + n_casts]
        outs = refs[n_in + n_casts:n_in + n_casts + n_out]
        cast_out = refs[n_in + n_casts + n_out:n_in + 2 * n_casts + n_out]

        def run_casts():
            for i_ref, o_ref, (_, n_chunks) in zip(cast_in, cast_out, casts):
                if n_chunks == 1:
                    o_ref[...] = i_ref[...].astype(BF16)
                else:
                    width = i_ref.shape[1] // n_chunks
                    for c in range(n_chunks):
                        o_ref[c] = i_ref[:, c * width:(c + 1) * width].astype(BF16)

        kernel_fn(run_casts, *ins, *outs, *refs[n_in + 2 * n_casts + n_out:])

    res = pl.pallas_call(
        body,
        grid=grid,
        in_specs=list(in_specs) + cast_in_specs,
        out_specs=list(out_specs) + cast_out_specs,
        out_shape=list(out_shape) + cast_shapes,
        scratch_shapes=list(scratch_shapes),
        compiler_params=_params(len(grid)),
        name=name,
    )(*args, *[a for a, _ in casts])
    return res[:n_out], res[n_out:]


def _store_heads(o_ref, layout, h, heads, rows, value):
    if layout == FLAT:
        o_ref[:, h * HEAD_DIM:(h + 1) * HEAD_DIM] = value
    else:
        o_ref[pl.ds(h, rows, stride=heads), :] = value


def _out_block(layout, tm, heads):
    return (tm, heads * HEAD_DIM) if layout == FLAT else (tm * heads, HEAD_DIM)


def _rope_rows(pos):
    lane = lax.broadcasted_iota(jnp.int32, (1, HEAD_DIM), 1)
    k = (lane % ROPE_HALF).astype(F32)
    inv = jnp.where(lane < ROPE_DIM, jnp.power(jnp.float32(ROPE_THETA), -k * 2.0 / ROPE_DIM), 0.0)
    ang = pos * inv
    return jnp.cos(ang), jnp.sin(ang) * jnp.where(lane < ROPE_HALF, -1.0, 1.0)


def _norm_proj_kernel(run_casts, x_ref, g_ref, w_ref, *out_refs, tm, ncol, outputs, sample_pos):
    run_casts()
    *out_refs, cos_ref, sin_ref, coss_ref, sins_ref = out_refs
    heads = ncol // HEAD_DIM
    xn = _rmsnorm(x_ref[...], g_ref[...]).astype(BF16)
    for c in sorted({c for c, _ in outputs}):
        acc = _dot(xn, w_ref[:, c * ncol:(c + 1) * ncol].astype(BF16))
        for h in range(heads):
            for (oc, layout), o_ref in zip(outputs, out_refs):
                if oc == c:
                    _store_heads(o_ref, layout, h, heads, tm, acc[:, h * HEAD_DIM:(h + 1) * HEAD_DIM])
    rows = cos_ref.shape[0]
    pos = pl.program_id(0) * rows + lax.broadcasted_iota(jnp.int32, (rows, 1), 0)
    cos_ref[...], sin_ref[...] = _rope_rows(pos.astype(F32))
    coss_ref[...], sins_ref[...] = _rope_rows(jnp.full((coss_ref.shape[0], 1), sample_pos, F32))


def _norm_proj(x, g, w, *, tm, ncol, outputs, name, casts=(), seq, n_sample, sample_pos):
    m, d = x.shape
    heads = ncol // HEAD_DIM
    n_steps = m // tm
    assert w.shape[0] == d and w.shape[1] % ncol == 0 and m % tm == 0 and seq % (8 * n_steps) == 0
    kern = functools.partial(_norm_proj_kernel, tm=tm, ncol=ncol, outputs=outputs, sample_pos=sample_pos)
    blocks = [_out_block(layout, tm, heads) for _, layout in outputs]
    table = pl.BlockSpec((seq // n_steps, HEAD_DIM), lambda i: (i, 0))
    sample_table = pl.BlockSpec((n_sample, HEAD_DIM), lambda i: (0, 0))
    return _call_with_casts(
        kern,
        grid=(n_steps,),
        in_specs=[
            pl.BlockSpec((tm, d), lambda i: (i, 0)),
            _resident((1, d), lambda i: (0, 0)),
            _resident(w.shape, lambda i: (0, 0)),
        ],
        out_specs=[pl.BlockSpec(b, lambda i: (i, 0)) for b in blocks] + [table, table, sample_table, sample_table],
        out_shape=([jax.ShapeDtypeStruct((n_steps * b[0], b[1]), F32) for b in blocks]
                   + [jax.ShapeDtypeStruct((seq, HEAD_DIM), F32)] * 2
                   + [jax.ShapeDtypeStruct((n_sample, HEAD_DIM), F32)] * 2),
        args=(x, g, w),
        casts=casts,
        name=name,
    )


def _window_sums(ext_ref, lvl_refs, g, w, rows):
    cols = slice(g * POOL_GROUP, (g + 1) * POOL_GROUP)
    levels = w.bit_length() - 1
    src, src_cols = ext_ref, cols
    for level in range(levels):
        shift = 1 << level
        last = level == levels - 1
        lo = HALO if last else 8 * (level + 1)
        assert lo - shift >= 8 * level
        total = src[lo:rows, src_cols] + src[lo - shift:rows - shift, src_cols]
        if last:
            return total
        lvl_refs[level % 2][lo:rows, :] = total
        src, src_cols = lvl_refs[level % 2], slice(None)


def _rope(a, cos, sin):
    lane = lax.broadcasted_iota(jnp.int32, a.shape, 1)
    partner = jnp.where(lane < ROPE_HALF, pltpu.roll(a, HEAD_DIM - ROPE_HALF, 1), pltpu.roll(a, ROPE_HALF, 1))
    return a * cos + partner * sin


def _in_proj_pool_kernel(run_casts, x_ref, xnext_ref, xs_ref, g_ref, w_ref, cos_ref, sin_ref, coss_ref, sins_ref,
                         wp_ref, scale_ref,
                         pool_ref, q_ref, k_ref, v_ref, kh_ref, vh_ref, state_ref, us_ref, qsh_ref, ksh_ref, vsh_ref,
                         xn_ref, ext_ref, lvl_a_ref, lvl_b_ref, *, tm, tiles_per_seq):
    step = pl.program_id(0)
    t_in_seq = step % tiles_per_seq
    n_sample = xs_ref.shape[0]

    def tile(xn, with_sample):
        run_casts()
        u = _dot(xn, w_ref[:, 0:POOL_WIDTH])
        ext_ref[HALO:HALO + tm, :] = u[0:tm]
        if with_sample:
            us_ref[...] = u[tm:]
        cos, sin = cos_ref[...], sin_ref[...]
        outs = ((q_ref, None, qsh_ref), (k_ref, kh_ref, ksh_ref), (v_ref, vh_ref, vsh_ref))
        for c, (major_ref, heads_ref, sample_ref) in enumerate(outs, start=1):
            acc = _dot(xn, w_ref[:, c * ATTN_WIDTH:(c + 1) * ATTN_WIDTH])
            for h in range(N_HEADS):
                a = acc[0:tm, h * HEAD_DIM:(h + 1) * HEAD_DIM]
                if major_ref is not v_ref:
                    a = _rope(a, cos, sin)
                major_ref[h] = a
                if heads_ref is not None:
                    _store_heads(heads_ref, HEADS, h, N_HEADS, tm, a)
                if with_sample:
                    a = acc[tm:, h * HEAD_DIM:(h + 1) * HEAD_DIM]
                    if major_ref is not v_ref:
                        a = _rope(a, coss_ref[...], sins_ref[...])
                    _store_heads(sample_ref, HEADS, h, N_HEADS, n_sample, a)

        pos = t_in_seq * tm + lax.broadcasted_iota(jnp.int32, (tm, 1), 0)
        for g, w in enumerate(POOL_WINDOWS):
            cols = slice(g * POOL_GROUP, (g + 1) * POOL_GROUP)
            wsum = _window_sums(ext_ref, (lvl_a_ref, lvl_b_ref), g, w, HALO + tm)
            cnt = jnp.minimum(w, pos + 1).astype(F32)
            pooled = (wsum / cnt - ext_ref[HALO:HALO + tm, cols]).astype(BF16)
            pool_ref[:, cols] = (_dot(pooled, wp_ref[g].astype(BF16)) * scale_ref[:, cols]).astype(pool_ref.dtype)
        state_ref[...] = ext_ref[HALO + tm - STATE_ROWS:HALO + tm, :]
        ext_ref[0:HALO, :] = ext_ref[tm:tm + HALO, :]

        xn_ref[(step + 1) % 2] = _rmsnorm(xnext_ref[...], g_ref[...]).astype(BF16)

    @pl.when(t_in_seq == 0)
    def _():
        ext_ref[0:HALO, :] = jnp.zeros((HALO, POOL_WIDTH), F32)

    @pl.when(step == 0)
    def _():
        rows = jnp.concatenate([x_ref[...], xs_ref[...]], axis=0)
        tile(_rmsnorm(rows, g_ref[...]).astype(BF16), True)

    @pl.when(step > 0)
    def _():
        tile(xn_ref[step % 2], False)


def _in_proj_pool(x, xs, g, w, cos, sin, cos_s, sin_s, w_pool, pool_scale, *, tm, seq, casts=()):
    m, d = x.shape
    ms = xs.shape[0]
    tiles_per_seq = seq // tm
    n_seq = m // seq
    assert POOL_WIDTH == ATTN_WIDTH and w.shape == (d, POOL_WIDTH + 3 * ATTN_WIDTH)
    kern = functools.partial(_in_proj_pool_kernel, tm=tm, tiles_per_seq=tiles_per_seq)
    row_tile = lambda width: pl.BlockSpec((tm, width), lambda i: (i, 0))
    heads_tile = pl.BlockSpec((tm * N_HEADS, HEAD_DIM), lambda i: (i, 0))
    table = pl.BlockSpec((tm, HEAD_DIM), lambda i: (i % tiles_per_seq, 0))
    whole = lambda shape: _resident(shape, lambda i: (0,) * len(shape))
    head_major = jax.ShapeDtypeStruct((N_HEADS, m, HEAD_DIM), F32)
    head_major_tile = pl.BlockSpec((N_HEADS, tm, HEAD_DIM), lambda i: (0, i, 0))
    by_heads = jax.ShapeDtypeStruct((m * N_HEADS, HEAD_DIM), F32)
    sample_heads = jax.ShapeDtypeStruct((ms * N_HEADS, HEAD_DIM), F32)
    sample_heads_spec = pl.BlockSpec((ms * N_HEADS, HEAD_DIM), lambda i: (0, 0))
    return _call_with_casts(
        kern,
        grid=(m // tm,),
        in_specs=[
            row_tile(d),
            pl.BlockSpec((tm, d), lambda i: (jnp.minimum(i + 1, m // tm - 1), 0)),
            whole((ms, d)),
            whole((1, d)),
            whole(w.shape),
            table,
            table,
            whole((ms, HEAD_DIM)),
            whole((ms, HEAD_DIM)),
            whole(w_pool.shape),
            whole((1, POOL_WIDTH)),
        ],
        out_specs=[row_tile(POOL_WIDTH), head_major_tile, head_major_tile, head_major_tile,
                   heads_tile, heads_tile,
                   pl.BlockSpec((STATE_ROWS, POOL_WIDTH), lambda i: (i // tiles_per_seq, 0)),
                   pl.BlockSpec((ms, POOL_WIDTH), lambda i: (0, 0)),
                   sample_heads_spec, sample_heads_spec, sample_heads_spec],
        out_shape=[jax.ShapeDtypeStruct((m, POOL_WIDTH), BF16), head_major, head_major, head_major, by_heads, by_heads,
                   jax.ShapeDtypeStruct((n_seq * STATE_ROWS, POOL_WIDTH), F32),
                   jax.ShapeDtypeStruct((ms, POOL_WIDTH), F32), sample_heads, sample_heads, sample_heads],
        scratch_shapes=[pltpu.VMEM((2, tm, d), BF16),
                        pltpu.VMEM((HALO + tm, POOL_WIDTH), F32),
                        pltpu.VMEM((HALO + tm, POOL_GROUP), F32),
                        pltpu.VMEM((HALO + tm, POOL_GROUP), F32)],
        args=(x, x, xs, g, w, cos, sin, cos_s, sin_s, w_pool, pool_scale),
        casts=casts,
        name="in_proj_pool",
    )


def _mix_sample_kernel(u_ref, prev_ref, wp_ref, scale_ref, attn_ref, x_ref, wo_ref, gpost_ref, gpre_ref, wq_ref,
                       state_ref, x1_ref, qm_ref, *, pos):
    n = u_ref.shape[0]
    for j in range(POOL_STATE - 1):
        state_ref[:, j, :] = prev_ref[:, j + 1, :]
    state_ref[:, POOL_STATE - 1, :] = u_ref[...]
    y = jnp.zeros(x_ref.shape, F32)
    for g, w in enumerate(POOL_WINDOWS):
        cols = slice(g * POOL_GROUP, (g + 1) * POOL_GROUP)
        wsum = u_ref[:, cols]
        for back in range(1, w):
            wsum = wsum + prev_ref[:, POOL_STATE - back, cols]
        cnt = float(min(w, pos + 1))
        pooled = (wsum / cnt - u_ref[:, cols]).astype(BF16)
        pool_out = (_dot(pooled, wp_ref[g].astype(BF16)) * scale_ref[:, cols]).astype(BF16)
        y = y + _dot(pool_out, wo_ref[cols, :])
    for h in range(N_HEADS):
        rows = slice(POOL_WIDTH + h * HEAD_DIM, POOL_WIDTH + (h + 1) * HEAD_DIM)
        y = y + _dot(attn_ref[:, h, :].astype(BF16), wo_ref[rows, :])
    x1 = x_ref[...] + _rmsnorm(y, gpost_ref[...])
    x1_ref[...] = x1
    qm = _dot(_rmsnorm(x1, gpre_ref[...]).astype(BF16), wq_ref[...])
    for h in range(MEM_HEADS):
        _store_heads(qm_ref, HEADS, h, MEM_HEADS, n, qm[:, h * HEAD_DIM:(h + 1) * HEAD_DIM])


def _mix_sample(u, prev, w_pool, pool_scale, attn, x, w_out, g_post, g_pre, w_xq, *, pos):
    n, d = x.shape
    kern = functools.partial(_mix_sample_kernel, pos=pos)
    return pl.pallas_call(
        kern,
        out_shape=[jax.ShapeDtypeStruct(prev.shape, F32), jax.ShapeDtypeStruct((n, d), F32),
                   jax.ShapeDtypeStruct((n * MEM_HEADS, HEAD_DIM), F32)],
        compiler_params=pltpu.CompilerParams(vmem_limit_bytes=VMEM_LIMIT_BYTES),
        name="mix_sample",
    )(u, prev, w_pool, pool_scale, attn, x, w_out, g_post, g_pre, w_xq)


def _run_blocks(count, unroll, body):
    trips = count // unroll
    if trips > 1:
        def trip(t, carry):
            for j in range(unroll):
                body(t * unroll + j)
            return carry
        lax.fori_loop(0, trips, trip, 0)
        done = trips * unroll
    else:
        done = 0
    for idx in range(done, count):
        body(idx)


def _attn_prompt_kernel(run_casts, q_ref, k_ref, v_ref, o_ref, *scratch, seq, unroll):
    run_casts()
    n_br = len(DILATED)
    ob_refs, mb_refs, sb_refs = scratch[:n_br], scratch[n_br:2 * n_br], scratch[2 * n_br:3 * n_br]
    p_ref, bias_ref = scratch[3 * n_br:]
    exp2_scale = HEAD_DIM ** -0.5 * LOG2_E
    qi = lax.broadcasted_iota(jnp.int32, (BLOCK, 2 * BLOCK), 0)
    kj = lax.broadcasted_iota(jnp.int32, (BLOCK, 2 * BLOCK), 1)
    in_band = ((kj < BLOCK) & (kj >= qi)) | ((kj >= BLOCK) & (kj - BLOCK <= qi))
    bias_ref[...] = jnp.where(in_band, 0.0, NEG_INF)
    ones = jnp.ones((2 * BLOCK, HEAD_DIM), BF16)

    def rows(block, r, d):
        if d == 1:
            start = block * BLOCK
            return pl.ds(start if isinstance(start, int) else pl.multiple_of(start, BLOCK), BLOCK)
        return pl.ds(block * (BLOCK * d) + r, BLOCK, stride=d)

    def out_rows(block, r, d):
        if d == PITCHED_DILATION:
            start = r * PITCH
            return pl.ds(start if isinstance(start, int) else pl.multiple_of(start, 8), BLOCK)
        return rows(block, r, d)

    def keys(ref, block, r, d, with_prev):
        cur = ref[rows(block, r, d), :]
        if not with_prev:
            return cur.astype(BF16)
        return jnp.concatenate([ref[rows(block - 1, r, d), :], cur], axis=0).astype(BF16)

    for g, (_, d) in enumerate(DILATED):
        n_blocks = seq // d // BLOCK
        for with_prev in (False, True):
            per_r = n_blocks - 1 if with_prev else 1
            count = d * per_r
            if count == 0:
                continue
            cols = slice(0, 2 * BLOCK) if with_prev else slice(BLOCK, 2 * BLOCK)
            width = 2 * BLOCK if with_prev else BLOCK

            def locate(idx, per_r=per_r, with_prev=with_prev):
                return (idx % per_r + 1, idx // per_r) if with_prev else (0, idx)

            def probabilities(idx, g=g, d=d, with_prev=with_prev, cols=cols, width=width, locate=locate):
                b, r = locate(idx)
                q = q_ref[rows(b, r, d), :].astype(BF16)
                s = _dot_nt(q, keys(k_ref, b, r, d, with_prev)) + bias_ref[:, cols]
                m = jnp.max(s, axis=-1, keepdims=True)
                p_ref[idx, :, 0:width] = jnp.exp2((s - m) * exp2_scale).astype(BF16)
                mb_refs[g][out_rows(b, r, d), :] = jnp.broadcast_to(m, (BLOCK, HEAD_DIM))

            def values(idx, g=g, d=d, with_prev=with_prev, width=width, locate=locate):
                b, r = locate(idx)
                v1 = jnp.concatenate([keys(v_ref, b, r, d, with_prev), ones[0:width]], axis=1)
                acc = _dot(p_ref[idx, :, 0:width], v1)
                ob_refs[g][out_rows(b, r, d), :] = acc[:, 0:HEAD_DIM]
                sb_refs[g][out_rows(b, r, d), :] = acc[:, HEAD_DIM:]

            _run_blocks(count, unroll, probabilities)
            _run_blocks(count, unroll, values)

    chunk = 256

    def combine(c, carry):
        rw = pl.ds(pl.multiple_of(c * chunk, chunk), chunk)

        def read(ref, g):
            if DILATED[g][1] != PITCHED_DILATION:
                return ref[rw, :]
            pieces = [ref[pl.ds((j % 2) * 8 * PITCH + c * (chunk // PITCHED_DILATION) + j // 2, 8, stride=PITCH), :]
                      for j in range(chunk // 8)]
            return jnp.concatenate(pieces, axis=0)

        ms = [read(mb, g) for g, mb in enumerate(mb_refs)]
        m = functools.reduce(jnp.maximum, ms)
        ws = [jnp.exp2((mg - m) * exp2_scale) for mg in ms]
        num = functools.reduce(jnp.add, [w * read(ob, g) for g, (w, ob) in enumerate(zip(ws, ob_refs))])
        den = functools.reduce(jnp.add, [w * read(sb, g) for g, (w, sb) in enumerate(zip(ws, sb_refs))])
        o_ref[rw, :] = (num / den).astype(o_ref.dtype)
        return carry

    lax.fori_loop(0, seq // chunk, combine, 0)


def _attn_prompt(q, k, v, *, seq, casts=()):
    n_heads, m, _ = q.shape
    n_seq = m // seq
    spec = pl.BlockSpec((None, seq, HEAD_DIM), lambda n, h: (h, n, 0))
    kern = functools.partial(_attn_prompt_kernel, seq=seq, unroll=16)
    (attn,), cast_out = _call_with_casts(
        kern,
        grid=(n_seq, n_heads),
        in_specs=[spec, spec, spec],
        out_specs=[spec],
        out_shape=[jax.ShapeDtypeStruct(q.shape, BF16)],
        scratch_shapes=(
            [pltpu.VMEM((PITCHED_DILATION * PITCH if d == PITCHED_DILATION else seq, HEAD_DIM), F32)
             for _ in range(3) for _, d in DILATED]
            + [pltpu.VMEM((seq // BLOCK, BLOCK, 2 * BLOCK), BF16),
               pltpu.VMEM((BLOCK, 2 * BLOCK), F32)]),
        args=(q, k, v),
        casts=casts,
        step_of=lambda n, h: n * n_heads + h,
        name="attn_prompt",
    )
    return attn, cast_out


def _attn_sample_kernel(q_ref, kn_ref, vn_ref, *refs, group):
    n_br = len(DILATED)
    kc_refs, vc_refs, o_ref = refs[:n_br], refs[n_br:2 * n_br], refs[2 * n_br]
    for j in range(group):
        q = q_ref[j] * (HEAD_DIM ** -0.5 * LOG2_E)
        t_new = jnp.sum(q * kn_ref[j], axis=-1, keepdims=True)
        t_br = [jnp.sum(kc[j] * q[None], axis=-1, keepdims=True) for kc in kc_refs]
        m = t_new
        for t in t_br:
            m = jnp.maximum(m, jnp.max(t, axis=0))
        p_new = jnp.exp2(t_new - m) * float(n_br)
        den = p_new
        num = p_new * vn_ref[j]
        for t, vc in zip(t_br, vc_refs):
            p = jnp.exp2(t - m[None])
            den = den + jnp.sum(p, axis=0)
            num = num + jnp.sum(p * vc[j], axis=0)
        o_ref[j] = num / den


def _attn_sample_operands(q, k_new, v_new, k_cache, v_cache, *, group):
    n, heads, _ = q.shape
    win = k_cache.shape[1]
    row = pl.BlockSpec((group, heads, HEAD_DIM), lambda i: (i, 0, 0))
    cache_specs, k_views, v_views = [], [], []
    for w, d in DILATED:
        assert w == BLOCK * d and win % w == 0
        last = win // w - 1
        cache_specs.append(pl.BlockSpec((group, BLOCK, None, heads, HEAD_DIM),
                                        lambda i, last=last: (i, last, 0, 0, 0)))
        k_views.append(k_cache.reshape(n, win // d, d, heads, HEAD_DIM))
        v_views.append(v_cache.reshape(n, win // d, d, heads, HEAD_DIM))
    return ([row, row, row] + cache_specs + cache_specs, row, jax.ShapeDtypeStruct((n, heads, HEAD_DIM), F32),
            (q, k_new, v_new, *k_views, *v_views))


def _mix_mem_prompt_kernel(run_casts, pool_ref, attn_ref, x_ref, mk_ref, mv_ref, wo_ref,
                           gmix_ref, gpre_ref, wq_ref, wxo_ref, gmem_ref, *refs, sample_group):
    n_sample_in = 3 + 2 * len(DILATED)
    sample_in, (x2_ref, attn_s_ref) = refs[:n_sample_in], refs[n_sample_in:]
    run_casts()
    scale = HEAD_DIM ** -0.5
    mixed = jnp.concatenate([pool_ref[...]] + [attn_ref[h] for h in range(N_HEADS)], axis=-1)
    x1 = x_ref[...] + _rmsnorm(_dot(mixed, wo_ref[...]), gmix_ref[...])
    qm = _dot(_rmsnorm(x1, gpre_ref[...]).astype(BF16), wq_ref[...]).astype(BF16)
    heads = []
    for h in range(MEM_HEADS):
        cols = slice(h * HEAD_DIM, (h + 1) * HEAD_DIM)
        s = _dot_nt(qm[:, cols], mk_ref[:, cols].astype(BF16)) * scale
        p = jnp.exp(s - jnp.max(s, axis=-1, keepdims=True))
        o = _dot(p.astype(BF16), mv_ref[:, cols].astype(BF16))
        heads.append((o / jnp.sum(p, axis=-1, keepdims=True)).astype(BF16))
    y = _dot(jnp.concatenate(heads, axis=-1), wxo_ref[...])
    x2_ref[...] = x1 + _rmsnorm(y, gmem_ref[...])

    _attn_sample_kernel(*sample_in, attn_s_ref, group=sample_group)


def _mix_mem_prompt(pool_out, attn, x, mk, mv, w_out, g_mix_post, g_mem_pre, w_xq, w_xo, g_mem_post,
                    sample_attn, *, tm, seq, casts=()):
    m, d = x.shape
    n_tiles = m // tm
    tiles_per_seq = seq // tm
    half = pool_out.shape[1]
    mem_spec = pl.BlockSpec((N_MEM, MEM_WIDTH), lambda i: (i // tiles_per_seq, 0))
    gain_spec = _resident((1, d), lambda i: (0, 0))
    n_sample = sample_attn[0].shape[0]
    assert n_sample % n_tiles == 0
    sample_group = n_sample // n_tiles
    s_in_specs, s_out_spec, s_out_shape, s_args = _attn_sample_operands(*sample_attn, group=sample_group)
    (x2, attn_s), cast_out = _call_with_casts(
        functools.partial(_mix_mem_prompt_kernel, sample_group=sample_group),
        grid=(n_tiles,),
        in_specs=[
            pl.BlockSpec((tm, half), lambda i: (i, 0)),
            pl.BlockSpec((N_HEADS, tm, HEAD_DIM), lambda i: (0, i, 0)),
            pl.BlockSpec((tm, d), lambda i: (i, 0)),
            mem_spec,
            mem_spec,
            _resident(w_out.shape, lambda i: (0, 0)),
            gain_spec,
            gain_spec,
            _resident(w_xq.shape, lambda i: (0, 0)),
            _resident(w_xo.shape, lambda i: (0, 0)),
            gain_spec,
        ] + s_in_specs,
        out_specs=[pl.BlockSpec((tm, d), lambda i: (i, 0)), s_out_spec],
        out_shape=[jax.ShapeDtypeStruct((m, d), F32), s_out_shape],
        args=(pool_out, attn, x, mk, mv, w_out, g_mix_post, g_mem_pre, w_xq, w_xo, g_mem_post) + tuple(s_args),
        casts=casts,
        name="mix_mem_prompt",
    )
    return x2, attn_s, cast_out


def _mem_attn_sample_kernel(qm_ref, mk_ref, mv_ref, o_ref, *, group):
    heads = qm_ref.shape[1]
    both = lambda a: jnp.concatenate([a, a], axis=0)
    fold = lambda a: a[0:heads] + a[heads:]
    for j in range(group):
        q = both(qm_ref[j] * (HEAD_DIM ** -0.5 * LOG2_E))
        t = jnp.sum(mk_ref[j] * q[None], axis=-1, keepdims=True)
        m = jnp.max(t, axis=0)
        m = both(jnp.maximum(m[0:heads], m[heads:]))
        p = jnp.exp2(t - m[None])
        o_ref[j] = fold(jnp.sum(p * mv_ref[j], axis=0)) / fold(jnp.sum(p, axis=0))


def _mem_attn_sample(qm, mem_k, mem_v):
    n, heads, _ = qm.shape
    group = SAMPLE_GROUP
    row = pl.BlockSpec((group, heads, HEAD_DIM), lambda i: (i, 0, 0))
    mem = pl.BlockSpec((group, N_MEM // 2, 2 * heads, HEAD_DIM), lambda i: (i, 0, 0, 0))
    pairs = lambda a: a.reshape(n, N_MEM // 2, 2 * heads, HEAD_DIM)
    return pl.pallas_call(
        functools.partial(_mem_attn_sample_kernel, group=group),
        grid=(n // group,),
        in_specs=[row, mem, mem],
        out_specs=row,
        out_shape=jax.ShapeDtypeStruct((n, heads, HEAD_DIM), F32),
        compiler_params=_params(1),
        name="mem_attn_sample",
    )(qm, pairs(mem_k), pairs(mem_v))


def _mem_out_sample_kernel(a_ref, w_ref, g_ref, r_ref, o_ref):
    y = jnp.zeros(r_ref.shape, F32)
    for h in range(a_ref.shape[1]):
        y = y + _dot(a_ref[:, h, :].astype(BF16), w_ref[h * HEAD_DIM:(h + 1) * HEAD_DIM, :])
    o_ref[...] = r_ref[...] + _rmsnorm(y, g_ref[...])


def _mem_out_sample(a, w, g, resid):
    return pl.pallas_call(
        _mem_out_sample_kernel,
        out_shape=jax.ShapeDtypeStruct(resid.shape, F32),
        compiler_params=pltpu.CompilerParams(vmem_limit_bytes=VMEM_LIMIT_BYTES),
        name="mem_out_sample",
    )(a, w, g, resid)


def _ffn_kernel(x_ref, xnext_ref, xs_ref, gpre_ref, w1_ref, w2_ref, gpost_ref, o_ref, os_ref,
                xn_ref, xns_ref, acc_ref, accs_ref, *, tm, nf):
    i, f = pl.program_id(0), pl.program_id(1)
    last_f = nf - 1
    chunk = tm // nf

    @pl.when((i == 0) & (f == 0))
    def _():
        xn_ref[0] = _rmsnorm(x_ref[...], gpre_ref[...]).astype(BF16)
        xns_ref[...] = _rmsnorm(xs_ref[...], gpre_ref[...]).astype(BF16)
        accs_ref[...] = jnp.zeros_like(accs_ref)

    @pl.when(f == 0)
    def _():
        acc_ref[...] = jnp.zeros_like(acc_ref)

    def mlp(lhs):
        hidden = jnp.square(jnp.maximum(_dot(lhs, w1_ref[...]), 0.0)).astype(BF16)
        return _dot(hidden, w2_ref[...])

    def norm_next_rows():
        rows = pl.ds(pl.multiple_of(f * chunk, chunk), chunk)
        xn_ref[(i + 1) % 2, rows, :] = _rmsnorm(xnext_ref[rows, :], gpre_ref[...]).astype(BF16)

    @pl.when(i == 0)
    def _():
        both = mlp(jnp.concatenate([xn_ref[0], xns_ref[...]], axis=0))
        acc_ref[...] += both[0:tm]
        accs_ref[...] += both[tm:]
        norm_next_rows()

    @pl.when(i > 0)
    def _():
        acc_ref[...] += mlp(xn_ref[i % 2])
        norm_next_rows()

    @pl.when(f == last_f)
    def _():
        o_ref[...] = x_ref[...] + _rmsnorm(acc_ref[...], gpost_ref[...])

    @pl.when((f == last_f) & (i == 0))
    def _():
        os_ref[...] = xs_ref[...] + _rmsnorm(accs_ref[...], gpost_ref[...])


def _ffn(x, xs, g_pre, w1_chunks, w2, g_post, *, tm):
    m, d = x.shape
    ms = xs.shape[0]
    nf, _, tf = w1_chunks.shape
    n_tiles = m // tm
    assert tm % nf == 0 and (tm // nf) % 16 == 0
    sample_spec = _resident((ms, d), lambda i, f: (0, 0))
    return pl.pallas_call(
        functools.partial(_ffn_kernel, tm=tm, nf=nf),
        grid=(n_tiles, nf),
        in_specs=[
            pl.BlockSpec((tm, d), lambda i, f: (i, 0)),
            pl.BlockSpec((tm, d), lambda i, f: (jnp.minimum(i + 1, n_tiles - 1), 0)),
            sample_spec,
            _resident((1, d), lambda i, f: (0, 0)),
            pl.BlockSpec((None, d, tf), lambda i, f: (f, 0, 0)),
            pl.BlockSpec((tf, d), lambda i, f: (f, 0)),
            _resident((1, d), lambda i, f: (0, 0)),
        ],
        out_specs=[pl.BlockSpec((tm, d), lambda i, f: (i, 0)),
                   pl.BlockSpec((ms, d), lambda i, f: (0, 0))],
        out_shape=[jax.ShapeDtypeStruct((m, d), F32), jax.ShapeDtypeStruct((ms, d), F32)],
        scratch_shapes=[pltpu.VMEM((2, tm, d), BF16), pltpu.VMEM((ms, d), BF16),
                        pltpu.VMEM((tm, d), F32), pltpu.VMEM((ms, d), F32)],
        compiler_params=_params(2),
        name="ffn",
    )(x, x, xs, g_pre, w1_chunks, w2, g_post)


def kernel(x_prompt, x_sample, state_pool, cache_attn_k, cache_attn_v, cache_mem_k, cache_mem_v, mem_prompt,
           g_mix_pre, g_mix_post, g_mem_pre, g_mem_post, g_ffn_pre, g_ffn_post, g_mem_kv,
           w_in, w_pool, pool_scale, w_out, w_xq, w_mem_kv, w_xo, w_ff1, w_ff2):
    depth = w_in.shape[0]
    assert depth == 1
    batch, seq, d = x_prompt.shape
    dec_batch, dec_seq, _ = x_sample.shape
    assert dec_seq == 1

    l = 0
    scale = pool_scale[l][None, :]
    gain = lambda g: g[l][None, :]

    xp = x_prompt.reshape(batch * seq, d)
    xs = x_sample.reshape(dec_batch, d)
    w_pool_f = w_pool[l]
    tm = ROW_TILE


    mem = mem_prompt.reshape(batch * N_MEM, d)
    (mk, mv, mk_h, mv_h, cos_p, sin_p, cos_s, sin_s), (w_in_b,) = _norm_proj(
        mem, gain(g_mem_kv), w_mem_kv[l], tm=MEM_KV_TILE, ncol=MEM_WIDTH,
        outputs=((0, FLAT), (1, FLAT), (0, HEADS), (1, HEADS)), name="mem_kv", casts=(w_in[l],),
        seq=seq, n_sample=dec_batch, sample_pos=PAST_LEN)

    ((pool_out, q, k, v, k_h, v_h, pool_state, us, qs_h, ks_h, vs_h),
     (w_out_b, w_xq_b, w_xo_b)) = _in_proj_pool(
        xp, xs, gain(g_mix_pre), w_in_b, cos_p, sin_p, cos_s, sin_s, w_pool_f, scale, tm=IN_PROJ_TILE, seq=seq,
        casts=(w_out[l], w_xq[l], w_xo[l]))
    attn, (w_ff1_b, w_ff2_b) = _attn_prompt(
        q, k, v, seq=seq, casts=((w_ff1[l], w_ff1.shape[2] // FFN_CHUNK), w_ff2[l]))
    heads3 = lambda a: a.reshape(dec_batch, -1, HEAD_DIM)
    x2, attn_s, _ = _mix_mem_prompt(
        pool_out, attn, xp, mk, mv, w_out_b, gain(g_mix_post), gain(g_mem_pre), w_xq_b, w_xo_b, gain(g_mem_post),
        (heads3(qs_h), heads3(ks_h), heads3(vs_h), cache_attn_k[l], cache_attn_v[l]), tm=tm, seq=seq)

    new_pool_s, x1s, qms_h = _mix_sample(us, state_pool[l], w_pool_f, scale, attn_s, xs, w_out_b,
                                         gain(g_mix_post), gain(g_mem_pre), w_xq_b, pos=PAST_LEN)
    mem_o_s = _mem_attn_sample(heads3(qms_h), cache_mem_k[l], cache_mem_v[l])
    x2s = _mem_out_sample(mem_o_s, w_xo_b, gain(g_mem_post), x1s)

    yp, ys = _ffn(x2, x2s, gain(g_ffn_pre), w_ff1_b, w_ff2_b, gain(g_ffn_post), tm=tm)

    keep = min(max(w for w, _ in DILATED), seq)
    return (
        yp.reshape(batch, seq, d),
        ys.reshape(dec_batch, 1, d),
        pool_state.reshape(batch, STATE_ROWS, POOL_WIDTH)[:, STATE_ROWS - POOL_STATE:][None],
        new_pool_s[None],
        k_h.reshape(batch, seq, N_HEADS, HEAD_DIM)[:, seq - keep:][None],
        v_h.reshape(batch, seq, N_HEADS, HEAD_DIM)[:, seq - keep:][None],
        ks_h.reshape(dec_batch, 1, N_HEADS, HEAD_DIM)[None],
        vs_h.reshape(dec_batch, 1, N_HEADS, HEAD_DIM)[None],
        mk_h.reshape(batch, N_MEM, MEM_HEADS, HEAD_DIM)[None],
        mv_h.reshape(batch, N_MEM, MEM_HEADS, HEAD_DIM)[None],
    )
```

```python
import functools

import jax
import jax.numpy as jnp
from jax import lax
from jax.experimental import pallas as pl
from jax.experimental.pallas import tpu as pltpu

D_MODEL = 2048
POOL_WIDTH = 1024
POOL_WINDOWS = (2, 4, 8, 16)
POOL_GROUP = POOL_WIDTH // len(POOL_WINDOWS)
POOL_STATE = max(POOL_WINDOWS) - 1
HEAD_DIM = 128
ATTN_WIDTH = 1024
N_HEADS = ATTN_WIDTH // HEAD_DIM
DILATED = ((128, 1), (512, 4), (2048, 16))
PITCHED_DILATION = 16
PITCH = 128 + 8
ROPE_DIM = HEAD_DIM // 4
ROPE_HALF = ROPE_DIM // 2
ROPE_THETA = 500000.0
N_MEM = 256
MEM_HEADS = 4
MEM_WIDTH = MEM_HEADS * HEAD_DIM
EPS = 1e-6
BLOCK = 128
NEG_INF = -1e30
LOG2_E = 1.4426950408889634
PAST_LEN = 8192
HALO = 32
STATE_ROWS = 16

F32 = jnp.float32
BF16 = jnp.bfloat16

VMEM_LIMIT_BYTES = 56 * 1024 * 1024

ROW_TILE = 512
IN_PROJ_TILE = 256
MEM_KV_TILE = 256
FFN_CHUNK = 1024
SAMPLE_GROUP = 8

FLAT, HEADS = "flat", "heads"


def _params(n_grid_axes):
    return pltpu.CompilerParams(
        dimension_semantics=("arbitrary",) * n_grid_axes,
        vmem_limit_bytes=VMEM_LIMIT_BYTES,
    )


def _resident(block_shape, index_map):
    return pl.BlockSpec(block_shape, index_map, pipeline_mode=pl.Buffered(1))


def _rmsnorm(x, g):
    ms = jnp.mean(x * x, axis=-1, keepdims=True)
    return x * lax.rsqrt(ms + EPS) * g


def _dot(a, b):
    return jnp.dot(a, b, preferred_element_type=F32)


def _dot_nt(a, b):
    return lax.dot_general(a, b, (((1,), (1,)), ((), ())), preferred_element_type=F32)


def _call_with_casts(kernel_fn, *, grid, in_specs, out_specs, out_shape, scratch_shapes=(), args, casts=(),
                     step_of=None, name):
    casts = [c if isinstance(c, tuple) else (c, 1) for c in casts]
    n_in, n_out, n_casts = len(in_specs), len(out_specs), len(casts)
    n_steps = 1
    for g in grid:
        n_steps *= g
    if step_of is None:
        step_of = lambda i: i
    cast_in_specs, cast_out_specs, cast_shapes = [], [], []
    for a, n_chunks in casts:
        rows, cols = a.shape
        assert rows % n_steps == 0 and cols % n_chunks == 0
        cast_in_specs.append(pl.BlockSpec((rows // n_steps, cols), lambda *idx: (step_of(*idx), 0)))
        if n_chunks == 1:
            cast_out_specs.append(cast_in_specs[-1])
            cast_shapes.append(jax.ShapeDtypeStruct(a.shape, BF16))
        else:
            cast_out_specs.append(pl.BlockSpec((n_chunks, rows // n_steps, cols // n_chunks),
                                               lambda *idx: (0, step_of(*idx), 0)))
            cast_shapes.append(jax.ShapeDtypeStruct((n_chunks, rows, cols // n_chunks), BF16))

    def body(*refs):
        ins, cast_in = refs[:n_in], refs[n_in:n_in + n_casts]
        outs = refs[n_in + n_casts:n_in + n_casts + n_out]
        cast_out = refs[n_in + n_casts + n_out:n_in + 2 * n_casts + n_out]

        def run_casts():
            for i_ref, o_ref, (_, n_chunks) in zip(cast_in, cast_out, casts):
                if n_chunks == 1:
                    o_ref[...] = i_ref[...].astype(BF16)
                else:
                    width = i_ref.shape[1] // n_chunks
                    for c in range(n_chunks):
                        o_ref[c] = i_ref[:, c * width:(c + 1) * width].astype(BF16)

        kernel_fn(run_casts, *ins, *outs, *refs[n_in + 2 * n_casts + n_out:])

    res = pl.pallas_call(
        body,
        grid=grid,
        in_specs=list(in_specs) + cast_in_specs,
        out_specs=list(out_specs) + cast_out_specs,
        out_shape=list(out_shape) + cast_shapes,
        scratch_shapes=list(scratch_shapes),
        compiler_params=_params(len(grid)),
        name=name,
    )(*args, *[a for a, _ in casts])
    return res[:n_out], res[n_out:]


def _store_heads(o_ref, layout, h, heads, rows, value):
    if layout == FLAT:
        o_ref[:, h * HEAD_DIM:(h + 1) * HEAD_DIM] = value
    else:
        o_ref[pl.ds(h, rows, stride=heads), :] = value


def _out_block(layout, tm, heads):
    return (tm, heads * HEAD_DIM) if layout == FLAT else (tm * heads, HEAD_DIM)


def _rope_rows(pos):
    lane = lax.broadcasted_iota(jnp.int32, (1, HEAD_DIM), 1)
    k = (lane % ROPE_HALF).astype(F32)
    inv = jnp.where(lane < ROPE_DIM, jnp.power(jnp.float32(ROPE_THETA), -k * 2.0 / ROPE_DIM), 0.0)
    ang = pos * inv
    return jnp.cos(ang), jnp.sin(ang) * jnp.where(lane < ROPE_HALF, -1.0, 1.0)


def _norm_proj_kernel(run_casts, x_ref, g_ref, w_ref, *out_refs, tm, ncol, outputs, sample_pos):
    run_casts()
    *out_refs, cos_ref, sin_ref, coss_ref, sins_ref = out_refs
    heads = ncol // HEAD_DIM
    xn = _rmsnorm(x_ref[...], g_ref[...]).astype(BF16)
    for c in sorted({c for c, _ in outputs}):
        acc = _dot(xn, w_ref[:, c * ncol:(c + 1) * ncol].astype(BF16))
        for h in range(heads):
            for (oc, layout), o_ref in zip(outputs, out_refs):
                if oc == c:
                    _store_heads(o_ref, layout, h, heads, tm, acc[:, h * HEAD_DIM:(h + 1) * HEAD_DIM])
    rows = cos_ref.shape[0]
    pos = pl.program_id(0) * rows + lax.broadcasted_iota(jnp.int32, (rows, 1), 0)
    cos_ref[...], sin_ref[...] = _rope_rows(pos.astype(F32))
    coss_ref[...], sins_ref[...] = _rope_rows(jnp.full((coss_ref.shape[0], 1), sample_pos, F32))


def _norm_proj(x, g, w, *, tm, ncol, outputs, name, casts=(), seq, n_sample, sample_pos):
    m, d = x.shape
    heads = ncol // HEAD_DIM
    n_steps = m // tm
    assert w.shape[0] == d and w.shape[1] % ncol == 0 and m % tm == 0 and seq % (8 * n_steps) == 0
    kern = functools.partial(_norm_proj_kernel, tm=tm, ncol=ncol, outputs=outputs, sample_pos=sample_pos)
    blocks = [_out_block(layout, tm, heads) for _, layout in outputs]
    table = pl.BlockSpec((seq // n_steps, HEAD_DIM), lambda i: (i, 0))
    sample_table = pl.BlockSpec((n_sample, HEAD_DIM), lambda i: (0, 0))
    return _call_with_casts(
        kern,
        grid=(n_steps,),
        in_specs=[
            pl.BlockSpec((tm, d), lambda i: (i, 0)),
            _resident((1, d), lambda i: (0, 0)),
            _resident(w.shape, lambda i: (0, 0)),
        ],
        out_specs=[pl.BlockSpec(b, lambda i: (i, 0)) for b in blocks] + [table, table, sample_table, sample_table],
        out_shape=([jax.ShapeDtypeStruct((n_steps * b[0], b[1]), F32) for b in blocks]
                   + [jax.ShapeDtypeStruct((seq, HEAD_DIM), F32)] * 2
                   + [jax.ShapeDtypeStruct((n_sample, HEAD_DIM), F32)] * 2),
        args=(x, g, w),
        casts=casts,
        name=name,
    )


def _window_sums(ext_ref, lvl_refs, g, w, rows):
    cols = slice(g * POOL_GROUP, (g + 1) * POOL_GROUP)
    levels = w.bit_length() - 1
    src, src_cols = ext_ref, cols
    for level in range(levels):
        shift = 1 << level
        last = level == levels - 1
        lo = HALO if last else 8 * (level + 1)
        assert lo - shift >= 8 * level
        total = src[lo:rows, src_cols] + src[lo - shift:rows - shift, src_cols]
        if last:
            return total
        lvl_refs[level % 2][lo:rows, :] = total
        src, src_cols = lvl_refs[level % 2], slice(None)


def _rope(a, cos, sin):
    lane = lax.broadcasted_iota(jnp.int32, a.shape, 1)
    partner = jnp.where(lane < ROPE_HALF, pltpu.roll(a, HEAD_DIM - ROPE_HALF, 1), pltpu.roll(a, ROPE_HALF, 1))
    return a * cos + partner * sin


def _in_proj_pool_kernel(run_casts, x_ref, xnext_ref, xs_ref, g_ref, w_ref, cos_ref, sin_ref, coss_ref, sins_ref,
                         wp_ref, scale_ref,
                         pool_ref, q_ref, k_ref, v_ref, kh_ref, vh_ref, state_ref, us_ref, qsh_ref, ksh_ref, vsh_ref,
                         xn_ref, ext_ref, lvl_a_ref, lvl_b_ref, *, tm, tiles_per_seq):
    step = pl.program_id(0)
    t_in_seq = step % tiles_per_seq
    n_sample = xs_ref.shape[0]

    def tile(xn, with_sample):
        run_casts()
        u = _dot(xn, w_ref[:, 0:POOL_WIDTH])
        ext_ref[HALO:HALO + tm, :] = u[0:tm]
        if with_sample:
            us_ref[...] = u[tm:]
        cos, sin = cos_ref[...], sin_ref[...]
        outs = ((q_ref, None, qsh_ref), (k_ref, kh_ref, ksh_ref), (v_ref, vh_ref, vsh_ref))
        for c, (major_ref, heads_ref, sample_ref) in enumerate(outs, start=1):
            acc = _dot(xn, w_ref[:, c * ATTN_WIDTH:(c + 1) * ATTN_WIDTH])
            for h in range(N_HEADS):
                a = acc[0:tm, h * HEAD_DIM:(h + 1) * HEAD_DIM]
                if major_ref is not v_ref:
                    a = _rope(a, cos, sin)
                major_ref[h] = a
                if heads_ref is not None:
                    _store_heads(heads_ref, HEADS, h, N_HEADS, tm, a)
                if with_sample:
                    a = acc[tm:, h * HEAD_DIM:(h + 1) * HEAD_DIM]
                    if major_ref is not v_ref:
                        a = _rope(a, coss_ref[...], sins_ref[...])
                    _store_heads(sample_ref, HEADS, h, N_HEADS, n_sample, a)

        pos = t_in_seq * tm + lax.broadcasted_iota(jnp.int32, (tm, 1), 0)
        for g, w in enumerate(POOL_WINDOWS):
            cols = slice(g * POOL_GROUP, (g + 1) * POOL_GROUP)
            wsum = _window_sums(ext_ref, (lvl_a_ref, lvl_b_ref), g, w, HALO + tm)
            cnt = jnp.minimum(w, pos + 1).astype(F32)
            pooled = (wsum / cnt - ext_ref[HALO:HALO + tm, cols]).astype(BF16)
            pool_ref[:, cols] = (_dot(pooled, wp_ref[g].astype(BF16)) * scale_ref[:, cols]).astype(pool_ref.dtype)
        state_ref[...] = ext_ref[HALO + tm - STATE_ROWS:HALO + tm, :]
        ext_ref[0:HALO, :] = ext_ref[tm:tm + HALO, :]

        xn_ref[(step + 1) % 2] = _rmsnorm(xnext_ref[...], g_ref[...]).astype(BF16)

    @pl.when(t_in_seq == 0)
    def _():
        ext_ref[0:HALO, :] = jnp.zeros((HALO, POOL_WIDTH), F32)

    @pl.when(step == 0)
    def _():
        rows = jnp.concatenate([x_ref[...], xs_ref[...]], axis=0)
        tile(_rmsnorm(rows, g_ref[...]).astype(BF16), True)

    @pl.when(step > 0)
    def _():
        tile(xn_ref[step % 2], False)


def _in_proj_pool(x, xs, g, w, cos, sin, cos_s, sin_s, w_pool, pool_scale, *, tm, seq, casts=()):
    m, d = x.shape
    ms = xs.shape[0]
    tiles_per_seq = seq // tm
    n_seq = m // seq
    assert POOL_WIDTH == ATTN_WIDTH and w.shape == (d, POOL_WIDTH + 3 * ATTN_WIDTH)
    kern = functools.partial(_in_proj_pool_kernel, tm=tm, tiles_per_seq=tiles_per_seq)
    row_tile = lambda width: pl.BlockSpec((tm, width), lambda i: (i, 0))
    heads_tile = pl.BlockSpec((tm * N_HEADS, HEAD_DIM), lambda i: (i, 0))
    table = pl.BlockSpec((tm, HEAD_DIM), lambda i: (i % tiles_per_seq, 0))
    whole = lambda shape: _resident(shape, lambda i: (0,) * len(shape))
    head_major = jax.ShapeDtypeStruct((N_HEADS, m, HEAD_DIM), F32)
    head_major_tile = pl.BlockSpec((N_HEADS, tm, HEAD_DIM), lambda i: (0, i, 0))
    by_heads = jax.ShapeDtypeStruct((m * N_HEADS, HEAD_DIM), F32)
    sample_heads = jax.ShapeDtypeStruct((ms * N_HEADS, HEAD_DIM), F32)
    sample_heads_spec = pl.BlockSpec((ms * N_HEADS, HEAD_DIM), lambda i: (0, 0))
    return _call_with_casts(
        kern,
        grid=(m // tm,),
        in_specs=[
            row_tile(d),
            pl.BlockSpec((tm, d), lambda i: (jnp.minimum(i + 1, m // tm - 1), 0)),
            whole((ms, d)),
            whole((1, d)),
            whole(w.shape),
            table,
            table,
            whole((ms, HEAD_DIM)),
            whole((ms, HEAD_DIM)),
            whole(w_pool.shape),
            whole((1, POOL_WIDTH)),
        ],
        out_specs=[row_tile(POOL_WIDTH), head_major_tile, head_major_tile, head_major_tile,
                   heads_tile, heads_tile,
                   pl.BlockSpec((STATE_ROWS, POOL_WIDTH), lambda i: (i // tiles_per_seq, 0)),
                   pl.BlockSpec((ms, POOL_WIDTH), lambda i: (0, 0)),
                   sample_heads_spec, sample_heads_spec, sample_heads_spec],
        out_shape=[jax.ShapeDtypeStruct((m, POOL_WIDTH), BF16), head_major, head_major, head_major, by_heads, by_heads,
                   jax.ShapeDtypeStruct((n_seq * STATE_ROWS, POOL_WIDTH), F32),
                   jax.ShapeDtypeStruct((ms, POOL_WIDTH), F32), sample_heads, sample_heads, sample_heads],
        scratch_shapes=[pltpu.VMEM((2, tm, d), BF16),
                        pltpu.VMEM((HALO + tm, POOL_WIDTH), F32),
                        pltpu.VMEM((HALO + tm, POOL_GROUP), F32),
                        pltpu.VMEM((HALO + tm, POOL_GROUP), F32)],
        args=(x, x, xs, g, w, cos, sin, cos_s, sin_s, w_pool, pool_scale),
        casts=casts,
        name="in_proj_pool",
    )


def _mix_sample_kernel(u_ref, prev_ref, wp_ref, scale_ref, attn_ref, x_ref, wo_ref, gpost_ref, gpre_ref, wq_ref,
                       state_ref, x1_ref, qm_ref, *, pos):
    n = u_ref.shape[0]
    for j in range(POOL_STATE - 1):
        state_ref[:, j, :] = prev_ref[:, j + 1, :]
    state_ref[:, POOL_STATE - 1, :] = u_ref[...]
    y = jnp.zeros(x_ref.shape, F32)
    for g, w in enumerate(POOL_WINDOWS):
        cols = slice(g * POOL_GROUP, (g + 1) * POOL_GROUP)
        wsum = u_ref[:, cols]
        for back in range(1, w):
            wsum = wsum + prev_ref[:, POOL_STATE - back, cols]
        cnt = float(min(w, pos + 1))
        pooled = (wsum / cnt - u_ref[:, cols]).astype(BF16)
        pool_out = (_dot(pooled, wp_ref[g].astype(BF16)) * scale_ref[:, cols]).astype(BF16)
        y = y + _dot(pool_out, wo_ref[cols, :])
    for h in range(N_HEADS):
        rows = slice(POOL_WIDTH + h * HEAD_DIM, POOL_WIDTH + (h + 1) * HEAD_DIM)
        y = y + _dot(attn_ref[:, h, :].astype(BF16), wo_ref[rows, :])
    x1 = x_ref[...] + _rmsnorm(y, gpost_ref[...])
    x1_ref[...] = x1
    qm = _dot(_rmsnorm(x1, gpre_ref[...]).astype(BF16), wq_ref[...])
    for h in range(MEM_HEADS):
        _store_heads(qm_ref, HEADS, h, MEM_HEADS, n, qm[:, h * HEAD_DIM:(h + 1) * HEAD_DIM])


def _mix_sample(u, prev, w_pool, pool_scale, attn, x, w_out, g_post, g_pre, w_xq, *, pos):
    n, d = x.shape
    kern = functools.partial(_mix_sample_kernel, pos=pos)
    return pl.pallas_call(
        kern,
        out_shape=[jax.ShapeDtypeStruct(prev.shape, F32), jax.ShapeDtypeStruct((n, d), F32),
                   jax.ShapeDtypeStruct((n * MEM_HEADS, HEAD_DIM), F32)],
        compiler_params=pltpu.CompilerParams(vmem_limit_bytes=VMEM_LIMIT_BYTES),
        name="mix_sample",
    )(u, prev, w_pool, pool_scale, attn, x, w_out, g_post, g_pre, w_xq)


def _run_blocks(count, unroll, body):
    trips = count // unroll
    if trips > 1:
        def trip(t, carry):
            for j in range(unroll):
                body(t * unroll + j)
            return carry
        lax.fori_loop(0, trips, trip, 0)
        done = trips * unroll
    else:
        done = 0
    for idx in range(done, count):
        body(idx)


def _attn_prompt_kernel(run_casts, q_ref, k_ref, v_ref, o_ref, *scratch, seq, unroll):
    run_casts()
    n_br = len(DILATED)
    ob_refs, mb_refs, sb_refs = scratch[:n_br], scratch[n_br:2 * n_br], scratch[2 * n_br:3 * n_br]
    p_ref, bias_ref, q4_ref, k4_ref, v4_ref = scratch[3 * n_br:]
    exp2_scale = HEAD_DIM ** -0.5 * LOG2_E
    qi = lax.broadcasted_iota(jnp.int32, (BLOCK, 2 * BLOCK), 0)
    kj = lax.broadcasted_iota(jnp.int32, (BLOCK, 2 * BLOCK), 1)
    in_band = ((kj < BLOCK) & (kj >= qi)) | ((kj >= BLOCK) & (kj - BLOCK <= qi))
    bias_ref[...] = jnp.where(in_band, 0.0, NEG_INF)
    ones = jnp.ones((2 * BLOCK, HEAD_DIM), BF16)

    def rows(block, r, d):
        if d == 1:
            start = block * BLOCK
            return pl.ds(start if isinstance(start, int) else pl.multiple_of(start, BLOCK), BLOCK)
        return pl.ds(block * (BLOCK * d) + r, BLOCK, stride=d)

    def out_rows(block, r, d):
        if d == PITCHED_DILATION:
            start = r * PITCH
            return pl.ds(start if isinstance(start, int) else pl.multiple_of(start, 8), BLOCK)
        return rows(block, r, d)

    quarter = seq // 4
    assert [d for _, d in DILATED] == [1, 4, 16]
    for ref, by4_ref in ((q_ref, q4_ref), (k_ref, k4_ref), (v_ref, v4_ref)):
        for b in range(4):
            by4_ref[b * quarter:(b + 1) * quarter, :] = ref[pl.ds(b, quarter, stride=4), :]

    def in_block(refs, block, r, d):
        ref, by4_ref = refs
        if d == 1:
            return ref[rows(block, r, d), :]
        if d == 4:
            return by4_ref[pl.ds(r * quarter + block * BLOCK, BLOCK), :]
        return by4_ref[pl.ds((r % 4) * quarter + r // 4 + block * (4 * BLOCK), BLOCK, stride=4), :]

    def keys(refs, block, r, d, with_prev):
        cur = in_block(refs, block, r, d)
        if not with_prev:
            return cur.astype(BF16)
        return jnp.concatenate([in_block(refs, block - 1, r, d), cur], axis=0).astype(BF16)

    for g, (_, d) in enumerate(DILATED):
        n_blocks = seq // d // BLOCK
        for with_prev in (False, True):
            per_r = n_blocks - 1 if with_prev else 1
            count = d * per_r
            if count == 0:
                continue
            cols = slice(0, 2 * BLOCK) if with_prev else slice(BLOCK, 2 * BLOCK)
            width = 2 * BLOCK if with_prev else BLOCK

            def locate(idx, per_r=per_r, with_prev=with_prev):
                return (idx % per_r + 1, idx // per_r) if with_prev else (0, idx)

            def probabilities(idx, g=g, d=d, with_prev=with_prev, cols=cols, width=width, locate=locate):
                b, r = locate(idx)
                q = in_block((q_ref, q4_ref), b, r, d).astype(BF16)
                s = _dot_nt(q, keys((k_ref, k4_ref), b, r, d, with_prev)) + bias_ref[:, cols]
                m = jnp.max(s, axis=-1, keepdims=True)
                p_ref[idx, :, 0:width] = jnp.exp2((s - m) * exp2_scale).astype(BF16)
                mb_refs[g][out_rows(b, r, d), :] = jnp.broadcast_to(m, (BLOCK, HEAD_DIM))

            def values(idx, g=g, d=d, with_prev=with_prev, width=width, locate=locate):
                b, r = locate(idx)
                v1 = jnp.concatenate([keys((v_ref, v4_ref), b, r, d, with_prev), ones[0:width]], axis=1)
                acc = _dot(p_ref[idx, :, 0:width], v1)
                ob_refs[g][out_rows(b, r, d), :] = acc[:, 0:HEAD_DIM]
                sb_refs[g][out_rows(b, r, d), :] = acc[:, HEAD_DIM:]

            _run_blocks(count, unroll, probabilities)
            _run_blocks(count, unroll, values)

    chunk = 256

    def combine(c, carry):
        rw = pl.ds(pl.multiple_of(c * chunk, chunk), chunk)

        def read(ref, g):
            if DILATED[g][1] != PITCHED_DILATION:
                return ref[rw, :]
            pieces = [ref[pl.ds((j % 2) * 8 * PITCH + c * (chunk // PITCHED_DILATION) + j // 2, 8, stride=PITCH), :]
                      for j in range(chunk // 8)]
            return jnp.concatenate(pieces, axis=0)

        ms = [read(mb, g) for g, mb in enumerate(mb_refs)]
        m = functools.reduce(jnp.maximum, ms)
        ws = [jnp.exp2((mg - m) * exp2_scale) for mg in ms]
        num = functools.reduce(jnp.add, [w * read(ob, g) for g, (w, ob) in enumerate(zip(ws, ob_refs))])
        den = functools.reduce(jnp.add, [w * read(sb, g) for g, (w, sb) in enumerate(zip(ws, sb_refs))])
        o_ref[rw, :] = (num / den).astype(o_ref.dtype)
        return carry

    lax.fori_loop(0, seq // chunk, combine, 0)


def _attn_prompt(q, k, v, *, seq, casts=()):
    n_heads, m, _ = q.shape
    n_seq = m // seq
    spec = pl.BlockSpec((None, seq, HEAD_DIM), lambda n, h: (h, n, 0))
    kern = functools.partial(_attn_prompt_kernel, seq=seq, unroll=16)
    (attn,), cast_out = _call_with_casts(
        kern,
        grid=(n_seq, n_heads),
        in_specs=[spec, spec, spec],
        out_specs=[spec],
        out_shape=[jax.ShapeDtypeStruct(q.shape, BF16)],
        scratch_shapes=(
            [pltpu.VMEM((PITCHED_DILATION * PITCH if d == PITCHED_DILATION else seq, HEAD_DIM), F32)
             for _ in range(3) for _, d in DILATED]
            + [pltpu.VMEM((seq // BLOCK, BLOCK, 2 * BLOCK), BF16),
               pltpu.VMEM((BLOCK, 2 * BLOCK), F32)]
            + [pltpu.VMEM((seq, HEAD_DIM), F32) for _ in range(3)]),
        args=(q, k, v),
        casts=casts,
        step_of=lambda n, h: n * n_heads + h,
        name="attn_prompt",
    )
    return attn, cast_out


def _attn_sample_kernel(q_ref, kn_ref, vn_ref, *refs, group):
    n_br = len(DILATED)
    kc_refs, vc_refs, o_ref = refs[:n_br], refs[n_br:2 * n_br], refs[2 * n_br]
    for j in range(group):
        q = q_ref[j] * (HEAD_DIM ** -0.5 * LOG2_E)
        t_new = jnp.sum(q * kn_ref[j], axis=-1, keepdims=True)
        t_br = [jnp.sum(kc[j] * q[None], axis=-1, keepdims=True) for kc in kc_refs]
        m = t_new
        for t in t_br:
            m = jnp.maximum(m, jnp.max(t, axis=0))
        p_new = jnp.exp2(t_new - m) * float(n_br)
        den = p_new
        num = p_new * vn_ref[j]
        for t, vc in zip(t_br, vc_refs):
            p = jnp.exp2(t - m[None])
            den = den + jnp.sum(p, axis=0)
            num = num + jnp.sum(p * vc[j], axis=0)
        o_ref[j] = num / den


def _attn_sample_operands(q, k_new, v_new, k_cache, v_cache, *, group):
    n, heads, _ = q.shape
    win = k_cache.shape[1]
    row = pl.BlockSpec((group, heads, HEAD_DIM), lambda i: (i, 0, 0))
    cache_specs, k_views, v_views = [], [], []
    for w, d in DILATED:
        assert w == BLOCK * d and win % w == 0
        last = win // w - 1
        cache_specs.append(pl.BlockSpec((group, BLOCK, None, heads, HEAD_DIM),
                                        lambda i, last=last: (i, last, 0, 0, 0)))
        k_views.append(k_cache.reshape(n, win // d, d, heads, HEAD_DIM))
        v_views.append(v_cache.reshape(n, win // d, d, heads, HEAD_DIM))
    return ([row, row, row] + cache_specs + cache_specs, row, jax.ShapeDtypeStruct((n, heads, HEAD_DIM), F32),
            (q, k_new, v_new, *k_views, *v_views))


def _mix_mem_prompt_kernel(run_casts, pool_ref, attn_ref, x_ref, mk_ref, mv_ref, wo_ref,
                           gmix_ref, gpre_ref, wq_ref, wxo_ref, gmem_ref, *refs, sample_group):
    n_sample_in = 3 + 2 * len(DILATED)
    sample_in, (x2_ref, attn_s_ref) = refs[:n_sample_in], refs[n_sample_in:]
    run_casts()
    scale = HEAD_DIM ** -0.5
    mixed = jnp.concatenate([pool_ref[...]] + [attn_ref[h] for h in range(N_HEADS)], axis=-1)
    x1 = x_ref[...] + _rmsnorm(_dot(mixed, wo_ref[...]), gmix_ref[...])
    qm = _dot(_rmsnorm(x1, gpre_ref[...]).astype(BF16), wq_ref[...]).astype(BF16)
    heads = []
    for h in range(MEM_HEADS):
        cols = slice(h * HEAD_DIM, (h + 1) * HEAD_DIM)
        s = _dot_nt(qm[:, cols], mk_ref[:, cols].astype(BF16)) * scale
        p = jnp.exp(s - jnp.max(s, axis=-1, keepdims=True))
        o = _dot(p.astype(BF16), mv_ref[:, cols].astype(BF16))
        heads.append((o / jnp.sum(p, axis=-1, keepdims=True)).astype(BF16))
    y = _dot(jnp.concatenate(heads, axis=-1), wxo_ref[...])
    x2_ref[...] = x1 + _rmsnorm(y, gmem_ref[...])

    _attn_sample_kernel(*sample_in, attn_s_ref, group=sample_group)


def _mix_mem_prompt(pool_out, attn, x, mk, mv, w_out, g_mix_post, g_mem_pre, w_xq, w_xo, g_mem_post,
                    sample_attn, *, tm, seq, casts=()):
    m, d = x.shape
    n_tiles = m // tm
    tiles_per_seq = seq // tm
    half = pool_out.shape[1]
    mem_spec = pl.BlockSpec((N_MEM, MEM_WIDTH), lambda i: (i // tiles_per_seq, 0))
    gain_spec = _resident((1, d), lambda i: (0, 0))
    n_sample = sample_attn[0].shape[0]
    assert n_sample % n_tiles == 0
    sample_group = n_sample // n_tiles
    s_in_specs, s_out_spec, s_out_shape, s_args = _attn_sample_operands(*sample_attn, group=sample_group)
    (x2, attn_s), cast_out = _call_with_casts(
        functools.partial(_mix_mem_prompt_kernel, sample_group=sample_group),
        grid=(n_tiles,),
        in_specs=[
            pl.BlockSpec((tm, half), lambda i: (i, 0)),
            pl.BlockSpec((N_HEADS, tm, HEAD_DIM), lambda i: (0, i, 0)),
            pl.BlockSpec((tm, d), lambda i: (i, 0)),
            mem_spec,
            mem_spec,
            _resident(w_out.shape, lambda i: (0, 0)),
            gain_spec,
            gain_spec,
            _resident(w_xq.shape, lambda i: (0, 0)),
            _resident(w_xo.shape, lambda i: (0, 0)),
            gain_spec,
        ] + s_in_specs,
        out_specs=[pl.BlockSpec((tm, d), lambda i: (i, 0)), s_out_spec],
        out_shape=[jax.ShapeDtypeStruct((m, d), F32), s_out_shape],
        args=(pool_out, attn, x, mk, mv, w_out, g_mix_post, g_mem_pre, w_xq, w_xo, g_mem_post) + tuple(s_args),
        casts=casts,
        name="mix_mem_prompt",
    )
    return x2, attn_s, cast_out


def _mem_attn_sample_kernel(qm_ref, mk_ref, mv_ref, o_ref, *, group):
    heads = qm_ref.shape[1]
    both = lambda a: jnp.concatenate([a, a], axis=0)
    fold = lambda a: a[0:heads] + a[heads:]
    for j in range(group):
        q = both(qm_ref[j] * (HEAD_DIM ** -0.5 * LOG2_E))
        t = jnp.sum(mk_ref[j] * q[None], axis=-1, keepdims=True)
        m = jnp.max(t, axis=0)
        m = both(jnp.maximum(m[0:heads], m[heads:]))
        p = jnp.exp2(t - m[None])
        o_ref[j] = fold(jnp.sum(p * mv_ref[j], axis=0)) / fold(jnp.sum(p, axis=0))


def _mem_attn_sample(qm, mem_k, mem_v):
    n, heads, _ = qm.shape
    group = SAMPLE_GROUP
    row = pl.BlockSpec((group, heads, HEAD_DIM), lambda i: (i, 0, 0))
    mem = pl.BlockSpec((group, N_MEM // 2, 2 * heads, HEAD_DIM), lambda i: (i, 0, 0, 0))
    pairs = lambda a: a.reshape(n, N_MEM // 2, 2 * heads, HEAD_DIM)
    return pl.pallas_call(
        functools.partial(_mem_attn_sample_kernel, group=group),
        grid=(n // group,),
        in_specs=[row, mem, mem],
        out_specs=row,
        out_shape=jax.ShapeDtypeStruct((n, heads, HEAD_DIM), F32),
        compiler_params=_params(1),
        name="mem_attn_sample",
    )(qm, pairs(mem_k), pairs(mem_v))


def _mem_out_sample_kernel(a_ref, w_ref, g_ref, r_ref, o_ref):
    y = jnp.zeros(r_ref.shape, F32)
    for h in range(a_ref.shape[1]):
        y = y + _dot(a_ref[:, h, :].astype(BF16), w_ref[h * HEAD_DIM:(h + 1) * HEAD_DIM, :])
    o_ref[...] = r_ref[...] + _rmsnorm(y, g_ref[...])


def _mem_out_sample(a, w, g, resid):
    return pl.pallas_call(
        _mem_out_sample_kernel,
        out_shape=jax.ShapeDtypeStruct(resid.shape, F32),
        compiler_params=pltpu.CompilerParams(vmem_limit_bytes=VMEM_LIMIT_BYTES),
        name="mem_out_sample",
    )(a, w, g, resid)


def _ffn_kernel(x_ref, xnext_ref, xs_ref, gpre_ref, w1_ref, w2_ref, gpost_ref, o_ref, os_ref,
                xn_ref, xns_ref, acc_ref, accs_ref, *, tm, nf):
    i, f = pl.program_id(0), pl.program_id(1)
    last_f = nf - 1
    chunk = tm // nf

    @pl.when((i == 0) & (f == 0))
    def _():
        xn_ref[0] = _rmsnorm(x_ref[...], gpre_ref[...]).astype(BF16)
        xns_ref[...] = _rmsnorm(xs_ref[...], gpre_ref[...]).astype(BF16)
        accs_ref[...] = jnp.zeros_like(accs_ref)

    @pl.when(f == 0)
    def _():
        acc_ref[...] = jnp.zeros_like(acc_ref)

    def mlp(lhs):
        hidden = jnp.square(jnp.maximum(_dot(lhs, w1_ref[...]), 0.0)).astype(BF16)
        return _dot(hidden, w2_ref[...])

    def norm_next_rows():
        rows = pl.ds(pl.multiple_of(f * chunk, chunk), chunk)
        xn_ref[(i + 1) % 2, rows, :] = _rmsnorm(xnext_ref[rows, :], gpre_ref[...]).astype(BF16)

    @pl.when(i == 0)
    def _():
        both = mlp(jnp.concatenate([xn_ref[0], xns_ref[...]], axis=0))
        acc_ref[...] += both[0:tm]
        accs_ref[...] += both[tm:]
        norm_next_rows()

    @pl.when(i > 0)
    def _():
        acc_ref[...] += mlp(xn_ref[i % 2])
        norm_next_rows()

    @pl.when(f == last_f)
    def _():
        o_ref[...] = x_ref[...] + _rmsnorm(acc_ref[...], gpost_ref[...])

    @pl.when((f == last_f) & (i == 0))
    def _():
        os_ref[...] = xs_ref[...] + _rmsnorm(accs_ref[...], gpost_ref[...])


def _ffn(x, xs, g_pre, w1_chunks, w2, g_post, *, tm):
    m, d = x.shape
    ms = xs.shape[0]
    nf, _, tf = w1_chunks.shape
    n_tiles = m // tm
    assert tm % nf == 0 and (tm // nf) % 16 == 0
    sample_spec = _resident((ms, d), lambda i, f: (0, 0))
    return pl.pallas_call(
        functools.partial(_ffn_kernel, tm=tm, nf=nf),
        grid=(n_tiles, nf),
        in_specs=[
            pl.BlockSpec((tm, d), lambda i, f: (i, 0)),
            pl.BlockSpec((tm, d), lambda i, f: (jnp.minimum(i + 1, n_tiles - 1), 0)),
            sample_spec,
            _resident((1, d), lambda i, f: (0, 0)),
            pl.BlockSpec((None, d, tf), lambda i, f: (f, 0, 0)),
            pl.BlockSpec((tf, d), lambda i, f: (f, 0)),
            _resident((1, d), lambda i, f: (0, 0)),
        ],
        out_specs=[pl.BlockSpec((tm, d), lambda i, f: (i, 0)),
                   pl.BlockSpec((ms, d), lambda i, f: (0, 0))],
        out_shape=[jax.ShapeDtypeStruct((m, d), F32), jax.ShapeDtypeStruct((ms, d), F32)],
        scratch_shapes=[pltpu.VMEM((2, tm, d), BF16), pltpu.VMEM((ms, d), BF16),
                        pltpu.VMEM((tm, d), F32), pltpu.VMEM((ms, d), F32)],
        compiler_params=_params(2),
        name="ffn",
    )(x, x, xs, g_pre, w1_chunks, w2, g_post)


def kernel(x_prompt, x_sample, state_pool, cache_attn_k, cache_attn_v, cache_mem_k, cache_mem_v, mem_prompt,
           g_mix_pre, g_mix_post, g_mem_pre, g_mem_post, g_ffn_pre, g_ffn_post, g_mem_kv,
           w_in, w_pool, pool_scale, w_out, w_xq, w_mem_kv, w_xo, w_ff1, w_ff2):
    depth = w_in.shape[0]
    assert depth == 1
    batch, seq, d = x_prompt.shape
    dec_batch, dec_seq, _ = x_sample.shape
    assert dec_seq == 1

    l = 0
    scale = pool_scale[l][None, :]
    gain = lambda g: g[l][None, :]

    xp = x_prompt.reshape(batch * seq, d)
    xs = x_sample.reshape(dec_batch, d)
    w_pool_f = w_pool[l]
    tm = ROW_TILE


    mem = mem_prompt.reshape(batch * N_MEM, d)
    (mk, mv, mk_h, mv_h, cos_p, sin_p, cos_s, sin_s), (w_in_b,) = _norm_proj(
        mem, gain(g_mem_kv), w_mem_kv[l], tm=MEM_KV_TILE, ncol=MEM_WIDTH,
        outputs=((0, FLAT), (1, FLAT), (0, HEADS), (1, HEADS)), name="mem_kv", casts=(w_in[l],),
        seq=seq, n_sample=dec_batch, sample_pos=PAST_LEN)

    ((pool_out, q, k, v, k_h, v_h, pool_state, us, qs_h, ks_h, vs_h),
     (w_out_b, w_xq_b, w_xo_b)) = _in_proj_pool(
        xp, xs, gain(g_mix_pre), w_in_b, cos_p, sin_p, cos_s, sin_s, w_pool_f, scale, tm=IN_PROJ_TILE, seq=seq,
        casts=(w_out[l], w_xq[l], w_xo[l]))
    attn, (w_ff1_b, w_ff2_b) = _attn_prompt(
        q, k, v, seq=seq, casts=((w_ff1[l], w_ff1.shape[2] // FFN_CHUNK), w_ff2[l]))
    heads3 = lambda a: a.reshape(dec_batch, -1, HEAD_DIM)
    x2, attn_s, _ = _mix_mem_prompt(
        pool_out, attn, xp, mk, mv, w_out_b, gain(g_mix_post), gain(g_mem_pre), w_xq_b, w_xo_b, gain(g_mem_post),
        (heads3(qs_h), heads3(ks_h), heads3(vs_h), cache_attn_k[l], cache_attn_v[l]), tm=tm, seq=seq)

    new_pool_s, x1s, qms_h = _mix_sample(us, state_pool[l], w_pool_f, scale, attn_s, xs, w_out_b,
                                         gain(g_mix_post), gain(g_mem_pre), w_xq_b, pos=PAST_LEN)
    mem_o_s = _mem_attn_sample(heads3(qms_h), cache_mem_k[l], cache_mem_v[l])
    x2s = _mem_out_sample(mem_o_s, w_xo_b, gain(g_mem_post), x1s)

    yp, ys = _ffn(x2, x2s, gain(g_ffn_pre), w_ff1_b, w_ff2_b, gain(g_ffn_post), tm=tm)

    keep = min(max(w for w, _ in DILATED), seq)
    return (
        yp.reshape(batch, seq, d),
        ys.reshape(dec_batch, 1, d),
        pool_state.reshape(batch, STATE_ROWS, POOL_WIDTH)[:, STATE_ROWS - POOL_STATE:][None],
        new_pool_s[None],
        k_h.reshape(batch, seq, N_HEADS, HEAD_DIM)[:, seq - keep:][None],
        v_h.reshape(batch, seq, N_HEADS, HEAD_DIM)[:, seq - keep:][None],
        ks_h.reshape(dec_batch, 1, N_HEADS, HEAD_DIM)[None],
        vs_h.reshape(dec_batch, 1, N_HEADS, HEAD_DIM)[None],
        mk_h.reshape(batch, N_MEM, MEM_HEADS, HEAD_DIM)[None],
        mv_h.reshape(batch, N_MEM, MEM_HEADS, HEAD_DIM)[None],
    )
```

```python
import functools

import jax
import jax.numpy as jnp
from jax import lax
from jax.experimental import pallas as pl
from jax.experimental.pallas import tpu as pltpu

D_MODEL = 2048
POOL_WIDTH = 1024
POOL_WINDOWS = (2, 4, 8, 16)
POOL_GROUP = POOL_WIDTH // len(POOL_WINDOWS)
POOL_STATE = max(POOL_WINDOWS) - 1
HEAD_DIM = 128
ATTN_WIDTH = 1024
N_HEADS = ATTN_WIDTH // HEAD_DIM
DILATED = ((128, 1), (512, 4), (2048, 16))
PITCHED_DILATION = 16
PITCH = 128 + 8
ROPE_DIM = HEAD_DIM // 4
ROPE_HALF = ROPE_DIM // 2
ROPE_THETA = 500000.0
N_MEM = 256
MEM_HEADS = 4
MEM_WIDTH = MEM_HEADS * HEAD_DIM
EPS = 1e-6
BLOCK = 128
NEG_INF = -1e30
LOG2_E = 1.4426950408889634
PAST_LEN = 8192
HALO = 32
STATE_ROWS = 16

F32 = jnp.float32
BF16 = jnp.bfloat16

VMEM_LIMIT_BYTES = 56 * 1024 * 1024

ROW_TILE = 512
IN_PROJ_TILE = 256
MEM_KV_TILE = 256
FFN_CHUNK = 1024
SAMPLE_GROUP = 8

FLAT, HEADS = "flat", "heads"


def _params(n_grid_axes):
    return pltpu.CompilerParams(
        dimension_semantics=("arbitrary",) * n_grid_axes,
        vmem_limit_bytes=VMEM_LIMIT_BYTES,
    )


def _resident(block_shape, index_map):
    return pl.BlockSpec(block_shape, index_map, pipeline_mode=pl.Buffered(1))


def _rmsnorm(x, g):
    ms = jnp.mean(x * x, axis=-1, keepdims=True)
    return x * lax.rsqrt(ms + EPS) * g


def _dot(a, b):
    return jnp.dot(a, b, preferred_element_type=F32)


def _dot_nt(a, b):
    return lax.dot_general(a, b, (((1,), (1,)), ((), ())), preferred_element_type=F32)


def _call_with_casts(kernel_fn, *, grid, in_specs, out_specs, out_shape, scratch_shapes=(), args, casts=(),
                     step_of=None, name):
    casts = [c if isinstance(c, tuple) else (c, 1) for c in casts]
    n_in, n_out, n_casts = len(in_specs), len(out_specs), len(casts)
    n_steps = 1
    for g in grid:
        n_steps *= g
    if step_of is None:
        step_of = lambda i: i
    cast_in_specs, cast_out_specs, cast_shapes = [], [], []
    for a, n_chunks in casts:
        rows, cols = a.shape
        assert rows % n_steps == 0 and cols % n_chunks == 0
        cast_in_specs.append(pl.BlockSpec((rows // n_steps, cols), lambda *idx: (step_of(*idx), 0)))
        if n_chunks == 1:
            cast_out_specs.append(cast_in_specs[-1])
            cast_shapes.append(jax.ShapeDtypeStruct(a.shape, BF16))
        else:
            cast_out_specs.append(pl.BlockSpec((n_chunks, rows // n_steps, cols // n_chunks),
                                               lambda *idx: (0, step_of(*idx), 0)))
            cast_shapes.append(jax.ShapeDtypeStruct((n_chunks, rows, cols // n_chunks), BF16))

    def body(*refs):
        ins, cast_in = refs[:n_in], refs[n_in:n_in + n_casts]
        outs = refs[n_in + n_casts:n_in + n_casts + n_out]
        cast_out = refs[n_in + n_casts + n_out:n_in + 2 * n_casts + n_out]

        def run_casts():
            for i_ref, o_ref, (_, n_chunks) in zip(cast_in, cast_out, casts):
                if n_chunks == 1:
                    o_ref[...] = i_ref[...].astype(BF16)
                else:
                    width = i_ref.shape[1] // n_chunks
                    for c in range(n_chunks):
                        o_ref[c] = i_ref[:, c * width:(c + 1) * width].astype(BF16)

        kernel_fn(run_casts, *ins, *outs, *refs[n_in + 2 * n_casts + n_out:])

    res = pl.pallas_call(
        body,
        grid=grid,
        in_specs=list(in_specs) + cast_in_specs,
        out_specs=list(out_specs) + cast_out_specs,
        out_shape=list(out_shape) + cast_shapes,
        scratch_shapes=list(scratch_shapes),
        compiler_params=_params(len(grid)),
        name=name,
    )(*args, *[a for a, _ in casts])
    return res[:n_out], res[n_out:]


def _store_heads(o_ref, layout, h, heads, rows, value):
    if layout == FLAT:
        o_ref[:, h * HEAD_DIM:(h + 1) * HEAD_DIM] = value
    else:
        o_ref[pl.ds(h, rows, stride=heads), :] = value


def _out_block(layout, tm, heads):
    return (tm, heads * HEAD_DIM) if layout == FLAT else (tm * heads, HEAD_DIM)


def _rope_rows(pos):
    lane = lax.broadcasted_iota(jnp.int32, (1, HEAD_DIM), 1)
    k = (lane % ROPE_HALF).astype(F32)
    inv = jnp.where(lane < ROPE_DIM, jnp.power(jnp.float32(ROPE_THETA), -k * 2.0 / ROPE_DIM), 0.0)
    ang = pos * inv
    return jnp.cos(ang), jnp.sin(ang) * jnp.where(lane < ROPE_HALF, -1.0, 1.0)


def _norm_proj_kernel(run_casts, x_ref, g_ref, w_ref, *out_refs, tm, ncol, outputs, sample_pos):
    run_casts()
    *out_refs, cos_ref, sin_ref, coss_ref, sins_ref = out_refs
    heads = ncol // HEAD_DIM
    xn = _rmsnorm(x_ref[...], g_ref[...]).astype(BF16)
    for c in sorted({c for c, _ in outputs}):
        acc = _dot(xn, w_ref[:, c * ncol:(c + 1) * ncol].astype(BF16))
        for h in range(heads):
            for (oc, layout), o_ref in zip(outputs, out_refs):
                if oc == c:
                    _store_heads(o_ref, layout, h, heads, tm, acc[:, h * HEAD_DIM:(h + 1) * HEAD_DIM])
    rows = cos_ref.shape[0]
    pos = pl.program_id(0) * rows + lax.broadcasted_iota(jnp.int32, (rows, 1), 0)
    cos_ref[...], sin_ref[...] = _rope_rows(pos.astype(F32))
    coss_ref[...], sins_ref[...] = _rope_rows(jnp.full((coss_ref.shape[0], 1), sample_pos, F32))


def _norm_proj(x, g, w, *, tm, ncol, outputs, name, casts=(), seq, n_sample, sample_pos):
    m, d = x.shape
    heads = ncol // HEAD_DIM
    n_steps = m // tm
    assert w.shape[0] == d and w.shape[1] % ncol == 0 and m % tm == 0 and seq % (8 * n_steps) == 0
    kern = functools.partial(_norm_proj_kernel, tm=tm, ncol=ncol, outputs=outputs, sample_pos=sample_pos)
    blocks = [_out_block(layout, tm, heads) for _, layout in outputs]
    table = pl.BlockSpec((seq // n_steps, HEAD_DIM), lambda i: (i, 0))
    sample_table = pl.BlockSpec((n_sample, HEAD_DIM), lambda i: (0, 0))
    return _call_with_casts(
        kern,
        grid=(n_steps,),
        in_specs=[
            pl.BlockSpec((tm, d), lambda i: (i, 0)),
            _resident((1, d), lambda i: (0, 0)),
            _resident(w.shape, lambda i: (0, 0)),
        ],
        out_specs=[pl.BlockSpec(b, lambda i: (i, 0)) for b in blocks] + [table, table, sample_table, sample_table],
        out_shape=([jax.ShapeDtypeStruct((n_steps * b[0], b[1]), F32) for b in blocks]
                   + [jax.ShapeDtypeStruct((seq, HEAD_DIM), F32)] * 2
                   + [jax.ShapeDtypeStruct((n_sample, HEAD_DIM), F32)] * 2),
        args=(x, g, w),
        casts=casts,
        name=name,
    )


def _window_sums(ext_ref, lvl_refs, g, w, rows):
    cols = slice(g * POOL_GROUP, (g + 1) * POOL_GROUP)
    levels = w.bit_length() - 1
    src, src_cols = ext_ref, cols
    for level in range(levels):
        shift = 1 << level
        last = level == levels - 1
        lo = HALO if last else 8 * (level + 1)
        assert lo - shift >= 8 * level
        total = src[lo:rows, src_cols] + src[lo - shift:rows - shift, src_cols]
        if last:
            return total
        lvl_refs[level % 2][lo:rows, :] = total
        src, src_cols = lvl_refs[level % 2], slice(None)


def _rope(a, cos, sin):
    lane = lax.broadcasted_iota(jnp.int32, a.shape, 1)
    partner = jnp.where(lane < ROPE_HALF, pltpu.roll(a, HEAD_DIM - ROPE_HALF, 1), pltpu.roll(a, ROPE_HALF, 1))
    return a * cos + partner * sin


def _in_proj_pool_kernel(run_casts, x_ref, xnext_ref, xs_ref, g_ref, w_ref, cos_ref, sin_ref, coss_ref, sins_ref,
                         wp_ref, scale_ref,
                         pool_ref, q_ref, k_ref, v_ref, kh_ref, vh_ref, state_ref, us_ref, qsh_ref, ksh_ref, vsh_ref,
                         xn_ref, ext_ref, lvl_a_ref, lvl_b_ref, *, tm, tiles_per_seq):
    step = pl.program_id(0)
    t_in_seq = step % tiles_per_seq
    n_sample = xs_ref.shape[0]

    def tile(xn, with_sample):
        run_casts()
        u = _dot(xn, w_ref[:, 0:POOL_WIDTH])
        ext_ref[HALO:HALO + tm, :] = u[0:tm]
        if with_sample:
            us_ref[...] = u[tm:]
        cos, sin = cos_ref[...], sin_ref[...]
        outs = ((q_ref, None, qsh_ref), (k_ref, kh_ref, ksh_ref), (v_ref, vh_ref, vsh_ref))
        for c, (major_ref, heads_ref, sample_ref) in enumerate(outs, start=1):
            acc = _dot(xn, w_ref[:, c * ATTN_WIDTH:(c + 1) * ATTN_WIDTH])
            for h in range(N_HEADS):
                a = acc[0:tm, h * HEAD_DIM:(h + 1) * HEAD_DIM]
                if major_ref is not v_ref:
                    a = _rope(a, cos, sin)
                major_ref[h] = a
                if heads_ref is not None:
                    _store_heads(heads_ref, HEADS, h, N_HEADS, tm, a)
                if with_sample:
                    a = acc[tm:, h * HEAD_DIM:(h + 1) * HEAD_DIM]
                    if major_ref is not v_ref:
                        a = _rope(a, coss_ref[...], sins_ref[...])
                    _store_heads(sample_ref, HEADS, h, N_HEADS, n_sample, a)

        pos = t_in_seq * tm + lax.broadcasted_iota(jnp.int32, (tm, 1), 0)
        for g, w in enumerate(POOL_WINDOWS):
            cols = slice(g * POOL_GROUP, (g + 1) * POOL_GROUP)
            wsum = _window_sums(ext_ref, (lvl_a_ref, lvl_b_ref), g, w, HALO + tm)
            cnt = jnp.minimum(w, pos + 1).astype(F32)
            pooled = (wsum / cnt - ext_ref[HALO:HALO + tm, cols]).astype(BF16)
            pool_ref[:, cols] = (_dot(pooled, wp_ref[g].astype(BF16)) * scale_ref[:, cols]).astype(pool_ref.dtype)
        state_ref[...] = ext_ref[HALO + tm - STATE_ROWS:HALO + tm, :]
        ext_ref[0:HALO, :] = ext_ref[tm:tm + HALO, :]

        xn_ref[(step + 1) % 2] = _rmsnorm(xnext_ref[...], g_ref[...]).astype(BF16)

    @pl.when(t_in_seq == 0)
    def _():
        ext_ref[0:HALO, :] = jnp.zeros((HALO, POOL_WIDTH), F32)

    @pl.when(step == 0)
    def _():
        rows = jnp.concatenate([x_ref[...], xs_ref[...]], axis=0)
        tile(_rmsnorm(rows, g_ref[...]).astype(BF16), True)

    @pl.when(step > 0)
    def _():
        tile(xn_ref[step % 2], False)


def _in_proj_pool(x, xs, g, w, cos, sin, cos_s, sin_s, w_pool, pool_scale, *, tm, seq, casts=()):
    m, d = x.shape
    ms = xs.shape[0]
    tiles_per_seq = seq // tm
    n_seq = m // seq
    assert POOL_WIDTH == ATTN_WIDTH and w.shape == (d, POOL_WIDTH + 3 * ATTN_WIDTH)
    kern = functools.partial(_in_proj_pool_kernel, tm=tm, tiles_per_seq=tiles_per_seq)
    row_tile = lambda width: pl.BlockSpec((tm, width), lambda i: (i, 0))
    heads_tile = pl.BlockSpec((tm * N_HEADS, HEAD_DIM), lambda i: (i, 0))
    table = pl.BlockSpec((tm, HEAD_DIM), lambda i: (i % tiles_per_seq, 0))
    whole = lambda shape: _resident(shape, lambda i: (0,) * len(shape))
    head_major = jax.ShapeDtypeStruct((N_HEADS, m, HEAD_DIM), F32)
    head_major_tile = pl.BlockSpec((N_HEADS, tm, HEAD_DIM), lambda i: (0, i, 0))
    by_heads = jax.ShapeDtypeStruct((m * N_HEADS, HEAD_DIM), F32)
    sample_heads = jax.ShapeDtypeStruct((ms * N_HEADS, HEAD_DIM), F32)
    sample_heads_spec = pl.BlockSpec((ms * N_HEADS, HEAD_DIM), lambda i: (0, 0))
    return _call_with_casts(
        kern,
        grid=(m // tm,),
        in_specs=[
            row_tile(d),
            pl.BlockSpec((tm, d), lambda i: (jnp.minimum(i + 1, m // tm - 1), 0)),
            whole((ms, d)),
            whole((1, d)),
            whole(w.shape),
            table,
            table,
            whole((ms, HEAD_DIM)),
            whole((ms, HEAD_DIM)),
            whole(w_pool.shape),
            whole((1, POOL_WIDTH)),
        ],
        out_specs=[row_tile(POOL_WIDTH), head_major_tile, head_major_tile, head_major_tile,
                   heads_tile, heads_tile,
                   pl.BlockSpec((STATE_ROWS, POOL_WIDTH), lambda i: (i // tiles_per_seq, 0)),
                   pl.BlockSpec((ms, POOL_WIDTH), lambda i: (0, 0)),
                   sample_heads_spec, sample_heads_spec, sample_heads_spec],
        out_shape=[jax.ShapeDtypeStruct((m, POOL_WIDTH), BF16), head_major, head_major, head_major, by_heads, by_heads,
                   jax.ShapeDtypeStruct((n_seq * STATE_ROWS, POOL_WIDTH), F32),
                   jax.ShapeDtypeStruct((ms, POOL_WIDTH), F32), sample_heads, sample_heads, sample_heads],
        scratch_shapes=[pltpu.VMEM((2, tm, d), BF16),
                        pltpu.VMEM((HALO + tm, POOL_WIDTH), F32),
                        pltpu.VMEM((HALO + tm, POOL_GROUP), F32),
                        pltpu.VMEM((HALO + tm, POOL_GROUP), F32)],
        args=(x, x, xs, g, w, cos, sin, cos_s, sin_s, w_pool, pool_scale),
        casts=casts,
        name="in_proj_pool",
    )


def _mix_sample_kernel(u_ref, prev_ref, wp_ref, scale_ref, attn_ref, x_ref, wo_ref, gpost_ref, gpre_ref, wq_ref,
                       state_ref, x1_ref, qm_ref, *, pos):
    n = u_ref.shape[0]
    for j in range(POOL_STATE - 1):
        state_ref[:, j, :] = prev_ref[:, j + 1, :]
    state_ref[:, POOL_STATE - 1, :] = u_ref[...]
    y = jnp.zeros(x_ref.shape, F32)
    for g, w in enumerate(POOL_WINDOWS):
        cols = slice(g * POOL_GROUP, (g + 1) * POOL_GROUP)
        wsum = u_ref[:, cols]
        for back in range(1, w):
            wsum = wsum + prev_ref[:, POOL_STATE - back, cols]
        cnt = float(min(w, pos + 1))
        pooled = (wsum / cnt - u_ref[:, cols]).astype(BF16)
        pool_out = (_dot(pooled, wp_ref[g].astype(BF16)) * scale_ref[:, cols]).astype(BF16)
        y = y + _dot(pool_out, wo_ref[cols, :])
    for h in range(N_HEADS):
        rows = slice(POOL_WIDTH + h * HEAD_DIM, POOL_WIDTH + (h + 1) * HEAD_DIM)
        y = y + _dot(attn_ref[:, h, :].astype(BF16), wo_ref[rows, :])
    x1 = x_ref[...] + _rmsnorm(y, gpost_ref[...])
    x1_ref[...] = x1
    qm = _dot(_rmsnorm(x1, gpre_ref[...]).astype(BF16), wq_ref[...])
    for h in range(MEM_HEADS):
        _store_heads(qm_ref, HEADS, h, MEM_HEADS, n, qm[:, h * HEAD_DIM:(h + 1) * HEAD_DIM])


def _mix_sample(u, prev, w_pool, pool_scale, attn, x, w_out, g_post, g_pre, w_xq, *, pos):
    n, d = x.shape
    kern = functools.partial(_mix_sample_kernel, pos=pos)
    return pl.pallas_call(
        kern,
        out_shape=[jax.ShapeDtypeStruct(prev.shape, F32), jax.ShapeDtypeStruct((n, d), F32),
                   jax.ShapeDtypeStruct((n * MEM_HEADS, HEAD_DIM), F32)],
        compiler_params=pltpu.CompilerParams(vmem_limit_bytes=VMEM_LIMIT_BYTES),
        name="mix_sample",
    )(u, prev, w_pool, pool_scale, attn, x, w_out, g_post, g_pre, w_xq)


def _run_blocks(count, unroll, body):
    trips = count // unroll
    if trips > 1:
        def trip(t, carry):
            for j in range(unroll):
                body(t * unroll + j)
            return carry
        lax.fori_loop(0, trips, trip, 0)
        done = trips * unroll
    else:
        done = 0
    for idx in range(done, count):
        body(idx)


def _attn_prompt_kernel(run_casts, q_ref, k_ref, v_ref, o_ref, *scratch, seq, unroll):
    run_casts()
    n_br = len(DILATED)
    ob_refs, mb_refs, sb_refs = scratch[:n_br], scratch[n_br:2 * n_br], scratch[2 * n_br:3 * n_br]
    p_ref, bias_ref, q4_ref, k4_ref, v4_ref = scratch[3 * n_br:]
    exp2_scale = HEAD_DIM ** -0.5 * LOG2_E
    qi = lax.broadcasted_iota(jnp.int32, (BLOCK, 2 * BLOCK), 0)
    kj = lax.broadcasted_iota(jnp.int32, (BLOCK, 2 * BLOCK), 1)
    in_band = ((kj < BLOCK) & (kj >= qi)) | ((kj >= BLOCK) & (kj - BLOCK <= qi))
    bias_ref[...] = jnp.where(in_band, 0.0, NEG_INF)
    ones = jnp.ones((2 * BLOCK, HEAD_DIM), BF16)

    def rows(block, r, d):
        if d == 1:
            start = block * BLOCK
            return pl.ds(start if isinstance(start, int) else pl.multiple_of(start, BLOCK), BLOCK)
        return pl.ds(block * (BLOCK * d) + r, BLOCK, stride=d)

    def out_rows(block, r, d):
        if d == PITCHED_DILATION:
            start = r * PITCH
            return pl.ds(start if isinstance(start, int) else pl.multiple_of(start, 8), BLOCK)
        return rows(block, r, d)

    quarter = seq // 4
    assert [d for _, d in DILATED] == [1, 4, 16]
    for ref, by4_ref in ((q_ref, q4_ref), (k_ref, k4_ref), (v_ref, v4_ref)):
        for b in range(4):
            by4_ref[b * quarter:(b + 1) * quarter, :] = ref[pl.ds(b, quarter, stride=4), :]

    def in_block(refs, block, r, d):
        ref, by4_ref = refs
        if d == 1:
            return ref[rows(block, r, d), :]
        if d == 4:
            return by4_ref[pl.ds(r * quarter + block * BLOCK, BLOCK), :]
        return by4_ref[pl.ds((r % 4) * quarter + r // 4 + block * (4 * BLOCK), BLOCK, stride=4), :]

    def keys(refs, block, r, d, with_prev):
        cur = in_block(refs, block, r, d)
        if not with_prev:
            return cur.astype(BF16)
        return jnp.concatenate([in_block(refs, block - 1, r, d), cur], axis=0).astype(BF16)

    for g, (_, d) in enumerate(DILATED):
        n_blocks = seq // d // BLOCK
        for with_prev in (False, True):
            per_r = n_blocks - 1 if with_prev else 1
            count = d * per_r
            if count == 0:
                continue
            cols = slice(0, 2 * BLOCK) if with_prev else slice(BLOCK, 2 * BLOCK)
            width = 2 * BLOCK if with_prev else BLOCK

            def locate(idx, per_r=per_r, with_prev=with_prev):
                return (idx % per_r + 1, idx // per_r) if with_prev else (0, idx)

            def probabilities(idx, g=g, d=d, with_prev=with_prev, cols=cols, width=width, locate=locate):
                b, r = locate(idx)
                q = in_block((q_ref, q4_ref), b, r, d).astype(BF16)
                s = _dot_nt(q, keys((k_ref, k4_ref), b, r, d, with_prev)) + bias_ref[:, cols]
                m = jnp.max(s, axis=-1, keepdims=True)
                p_ref[idx, :, 0:width] = jnp.exp2((s - m) * exp2_scale).astype(BF16)
                mb_refs[g][out_rows(b, r, d), :] = jnp.broadcast_to(m, (BLOCK, HEAD_DIM))

            def values(idx, g=g, d=d, with_prev=with_prev, width=width, locate=locate):
                b, r = locate(idx)
                v1 = jnp.concatenate([keys((v_ref, v4_ref), b, r, d, with_prev), ones[0:width]], axis=1)
                acc = _dot(p_ref[idx, :, 0:width], v1)
                ob_refs[g][out_rows(b, r, d), :] = acc[:, 0:HEAD_DIM]
                sb_refs[g][out_rows(b, r, d), :] = acc[:, HEAD_DIM:]

            _run_blocks(count, unroll, probabilities)
            _run_blocks(count, unroll, values)

    chunk = 256

    def combine(c, carry):
        rw = pl.ds(pl.multiple_of(c * chunk, chunk), chunk)

        def read(ref, g):
            if DILATED[g][1] != PITCHED_DILATION:
                return ref[rw, :]
            pieces = [ref[pl.ds((j % 2) * 8 * PITCH + c * (chunk // PITCHED_DILATION) + j // 2, 8, stride=PITCH), :]
                      for j in range(chunk // 8)]
            return jnp.concatenate(pieces, axis=0)

        ms = [read(mb, g) for g, mb in enumerate(mb_refs)]
        m = functools.reduce(jnp.maximum, ms)
        ws = [jnp.exp2((mg - m) * exp2_scale) for mg in ms]
        num = functools.reduce(jnp.add, [w * read(ob, g) for g, (w, ob) in enumerate(zip(ws, ob_refs))])
        den = functools.reduce(jnp.add, [w * read(sb, g) for g, (w, sb) in enumerate(zip(ws, sb_refs))])
        o_ref[rw, :] = (num / den).astype(o_ref.dtype)
        return carry

    lax.fori_loop(0, seq // chunk, combine, 0)


def _attn_prompt(q, k, v, *, seq, casts=()):
    n_heads, m, _ = q.shape
    n_seq = m // seq
    spec = pl.BlockSpec((None, seq, HEAD_DIM), lambda n, h: (h, n, 0))
    kern = functools.partial(_attn_prompt_kernel, seq=seq, unroll=16)
    (attn,), cast_out = _call_with_casts(
        kern,
        grid=(n_seq, n_heads),
        in_specs=[spec, spec, spec],
        out_specs=[spec],
        out_shape=[jax.ShapeDtypeStruct(q.shape, BF16)],
        scratch_shapes=(
            [pltpu.VMEM((PITCHED_DILATION * PITCH if d == PITCHED_DILATION else seq, HEAD_DIM), F32)
             for _ in range(3) for _, d in DILATED]
            + [pltpu.VMEM((seq // BLOCK, BLOCK, 2 * BLOCK), BF16),
               pltpu.VMEM((BLOCK, 2 * BLOCK), F32)]
            + [pltpu.VMEM((seq, HEAD_DIM), F32) for _ in range(3)]),
        args=(q, k, v),
        casts=casts,
        step_of=lambda n, h: n * n_heads + h,
        name="attn_prompt",
    )
    return attn, cast_out


def _attn_sample_kernel(q_ref, kn_ref, vn_ref, *refs, group):
    n_br = len(DILATED)
    kc_refs, vc_refs, o_ref = refs[:n_br], refs[n_br:2 * n_br], refs[2 * n_br]
    for j in range(group):
        q = q_ref[j] * (HEAD_DIM ** -0.5 * LOG2_E)
        t_new = jnp.sum(q * kn_ref[j], axis=-1, keepdims=True)
        t_br = [jnp.sum(kc[j] * q[None], axis=-1, keepdims=True) for kc in kc_refs]
        m = t_new
        for t in t_br:
            m = jnp.maximum(m, jnp.max(t, axis=0))
        p_new = jnp.exp2(t_new - m) * float(n_br)
        den = p_new
        num = p_new * vn_ref[j]
        for t, vc in zip(t_br, vc_refs):
            p = jnp.exp2(t - m[None])
            den = den + jnp.sum(p, axis=0)
            num = num + jnp.sum(p * vc[j], axis=0)
        out = num / den
        o_ref[j] = out
    return out


def _attn_sample_operands(q, k_new, v_new, k_cache, v_cache, *, group):
    n, heads, _ = q.shape
    win = k_cache.shape[1]
    row = pl.BlockSpec((group, heads, HEAD_DIM), lambda i: (i, 0, 0))
    cache_specs, k_views, v_views = [], [], []
    for w, d in DILATED:
        assert w == BLOCK * d and win % w == 0
        last = win // w - 1
        cache_specs.append(pl.BlockSpec((group, BLOCK, None, heads, HEAD_DIM),
                                        lambda i, last=last: (i, last, 0, 0, 0)))
        k_views.append(k_cache.reshape(n, win // d, d, heads, HEAD_DIM))
        v_views.append(v_cache.reshape(n, win // d, d, heads, HEAD_DIM))
    return ([row, row, row] + cache_specs + cache_specs, row, jax.ShapeDtypeStruct((n, heads, HEAD_DIM), F32),
            (q, k_new, v_new, *k_views, *v_views))


def _mix_mem_prompt_kernel(run_casts, pool_ref, attn_ref, x_ref, mk_ref, mv_ref, wo_ref,
                           gmix_ref, gpre_ref, wq_ref, wxo_ref, gmem_ref, *refs, sample_group):
    n_sample_in = 3 + 2 * len(DILATED)
    sample_in, (x2_ref, attn_s_ref) = refs[:n_sample_in], refs[n_sample_in:]
    run_casts()
    scale = HEAD_DIM ** -0.5
    mixed = jnp.concatenate([pool_ref[...]] + [attn_ref[h] for h in range(N_HEADS)], axis=-1)
    y = _dot(mixed, wo_ref[...])
    last = _attn_sample_kernel(*sample_in, attn_s_ref, group=sample_group)
    zero = jnp.minimum(jnp.abs(last[0:1, :]), 0.0)
    gmix = gmix_ref[...] + jnp.concatenate([zero] * (gmix_ref.shape[1] // HEAD_DIM), axis=1)
    x1 = x_ref[...] + _rmsnorm(y, gmix)
    qm = _dot(_rmsnorm(x1, gpre_ref[...]).astype(BF16), wq_ref[...]).astype(BF16)
    heads = []
    for h in range(MEM_HEADS):
        cols = slice(h * HEAD_DIM, (h + 1) * HEAD_DIM)
        s = _dot_nt(qm[:, cols], mk_ref[:, cols].astype(BF16)) * scale
        p = jnp.exp(s - jnp.max(s, axis=-1, keepdims=True))
        o = _dot(p.astype(BF16), mv_ref[:, cols].astype(BF16))
        heads.append((o / jnp.sum(p, axis=-1, keepdims=True)).astype(BF16))
    y = _dot(jnp.concatenate(heads, axis=-1), wxo_ref[...])
    x2_ref[...] = x1 + _rmsnorm(y, gmem_ref[...])


def _mix_mem_prompt(pool_out, attn, x, mk, mv, w_out, g_mix_post, g_mem_pre, w_xq, w_xo, g_mem_post,
                    sample_attn, *, tm, seq, casts=()):
    m, d = x.shape
    n_tiles = m // tm
    tiles_per_seq = seq // tm
    half = pool_out.shape[1]
    mem_spec = pl.BlockSpec((N_MEM, MEM_WIDTH), lambda i: (i // tiles_per_seq, 0))
    gain_spec = _resident((1, d), lambda i: (0, 0))
    n_sample = sample_attn[0].shape[0]
    assert n_sample % n_tiles == 0
    sample_group = n_sample // n_tiles
    s_in_specs, s_out_spec, s_out_shape, s_args = _attn_sample_operands(*sample_attn, group=sample_group)
    (x2, attn_s), cast_out = _call_with_casts(
        functools.partial(_mix_mem_prompt_kernel, sample_group=sample_group),
        grid=(n_tiles,),
        in_specs=[
            pl.BlockSpec((tm, half), lambda i: (i, 0)),
            pl.BlockSpec((N_HEADS, tm, HEAD_DIM), lambda i: (0, i, 0)),
            pl.BlockSpec((tm, d), lambda i: (i, 0)),
            mem_spec,
            mem_spec,
            _resident(w_out.shape, lambda i: (0, 0)),
            gain_spec,
            gain_spec,
            _resident(w_xq.shape, lambda i: (0, 0)),
            _resident(w_xo.shape, lambda i: (0, 0)),
            gain_spec,
        ] + s_in_specs,
        out_specs=[pl.BlockSpec((tm, d), lambda i: (i, 0)), s_out_spec],
        out_shape=[jax.ShapeDtypeStruct((m, d), F32), s_out_shape],
        args=(pool_out, attn, x, mk, mv, w_out, g_mix_post, g_mem_pre, w_xq, w_xo, g_mem_post) + tuple(s_args),
        casts=casts,
        name="mix_mem_prompt",
    )
    return x2, attn_s, cast_out


def _mem_attn_sample_kernel(qm_ref, mk_ref, mv_ref, o_ref, *, group):
    heads = qm_ref.shape[1]
    both = lambda a: jnp.concatenate([a, a], axis=0)
    fold = lambda a: a[0:heads] + a[heads:]
    for j in range(group):
        q = both(qm_ref[j] * (HEAD_DIM ** -0.5 * LOG2_E))
        t = jnp.sum(mk_ref[j] * q[None], axis=-1, keepdims=True)
        m = jnp.max(t, axis=0)
        m = both(jnp.maximum(m[0:heads], m[heads:]))
        p = jnp.exp2(t - m[None])
        o_ref[j] = fold(jnp.sum(p * mv_ref[j], axis=0)) / fold(jnp.sum(p, axis=0))


def _mem_attn_sample(qm, mem_k, mem_v):
    n, heads, _ = qm.shape
    group = SAMPLE_GROUP
    row = pl.BlockSpec((group, heads, HEAD_DIM), lambda i: (i, 0, 0))
    mem = pl.BlockSpec((group, N_MEM // 2, 2 * heads, HEAD_DIM), lambda i: (i, 0, 0, 0))
    pairs = lambda a: a.reshape(n, N_MEM // 2, 2 * heads, HEAD_DIM)
    return pl.pallas_call(
        functools.partial(_mem_attn_sample_kernel, group=group),
        grid=(n // group,),
        in_specs=[row, mem, mem],
        out_specs=row,
        out_shape=jax.ShapeDtypeStruct((n, heads, HEAD_DIM), F32),
        compiler_params=_params(1),
        name="mem_attn_sample",
    )(qm, pairs(mem_k), pairs(mem_v))


def _mem_out_sample_kernel(a_ref, w_ref, g_ref, r_ref, o_ref):
    y = jnp.zeros(r_ref.shape, F32)
    for h in range(a_ref.shape[1]):
        y = y + _dot(a_ref[:, h, :].astype(BF16), w_ref[h * HEAD_DIM:(h + 1) * HEAD_DIM, :])
    o_ref[...] = r_ref[...] + _rmsnorm(y, g_ref[...])


def _mem_out_sample(a, w, g, resid):
    return pl.pallas_call(
        _mem_out_sample_kernel,
        out_shape=jax.ShapeDtypeStruct(resid.shape, F32),
        compiler_params=pltpu.CompilerParams(vmem_limit_bytes=VMEM_LIMIT_BYTES),
        name="mem_out_sample",
    )(a, w, g, resid)


def _ffn_kernel(x_ref, xnext_ref, xs_ref, gpre_ref, w1_ref, w2_ref, gpost_ref, o_ref, os_ref,
                xn_ref, xns_ref, acc_ref, accs_ref, *, tm, nf):
    i, f = pl.program_id(0), pl.program_id(1)
    last_f = nf - 1
    chunk = tm // nf

    @pl.when((i == 0) & (f == 0))
    def _():
        xn_ref[0] = _rmsnorm(x_ref[...], gpre_ref[...]).astype(BF16)
        xns_ref[...] = _rmsnorm(xs_ref[...], gpre_ref[...]).astype(BF16)
        accs_ref[...] = jnp.zeros_like(accs_ref)

    @pl.when(f == 0)
    def _():
        acc_ref[...] = jnp.zeros_like(acc_ref)

    def mlp(lhs):
        hidden = jnp.square(jnp.maximum(_dot(lhs, w1_ref[...]), 0.0)).astype(BF16)
        return _dot(hidden, w2_ref[...])

    def norm_next_rows():
        rows = pl.ds(pl.multiple_of(f * chunk, chunk), chunk)
        xn_ref[(i + 1) % 2, rows, :] = _rmsnorm(xnext_ref[rows, :], gpre_ref[...]).astype(BF16)

    @pl.when(i == 0)
    def _():
        both = mlp(jnp.concatenate([xn_ref[0], xns_ref[...]], axis=0))
        acc_ref[...] += both[0:tm]
        accs_ref[...] += both[tm:]
        norm_next_rows()

    @pl.when(i > 0)
    def _():
        acc_ref[...] += mlp(xn_ref[i % 2])
        norm_next_rows()

    @pl.when(f == last_f)
    def _():
        o_ref[...] = x_ref[...] + _rmsnorm(acc_ref[...], gpost_ref[...])

    @pl.when((f == last_f) & (i == 0))
    def _():
        os_ref[...] = xs_ref[...] + _rmsnorm(accs_ref[...], gpost_ref[...])


def _ffn(x, xs, g_pre, w1_chunks, w2, g_post, *, tm):
    m, d = x.shape
    ms = xs.shape[0]
    nf, _, tf = w1_chunks.shape
    n_tiles = m // tm
    assert tm % nf == 0 and (tm // nf) % 16 == 0
    sample_spec = _resident((ms, d), lambda i, f: (0, 0))
    return pl.pallas_call(
        functools.partial(_ffn_kernel, tm=tm, nf=nf),
        grid=(n_tiles, nf),
        in_specs=[
            pl.BlockSpec((tm, d), lambda i, f: (i, 0)),
            pl.BlockSpec((tm, d), lambda i, f: (jnp.minimum(i + 1, n_tiles - 1), 0)),
            sample_spec,
            _resident((1, d), lambda i, f: (0, 0)),
            pl.BlockSpec((None, d, tf), lambda i, f: (f, 0, 0)),
            pl.BlockSpec((tf, d), lambda i, f: (f, 0)),
            _resident((1, d), lambda i, f: (0, 0)),
        ],
        out_specs=[pl.BlockSpec((tm, d), lambda i, f: (i, 0)),
                   pl.BlockSpec((ms, d), lambda i, f: (0, 0))],
        out_shape=[jax.ShapeDtypeStruct((m, d), F32), jax.ShapeDtypeStruct((ms, d), F32)],
        scratch_shapes=[pltpu.VMEM((2, tm, d), BF16), pltpu.VMEM((ms, d), BF16),
                        pltpu.VMEM((tm, d), F32), pltpu.VMEM((ms, d), F32)],
        compiler_params=_params(2),
        name="ffn",
    )(x, x, xs, g_pre, w1_chunks, w2, g_post)


def kernel(x_prompt, x_sample, state_pool, cache_attn_k, cache_attn_v, cache_mem_k, cache_mem_v, mem_prompt,
           g_mix_pre, g_mix_post, g_mem_pre, g_mem_post, g_ffn_pre, g_ffn_post, g_mem_kv,
           w_in, w_pool, pool_scale, w_out, w_xq, w_mem_kv, w_xo, w_ff1, w_ff2):
    depth = w_in.shape[0]
    assert depth == 1
    batch, seq, d = x_prompt.shape
    dec_batch, dec_seq, _ = x_sample.shape
    assert dec_seq == 1

    l = 0
    scale = pool_scale[l][None, :]
    gain = lambda g: g[l][None, :]

    xp = x_prompt.reshape(batch * seq, d)
    xs = x_sample.reshape(dec_batch, d)
    w_pool_f = w_pool[l]
    tm = ROW_TILE


    mem = mem_prompt.reshape(batch * N_MEM, d)
    (mk, mv, mk_h, mv_h, cos_p, sin_p, cos_s, sin_s), (w_in_b,) = _norm_proj(
        mem, gain(g_mem_kv), w_mem_kv[l], tm=MEM_KV_TILE, ncol=MEM_WIDTH,
        outputs=((0, FLAT), (1, FLAT), (0, HEADS), (1, HEADS)), name="mem_kv", casts=(w_in[l],),
        seq=seq, n_sample=dec_batch, sample_pos=PAST_LEN)

    ((pool_out, q, k, v, k_h, v_h, pool_state, us, qs_h, ks_h, vs_h),
     (w_out_b, w_xq_b, w_xo_b)) = _in_proj_pool(
        xp, xs, gain(g_mix_pre), w_in_b, cos_p, sin_p, cos_s, sin_s, w_pool_f, scale, tm=IN_PROJ_TILE, seq=seq,
        casts=(w_out[l], w_xq[l], w_xo[l]))
    attn, (w_ff1_b, w_ff2_b) = _attn_prompt(
        q, k, v, seq=seq, casts=((w_ff1[l], w_ff1.shape[2] // FFN_CHUNK), w_ff2[l]))
    heads3 = lambda a: a.reshape(dec_batch, -1, HEAD_DIM)
    x2, attn_s, _ = _mix_mem_prompt(
        pool_out, attn, xp, mk, mv, w_out_b, gain(g_mix_post), gain(g_mem_pre), w_xq_b, w_xo_b, gain(g_mem_post),
        (heads3(qs_h), heads3(ks_h), heads3(vs_h), cache_attn_k[l], cache_attn_v[l]), tm=tm, seq=seq)

    new_pool_s, x1s, qms_h = _mix_sample(us, state_pool[l], w_pool_f, scale, attn_s, xs, w_out_b,
                                         gain(g_mix_post), gain(g_mem_pre), w_xq_b, pos=PAST_LEN)
    mem_o_s = _mem_attn_sample(heads3(qms_h), cache_mem_k[l], cache_mem_v[l])
    x2s = _mem_out_sample(mem_o_s, w_xo_b, gain(g_mem_post), x1s)

    yp, ys = _ffn(x2, x2s, gain(g_ffn_pre), w_ff1_b, w_ff2_b, gain(g_ffn_post), tm=tm)

    keep = min(max(w for w, _ in DILATED), seq)
    return (
        yp.reshape(batch, seq, d),
        ys.reshape(dec_batch, 1, d),
        pool_state.reshape(batch, STATE_ROWS, POOL_WIDTH)[:, STATE_ROWS - POOL_STATE:][None],
        new_pool_s[None],
        k_h.reshape(batch, seq, N_HEADS, HEAD_DIM)[:, seq - keep:][None],
        v_h.reshape(batch, seq, N_HEADS, HEAD_DIM)[:, seq - keep:][None],
        ks_h.reshape(dec_batch, 1, N_HEADS, HEAD_DIM)[None],
        vs_h.reshape(dec_batch, 1, N_HEADS, HEAD_DIM)[None],
        mk_h.reshape(batch, N_MEM, MEM_HEADS, HEAD_DIM)[None],
        mv_h.reshape(batch, N_MEM, MEM_HEADS, HEAD_DIM)[None],
    )
```

```python
import functools

import jax
import jax.numpy as jnp
from jax import lax
from jax.experimental import pallas as pl
from jax.experimental.pallas import tpu as pltpu

D_MODEL = 2048
POOL_WIDTH = 1024
POOL_WINDOWS = (2, 4, 8, 16)
POOL_GROUP = POOL_WIDTH // len(POOL_WINDOWS)
POOL_STATE = max(POOL_WINDOWS) - 1
HEAD_DIM = 128
ATTN_WIDTH = 1024
N_HEADS = ATTN_WIDTH // HEAD_DIM
DILATED = ((128, 1), (512, 4), (2048, 16))
PITCHED_DILATION = 16
PITCH = 128 + 8
ROPE_DIM = HEAD_DIM // 4
ROPE_HALF = ROPE_DIM // 2
ROPE_THETA = 500000.0
N_MEM = 256
MEM_HEADS = 4
MEM_WIDTH = MEM_HEADS * HEAD_DIM
EPS = 1e-6
BLOCK = 128
NEG_INF = -1e30
LOG2_E = 1.4426950408889634
PAST_LEN = 8192
HALO = 32
STATE_ROWS = 16

F32 = jnp.float32
BF16 = jnp.bfloat16

VMEM_LIMIT_BYTES = 56 * 1024 * 1024

ROW_TILE = 512
IN_PROJ_TILE = 256
MEM_KV_TILE = 256
FFN_CHUNK = 1024
SAMPLE_GROUP = 8

FLAT, HEADS = "flat", "heads"


def _params(n_grid_axes):
    return pltpu.CompilerParams(
        dimension_semantics=("arbitrary",) * n_grid_axes,
        vmem_limit_bytes=VMEM_LIMIT_BYTES,
    )


def _resident(block_shape, index_map):
    return pl.BlockSpec(block_shape, index_map, pipeline_mode=pl.Buffered(1))


def _rmsnorm(x, g):
    ms = jnp.mean(x * x, axis=-1, keepdims=True)
    return x * lax.rsqrt(ms + EPS) * g


def _dot(a, b):
    return jnp.dot(a, b, preferred_element_type=F32)


def _dot_nt(a, b):
    return lax.dot_general(a, b, (((1,), (1,)), ((), ())), preferred_element_type=F32)


def _call_with_casts(kernel_fn, *, grid, in_specs, out_specs, out_shape, scratch_shapes=(), args, casts=(),
                     step_of=None, name):
    casts = [c if isinstance(c, tuple) else (c, 1) for c in casts]
    n_in, n_out, n_casts = len(in_specs), len(out_specs), len(casts)
    n_steps = 1
    for g in grid:
        n_steps *= g
    if step_of is None:
        step_of = lambda i: i
    cast_in_specs, cast_out_specs, cast_shapes = [], [], []
    for a, n_chunks in casts:
        rows, cols = a.shape
        assert rows % n_steps == 0 and cols % n_chunks == 0
        cast_in_specs.append(pl.BlockSpec((rows // n_steps, cols), lambda *idx: (step_of(*idx), 0)))
        if n_chunks == 1:
            cast_out_specs.append(cast_in_specs[-1])
            cast_shapes.append(jax.ShapeDtypeStruct(a.shape, BF16))
        else:
            cast_out_specs.append(pl.BlockSpec((n_chunks, rows // n_steps, cols // n_chunks),
                                               lambda *idx: (0, step_of(*idx), 0)))
            cast_shapes.append(jax.ShapeDtypeStruct((n_chunks, rows, cols // n_chunks), BF16))

    def body(*refs):
        ins, cast_in = refs[:n_in], refs[n_in:n_in + n_casts]
        outs = refs[n_in + n_casts:n_in + n_casts + n_out]
        cast_out = refs[n_in + n_casts + n_out:n_in + 2 * n_casts + n_out]

        def run_casts():
            for i_ref, o_ref, (_, n_chunks) in zip(cast_in, cast_out, casts):
                if n_chunks == 1:
                    o_ref[...] = i_ref[...].astype(BF16)
                else:
                    width = i_ref.shape[1] // n_chunks
                    for c in range(n_chunks):
                        o_ref[c] = i_ref[:, c * width:(c + 1) * width].astype(BF16)

        kernel_fn(run_casts, *ins, *outs, *refs[n_in + 2 * n_casts + n_out:])

    res = pl.pallas_call(
        body,
        grid=grid,
        in_specs=list(in_specs) + cast_in_specs,
        out_specs=list(out_specs) + cast_out_specs,
        out_shape=list(out_shape) + cast_shapes,
        scratch_shapes=list(scratch_shapes),
        compiler_params=_params(len(grid)),
        name=name,
    )(*args, *[a for a, _ in casts])
    return res[:n_out], res[n_out:]


def _store_heads(o_ref, layout, h, heads, rows, value):
    if layout == FLAT:
        o_ref[:, h * HEAD_DIM:(h + 1) * HEAD_DIM] = value
    else:
        o_ref[pl.ds(h, rows, stride=heads), :] = value


def _out_block(layout, tm, heads):
    return (tm, heads * HEAD_DIM) if layout == FLAT else (tm * heads, HEAD_DIM)


def _rope_rows(pos):
    lane = lax.broadcasted_iota(jnp.int32, (1, HEAD_DIM), 1)
    k = (lane % ROPE_HALF).astype(F32)
    inv = jnp.where(lane < ROPE_DIM, jnp.power(jnp.float32(ROPE_THETA), -k * 2.0 / ROPE_DIM), 0.0)
    ang = pos * inv
    return jnp.cos(ang), jnp.sin(ang) * jnp.where(lane < ROPE_HALF, -1.0, 1.0)


def _norm_proj_kernel(run_casts, x_ref, g_ref, w_ref, *out_refs, tm, ncol, outputs, sample_pos):
    run_casts()
    *out_refs, cos_ref, sin_ref, coss_ref, sins_ref = out_refs
    heads = ncol // HEAD_DIM
    xn = _rmsnorm(x_ref[...], g_ref[...]).astype(BF16)
    for c in sorted({c for c, _ in outputs}):
        acc = _dot(xn, w_ref[:, c * ncol:(c + 1) * ncol].astype(BF16))
        for h in range(heads):
            for (oc, layout), o_ref in zip(outputs, out_refs):
                if oc == c:
                    _store_heads(o_ref, layout, h, heads, tm, acc[:, h * HEAD_DIM:(h + 1) * HEAD_DIM])
    rows = cos_ref.shape[0]
    pos = pl.program_id(0) * rows + lax.broadcasted_iota(jnp.int32, (rows, 1), 0)
    cos_ref[...], sin_ref[...] = _rope_rows(pos.astype(F32))
    coss_ref[...], sins_ref[...] = _rope_rows(jnp.full((coss_ref.shape[0], 1), sample_pos, F32))


def _norm_proj(x, g, w, *, tm, ncol, outputs, name, casts=(), seq, n_sample, sample_pos):
    m, d = x.shape
    heads = ncol // HEAD_DIM
    n_steps = m // tm
    assert w.shape[0] == d and w.shape[1] % ncol == 0 and m % tm == 0 and seq % (8 * n_steps) == 0
    kern = functools.partial(_norm_proj_kernel, tm=tm, ncol=ncol, outputs=outputs, sample_pos=sample_pos)
    blocks = [_out_block(layout, tm, heads) for _, layout in outputs]
    table = pl.BlockSpec((seq // n_steps, HEAD_DIM), lambda i: (i, 0))
    sample_table = pl.BlockSpec((n_sample, HEAD_DIM), lambda i: (0, 0))
    return _call_with_casts(
        kern,
        grid=(n_steps,),
        in_specs=[
            pl.BlockSpec((tm, d), lambda i: (i, 0)),
            _resident((1, d), lambda i: (0, 0)),
            _resident(w.shape, lambda i: (0, 0)),
        ],
        out_specs=[pl.BlockSpec(b, lambda i: (i, 0)) for b in blocks] + [table, table, sample_table, sample_table],
        out_shape=([jax.ShapeDtypeStruct((n_steps * b[0], b[1]), F32) for b in blocks]
                   + [jax.ShapeDtypeStruct((seq, HEAD_DIM), F32)] * 2
                   + [jax.ShapeDtypeStruct((n_sample, HEAD_DIM), F32)] * 2),
        args=(x, g, w),
        casts=casts,
        name=name,
    )


def _window_sums(ext_ref, lvl_refs, g, w, rows):
    cols = slice(g * POOL_GROUP, (g + 1) * POOL_GROUP)
    levels = w.bit_length() - 1
    src, src_cols = ext_ref, cols
    for level in range(levels):
        shift = 1 << level
        last = level == levels - 1
        lo = HALO if last else 8 * (level + 1)
        assert lo - shift >= 8 * level
        total = src[lo:rows, src_cols] + src[lo - shift:rows - shift, src_cols]
        if last:
            return total
        lvl_refs[level % 2][lo:rows, :] = total
        src, src_cols = lvl_refs[level % 2], slice(None)


def _rope(a, cos, sin):
    lane = lax.broadcasted_iota(jnp.int32, a.shape, 1)
    partner = jnp.where(lane < ROPE_HALF, pltpu.roll(a, HEAD_DIM - ROPE_HALF, 1), pltpu.roll(a, ROPE_HALF, 1))
    return a * cos + partner * sin


def _in_proj_pool_kernel(run_casts, x_ref, xnext_ref, xs_ref, g_ref, w_ref, cos_ref, sin_ref, coss_ref, sins_ref,
                         wp_ref, scale_ref,
                         pool_ref, q_ref, k_ref, v_ref, kh_ref, vh_ref, state_ref, us_ref, qsh_ref, ksh_ref, vsh_ref,
                         xn_ref, ext_ref, lvl_a_ref, lvl_b_ref, *, tm, tiles_per_seq):
    step = pl.program_id(0)
    t_in_seq = step % tiles_per_seq
    n_sample = xs_ref.shape[0]

    def tile(xn, with_sample):
        run_casts()
        u = _dot(xn, w_ref[:, 0:POOL_WIDTH])
        ext_ref[HALO:HALO + tm, :] = u[0:tm]
        if with_sample:
            us_ref[...] = u[tm:]
        cos, sin = cos_ref[...], sin_ref[...]
        outs = ((q_ref, None, qsh_ref), (k_ref, kh_ref, ksh_ref), (v_ref, vh_ref, vsh_ref))
        for c, (major_ref, heads_ref, sample_ref) in enumerate(outs, start=1):
            acc = _dot(xn, w_ref[:, c * ATTN_WIDTH:(c + 1) * ATTN_WIDTH])
            for h in range(N_HEADS):
                a = acc[0:tm, h * HEAD_DIM:(h + 1) * HEAD_DIM]
                if major_ref is not v_ref:
                    a = _rope(a, cos, sin)
                major_ref[h] = a
                if heads_ref is not None:
                    _store_heads(heads_ref, HEADS, h, N_HEADS, tm, a)
                if with_sample:
                    a = acc[tm:, h * HEAD_DIM:(h + 1) * HEAD_DIM]
                    if major_ref is not v_ref:
                        a = _rope(a, coss_ref[...], sins_ref[...])
                    _store_heads(sample_ref, HEADS, h, N_HEADS, n_sample, a)

        pos = t_in_seq * tm + lax.broadcasted_iota(jnp.int32, (tm, 1), 0)
        for g, w in enumerate(POOL_WINDOWS):
            cols = slice(g * POOL_GROUP, (g + 1) * POOL_GROUP)
            wsum = _window_sums(ext_ref, (lvl_a_ref, lvl_b_ref), g, w, HALO + tm)
            cnt = jnp.minimum(w, pos + 1).astype(F32)
            pooled = (wsum / cnt - ext_ref[HALO:HALO + tm, cols]).astype(BF16)
            pool_ref[:, cols] = (_dot(pooled, wp_ref[g].astype(BF16)) * scale_ref[:, cols]).astype(pool_ref.dtype)
        state_ref[...] = ext_ref[HALO + tm - STATE_ROWS:HALO + tm, :]
        ext_ref[0:HALO, :] = ext_ref[tm:tm + HALO, :]

        xn_ref[(step + 1) % 2] = _rmsnorm(xnext_ref[...], g_ref[...]).astype(BF16)

    @pl.when(t_in_seq == 0)
    def _():
        ext_ref[0:HALO, :] = jnp.zeros((HALO, POOL_WIDTH), F32)

    @pl.when(step == 0)
    def _():
        rows = jnp.concatenate([x_ref[...], xs_ref[...]], axis=0)
        tile(_rmsnorm(rows, g_ref[...]).astype(BF16), True)

    @pl.when(step > 0)
    def _():
        tile(xn_ref[step % 2], False)


def _in_proj_pool(x, xs, g, w, cos, sin, cos_s, sin_s, w_pool, pool_scale, *, tm, seq, casts=()):
    m, d = x.shape
    ms = xs.shape[0]
    tiles_per_seq = seq // tm
    n_seq = m // seq
    assert POOL_WIDTH == ATTN_WIDTH and w.shape == (d, POOL_WIDTH + 3 * ATTN_WIDTH)
    kern = functools.partial(_in_proj_pool_kernel, tm=tm, tiles_per_seq=tiles_per_seq)
    row_tile = lambda width: pl.BlockSpec((tm, width), lambda i: (i, 0))
    heads_tile = pl.BlockSpec((tm * N_HEADS, HEAD_DIM), lambda i: (i, 0))
    table = pl.BlockSpec((tm, HEAD_DIM), lambda i: (i % tiles_per_seq, 0))
    whole = lambda shape: _resident(shape, lambda i: (0,) * len(shape))
    head_major = jax.ShapeDtypeStruct((N_HEADS, m, HEAD_DIM), F32)
    head_major_tile = pl.BlockSpec((N_HEADS, tm, HEAD_DIM), lambda i: (0, i, 0))
    by_heads = jax.ShapeDtypeStruct((m * N_HEADS, HEAD_DIM), F32)
    sample_heads = jax.ShapeDtypeStruct((ms * N_HEADS, HEAD_DIM), F32)
    sample_heads_spec = pl.BlockSpec((ms * N_HEADS, HEAD_DIM), lambda i: (0, 0))
    return _call_with_casts(
        kern,
        grid=(m // tm,),
        in_specs=[
            row_tile(d),
            pl.BlockSpec((tm, d), lambda i: (jnp.minimum(i + 1, m // tm - 1), 0)),
            whole((ms, d)),
            whole((1, d)),
            whole(w.shape),
            table,
            table,
            whole((ms, HEAD_DIM)),
            whole((ms, HEAD_DIM)),
            whole(w_pool.shape),
            whole((1, POOL_WIDTH)),
        ],
        out_specs=[row_tile(POOL_WIDTH), head_major_tile, head_major_tile, head_major_tile,
                   heads_tile, heads_tile,
                   pl.BlockSpec((STATE_ROWS, POOL_WIDTH), lambda i: (i // tiles_per_seq, 0)),
                   pl.BlockSpec((ms, POOL_WIDTH), lambda i: (0, 0)),
                   sample_heads_spec, sample_heads_spec, sample_heads_spec],
        out_shape=[jax.ShapeDtypeStruct((m, POOL_WIDTH), BF16), head_major, head_major, head_major, by_heads, by_heads,
                   jax.ShapeDtypeStruct((n_seq * STATE_ROWS, POOL_WIDTH), F32),
                   jax.ShapeDtypeStruct((ms, POOL_WIDTH), F32), sample_heads, sample_heads, sample_heads],
        scratch_shapes=[pltpu.VMEM((2, tm, d), BF16),
                        pltpu.VMEM((HALO + tm, POOL_WIDTH), F32),
                        pltpu.VMEM((HALO + tm, POOL_GROUP), F32),
                        pltpu.VMEM((HALO + tm, POOL_GROUP), F32)],
        args=(x, x, xs, g, w, cos, sin, cos_s, sin_s, w_pool, pool_scale),
        casts=casts,
        name="in_proj_pool",
    )


def _mix_sample_kernel(u_ref, prev_ref, wp_ref, scale_ref, attn_ref, x_ref, wo_ref, gpost_ref, gpre_ref, wq_ref,
                       state_ref, x1_ref, qm_ref, *, pos):
    n = u_ref.shape[0]
    for j in range(POOL_STATE - 1):
        state_ref[:, j, :] = prev_ref[:, j + 1, :]
    state_ref[:, POOL_STATE - 1, :] = u_ref[...]
    y = jnp.zeros(x_ref.shape, F32)
    for g, w in enumerate(POOL_WINDOWS):
        cols = slice(g * POOL_GROUP, (g + 1) * POOL_GROUP)
        wsum = u_ref[:, cols]
        for back in range(1, w):
            wsum = wsum + prev_ref[:, POOL_STATE - back, cols]
        cnt = float(min(w, pos + 1))
        pooled = (wsum / cnt - u_ref[:, cols]).astype(BF16)
        pool_out = (_dot(pooled, wp_ref[g].astype(BF16)) * scale_ref[:, cols]).astype(BF16)
        y = y + _dot(pool_out, wo_ref[cols, :])
    for h in range(N_HEADS):
        rows = slice(POOL_WIDTH + h * HEAD_DIM, POOL_WIDTH + (h + 1) * HEAD_DIM)
        y = y + _dot(attn_ref[:, h, :].astype(BF16), wo_ref[rows, :])
    x1 = x_ref[...] + _rmsnorm(y, gpost_ref[...])
    x1_ref[...] = x1
    qm = _dot(_rmsnorm(x1, gpre_ref[...]).astype(BF16), wq_ref[...])
    for h in range(MEM_HEADS):
        _store_heads(qm_ref, HEADS, h, MEM_HEADS, n, qm[:, h * HEAD_DIM:(h + 1) * HEAD_DIM])


def _mix_sample(u, prev, w_pool, pool_scale, attn, x, w_out, g_post, g_pre, w_xq, *, pos):
    n, d = x.shape
    kern = functools.partial(_mix_sample_kernel, pos=pos)
    return pl.pallas_call(
        kern,
        out_shape=[jax.ShapeDtypeStruct(prev.shape, F32), jax.ShapeDtypeStruct((n, d), F32),
                   jax.ShapeDtypeStruct((n * MEM_HEADS, HEAD_DIM), F32)],
        compiler_params=pltpu.CompilerParams(vmem_limit_bytes=VMEM_LIMIT_BYTES),
        name="mix_sample",
    )(u, prev, w_pool, pool_scale, attn, x, w_out, g_post, g_pre, w_xq)


def _attn_prompt_kernel(run_casts, q_ref, k_ref, v_ref, o_ref, *scratch, seq):
    run_casts()
    n_br = len(DILATED)
    ob_refs, mb_refs, sb_refs = scratch[:n_br], scratch[n_br:2 * n_br], scratch[2 * n_br:3 * n_br]
    p_ref, bias_ref, q4_ref, k4_ref, v4_ref = scratch[3 * n_br:]
    exp2_scale = HEAD_DIM ** -0.5 * LOG2_E
    qi = lax.broadcasted_iota(jnp.int32, (BLOCK, 2 * BLOCK), 0)
    kj = lax.broadcasted_iota(jnp.int32, (BLOCK, 2 * BLOCK), 1)
    in_band = ((kj < BLOCK) & (kj >= qi)) | ((kj >= BLOCK) & (kj - BLOCK <= qi))
    bias_ref[...] = jnp.where(in_band, 0.0, NEG_INF)
    ones = jnp.ones((2 * BLOCK, HEAD_DIM), BF16)

    def rows(block, r, d):
        if d == 1:
            start = block * BLOCK
            return pl.ds(start if isinstance(start, int) else pl.multiple_of(start, BLOCK), BLOCK)
        return pl.ds(block * (BLOCK * d) + r, BLOCK, stride=d)

    def out_rows(block, r, d):
        if d == PITCHED_DILATION:
            start = r * PITCH
            return pl.ds(start if isinstance(start, int) else pl.multiple_of(start, 8), BLOCK)
        return rows(block, r, d)

    quarter = seq // 4
    assert [d for _, d in DILATED] == [1, 4, 16]
    for ref, by4_ref in ((q_ref, q4_ref), (k_ref, k4_ref), (v_ref, v4_ref)):
        for b in range(4):
            by4_ref[b * quarter:(b + 1) * quarter, :] = ref[pl.ds(b, quarter, stride=4), :]

    def in_block(refs, block, r, d):
        ref, by4_ref = refs
        if d == 1:
            return ref[rows(block, r, d), :]
        if d == 4:
            return by4_ref[pl.ds(r * quarter + block * BLOCK, BLOCK), :]
        return by4_ref[pl.ds((r % 4) * quarter + r // 4 + block * (4 * BLOCK), BLOCK, stride=4), :]

    def keys(refs, block, r, d, with_prev):
        cur = in_block(refs, block, r, d)
        if not with_prev:
            return cur.astype(BF16)
        return jnp.concatenate([in_block(refs, block - 1, r, d), cur], axis=0).astype(BF16)

    groups, base = [], 0
    for g, (_, d) in enumerate(DILATED):
        n_blocks = seq // d // BLOCK
        for with_prev in (False, True):
            per_r = n_blocks - 1 if with_prev else 1
            count = d * per_r
            if count == 0:
                continue
            cols = slice(0, 2 * BLOCK) if with_prev else slice(BLOCK, 2 * BLOCK)
            width = 2 * BLOCK if with_prev else BLOCK

            def locate(idx, per_r=per_r, with_prev=with_prev):
                return (idx % per_r + 1, idx // per_r) if with_prev else (0, idx)

            def probabilities(idx, g=g, d=d, with_prev=with_prev, cols=cols, width=width, locate=locate, base=base):
                b, r = locate(idx)
                q = in_block((q_ref, q4_ref), b, r, d).astype(BF16)
                s = _dot_nt(q, keys((k_ref, k4_ref), b, r, d, with_prev)) + bias_ref[:, cols]
                m = jnp.max(s, axis=-1, keepdims=True)
                p_ref[base + idx, :, 0:width] = jnp.exp2((s - m) * exp2_scale).astype(BF16)
                mb_refs[g][out_rows(b, r, d), :] = jnp.broadcast_to(m, (BLOCK, HEAD_DIM))

            def values(idx, g=g, d=d, with_prev=with_prev, width=width, locate=locate, base=base):
                b, r = locate(idx)
                v1 = jnp.concatenate([keys((v_ref, v4_ref), b, r, d, with_prev), ones[0:width]], axis=1)
                acc = _dot(p_ref[base + idx, :, 0:width], v1)
                ob_refs[g][out_rows(b, r, d), :] = acc[:, 0:HEAD_DIM]
                sb_refs[g][out_rows(b, r, d), :] = acc[:, HEAD_DIM:]

            groups.append((count, probabilities, values))
            base += count

    for count, probabilities, _ in groups:
        for idx in range(count):
            probabilities(idx)
    for count, _, values in groups:
        for idx in range(count):
            values(idx)

    chunk = 256

    def combine(c, carry):
        rw = pl.ds(pl.multiple_of(c * chunk, chunk), chunk)

        def read(ref, g):
            if DILATED[g][1] != PITCHED_DILATION:
                return ref[rw, :]
            pieces = [ref[pl.ds((j % 2) * 8 * PITCH + c * (chunk // PITCHED_DILATION) + j // 2, 8, stride=PITCH), :]
                      for j in range(chunk // 8)]
            return jnp.concatenate(pieces, axis=0)

        ms = [read(mb, g) for g, mb in enumerate(mb_refs)]
        m = functools.reduce(jnp.maximum, ms)
        ws = [jnp.exp2((mg - m) * exp2_scale) for mg in ms]
        num = functools.reduce(jnp.add, [w * read(ob, g) for g, (w, ob) in enumerate(zip(ws, ob_refs))])
        den = functools.reduce(jnp.add, [w * read(sb, g) for g, (w, sb) in enumerate(zip(ws, sb_refs))])
        o_ref[rw, :] = (num / den).astype(o_ref.dtype)
        return carry

    lax.fori_loop(0, seq // chunk, combine, 0)


def _attn_prompt(q, k, v, *, seq, casts=()):
    n_heads, m, _ = q.shape
    n_seq = m // seq
    spec = pl.BlockSpec((None, seq, HEAD_DIM), lambda n, h: (h, n, 0))
    kern = functools.partial(_attn_prompt_kernel, seq=seq)
    (attn,), cast_out = _call_with_casts(
        kern,
        grid=(n_seq, n_heads),
        in_specs=[spec, spec, spec],
        out_specs=[spec],
        out_shape=[jax.ShapeDtypeStruct(q.shape, BF16)],
        scratch_shapes=(
            [pltpu.VMEM((PITCHED_DILATION * PITCH if d == PITCHED_DILATION else seq, HEAD_DIM), F32)
             for _ in range(3) for _, d in DILATED]
            + [pltpu.VMEM((len(DILATED) * seq // BLOCK, BLOCK, 2 * BLOCK), BF16),
               pltpu.VMEM((BLOCK, 2 * BLOCK), F32)]
            + [pltpu.VMEM((seq, HEAD_DIM), F32) for _ in range(3)]),
        args=(q, k, v),
        casts=casts,
        step_of=lambda n, h: n * n_heads + h,
        name="attn_prompt",
    )
    return attn, cast_out


def _attn_sample_kernel(q_ref, kn_ref, vn_ref, *refs, group):
    n_br = len(DILATED)
    kc_refs, vc_refs, o_ref = refs[:n_br], refs[n_br:2 * n_br], refs[2 * n_br]
    for j in range(group):
        q = q_ref[j] * (HEAD_DIM ** -0.5 * LOG2_E)
        t_new = jnp.sum(q * kn_ref[j], axis=-1, keepdims=True)
        t_br = [jnp.sum(kc[j] * q[None], axis=-1, keepdims=True) for kc in kc_refs]
        m = t_new
        for t in t_br:
            m = jnp.maximum(m, jnp.max(t, axis=0))
        p_new = jnp.exp2(t_new - m) * float(n_br)
        den = p_new
        num = p_new * vn_ref[j]
        for t, vc in zip(t_br, vc_refs):
            p = jnp.exp2(t - m[None])
            den = den + jnp.sum(p, axis=0)
            num = num + jnp.sum(p * vc[j], axis=0)
        out = num / den
        o_ref[j] = out
    return out


def _attn_sample_operands(q, k_new, v_new, k_cache, v_cache, *, group):
    n, heads, _ = q.shape
    win = k_cache.shape[1]
    row = pl.BlockSpec((group, heads, HEAD_DIM), lambda i: (i, 0, 0))
    cache_specs, k_views, v_views = [], [], []
    for w, d in DILATED:
        assert w == BLOCK * d and win % w == 0
        last = win // w - 1
        cache_specs.append(pl.BlockSpec((group, BLOCK, None, heads, HEAD_DIM),
                                        lambda i, last=last: (i, last, 0, 0, 0)))
        k_views.append(k_cache.reshape(n, win // d, d, heads, HEAD_DIM))
        v_views.append(v_cache.reshape(n, win // d, d, heads, HEAD_DIM))
    return ([row, row, row] + cache_specs + cache_specs, row, jax.ShapeDtypeStruct((n, heads, HEAD_DIM), F32),
            (q, k_new, v_new, *k_views, *v_views))


def _mix_mem_prompt_kernel(run_casts, pool_ref, attn_ref, x_ref, mk_ref, mv_ref, wo_ref,
                           gmix_ref, gpre_ref, wq_ref, wxo_ref, gmem_ref, *refs, sample_group):
    n_sample_in = 3 + 2 * len(DILATED)
    sample_in, (x2_ref, attn_s_ref) = refs[:n_sample_in], refs[n_sample_in:]
    run_casts()
    scale = HEAD_DIM ** -0.5
    mixed = jnp.concatenate([pool_ref[...]] + [attn_ref[h] for h in range(N_HEADS)], axis=-1)
    y = _dot(mixed, wo_ref[...])
    last = _attn_sample_kernel(*sample_in, attn_s_ref, group=sample_group)
    zero = jnp.minimum(jnp.abs(last[0:1, :]), 0.0)
    gmix = gmix_ref[...] + jnp.concatenate([zero] * (gmix_ref.shape[1] // HEAD_DIM), axis=1)
    x1 = x_ref[...] + _rmsnorm(y, gmix)
    qm = _dot(_rmsnorm(x1, gpre_ref[...]).astype(BF16), wq_ref[...]).astype(BF16)
    heads = []
    for h in range(MEM_HEADS):
        cols = slice(h * HEAD_DIM, (h + 1) * HEAD_DIM)
        s = _dot_nt(qm[:, cols], mk_ref[:, cols].astype(BF16)) * scale
        p = jnp.exp(s - jnp.max(s, axis=-1, keepdims=True))
        o = _dot(p.astype(BF16), mv_ref[:, cols].astype(BF16))
        heads.append((o / jnp.sum(p, axis=-1, keepdims=True)).astype(BF16))
    y = _dot(jnp.concatenate(heads, axis=-1), wxo_ref[...])
    x2_ref[...] = x1 + _rmsnorm(y, gmem_ref[...])


def _mix_mem_prompt(pool_out, attn, x, mk, mv, w_out, g_mix_post, g_mem_pre, w_xq, w_xo, g_mem_post,
                    sample_attn, *, tm, seq, casts=()):
    m, d = x.shape
    n_tiles = m // tm
    tiles_per_seq = seq // tm
    half = pool_out.shape[1]
    mem_spec = pl.BlockSpec((N_MEM, MEM_WIDTH), lambda i: (i // tiles_per_seq, 0))
    gain_spec = _resident((1, d), lambda i: (0, 0))
    n_sample = sample_attn[0].shape[0]
    assert n_sample % n_tiles == 0
    sample_group = n_sample // n_tiles
    s_in_specs, s_out_spec, s_out_shape, s_args = _attn_sample_operands(*sample_attn, group=sample_group)
    (x2, attn_s), cast_out = _call_with_casts(
        functools.partial(_mix_mem_prompt_kernel, sample_group=sample_group),
        grid=(n_tiles,),
        in_specs=[
            pl.BlockSpec((tm, half), lambda i: (i, 0)),
            pl.BlockSpec((N_HEADS, tm, HEAD_DIM), lambda i: (0, i, 0)),
            pl.BlockSpec((tm, d), lambda i: (i, 0)),
            mem_spec,
            mem_spec,
            _resident(w_out.shape, lambda i: (0, 0)),
            gain_spec,
            gain_spec,
            _resident(w_xq.shape, lambda i: (0, 0)),
            _resident(w_xo.shape, lambda i: (0, 0)),
            gain_spec,
        ] + s_in_specs,
        out_specs=[pl.BlockSpec((tm, d), lambda i: (i, 0)), s_out_spec],
        out_shape=[jax.ShapeDtypeStruct((m, d), F32), s_out_shape],
        args=(pool_out, attn, x, mk, mv, w_out, g_mix_post, g_mem_pre, w_xq, w_xo, g_mem_post) + tuple(s_args),
        casts=casts,
        name="mix_mem_prompt",
    )
    return x2, attn_s, cast_out


def _mem_attn_sample_kernel(qm_ref, mk_ref, mv_ref, o_ref, *, group):
    heads = qm_ref.shape[1]
    both = lambda a: jnp.concatenate([a, a], axis=0)
    fold = lambda a: a[0:heads] + a[heads:]
    for j in range(group):
        q = both(qm_ref[j] * (HEAD_DIM ** -0.5 * LOG2_E))
        t = jnp.sum(mk_ref[j] * q[None], axis=-1, keepdims=True)
        m = jnp.max(t, axis=0)
        m = both(jnp.maximum(m[0:heads], m[heads:]))
        p = jnp.exp2(t - m[None])
        o_ref[j] = fold(jnp.sum(p * mv_ref[j], axis=0)) / fold(jnp.sum(p, axis=0))


def _mem_attn_sample(qm, mem_k, mem_v):
    n, heads, _ = qm.shape
    group = SAMPLE_GROUP
    row = pl.BlockSpec((group, heads, HEAD_DIM), lambda i: (i, 0, 0))
    mem = pl.BlockSpec((group, N_MEM // 2, 2 * heads, HEAD_DIM), lambda i: (i, 0, 0, 0))
    pairs = lambda a: a.reshape(n, N_MEM // 2, 2 * heads, HEAD_DIM)
    return pl.pallas_call(
        functools.partial(_mem_attn_sample_kernel, group=group),
        grid=(n // group,),
        in_specs=[row, mem, mem],
        out_specs=row,
        out_shape=jax.ShapeDtypeStruct((n, heads, HEAD_DIM), F32),
        compiler_params=_params(1),
        name="mem_attn_sample",
    )(qm, pairs(mem_k), pairs(mem_v))


def _mem_out_sample_kernel(a_ref, w_ref, g_ref, r_ref, o_ref):
    y = jnp.zeros(r_ref.shape, F32)
    for h in range(a_ref.shape[1]):
        y = y + _dot(a_ref[:, h, :].astype(BF16), w_ref[h * HEAD_DIM:(h + 1) * HEAD_DIM, :])
    o_ref[...] = r_ref[...] + _rmsnorm(y, g_ref[...])


def _mem_out_sample(a, w, g, resid):
    return pl.pallas_call(
        _mem_out_sample_kernel,
        out_shape=jax.ShapeDtypeStruct(resid.shape, F32),
        compiler_params=pltpu.CompilerParams(vmem_limit_bytes=VMEM_LIMIT_BYTES),
        name="mem_out_sample",
    )(a, w, g, resid)


def _ffn_kernel(x_ref, xnext_ref, xs_ref, gpre_ref, w1_ref, w2_ref, gpost_ref, o_ref, os_ref,
                xn_ref, xns_ref, acc_ref, accs_ref, *, tm, nf):
    i, f = pl.program_id(0), pl.program_id(1)
    last_f = nf - 1
    chunk = tm // nf

    @pl.when((i == 0) & (f == 0))
    def _():
        xn_ref[0] = _rmsnorm(x_ref[...], gpre_ref[...]).astype(BF16)
        xns_ref[...] = _rmsnorm(xs_ref[...], gpre_ref[...]).astype(BF16)
        accs_ref[...] = jnp.zeros_like(accs_ref)

    @pl.when(f == 0)
    def _():
        acc_ref[...] = jnp.zeros_like(acc_ref)

    def mlp(lhs):
        hidden = jnp.square(jnp.maximum(_dot(lhs, w1_ref[...]), 0.0)).astype(BF16)
        return _dot(hidden, w2_ref[...])

    def norm_next_rows():
        rows = pl.ds(pl.multiple_of(f * chunk, chunk), chunk)
        xn_ref[(i + 1) % 2, rows, :] = _rmsnorm(xnext_ref[rows, :], gpre_ref[...]).astype(BF16)

    @pl.when(i == 0)
    def _():
        both = mlp(jnp.concatenate([xn_ref[0], xns_ref[...]], axis=0))
        acc_ref[...] += both[0:tm]
        accs_ref[...] += both[tm:]
        norm_next_rows()

    @pl.when(i > 0)
    def _():
        acc_ref[...] += mlp(xn_ref[i % 2])
        norm_next_rows()

    @pl.when(f == last_f)
    def _():
        o_ref[...] = x_ref[...] + _rmsnorm(acc_ref[...], gpost_ref[...])

    @pl.when((f == last_f) & (i == 0))
    def _():
        os_ref[...] = xs_ref[...] + _rmsnorm(accs_ref[...], gpost_ref[...])


def _ffn(x, xs, g_pre, w1_chunks, w2, g_post, *, tm):
    m, d = x.shape
    ms = xs.shape[0]
    nf, _, tf = w1_chunks.shape
    n_tiles = m // tm
    assert tm % nf == 0 and (tm // nf) % 16 == 0
    sample_spec = _resident((ms, d), lambda i, f: (0, 0))
    return pl.pallas_call(
        functools.partial(_ffn_kernel, tm=tm, nf=nf),
        grid=(n_tiles, nf),
        in_specs=[
            pl.BlockSpec((tm, d), lambda i, f: (i, 0)),
            pl.BlockSpec((tm, d), lambda i, f: (jnp.minimum(i + 1, n_tiles - 1), 0)),
            sample_spec,
            _resident((1, d), lambda i, f: (0, 0)),
            pl.BlockSpec((None, d, tf), lambda i, f: (f, 0, 0)),
            pl.BlockSpec((tf, d), lambda i, f: (f, 0)),
            _resident((1, d), lambda i, f: (0, 0)),
        ],
        out_specs=[pl.BlockSpec((tm, d), lambda i, f: (i, 0)),
                   pl.BlockSpec((ms, d), lambda i, f: (0, 0))],
        out_shape=[jax.ShapeDtypeStruct((m, d), F32), jax.ShapeDtypeStruct((ms, d), F32)],
        scratch_shapes=[pltpu.VMEM((2, tm, d), BF16), pltpu.VMEM((ms, d), BF16),
                        pltpu.VMEM((tm, d), F32), pltpu.VMEM((ms, d), F32)],
        compiler_params=_params(2),
        name="ffn",
    )(x, x, xs, g_pre, w1_chunks, w2, g_post)


def kernel(x_prompt, x_sample, state_pool, cache_attn_k, cache_attn_v, cache_mem_k, cache_mem_v, mem_prompt,
           g_mix_pre, g_mix_post, g_mem_pre, g_mem_post, g_ffn_pre, g_ffn_post, g_mem_kv,
           w_in, w_pool, pool_scale, w_out, w_xq, w_mem_kv, w_xo, w_ff1, w_ff2):
    depth = w_in.shape[0]
    assert depth == 1
    batch, seq, d = x_prompt.shape
    dec_batch, dec_seq, _ = x_sample.shape
    assert dec_seq == 1

    l = 0
    scale = pool_scale[l][None, :]
    gain = lambda g: g[l][None, :]

    xp = x_prompt.reshape(batch * seq, d)
    xs = x_sample.reshape(dec_batch, d)
    w_pool_f = w_pool[l]
    tm = ROW_TILE


    mem = mem_prompt.reshape(batch * N_MEM, d)
    (mk, mv, mk_h, mv_h, cos_p, sin_p, cos_s, sin_s), (w_in_b,) = _norm_proj(
        mem, gain(g_mem_kv), w_mem_kv[l], tm=MEM_KV_TILE, ncol=MEM_WIDTH,
        outputs=((0, FLAT), (1, FLAT), (0, HEADS), (1, HEADS)), name="mem_kv", casts=(w_in[l],),
        seq=seq, n_sample=dec_batch, sample_pos=PAST_LEN)

    ((pool_out, q, k, v, k_h, v_h, pool_state, us, qs_h, ks_h, vs_h),
     (w_out_b, w_xq_b, w_xo_b)) = _in_proj_pool(
        xp, xs, gain(g_mix_pre), w_in_b, cos_p, sin_p, cos_s, sin_s, w_pool_f, scale, tm=IN_PROJ_TILE, seq=seq,
        casts=(w_out[l], w_xq[l], w_xo[l]))
    attn, (w_ff1_b, w_ff2_b) = _attn_prompt(
        q, k, v, seq=seq, casts=((w_ff1[l], w_ff1.shape[2] // FFN_CHUNK), w_ff2[l]))
    heads3 = lambda a: a.reshape(dec_batch, -1, HEAD_DIM)
    x2, attn_s, _ = _mix_mem_prompt(
        pool_out, attn, xp, mk, mv, w_out_b, gain(g_mix_post), gain(g_mem_pre), w_xq_b, w_xo_b, gain(g_mem_post),
        (heads3(qs_h), heads3(ks_h), heads3(vs_h), cache_attn_k[l], cache_attn_v[l]), tm=tm, seq=seq)

    new_pool_s, x1s, qms_h = _mix_sample(us, state_pool[l], w_pool_f, scale, attn_s, xs, w_out_b,
                                         gain(g_mix_post), gain(g_mem_pre), w_xq_b, pos=PAST_LEN)
    mem_o_s = _mem_attn_sample(heads3(qms_h), cache_mem_k[l], cache_mem_v[l])
    x2s = _mem_out_sample(mem_o_s, w_xo_b, gain(g_mem_post), x1s)

    yp, ys = _ffn(x2, x2s, gain(g_ffn_pre), w_ff1_b, w_ff2_b, gain(g_ffn_post), tm=tm)

    keep = min(max(w for w, _ in DILATED), seq)
    return (
        yp.reshape(batch, seq, d),
        ys.reshape(dec_batch, 1, d),
        pool_state.reshape(batch, STATE_ROWS, POOL_WIDTH)[:, STATE_ROWS - POOL_STATE:][None],
        new_pool_s[None],
        k_h.reshape(batch, seq, N_HEADS, HEAD_DIM)[:, seq - keep:][None],
        v_h.reshape(batch, seq, N_HEADS, HEAD_DIM)[:, seq - keep:][None],
        ks_h.reshape(dec_batch, 1, N_HEADS, HEAD_DIM)[None],
        vs_h.reshape(dec_batch, 1, N_HEADS, HEAD_DIM)[None],
        mk_h.reshape(batch, N_MEM, MEM_HEADS, HEAD_DIM)[None],
        mv_h.reshape(batch, N_MEM, MEM_HEADS, HEAD_DIM)[None],
    )
```

```python
import functools

import jax
import jax.numpy as jnp
from jax import lax
from jax.experimental import pallas as pl
from jax.experimental.pallas import tpu as pltpu

D_MODEL = 2048
POOL_WIDTH = 1024
POOL_WINDOWS = (2, 4, 8, 16)
POOL_GROUP = POOL_WIDTH // len(POOL_WINDOWS)
POOL_STATE = max(POOL_WINDOWS) - 1
HEAD_DIM = 128
ATTN_WIDTH = 1024
N_HEADS = ATTN_WIDTH // HEAD_DIM
DILATED = ((128, 1), (512, 4), (2048, 16))
PITCHED_DILATION = 16
PITCH = 128 + 8
ROPE_DIM = HEAD_DIM // 4
ROPE_HALF = ROPE_DIM // 2
ROPE_THETA = 500000.0
N_MEM = 256
MEM_HEADS = 4
MEM_WIDTH = MEM_HEADS * HEAD_DIM
EPS = 1e-6
BLOCK = 128
NEG_INF = -1e30
LOG2_E = 1.4426950408889634
PAST_LEN = 8192
HALO = 32
STATE_ROWS = 16

F32 = jnp.float32
BF16 = jnp.bfloat16

VMEM_LIMIT_BYTES = 56 * 1024 * 1024

ROW_TILE = 512
IN_PROJ_TILE = 256
MEM_KV_TILE = 256
FFN_CHUNK = 1024
SAMPLE_GROUP = 8

FLAT, HEADS = "flat", "heads"


def _params(n_grid_axes):
    return pltpu.CompilerParams(
        dimension_semantics=("arbitrary",) * n_grid_axes,
        vmem_limit_bytes=VMEM_LIMIT_BYTES,
    )


def _resident(block_shape, index_map):
    return pl.BlockSpec(block_shape, index_map, pipeline_mode=pl.Buffered(1))


def _rmsnorm(x, g):
    ms = jnp.mean(x * x, axis=-1, keepdims=True)
    return x * lax.rsqrt(ms + EPS) * g


def _dot(a, b):
    return jnp.dot(a, b, preferred_element_type=F32)


def _dot_nt(a, b):
    return lax.dot_general(a, b, (((1,), (1,)), ((), ())), preferred_element_type=F32)


def _call_with_casts(kernel_fn, *, grid, in_specs, out_specs, out_shape, scratch_shapes=(), args, casts=(),
                     step_of=None, name):
    casts = [c if isinstance(c, tuple) else (c, 1) for c in casts]
    n_in, n_out, n_casts = len(in_specs), len(out_specs), len(casts)
    n_steps = 1
    for g in grid:
        n_steps *= g
    if step_of is None:
        step_of = lambda i: i
    cast_in_specs, cast_out_specs, cast_shapes = [], [], []
    for a, n_chunks in casts:
        rows, cols = a.shape
        assert rows % n_steps == 0 and cols % n_chunks == 0
        cast_in_specs.append(pl.BlockSpec((rows // n_steps, cols), lambda *idx: (step_of(*idx), 0)))
        if n_chunks == 1:
            cast_out_specs.append(cast_in_specs[-1])
            cast_shapes.append(jax.ShapeDtypeStruct(a.shape, BF16))
        else:
            cast_out_specs.append(pl.BlockSpec((n_chunks, rows // n_steps, cols // n_chunks),
                                               lambda *idx: (0, step_of(*idx), 0)))
            cast_shapes.append(jax.ShapeDtypeStruct((n_chunks, rows, cols // n_chunks), BF16))

    def body(*refs):
        ins, cast_in = refs[:n_in], refs[n_in:n_in + n_casts]
        outs = refs[n_in + n_casts:n_in + n_casts + n_out]
        cast_out = refs[n_in + n_casts + n_out:n_in + 2 * n_casts + n_out]

        def run_casts():
            for i_ref, o_ref, (_, n_chunks) in zip(cast_in, cast_out, casts):
                if n_chunks == 1:
                    o_ref[...] = i_ref[...].astype(BF16)
                else:
                    width = i_ref.shape[1] // n_chunks
                    for c in range(n_chunks):
                        o_ref[c] = i_ref[:, c * width:(c + 1) * width].astype(BF16)

        kernel_fn(run_casts, *ins, *outs, *refs[n_in + 2 * n_casts + n_out:])

    res = pl.pallas_call(
        body,
        grid=grid,
        in_specs=list(in_specs) + cast_in_specs,
        out_specs=list(out_specs) + cast_out_specs,
        out_shape=list(out_shape) + cast_shapes,
        scratch_shapes=list(scratch_shapes),
        compiler_params=_params(len(grid)),
        name=name,
    )(*args, *[a for a, _ in casts])
    return res[:n_out], res[n_out:]


def _store_heads(o_ref, layout, h, heads, rows, value):
    if layout == FLAT:
        o_ref[:, h * HEAD_DIM:(h + 1) * HEAD_DIM] = value
    else:
        o_ref[pl.ds(h, rows, stride=heads), :] = value


def _out_block(layout, tm, heads):
    return (tm, heads * HEAD_DIM) if layout == FLAT else (tm * heads, HEAD_DIM)


def _rope_rows(pos):
    lane = lax.broadcasted_iota(jnp.int32, (1, HEAD_DIM), 1)
    k = (lane % ROPE_HALF).astype(F32)
    inv = jnp.where(lane < ROPE_DIM, jnp.power(jnp.float32(ROPE_THETA), -k * 2.0 / ROPE_DIM), 0.0)
    ang = pos * inv
    return jnp.cos(ang), jnp.sin(ang) * jnp.where(lane < ROPE_HALF, -1.0, 1.0)


def _norm_proj_kernel(run_casts, x_ref, g_ref, w_ref, *out_refs, tm, ncol, outputs, sample_pos):
    run_casts()
    *out_refs, cos_ref, sin_ref, coss_ref, sins_ref = out_refs
    heads = ncol // HEAD_DIM
    xn = _rmsnorm(x_ref[...], g_ref[...]).astype(BF16)
    for c in sorted({c for c, _ in outputs}):
        acc = _dot(xn, w_ref[:, c * ncol:(c + 1) * ncol].astype(BF16))
        for h in range(heads):
            for (oc, layout), o_ref in zip(outputs, out_refs):
                if oc == c:
                    _store_heads(o_ref, layout, h, heads, tm, acc[:, h * HEAD_DIM:(h + 1) * HEAD_DIM])
    rows = cos_ref.shape[0]
    pos = pl.program_id(0) * rows + lax.broadcasted_iota(jnp.int32, (rows, 1), 0)
    cos_ref[...], sin_ref[...] = _rope_rows(pos.astype(F32))
    coss_ref[...], sins_ref[...] = _rope_rows(jnp.full((coss_ref.shape[0], 1), sample_pos, F32))


def _norm_proj(x, g, w, *, tm, ncol, outputs, name, casts=(), seq, n_sample, sample_pos):
    m, d = x.shape
    heads = ncol // HEAD_DIM
    n_steps = m // tm
    assert w.shape[0] == d and w.shape[1] % ncol == 0 and m % tm == 0 and seq % (8 * n_steps) == 0
    kern = functools.partial(_norm_proj_kernel, tm=tm, ncol=ncol, outputs=outputs, sample_pos=sample_pos)
    blocks = [_out_block(layout, tm, heads) for _, layout in outputs]
    table = pl.BlockSpec((seq // n_steps, HEAD_DIM), lambda i: (i, 0))
    sample_table = pl.BlockSpec((n_sample, HEAD_DIM), lambda i: (0, 0))
    return _call_with_casts(
        kern,
        grid=(n_steps,),
        in_specs=[
            pl.BlockSpec((tm, d), lambda i: (i, 0)),
            _resident((1, d), lambda i: (0, 0)),
            _resident(w.shape, lambda i: (0, 0)),
        ],
        out_specs=[pl.BlockSpec(b, lambda i: (i, 0)) for b in blocks] + [table, table, sample_table, sample_table],
        out_shape=([jax.ShapeDtypeStruct((n_steps * b[0], b[1]), F32) for b in blocks]
                   + [jax.ShapeDtypeStruct((seq, HEAD_DIM), F32)] * 2
                   + [jax.ShapeDtypeStruct((n_sample, HEAD_DIM), F32)] * 2),
        args=(x, g, w),
        casts=casts,
        name=name,
    )


def _window_sums(ext_ref, lvl_refs, g, w, rows):
    cols = slice(g * POOL_GROUP, (g + 1) * POOL_GROUP)
    levels = w.bit_length() - 1
    src, src_cols = ext_ref, cols
    for level in range(levels):
        shift = 1 << level
        last = level == levels - 1
        lo = HALO if last else 8 * (level + 1)
        assert lo - shift >= 8 * level
        total = src[lo:rows, src_cols] + src[lo - shift:rows - shift, src_cols]
        if last:
            return total
        lvl_refs[level % 2][lo:rows, :] = total
        src, src_cols = lvl_refs[level % 2], slice(None)


def _rope(a, cos, sin):
    lane = lax.broadcasted_iota(jnp.int32, a.shape, 1)
    partner = jnp.where(lane < ROPE_HALF, pltpu.roll(a, HEAD_DIM - ROPE_HALF, 1), pltpu.roll(a, ROPE_HALF, 1))
    return a * cos + partner * sin


def _in_proj_pool_kernel(run_casts, x_ref, xnext_ref, xs_ref, g_ref, w_ref, cos_ref, sin_ref, coss_ref, sins_ref,
                         wp_ref, scale_ref,
                         pool_ref, q_ref, k_ref, v_ref, kh_ref, vh_ref, state_ref, us_ref, qsh_ref, ksh_ref, vsh_ref,
                         xn_ref, ext_ref, lvl_a_ref, lvl_b_ref, *, tm, tiles_per_seq):
    step = pl.program_id(0)
    t_in_seq = step % tiles_per_seq
    n_sample = xs_ref.shape[0]

    def tile(xn, with_sample):
        run_casts()
        u = _dot(xn, w_ref[:, 0:POOL_WIDTH])
        ext_ref[HALO:HALO + tm, :] = u[0:tm]
        if with_sample:
            us_ref[...] = u[tm:]
        cos, sin = cos_ref[...], sin_ref[...]
        outs = ((q_ref, None, qsh_ref), (k_ref, kh_ref, ksh_ref), (v_ref, vh_ref, vsh_ref))
        for c, (major_ref, heads_ref, sample_ref) in enumerate(outs, start=1):
            acc = _dot(xn, w_ref[:, c * ATTN_WIDTH:(c + 1) * ATTN_WIDTH])
            for h in range(N_HEADS):
                a = acc[0:tm, h * HEAD_DIM:(h + 1) * HEAD_DIM]
                if major_ref is not v_ref:
                    a = _rope(a, cos, sin)
                major_ref[h] = a
                if heads_ref is not None:
                    _store_heads(heads_ref, HEADS, h, N_HEADS, tm, a)
                if with_sample:
                    a = acc[tm:, h * HEAD_DIM:(h + 1) * HEAD_DIM]
                    if major_ref is not v_ref:
                        a = _rope(a, coss_ref[...], sins_ref[...])
                    _store_heads(sample_ref, HEADS, h, N_HEADS, n_sample, a)

        pos = t_in_seq * tm + lax.broadcasted_iota(jnp.int32, (tm, 1), 0)
        for g, w in enumerate(POOL_WINDOWS):
            cols = slice(g * POOL_GROUP, (g + 1) * POOL_GROUP)
            wsum = _window_sums(ext_ref, (lvl_a_ref, lvl_b_ref), g, w, HALO + tm)
            cnt = jnp.minimum(w, pos + 1).astype(F32)
            pooled = (wsum / cnt - ext_ref[HALO:HALO + tm, cols]).astype(BF16)
            pool_ref[:, cols] = (_dot(pooled, wp_ref[g].astype(BF16)) * scale_ref[:, cols]).astype(pool_ref.dtype)
        state_ref[...] = ext_ref[HALO + tm - STATE_ROWS:HALO + tm, :]
        ext_ref[0:HALO, :] = ext_ref[tm:tm + HALO, :]

        xn_ref[(step + 1) % 2] = _rmsnorm(xnext_ref[...], g_ref[...]).astype(BF16)

    @pl.when(t_in_seq == 0)
    def _():
        ext_ref[0:HALO, :] = jnp.zeros((HALO, POOL_WIDTH), F32)

    @pl.when(step == 0)
    def _():
        rows = jnp.concatenate([x_ref[...], xs_ref[...]], axis=0)
        tile(_rmsnorm(rows, g_ref[...]).astype(BF16), True)

    @pl.when(step > 0)
    def _():
        tile(xn_ref[step % 2], False)


def _in_proj_pool(x, xs, g, w, cos, sin, cos_s, sin_s, w_pool, pool_scale, *, tm, seq, casts=()):
    m, d = x.shape
    ms = xs.shape[0]
    tiles_per_seq = seq // tm
    n_seq = m // seq
    assert POOL_WIDTH == ATTN_WIDTH and w.shape == (d, POOL_WIDTH + 3 * ATTN_WIDTH)
    kern = functools.partial(_in_proj_pool_kernel, tm=tm, tiles_per_seq=tiles_per_seq)
    row_tile = lambda width: pl.BlockSpec((tm, width), lambda i: (i, 0))
    heads_tile = pl.BlockSpec((tm * N_HEADS, HEAD_DIM), lambda i: (i, 0))
    table = pl.BlockSpec((tm, HEAD_DIM), lambda i: (i % tiles_per_seq, 0))
    whole = lambda shape: _resident(shape, lambda i: (0,) * len(shape))
    head_major = jax.ShapeDtypeStruct((N_HEADS, m, HEAD_DIM), F32)
    head_major_tile = pl.BlockSpec((N_HEADS, tm, HEAD_DIM), lambda i: (0, i, 0))
    by_heads = jax.ShapeDtypeStruct((m * N_HEADS, HEAD_DIM), F32)
    sample_heads = jax.ShapeDtypeStruct((ms * N_HEADS, HEAD_DIM), F32)
    sample_heads_spec = pl.BlockSpec((ms * N_HEADS, HEAD_DIM), lambda i: (0, 0))
    return _call_with_casts(
        kern,
        grid=(m // tm,),
        in_specs=[
            row_tile(d),
            pl.BlockSpec((tm, d), lambda i: (jnp.minimum(i + 1, m // tm - 1), 0)),
            whole((ms, d)),
            whole((1, d)),
            whole(w.shape),
            table,
            table,
            whole((ms, HEAD_DIM)),
            whole((ms, HEAD_DIM)),
            whole(w_pool.shape),
            whole((1, POOL_WIDTH)),
        ],
        out_specs=[row_tile(POOL_WIDTH), head_major_tile, head_major_tile, head_major_tile,
                   heads_tile, heads_tile,
                   pl.BlockSpec((STATE_ROWS, POOL_WIDTH), lambda i: (i // tiles_per_seq, 0)),
                   pl.BlockSpec((ms, POOL_WIDTH), lambda i: (0, 0)),
                   sample_heads_spec, sample_heads_spec, sample_heads_spec],
        out_shape=[jax.ShapeDtypeStruct((m, POOL_WIDTH), BF16), head_major, head_major, head_major, by_heads, by_heads,
                   jax.ShapeDtypeStruct((n_seq * STATE_ROWS, POOL_WIDTH), F32),
                   jax.ShapeDtypeStruct((ms, POOL_WIDTH), F32), sample_heads, sample_heads, sample_heads],
        scratch_shapes=[pltpu.VMEM((2, tm, d), BF16),
                        pltpu.VMEM((HALO + tm, POOL_WIDTH), F32),
                        pltpu.VMEM((HALO + tm, POOL_GROUP), F32),
                        pltpu.VMEM((HALO + tm, POOL_GROUP), F32)],
        args=(x, x, xs, g, w, cos, sin, cos_s, sin_s, w_pool, pool_scale),
        casts=casts,
        name="in_proj_pool",
    )


def _mix_sample_kernel(u_ref, prev_ref, wp_ref, scale_ref, attn_ref, x_ref, wo_ref, gpost_ref, gpre_ref, wq_ref,
                       state_ref, x1_ref, qm_ref, *, pos):
    n = u_ref.shape[0]
    for j in range(POOL_STATE - 1):
        state_ref[:, j, :] = prev_ref[:, j + 1, :]
    state_ref[:, POOL_STATE - 1, :] = u_ref[...]
    y = jnp.zeros(x_ref.shape, F32)
    for g, w in enumerate(POOL_WINDOWS):
        cols = slice(g * POOL_GROUP, (g + 1) * POOL_GROUP)
        wsum = u_ref[:, cols]
        for back in range(1, w):
            wsum = wsum + prev_ref[:, POOL_STATE - back, cols]
        cnt = float(min(w, pos + 1))
        pooled = (wsum / cnt - u_ref[:, cols]).astype(BF16)
        pool_out = (_dot(pooled, wp_ref[g].astype(BF16)) * scale_ref[:, cols]).astype(BF16)
        y = y + _dot(pool_out, wo_ref[cols, :])
    for h in range(N_HEADS):
        rows = slice(POOL_WIDTH + h * HEAD_DIM, POOL_WIDTH + (h + 1) * HEAD_DIM)
        y = y + _dot(attn_ref[:, h, :].astype(BF16), wo_ref[rows, :])
    x1 = x_ref[...] + _rmsnorm(y, gpost_ref[...])
    x1_ref[...] = x1
    qm = _dot(_rmsnorm(x1, gpre_ref[...]).astype(BF16), wq_ref[...])
    for h in range(MEM_HEADS):
        _store_heads(qm_ref, HEADS, h, MEM_HEADS, n, qm[:, h * HEAD_DIM:(h + 1) * HEAD_DIM])


def _mix_sample(u, prev, w_pool, pool_scale, attn, x, w_out, g_post, g_pre, w_xq, *, pos):
    n, d = x.shape
    kern = functools.partial(_mix_sample_kernel, pos=pos)
    return pl.pallas_call(
        kern,
        out_shape=[jax.ShapeDtypeStruct(prev.shape, F32), jax.ShapeDtypeStruct((n, d), F32),
                   jax.ShapeDtypeStruct((n * MEM_HEADS, HEAD_DIM), F32)],
        compiler_params=pltpu.CompilerParams(vmem_limit_bytes=VMEM_LIMIT_BYTES),
        name="mix_sample",
    )(u, prev, w_pool, pool_scale, attn, x, w_out, g_post, g_pre, w_xq)


def _attn_prompt_kernel(run_casts, q_ref, k_ref, v_ref, o_ref, *scratch, seq):
    run_casts()
    n_br = len(DILATED)
    ob_refs, mb_refs, sb_refs = scratch[:n_br], scratch[n_br:2 * n_br], scratch[2 * n_br:3 * n_br]
    p_ref, bias_ref, q4_ref, k4_ref, v4_ref = scratch[3 * n_br:]
    exp2_scale = HEAD_DIM ** -0.5 * LOG2_E
    qi = lax.broadcasted_iota(jnp.int32, (BLOCK, 2 * BLOCK), 0)
    kj = lax.broadcasted_iota(jnp.int32, (BLOCK, 2 * BLOCK), 1)
    in_band = ((kj < BLOCK) & (kj >= qi)) | ((kj >= BLOCK) & (kj - BLOCK <= qi))
    bias_ref[...] = jnp.where(in_band, 0.0, NEG_INF)
    ones = jnp.ones((2 * BLOCK, HEAD_DIM), BF16)

    def rows(block, r, d):
        if d == 1:
            start = block * BLOCK
            return pl.ds(start if isinstance(start, int) else pl.multiple_of(start, BLOCK), BLOCK)
        return pl.ds(block * (BLOCK * d) + r, BLOCK, stride=d)

    def out_rows(block, r, d):
        if d == PITCHED_DILATION:
            start = r * PITCH
            return pl.ds(start if isinstance(start, int) else pl.multiple_of(start, 8), BLOCK)
        return rows(block, r, d)

    quarter = seq // 4
    assert [d for _, d in DILATED] == [1, 4, 16]
    for ref, by4_ref in ((q_ref, q4_ref), (k_ref, k4_ref), (v_ref, v4_ref)):
        for b in range(4):
            by4_ref[b * quarter:(b + 1) * quarter, :] = ref[pl.ds(b, quarter, stride=4), :]

    def in_block(refs, block, r, d):
        ref, by4_ref = refs
        if d == 1:
            return ref[rows(block, r, d), :]
        if d == 4:
            return by4_ref[pl.ds(r * quarter + block * BLOCK, BLOCK), :]
        return by4_ref[pl.ds((r % 4) * quarter + r // 4 + block * (4 * BLOCK), BLOCK, stride=4), :]

    def keys(refs, block, r, d, with_prev):
        cur = in_block(refs, block, r, d)
        if not with_prev:
            return cur.astype(BF16)
        return jnp.concatenate([in_block(refs, block - 1, r, d), cur], axis=0).astype(BF16)

    groups, base = [], 0
    for g, (_, d) in enumerate(DILATED):
        n_blocks = seq // d // BLOCK
        for with_prev in (False, True):
            per_r = n_blocks - 1 if with_prev else 1
            count = d * per_r
            if count == 0:
                continue
            cols = slice(0, 2 * BLOCK) if with_prev else slice(BLOCK, 2 * BLOCK)
            width = 2 * BLOCK if with_prev else BLOCK

            def locate(idx, per_r=per_r, with_prev=with_prev):
                return (idx % per_r + 1, idx // per_r) if with_prev else (0, idx)

            def probabilities(idx, g=g, d=d, with_prev=with_prev, cols=cols, width=width, locate=locate, base=base):
                b, r = locate(idx)
                q = in_block((q_ref, q4_ref), b, r, d).astype(BF16)
                s = _dot_nt(q, keys((k_ref, k4_ref), b, r, d, with_prev)) + bias_ref[:, cols]
                m = jnp.max(s, axis=-1, keepdims=True)
                p_ref[base + idx, :, 0:width] = jnp.exp2((s - m) * exp2_scale).astype(BF16)
                mb_refs[g][out_rows(b, r, d), :] = jnp.broadcast_to(m, (BLOCK, HEAD_DIM))

            def values(idx, g=g, d=d, with_prev=with_prev, width=width, locate=locate, base=base):
                b, r = locate(idx)
                v1 = jnp.concatenate([keys((v_ref, v4_ref), b, r, d, with_prev), ones[0:width]], axis=1)
                acc = _dot(p_ref[base + idx, :, 0:width], v1)
                ob_refs[g][out_rows(b, r, d), :] = acc[:, 0:HEAD_DIM]
                sb_refs[g][out_rows(b, r, d), :] = acc[:, HEAD_DIM:]

            groups.append((count, probabilities, values))
            base += count

    for count, probabilities, _ in groups:
        for idx in range(count):
            probabilities(idx)
    for count, _, values in groups:
        for idx in range(count):
            values(idx)

    chunk = 1024

    def combine(c, carry):
        rw = pl.ds(pl.multiple_of(c * chunk, chunk), chunk)

        def read(ref, g):
            if DILATED[g][1] != PITCHED_DILATION:
                return ref[rw, :]
            pieces = [ref[pl.ds((j % 2) * 8 * PITCH + c * (chunk // PITCHED_DILATION) + j // 2, 8, stride=PITCH), :]
                      for j in range(chunk // 8)]
            return jnp.concatenate(pieces, axis=0)

        ms = [read(mb, g) for g, mb in enumerate(mb_refs)]
        m = functools.reduce(jnp.maximum, ms)
        ws = [jnp.exp2((mg - m) * exp2_scale) for mg in ms]
        num = functools.reduce(jnp.add, [w * read(ob, g) for g, (w, ob) in enumerate(zip(ws, ob_refs))])
        den = functools.reduce(jnp.add, [w * read(sb, g) for g, (w, sb) in enumerate(zip(ws, sb_refs))])
        o_ref[rw, :] = (num / den).astype(o_ref.dtype)
        return carry

    lax.fori_loop(0, seq // chunk, combine, 0)


def _attn_prompt(q, k, v, *, seq, casts=()):
    n_heads, m, _ = q.shape
    n_seq = m // seq
    spec = pl.BlockSpec((None, seq, HEAD_DIM), lambda n, h: (h, n, 0))
    kern = functools.partial(_attn_prompt_kernel, seq=seq)
    (attn,), cast_out = _call_with_casts(
        kern,
        grid=(n_seq, n_heads),
        in_specs=[spec, spec, spec],
        out_specs=[spec],
        out_shape=[jax.ShapeDtypeStruct(q.shape, BF16)],
        scratch_shapes=(
            [pltpu.VMEM((PITCHED_DILATION * PITCH if d == PITCHED_DILATION else seq, HEAD_DIM), F32)
             for _ in range(3) for _, d in DILATED]
            + [pltpu.VMEM((len(DILATED) * seq // BLOCK, BLOCK, 2 * BLOCK), BF16),
               pltpu.VMEM((BLOCK, 2 * BLOCK), F32)]
            + [pltpu.VMEM((seq, HEAD_DIM), F32) for _ in range(3)]),
        args=(q, k, v),
        casts=casts,
        step_of=lambda n, h: n * n_heads + h,
        name="attn_prompt",
    )
    return attn, cast_out


def _attn_sample_kernel(q_ref, kn_ref, vn_ref, *refs, group):
    n_br = len(DILATED)
    kc_refs, vc_refs, o_ref = refs[:n_br], refs[n_br:2 * n_br], refs[2 * n_br]
    for j in range(group):
        q = q_ref[j] * (HEAD_DIM ** -0.5 * LOG2_E)
        t_new = jnp.sum(q * kn_ref[j], axis=-1, keepdims=True)
        t_br = [jnp.sum(kc[j] * q[None], axis=-1, keepdims=True) for kc in kc_refs]
        m = t_new
        for t in t_br:
            m = jnp.maximum(m, jnp.max(t, axis=0))
        p_new = jnp.exp2(t_new - m) * float(n_br)
        den = p_new
        num = p_new * vn_ref[j]
        for t, vc in zip(t_br, vc_refs):
            p = jnp.exp2(t - m[None])
            den = den + jnp.sum(p, axis=0)
            num = num + jnp.sum(p * vc[j], axis=0)
        out = num / den
        o_ref[j] = out
    return out


def _attn_sample_operands(q, k_new, v_new, k_cache, v_cache, *, group):
    n, heads, _ = q.shape
    win = k_cache.shape[1]
    row = pl.BlockSpec((group, heads, HEAD_DIM), lambda i: (i, 0, 0))
    cache_specs, k_views, v_views = [], [], []
    for w, d in DILATED:
        assert w == BLOCK * d and win % w == 0
        last = win // w - 1
        cache_specs.append(pl.BlockSpec((group, BLOCK, None, heads, HEAD_DIM),
                                        lambda i, last=last: (i, last, 0, 0, 0)))
        k_views.append(k_cache.reshape(n, win // d, d, heads, HEAD_DIM))
        v_views.append(v_cache.reshape(n, win // d, d, heads, HEAD_DIM))
    return ([row, row, row] + cache_specs + cache_specs, row, jax.ShapeDtypeStruct((n, heads, HEAD_DIM), F32),
            (q, k_new, v_new, *k_views, *v_views))


def _mix_mem_prompt_kernel(run_casts, pool_ref, attn_ref, x_ref, mk_ref, mv_ref, wo_ref,
                           gmix_ref, gpre_ref, wq_ref, wxo_ref, gmem_ref, *refs, sample_group):
    n_sample_in = 3 + 2 * len(DILATED)
    sample_in, (x2_ref, attn_s_ref) = refs[:n_sample_in], refs[n_sample_in:]
    run_casts()
    scale = HEAD_DIM ** -0.5
    mixed = jnp.concatenate([pool_ref[...]] + [attn_ref[h] for h in range(N_HEADS)], axis=-1)
    y = _dot(mixed, wo_ref[...])
    last = _attn_sample_kernel(*sample_in, attn_s_ref, group=sample_group)
    zero = jnp.minimum(jnp.abs(last[0:1, :]), 0.0)
    gmix = gmix_ref[...] + jnp.concatenate([zero] * (gmix_ref.shape[1] // HEAD_DIM), axis=1)
    x1 = x_ref[...] + _rmsnorm(y, gmix)
    qm = _dot(_rmsnorm(x1, gpre_ref[...]).astype(BF16), wq_ref[...]).astype(BF16)
    heads = []
    for h in range(MEM_HEADS):
        cols = slice(h * HEAD_DIM, (h + 1) * HEAD_DIM)
        s = _dot_nt(qm[:, cols], mk_ref[:, cols].astype(BF16)) * scale
        p = jnp.exp(s - jnp.max(s, axis=-1, keepdims=True))
        o = _dot(p.astype(BF16), mv_ref[:, cols].astype(BF16))
        heads.append((o / jnp.sum(p, axis=-1, keepdims=True)).astype(BF16))
    y = _dot(jnp.concatenate(heads, axis=-1), wxo_ref[...])
    x2_ref[...] = x1 + _rmsnorm(y, gmem_ref[...])


def _mix_mem_prompt(pool_out, attn, x, mk, mv, w_out, g_mix_post, g_mem_pre, w_xq, w_xo, g_mem_post,
                    sample_attn, *, tm, seq, casts=()):
    m, d = x.shape
    n_tiles = m // tm
    tiles_per_seq = seq // tm
    half = pool_out.shape[1]
    mem_spec = pl.BlockSpec((N_MEM, MEM_WIDTH), lambda i: (i // tiles_per_seq, 0))
    gain_spec = _resident((1, d), lambda i: (0, 0))
    n_sample = sample_attn[0].shape[0]
    assert n_sample % n_tiles == 0
    sample_group = n_sample // n_tiles
    s_in_specs, s_out_spec, s_out_shape, s_args = _attn_sample_operands(*sample_attn, group=sample_group)
    (x2, attn_s), cast_out = _call_with_casts(
        functools.partial(_mix_mem_prompt_kernel, sample_group=sample_group),
        grid=(n_tiles,),
        in_specs=[
            pl.BlockSpec((tm, half), lambda i: (i, 0)),
            pl.BlockSpec((N_HEADS, tm, HEAD_DIM), lambda i: (0, i, 0)),
            pl.BlockSpec((tm, d), lambda i: (i, 0)),
            mem_spec,
            mem_spec,
            _resident(w_out.shape, lambda i: (0, 0)),
            gain_spec,
            gain_spec,
            _resident(w_xq.shape, lambda i: (0, 0)),
            _resident(w_xo.shape, lambda i: (0, 0)),
            gain_spec,
        ] + s_in_specs,
        out_specs=[pl.BlockSpec((tm, d), lambda i: (i, 0)), s_out_spec],
        out_shape=[jax.ShapeDtypeStruct((m, d), F32), s_out_shape],
        args=(pool_out, attn, x, mk, mv, w_out, g_mix_post, g_mem_pre, w_xq, w_xo, g_mem_post) + tuple(s_args),
        casts=casts,
        name="mix_mem_prompt",
    )
    return x2, attn_s, cast_out


def _mem_attn_sample_kernel(qm_ref, mk_ref, mv_ref, o_ref, *, group):
    heads = qm_ref.shape[1]
    both = lambda a: jnp.concatenate([a, a], axis=0)
    fold = lambda a: a[0:heads] + a[heads:]
    for j in range(group):
        q = both(qm_ref[j] * (HEAD_DIM ** -0.5 * LOG2_E))
        t = jnp.sum(mk_ref[j] * q[None], axis=-1, keepdims=True)
        m = jnp.max(t, axis=0)
        m = both(jnp.maximum(m[0:heads], m[heads:]))
        p = jnp.exp2(t - m[None])
        o_ref[j] = fold(jnp.sum(p * mv_ref[j], axis=0)) / fold(jnp.sum(p, axis=0))


def _mem_attn_sample(qm, mem_k, mem_v):
    n, heads, _ = qm.shape
    group = SAMPLE_GROUP
    row = pl.BlockSpec((group, heads, HEAD_DIM), lambda i: (i, 0, 0))
    mem = pl.BlockSpec((group, N_MEM // 2, 2 * heads, HEAD_DIM), lambda i: (i, 0, 0, 0))
    pairs = lambda a: a.reshape(n, N_MEM // 2, 2 * heads, HEAD_DIM)
    return pl.pallas_call(
        functools.partial(_mem_attn_sample_kernel, group=group),
        grid=(n // group,),
        in_specs=[row, mem, mem],
        out_specs=row,
        out_shape=jax.ShapeDtypeStruct((n, heads, HEAD_DIM), F32),
        compiler_params=_params(1),
        name="mem_attn_sample",
    )(qm, pairs(mem_k), pairs(mem_v))


def _mem_out_sample_kernel(a_ref, w_ref, g_ref, r_ref, o_ref):
    y = jnp.zeros(r_ref.shape, F32)
    for h in range(a_ref.shape[1]):
        y = y + _dot(a_ref[:, h, :].astype(BF16), w_ref[h * HEAD_DIM:(h + 1) * HEAD_DIM, :])
    o_ref[...] = r_ref[...] + _rmsnorm(y, g_ref[...])


def _mem_out_sample(a, w, g, resid):
    return pl.pallas_call(
        _mem_out_sample_kernel,
        out_shape=jax.ShapeDtypeStruct(resid.shape, F32),
        compiler_params=pltpu.CompilerParams(vmem_limit_bytes=VMEM_LIMIT_BYTES),
        name="mem_out_sample",
    )(a, w, g, resid)


def _ffn_kernel(x_ref, xnext_ref, xs_ref, gpre_ref, w1_ref, w2_ref, gpost_ref, o_ref, os_ref,
                xn_ref, xns_ref, acc_ref, accs_ref, *, tm, nf):
    i, f = pl.program_id(0), pl.program_id(1)
    last_f = nf - 1
    chunk = tm // nf

    @pl.when((i == 0) & (f == 0))
    def _():
        xn_ref[0] = _rmsnorm(x_ref[...], gpre_ref[...]).astype(BF16)
        xns_ref[...] = _rmsnorm(xs_ref[...], gpre_ref[...]).astype(BF16)
        accs_ref[...] = jnp.zeros_like(accs_ref)

    @pl.when(f == 0)
    def _():
        acc_ref[...] = jnp.zeros_like(acc_ref)

    def mlp(lhs):
        hidden = jnp.square(jnp.maximum(_dot(lhs, w1_ref[...]), 0.0)).astype(BF16)
        return _dot(hidden, w2_ref[...])

    def norm_next_rows():
        rows = pl.ds(pl.multiple_of(f * chunk, chunk), chunk)
        xn_ref[(i + 1) % 2, rows, :] = _rmsnorm(xnext_ref[rows, :], gpre_ref[...]).astype(BF16)

    @pl.when(i == 0)
    def _():
        both = mlp(jnp.concatenate([xn_ref[0], xns_ref[...]], axis=0))
        acc_ref[...] += both[0:tm]
        accs_ref[...] += both[tm:]
        norm_next_rows()

    @pl.when(i > 0)
    def _():
        acc_ref[...] += mlp(xn_ref[i % 2])
        norm_next_rows()

    @pl.when(f == last_f)
    def _():
        o_ref[...] = x_ref[...] + _rmsnorm(acc_ref[...], gpost_ref[...])

    @pl.when((f == last_f) & (i == 0))
    def _():
        os_ref[...] = xs_ref[...] + _rmsnorm(accs_ref[...], gpost_ref[...])


def _ffn(x, xs, g_pre, w1_chunks, w2, g_post, *, tm):
    m, d = x.shape
    ms = xs.shape[0]
    nf, _, tf = w1_chunks.shape
    n_tiles = m // tm
    assert tm % nf == 0 and (tm // nf) % 16 == 0
    sample_spec = _resident((ms, d), lambda i, f: (0, 0))
    return pl.pallas_call(
        functools.partial(_ffn_kernel, tm=tm, nf=nf),
        grid=(n_tiles, nf),
        in_specs=[
            pl.BlockSpec((tm, d), lambda i, f: (i, 0)),
            pl.BlockSpec((tm, d), lambda i, f: (jnp.minimum(i + 1, n_tiles - 1), 0)),
            sample_spec,
            _resident((1, d), lambda i, f: (0, 0)),
            pl.BlockSpec((None, d, tf), lambda i, f: (f, 0, 0)),
            pl.BlockSpec((tf, d), lambda i, f: (f, 0)),
            _resident((1, d), lambda i, f: (0, 0)),
        ],
        out_specs=[pl.BlockSpec((tm, d), lambda i, f: (i, 0)),
                   pl.BlockSpec((ms, d), lambda i, f: (0, 0))],
        out_shape=[jax.ShapeDtypeStruct((m, d), F32), jax.ShapeDtypeStruct((ms, d), F32)],
        scratch_shapes=[pltpu.VMEM((2, tm, d), BF16), pltpu.VMEM((ms, d), BF16),
                        pltpu.VMEM((tm, d), F32), pltpu.VMEM((ms, d), F32)],
        compiler_params=_params(2),
        name="ffn",
    )(x, x, xs, g_pre, w1_chunks, w2, g_post)


def kernel(x_prompt, x_sample, state_pool, cache_attn_k, cache_attn_v, cache_mem_k, cache_mem_v, mem_prompt,
           g_mix_pre, g_mix_post, g_mem_pre, g_mem_post, g_ffn_pre, g_ffn_post, g_mem_kv,
           w_in, w_pool, pool_scale, w_out, w_xq, w_mem_kv, w_xo, w_ff1, w_ff2):
    depth = w_in.shape[0]
    assert depth == 1
    batch, seq, d = x_prompt.shape
    dec_batch, dec_seq, _ = x_sample.shape
    assert dec_seq == 1

    l = 0
    scale = pool_scale[l][None, :]
    gain = lambda g: g[l][None, :]

    xp = x_prompt.reshape(batch * seq, d)
    xs = x_sample.reshape(dec_batch, d)
    w_pool_f = w_pool[l]
    tm = ROW_TILE


    mem = mem_prompt.reshape(batch * N_MEM, d)
    (mk, mv, mk_h, mv_h, cos_p, sin_p, cos_s, sin_s), (w_in_b,) = _norm_proj(
        mem, gain(g_mem_kv), w_mem_kv[l], tm=MEM_KV_TILE, ncol=MEM_WIDTH,
        outputs=((0, FLAT), (1, FLAT), (0, HEADS), (1, HEADS)), name="mem_kv", casts=(w_in[l],),
        seq=seq, n_sample=dec_batch, sample_pos=PAST_LEN)

    ((pool_out, q, k, v, k_h, v_h, pool_state, us, qs_h, ks_h, vs_h),
     (w_out_b, w_xq_b, w_xo_b)) = _in_proj_pool(
        xp, xs, gain(g_mix_pre), w_in_b, cos_p, sin_p, cos_s, sin_s, w_pool_f, scale, tm=IN_PROJ_TILE, seq=seq,
        casts=(w_out[l], w_xq[l], w_xo[l]))
    attn, (w_ff1_b, w_ff2_b) = _attn_prompt(
        q, k, v, seq=seq, casts=((w_ff1[l], w_ff1.shape[2] // FFN_CHUNK), w_ff2[l]))
    heads3 = lambda a: a.reshape(dec_batch, -1, HEAD_DIM)
    x2, attn_s, _ = _mix_mem_prompt(
        pool_out, attn, xp, mk, mv, w_out_b, gain(g_mix_post), gain(g_mem_pre), w_xq_b, w_xo_b, gain(g_mem_post),
        (heads3(qs_h), heads3(ks_h), heads3(vs_h), cache_attn_k[l], cache_attn_v[l]), tm=tm, seq=seq)

    new_pool_s, x1s, qms_h = _mix_sample(us, state_pool[l], w_pool_f, scale, attn_s, xs, w_out_b,
                                         gain(g_mix_post), gain(g_mem_pre), w_xq_b, pos=PAST_LEN)
    mem_o_s = _mem_attn_sample(heads3(qms_h), cache_mem_k[l], cache_mem_v[l])
    x2s = _mem_out_sample(mem_o_s, w_xo_b, gain(g_mem_post), x1s)

    yp, ys = _ffn(x2, x2s, gain(g_ffn_pre), w_ff1_b, w_ff2_b, gain(g_ffn_post), tm=tm)

    keep = min(max(w for w, _ in DILATED), seq)
    return (
        yp.reshape(batch, seq, d),
        ys.reshape(dec_batch, 1, d),
        pool_state.reshape(batch, STATE_ROWS, POOL_WIDTH)[:, STATE_ROWS - POOL_STATE:][None],
        new_pool_s[None],
        k_h.reshape(batch, seq, N_HEADS, HEAD_DIM)[:, seq - keep:][None],
        v_h.reshape(batch, seq, N_HEADS, HEAD_DIM)[:, seq - keep:][None],
        ks_h.reshape(dec_batch, 1, N_HEADS, HEAD_DIM)[None],
        vs_h.reshape(dec_batch, 1, N_HEADS, HEAD_DIM)[None],
        mk_h.reshape(batch, N_MEM, MEM_HEADS, HEAD_DIM)[None],
        mv_h.reshape(batch, N_MEM, MEM_HEADS, HEAD_DIM)[None],
    )
```

```python
import functools

import jax
import jax.numpy as jnp
from jax import lax
from jax.experimental import pallas as pl
from jax.experimental.pallas import tpu as pltpu

D_MODEL = 2048
POOL_WIDTH = 1024
POOL_WINDOWS = (2, 4, 8, 16)
POOL_GROUP = POOL_WIDTH // len(POOL_WINDOWS)
POOL_STATE = max(POOL_WINDOWS) - 1
HEAD_DIM = 128
ATTN_WIDTH = 1024
N_HEADS = ATTN_WIDTH // HEAD_DIM
DILATED = ((128, 1), (512, 4), (2048, 16))
PITCHED_DILATION = 16
PITCH = 128 + 8
ROPE_DIM = HEAD_DIM // 4
ROPE_HALF = ROPE_DIM // 2
ROPE_THETA = 500000.0
N_MEM = 256
MEM_HEADS = 4
MEM_WIDTH = MEM_HEADS * HEAD_DIM
EPS = 1e-6
BLOCK = 128
NEG_INF = -1e30
LOG2_E = 1.4426950408889634
PAST_LEN = 8192
HALO = 32
STATE_ROWS = 16

F32 = jnp.float32
BF16 = jnp.bfloat16

VMEM_LIMIT_BYTES = 60 * 1024 * 1024

ROW_TILE = 512
IN_PROJ_TILE = 256
MEM_KV_TILE = 256
FFN_CHUNK = 2048
SAMPLE_GROUP = 8

FLAT, HEADS = "flat", "heads"


def _params(n_grid_axes):
    return pltpu.CompilerParams(
        dimension_semantics=("arbitrary",) * n_grid_axes,
        vmem_limit_bytes=VMEM_LIMIT_BYTES,
    )


def _resident(block_shape, index_map):
    return pl.BlockSpec(block_shape, index_map, pipeline_mode=pl.Buffered(1))


def _rmsnorm(x, g):
    ms = jnp.mean(x * x, axis=-1, keepdims=True)
    return x * lax.rsqrt(ms + EPS) * g


def _dot(a, b):
    return jnp.dot(a, b, preferred_element_type=F32)


def _dot_nt(a, b):
    return lax.dot_general(a, b, (((1,), (1,)), ((), ())), preferred_element_type=F32)


def _call_with_casts(kernel_fn, *, grid, in_specs, out_specs, out_shape, scratch_shapes=(), args, casts=(),
                     step_of=None, name):
    casts = [c if isinstance(c, tuple) else (c, 1) for c in casts]
    n_in, n_out, n_casts = len(in_specs), len(out_specs), len(casts)
    n_steps = 1
    for g in grid:
        n_steps *= g
    if step_of is None:
        step_of = lambda i: i
    cast_in_specs, cast_out_specs, cast_shapes = [], [], []
    for a, n_chunks in casts:
        rows, cols = a.shape
        assert rows % n_steps == 0 and cols % n_chunks == 0
        cast_in_specs.append(pl.BlockSpec((rows // n_steps, cols), lambda *idx: (step_of(*idx), 0)))
        if n_chunks == 1:
            cast_out_specs.append(cast_in_specs[-1])
            cast_shapes.append(jax.ShapeDtypeStruct(a.shape, BF16))
        else:
            cast_out_specs.append(pl.BlockSpec((n_chunks, rows // n_steps, cols // n_chunks),
                                               lambda *idx: (0, step_of(*idx), 0)))
            cast_shapes.append(jax.ShapeDtypeStruct((n_chunks, rows, cols // n_chunks), BF16))

    def body(*refs):
        ins, cast_in = refs[:n_in], refs[n_in:n_in + n_casts]
        outs = refs[n_in + n_casts:n_in + n_casts + n_out]
        cast_out = refs[n_in + n_casts + n_out:n_in + 2 * n_casts + n_out]

        def run_casts():
            for i_ref, o_ref, (_, n_chunks) in zip(cast_in, cast_out, casts):
                if n_chunks == 1:
                    o_ref[...] = i_ref[...].astype(BF16)
                else:
                    width = i_ref.shape[1] // n_chunks
                    for c in range(n_chunks):
                        o_ref[c] = i_ref[:, c * width:(c + 1) * width].astype(BF16)

        kernel_fn(run_casts, *ins, *outs, *refs[n_in + 2 * n_casts + n_out:])

    res = pl.pallas_call(
        body,
        grid=grid,
        in_specs=list(in_specs) + cast_in_specs,
        out_specs=list(out_specs) + cast_out_specs,
        out_shape=list(out_shape) + cast_shapes,
        scratch_shapes=list(scratch_shapes),
        compiler_params=_params(len(grid)),
        name=name,
    )(*args, *[a for a, _ in casts])
    return res[:n_out], res[n_out:]


def _store_heads(o_ref, layout, h, heads, rows, value):
    if layout == FLAT:
        o_ref[:, h * HEAD_DIM:(h + 1) * HEAD_DIM] = value
    else:
        o_ref[pl.ds(h, rows, stride=heads), :] = value


def _out_block(layout, tm, heads):
    return (tm, heads * HEAD_DIM) if layout == FLAT else (tm * heads, HEAD_DIM)


def _rope_rows(pos):
    lane = lax.broadcasted_iota(jnp.int32, (1, HEAD_DIM), 1)
    k = (lane % ROPE_HALF).astype(F32)
    inv = jnp.where(lane < ROPE_DIM, jnp.power(jnp.float32(ROPE_THETA), -k * 2.0 / ROPE_DIM), 0.0)
    ang = pos * inv
    return jnp.cos(ang), jnp.sin(ang) * jnp.where(lane < ROPE_HALF, -1.0, 1.0)


def _norm_proj_kernel(run_casts, x_ref, g_ref, w_ref, *out_refs, tm, ncol, outputs, sample_pos):
    run_casts()
    *out_refs, cos_ref, sin_ref, coss_ref, sins_ref = out_refs
    heads = ncol // HEAD_DIM
    xn = _rmsnorm(x_ref[...], g_ref[...]).astype(BF16)
    for c in sorted({c for c, _ in outputs}):
        acc = _dot(xn, w_ref[:, c * ncol:(c + 1) * ncol].astype(BF16))
        for h in range(heads):
            for (oc, layout), o_ref in zip(outputs, out_refs):
                if oc == c:
                    _store_heads(o_ref, layout, h, heads, tm, acc[:, h * HEAD_DIM:(h + 1) * HEAD_DIM])
    rows = cos_ref.shape[0]
    pos = pl.program_id(0) * rows + lax.broadcasted_iota(jnp.int32, (rows, 1), 0)
    cos_ref[...], sin_ref[...] = _rope_rows(pos.astype(F32))
    coss_ref[...], sins_ref[...] = _rope_rows(jnp.full((coss_ref.shape[0], 1), sample_pos, F32))


def _norm_proj(x, g, w, *, tm, ncol, outputs, name, casts=(), seq, n_sample, sample_pos):
    m, d = x.shape
    heads = ncol // HEAD_DIM
    n_steps = m // tm
    assert w.shape[0] == d and w.shape[1] % ncol == 0 and m % tm == 0 and seq % (8 * n_steps) == 0
    kern = functools.partial(_norm_proj_kernel, tm=tm, ncol=ncol, outputs=outputs, sample_pos=sample_pos)
    blocks = [_out_block(layout, tm, heads) for _, layout in outputs]
    table = pl.BlockSpec((seq // n_steps, HEAD_DIM), lambda i: (i, 0))
    sample_table = pl.BlockSpec((n_sample, HEAD_DIM), lambda i: (0, 0))
    return _call_with_casts(
        kern,
        grid=(n_steps,),
        in_specs=[
            pl.BlockSpec((tm, d), lambda i: (i, 0)),
            _resident((1, d), lambda i: (0, 0)),
            _resident(w.shape, lambda i: (0, 0)),
        ],
        out_specs=[pl.BlockSpec(b, lambda i: (i, 0)) for b in blocks] + [table, table, sample_table, sample_table],
        out_shape=([jax.ShapeDtypeStruct((n_steps * b[0], b[1]), F32) for b in blocks]
                   + [jax.ShapeDtypeStruct((seq, HEAD_DIM), F32)] * 2
                   + [jax.ShapeDtypeStruct((n_sample, HEAD_DIM), F32)] * 2),
        args=(x, g, w),
        casts=casts,
        name=name,
    )


def _window_sums(ext_ref, lvl_refs, g, w, rows):
    cols = slice(g * POOL_GROUP, (g + 1) * POOL_GROUP)
    levels = w.bit_length() - 1
    src, src_cols = ext_ref, cols
    for level in range(levels):
        shift = 1 << level
        last = level == levels - 1
        lo = HALO if last else 8 * (level + 1)
        assert lo - shift >= 8 * level
        total = src[lo:rows, src_cols] + src[lo - shift:rows - shift, src_cols]
        if last:
            return total
        lvl_refs[level % 2][lo:rows, :] = total
        src, src_cols = lvl_refs[level % 2], slice(None)


def _rope(a, cos, sin):
    lane = lax.broadcasted_iota(jnp.int32, a.shape, 1)
    partner = jnp.where(lane < ROPE_HALF, pltpu.roll(a, HEAD_DIM - ROPE_HALF, 1), pltpu.roll(a, ROPE_HALF, 1))
    return a * cos + partner * sin


def _in_proj_pool_kernel(run_casts, x_ref, xnext_ref, xs_ref, g_ref, w_ref, cos_ref, sin_ref, coss_ref, sins_ref,
                         wp_ref, scale_ref,
                         pool_ref, q_ref, k_ref, v_ref, kh_ref, vh_ref, state_ref, us_ref, qsh_ref, ksh_ref, vsh_ref,
                         xn_ref, ext_ref, lvl_a_ref, lvl_b_ref, *, tm, tiles_per_seq):
    step = pl.program_id(0)
    t_in_seq = step % tiles_per_seq
    n_sample = xs_ref.shape[0]

    def tile(xn, with_sample):
        run_casts()
        u = _dot(xn, w_ref[:, 0:POOL_WIDTH])
        ext_ref[HALO:HALO + tm, :] = u[0:tm]
        if with_sample:
            us_ref[...] = u[tm:]
        cos, sin = cos_ref[...], sin_ref[...]
        outs = ((q_ref, None, qsh_ref), (k_ref, kh_ref, ksh_ref), (v_ref, vh_ref, vsh_ref))
        for c, (major_ref, heads_ref, sample_ref) in enumerate(outs, start=1):
            acc = _dot(xn, w_ref[:, c * ATTN_WIDTH:(c + 1) * ATTN_WIDTH])
            for h in range(N_HEADS):
                a = acc[0:tm, h * HEAD_DIM:(h + 1) * HEAD_DIM]
                if major_ref is not v_ref:
                    a = _rope(a, cos, sin)
                major_ref[h] = a
                if heads_ref is not None:
                    _store_heads(heads_ref, HEADS, h, N_HEADS, tm, a)
                if with_sample:
                    a = acc[tm:, h * HEAD_DIM:(h + 1) * HEAD_DIM]
                    if major_ref is not v_ref:
                        a = _rope(a, coss_ref[...], sins_ref[...])
                    _store_heads(sample_ref, HEADS, h, N_HEADS, n_sample, a)

        pos = t_in_seq * tm + lax.broadcasted_iota(jnp.int32, (tm, 1), 0)
        for g, w in enumerate(POOL_WINDOWS):
            cols = slice(g * POOL_GROUP, (g + 1) * POOL_GROUP)
            wsum = _window_sums(ext_ref, (lvl_a_ref, lvl_b_ref), g, w, HALO + tm)
            cnt = jnp.minimum(w, pos + 1).astype(F32)
            pooled = (wsum / cnt - ext_ref[HALO:HALO + tm, cols]).astype(BF16)
            pool_ref[:, cols] = (_dot(pooled, wp_ref[g].astype(BF16)) * scale_ref[:, cols]).astype(pool_ref.dtype)
        state_ref[...] = ext_ref[HALO + tm - STATE_ROWS:HALO + tm, :]
        ext_ref[0:HALO, :] = ext_ref[tm:tm + HALO, :]

        xn_ref[(step + 1) % 2] = _rmsnorm(xnext_ref[...], g_ref[...]).astype(BF16)

    @pl.when(t_in_seq == 0)
    def _():
        ext_ref[0:HALO, :] = jnp.zeros((HALO, POOL_WIDTH), F32)

    @pl.when(step == 0)
    def _():
        rows = jnp.concatenate([x_ref[...], xs_ref[...]], axis=0)
        tile(_rmsnorm(rows, g_ref[...]).astype(BF16), True)

    @pl.when(step > 0)
    def _():
        tile(xn_ref[step % 2], False)


def _in_proj_pool(x, xs, g, w, cos, sin, cos_s, sin_s, w_pool, pool_scale, *, tm, seq, casts=()):
    m, d = x.shape
    ms = xs.shape[0]
    tiles_per_seq = seq // tm
    n_seq = m // seq
    assert POOL_WIDTH == ATTN_WIDTH and w.shape == (d, POOL_WIDTH + 3 * ATTN_WIDTH)
    kern = functools.partial(_in_proj_pool_kernel, tm=tm, tiles_per_seq=tiles_per_seq)
    row_tile = lambda width: pl.BlockSpec((tm, width), lambda i: (i, 0))
    heads_tile = pl.BlockSpec((tm * N_HEADS, HEAD_DIM), lambda i: (i, 0))
    table = pl.BlockSpec((tm, HEAD_DIM), lambda i: (i % tiles_per_seq, 0))
    whole = lambda shape: _resident(shape, lambda i: (0,) * len(shape))
    head_major = jax.ShapeDtypeStruct((N_HEADS, m, HEAD_DIM), F32)
    head_major_tile = pl.BlockSpec((N_HEADS, tm, HEAD_DIM), lambda i: (0, i, 0))
    by_heads = jax.ShapeDtypeStruct((m * N_HEADS, HEAD_DIM), F32)
    sample_heads = jax.ShapeDtypeStruct((ms * N_HEADS, HEAD_DIM), F32)
    sample_heads_spec = pl.BlockSpec((ms * N_HEADS, HEAD_DIM), lambda i: (0, 0))
    return _call_with_casts(
        kern,
        grid=(m // tm,),
        in_specs=[
            row_tile(d),
            pl.BlockSpec((tm, d), lambda i: (jnp.minimum(i + 1, m // tm - 1), 0)),
            whole((ms, d)),
            whole((1, d)),
            whole(w.shape),
            table,
            table,
            whole((ms, HEAD_DIM)),
            whole((ms, HEAD_DIM)),
            whole(w_pool.shape),
            whole((1, POOL_WIDTH)),
        ],
        out_specs=[row_tile(POOL_WIDTH), head_major_tile, head_major_tile, head_major_tile,
                   heads_tile, heads_tile,
                   pl.BlockSpec((STATE_ROWS, POOL_WIDTH), lambda i: (i // tiles_per_seq, 0)),
                   pl.BlockSpec((ms, POOL_WIDTH), lambda i: (0, 0)),
                   sample_heads_spec, sample_heads_spec, sample_heads_spec],
        out_shape=[jax.ShapeDtypeStruct((m, POOL_WIDTH), BF16), head_major, head_major, head_major, by_heads, by_heads,
                   jax.ShapeDtypeStruct((n_seq * STATE_ROWS, POOL_WIDTH), F32),
                   jax.ShapeDtypeStruct((ms, POOL_WIDTH), F32), sample_heads, sample_heads, sample_heads],
        scratch_shapes=[pltpu.VMEM((2, tm, d), BF16),
                        pltpu.VMEM((HALO + tm, POOL_WIDTH), F32),
                        pltpu.VMEM((HALO + tm, POOL_GROUP), F32),
                        pltpu.VMEM((HALO + tm, POOL_GROUP), F32)],
        args=(x, x, xs, g, w, cos, sin, cos_s, sin_s, w_pool, pool_scale),
        casts=casts,
        name="in_proj_pool",
    )


def _mix_sample_kernel(u_ref, prev_ref, wp_ref, scale_ref, attn_ref, x_ref, wo_ref, gpost_ref, gpre_ref, wq_ref,
                       state_ref, x1_ref, qm_ref, *, pos):
    n = u_ref.shape[0]
    for j in range(POOL_STATE - 1):
        state_ref[:, j, :] = prev_ref[:, j + 1, :]
    state_ref[:, POOL_STATE - 1, :] = u_ref[...]
    y = jnp.zeros(x_ref.shape, F32)
    for g, w in enumerate(POOL_WINDOWS):
        cols = slice(g * POOL_GROUP, (g + 1) * POOL_GROUP)
        wsum = u_ref[:, cols]
        for back in range(1, w):
            wsum = wsum + prev_ref[:, POOL_STATE - back, cols]
        cnt = float(min(w, pos + 1))
        pooled = (wsum / cnt - u_ref[:, cols]).astype(BF16)
        pool_out = (_dot(pooled, wp_ref[g].astype(BF16)) * scale_ref[:, cols]).astype(BF16)
        y = y + _dot(pool_out, wo_ref[cols, :])
    for h in range(N_HEADS):
        rows = slice(POOL_WIDTH + h * HEAD_DIM, POOL_WIDTH + (h + 1) * HEAD_DIM)
        y = y + _dot(attn_ref[:, h, :].astype(BF16), wo_ref[rows, :])
    x1 = x_ref[...] + _rmsnorm(y, gpost_ref[...])
    x1_ref[...] = x1
    qm = _dot(_rmsnorm(x1, gpre_ref[...]).astype(BF16), wq_ref[...])
    for h in range(MEM_HEADS):
        _store_heads(qm_ref, HEADS, h, MEM_HEADS, n, qm[:, h * HEAD_DIM:(h + 1) * HEAD_DIM])


def _mix_sample(u, prev, w_pool, pool_scale, attn, x, w_out, g_post, g_pre, w_xq, *, pos):
    n, d = x.shape
    kern = functools.partial(_mix_sample_kernel, pos=pos)
    return pl.pallas_call(
        kern,
        out_shape=[jax.ShapeDtypeStruct(prev.shape, F32), jax.ShapeDtypeStruct((n, d), F32),
                   jax.ShapeDtypeStruct((n * MEM_HEADS, HEAD_DIM), F32)],
        compiler_params=pltpu.CompilerParams(vmem_limit_bytes=VMEM_LIMIT_BYTES),
        name="mix_sample",
    )(u, prev, w_pool, pool_scale, attn, x, w_out, g_post, g_pre, w_xq)


def _attn_prompt_kernel(run_casts, q_ref, k_ref, v_ref, o_ref, *scratch, seq):
    run_casts()
    n_br = len(DILATED)
    ob_refs, mb_refs, sb_refs = scratch[:n_br], scratch[n_br:2 * n_br], scratch[2 * n_br:3 * n_br]
    p_ref, bias_ref, q4_ref, k4_ref, v4_ref = scratch[3 * n_br:]
    exp2_scale = HEAD_DIM ** -0.5 * LOG2_E
    qi = lax.broadcasted_iota(jnp.int32, (BLOCK, 2 * BLOCK), 0)
    kj = lax.broadcasted_iota(jnp.int32, (BLOCK, 2 * BLOCK), 1)
    in_band = ((kj < BLOCK) & (kj >= qi)) | ((kj >= BLOCK) & (kj - BLOCK <= qi))
    bias_ref[...] = jnp.where(in_band, 0.0, NEG_INF)
    ones = jnp.ones((2 * BLOCK, HEAD_DIM), BF16)

    def rows(block, r, d):
        if d == 1:
            start = block * BLOCK
            return pl.ds(start if isinstance(start, int) else pl.multiple_of(start, BLOCK), BLOCK)
        return pl.ds(block * (BLOCK * d) + r, BLOCK, stride=d)

    def out_rows(block, r, d):
        if d == PITCHED_DILATION:
            start = r * PITCH
            return pl.ds(start if isinstance(start, int) else pl.multiple_of(start, 8), BLOCK)
        return rows(block, r, d)

    quarter = seq // 4
    assert [d for _, d in DILATED] == [1, 4, 16]
    for ref, by4_ref in ((q_ref, q4_ref), (k_ref, k4_ref), (v_ref, v4_ref)):
        for b in range(4):
            by4_ref[b * quarter:(b + 1) * quarter, :] = ref[pl.ds(b, quarter, stride=4), :]

    def in_block(refs, block, r, d):
        ref, by4_ref = refs
        if d == 1:
            return ref[rows(block, r, d), :]
        if d == 4:
            return by4_ref[pl.ds(r * quarter + block * BLOCK, BLOCK), :]
        return by4_ref[pl.ds((r % 4) * quarter + r // 4 + block * (4 * BLOCK), BLOCK, stride=4), :]

    def keys(refs, block, r, d, with_prev):
        cur = in_block(refs, block, r, d)
        if not with_prev:
            return cur.astype(BF16)
        return jnp.concatenate([in_block(refs, block - 1, r, d), cur], axis=0).astype(BF16)

    groups, base = [], 0
    for g, (_, d) in enumerate(DILATED):
        n_blocks = seq // d // BLOCK
        for with_prev in (False, True):
            per_r = n_blocks - 1 if with_prev else 1
            count = d * per_r
            if count == 0:
                continue
            cols = slice(0, 2 * BLOCK) if with_prev else slice(BLOCK, 2 * BLOCK)
            width = 2 * BLOCK if with_prev else BLOCK

            def locate(idx, per_r=per_r, with_prev=with_prev):
                return (idx % per_r + 1, idx // per_r) if with_prev else (0, idx)

            def probabilities(idx, g=g, d=d, with_prev=with_prev, cols=cols, width=width, locate=locate, base=base):
                b, r = locate(idx)
                q = in_block((q_ref, q4_ref), b, r, d).astype(BF16)
                s = _dot_nt(q, keys((k_ref, k4_ref), b, r, d, with_prev)) + bias_ref[:, cols]
                m = jnp.max(s, axis=-1, keepdims=True)
                p_ref[base + idx, :, 0:width] = jnp.exp2((s - m) * exp2_scale).astype(BF16)
                mb_refs[g][out_rows(b, r, d), :] = jnp.broadcast_to(m, (BLOCK, HEAD_DIM))

            def values(idx, g=g, d=d, with_prev=with_prev, width=width, locate=locate, base=base):
                b, r = locate(idx)
                v1 = jnp.concatenate([keys((v_ref, v4_ref), b, r, d, with_prev), ones[0:width]], axis=1)
                acc = _dot(p_ref[base + idx, :, 0:width], v1)
                ob_refs[g][out_rows(b, r, d), :] = acc[:, 0:HEAD_DIM]
                sb_refs[g][out_rows(b, r, d), :] = acc[:, HEAD_DIM:]

            groups.append((count, probabilities, values))
            base += count

    for count, probabilities, _ in groups:
        for idx in range(count):
            probabilities(idx)
    for count, _, values in groups:
        for idx in range(count):
            values(idx)

    chunk = 1024

    def combine(c, carry):
        rw = pl.ds(pl.multiple_of(c * chunk, chunk), chunk)

        def read(ref, g):
            if DILATED[g][1] != PITCHED_DILATION:
                return ref[rw, :]
            pieces = [ref[pl.ds((j % 2) * 8 * PITCH + c * (chunk // PITCHED_DILATION) + j // 2, 8, stride=PITCH), :]
                      for j in range(chunk // 8)]
            return jnp.concatenate(pieces, axis=0)

        ms = [read(mb, g) for g, mb in enumerate(mb_refs)]
        m = functools.reduce(jnp.maximum, ms)
        ws = [jnp.exp2((mg - m) * exp2_scale) for mg in ms]
        num = functools.reduce(jnp.add, [w * read(ob, g) for g, (w, ob) in enumerate(zip(ws, ob_refs))])
        den = functools.reduce(jnp.add, [w * read(sb, g) for g, (w, sb) in enumerate(zip(ws, sb_refs))])
        o_ref[rw, :] = (num / den).astype(o_ref.dtype)
        return carry

    lax.fori_loop(0, seq // chunk, combine, 0)


def _attn_prompt(q, k, v, *, seq, casts=()):
    n_heads, m, _ = q.shape
    n_seq = m // seq
    spec = pl.BlockSpec((None, seq, HEAD_DIM), lambda n, h: (h, n, 0))
    kern = functools.partial(_attn_prompt_kernel, seq=seq)
    (attn,), cast_out = _call_with_casts(
        kern,
        grid=(n_seq, n_heads),
        in_specs=[spec, spec, spec],
        out_specs=[spec],
        out_shape=[jax.ShapeDtypeStruct(q.shape, BF16)],
        scratch_shapes=(
            [pltpu.VMEM((PITCHED_DILATION * PITCH if d == PITCHED_DILATION else seq, HEAD_DIM), F32)
             for _ in range(3) for _, d in DILATED]
            + [pltpu.VMEM((len(DILATED) * seq // BLOCK, BLOCK, 2 * BLOCK), BF16),
               pltpu.VMEM((BLOCK, 2 * BLOCK), F32)]
            + [pltpu.VMEM((seq, HEAD_DIM), F32) for _ in range(3)]),
        args=(q, k, v),
        casts=casts,
        step_of=lambda n, h: n * n_heads + h,
        name="attn_prompt",
    )
    return attn, cast_out


def _attn_sample_kernel(q_ref, kn_ref, vn_ref, *refs, group):
    n_br = len(DILATED)
    kc_refs, vc_refs, o_ref = refs[:n_br], refs[n_br:2 * n_br], refs[2 * n_br]
    for j in range(group):
        q = q_ref[j] * (HEAD_DIM ** -0.5 * LOG2_E)
        t_new = jnp.sum(q * kn_ref[j], axis=-1, keepdims=True)
        t_br = [jnp.sum(kc[j] * q[None], axis=-1, keepdims=True) for kc in kc_refs]
        m = t_new
        for t in t_br:
            m = jnp.maximum(m, jnp.max(t, axis=0))
        p_new = jnp.exp2(t_new - m) * float(n_br)
        den = p_new
        num = p_new * vn_ref[j]
        for t, vc in zip(t_br, vc_refs):
            p = jnp.exp2(t - m[None])
            den = den + jnp.sum(p, axis=0)
            num = num + jnp.sum(p * vc[j], axis=0)
        out = num / den
        o_ref[j] = out
    return out


def _attn_sample_operands(q, k_new, v_new, k_cache, v_cache, *, group):
    n, heads, _ = q.shape
    win = k_cache.shape[1]
    row = pl.BlockSpec((group, heads, HEAD_DIM), lambda i: (i, 0, 0))
    cache_specs, k_views, v_views = [], [], []
    for w, d in DILATED:
        assert w == BLOCK * d and win % w == 0
        last = win // w - 1
        cache_specs.append(pl.BlockSpec((group, BLOCK, None, heads, HEAD_DIM),
                                        lambda i, last=last: (i, last, 0, 0, 0)))
        k_views.append(k_cache.reshape(n, win // d, d, heads, HEAD_DIM))
        v_views.append(v_cache.reshape(n, win // d, d, heads, HEAD_DIM))
    return ([row, row, row] + cache_specs + cache_specs, row, jax.ShapeDtypeStruct((n, heads, HEAD_DIM), F32),
            (q, k_new, v_new, *k_views, *v_views))


def _mix_mem_prompt_kernel(run_casts, pool_ref, attn_ref, x_ref, mk_ref, mv_ref, wo_ref,
                           gmix_ref, gpre_ref, wq_ref, wxo_ref, gmem_ref, *refs, sample_group):
    n_sample_in = 3 + 2 * len(DILATED)
    sample_in, (x2_ref, attn_s_ref) = refs[:n_sample_in], refs[n_sample_in:]
    run_casts()
    scale = HEAD_DIM ** -0.5
    mixed = jnp.concatenate([pool_ref[...]] + [attn_ref[h] for h in range(N_HEADS)], axis=-1)
    y = _dot(mixed, wo_ref[...])
    last = _attn_sample_kernel(*sample_in, attn_s_ref, group=sample_group)
    zero = jnp.minimum(jnp.abs(last[0:1, :]), 0.0)
    gmix = gmix_ref[...] + jnp.concatenate([zero] * (gmix_ref.shape[1] // HEAD_DIM), axis=1)
    x1 = x_ref[...] + _rmsnorm(y, gmix)
    qm = _dot(_rmsnorm(x1, gpre_ref[...]).astype(BF16), wq_ref[...]).astype(BF16)
    heads = []
    for h in range(MEM_HEADS):
        cols = slice(h * HEAD_DIM, (h + 1) * HEAD_DIM)
        s = _dot_nt(qm[:, cols], mk_ref[:, cols].astype(BF16)) * scale
        p = jnp.exp(s - jnp.max(s, axis=-1, keepdims=True))
        o = _dot(p.astype(BF16), mv_ref[:, cols].astype(BF16))
        heads.append((o / jnp.sum(p, axis=-1, keepdims=True)).astype(BF16))
    y = _dot(jnp.concatenate(heads, axis=-1), wxo_ref[...])
    x2_ref[...] = x1 + _rmsnorm(y, gmem_ref[...])


def _mix_mem_prompt(pool_out, attn, x, mk, mv, w_out, g_mix_post, g_mem_pre, w_xq, w_xo, g_mem_post,
                    sample_attn, *, tm, seq, casts=()):
    m, d = x.shape
    n_tiles = m // tm
    tiles_per_seq = seq // tm
    half = pool_out.shape[1]
    mem_spec = pl.BlockSpec((N_MEM, MEM_WIDTH), lambda i: (i // tiles_per_seq, 0))
    gain_spec = _resident((1, d), lambda i: (0, 0))
    n_sample = sample_attn[0].shape[0]
    assert n_sample % n_tiles == 0
    sample_group = n_sample // n_tiles
    s_in_specs, s_out_spec, s_out_shape, s_args = _attn_sample_operands(*sample_attn, group=sample_group)
    (x2, attn_s), cast_out = _call_with_casts(
        functools.partial(_mix_mem_prompt_kernel, sample_group=sample_group),
        grid=(n_tiles,),
        in_specs=[
            pl.BlockSpec((tm, half), lambda i: (i, 0)),
            pl.BlockSpec((N_HEADS, tm, HEAD_DIM), lambda i: (0, i, 0)),
            pl.BlockSpec((tm, d), lambda i: (i, 0)),
            mem_spec,
            mem_spec,
            _resident(w_out.shape, lambda i: (0, 0)),
            gain_spec,
            gain_spec,
            _resident(w_xq.shape, lambda i: (0, 0)),
            _resident(w_xo.shape, lambda i: (0, 0)),
            gain_spec,
        ] + s_in_specs,
        out_specs=[pl.BlockSpec((tm, d), lambda i: (i, 0)), s_out_spec],
        out_shape=[jax.ShapeDtypeStruct((m, d), F32), s_out_shape],
        args=(pool_out, attn, x, mk, mv, w_out, g_mix_post, g_mem_pre, w_xq, w_xo, g_mem_post) + tuple(s_args),
        casts=casts,
        name="mix_mem_prompt",
    )
    return x2, attn_s, cast_out


def _mem_attn_sample_kernel(qm_ref, mk_ref, mv_ref, o_ref, *, group):
    heads = qm_ref.shape[1]
    both = lambda a: jnp.concatenate([a, a], axis=0)
    fold = lambda a: a[0:heads] + a[heads:]
    for j in range(group):
        q = both(qm_ref[j] * (HEAD_DIM ** -0.5 * LOG2_E))
        t = jnp.sum(mk_ref[j] * q[None], axis=-1, keepdims=True)
        m = jnp.max(t, axis=0)
        m = both(jnp.maximum(m[0:heads], m[heads:]))
        p = jnp.exp2(t - m[None])
        o_ref[j] = fold(jnp.sum(p * mv_ref[j], axis=0)) / fold(jnp.sum(p, axis=0))


def _mem_attn_sample(qm, mem_k, mem_v):
    n, heads, _ = qm.shape
    group = SAMPLE_GROUP
    row = pl.BlockSpec((group, heads, HEAD_DIM), lambda i: (i, 0, 0))
    mem = pl.BlockSpec((group, N_MEM // 2, 2 * heads, HEAD_DIM), lambda i: (i, 0, 0, 0))
    pairs = lambda a: a.reshape(n, N_MEM // 2, 2 * heads, HEAD_DIM)
    return pl.pallas_call(
        functools.partial(_mem_attn_sample_kernel, group=group),
        grid=(n // group,),
        in_specs=[row, mem, mem],
        out_specs=row,
        out_shape=jax.ShapeDtypeStruct((n, heads, HEAD_DIM), F32),
        compiler_params=_params(1),
        name="mem_attn_sample",
    )(qm, pairs(mem_k), pairs(mem_v))


def _mem_out_sample_kernel(a_ref, w_ref, g_ref, r_ref, o_ref):
    y = jnp.zeros(r_ref.shape, F32)
    for h in range(a_ref.shape[1]):
        y = y + _dot(a_ref[:, h, :].astype(BF16), w_ref[h * HEAD_DIM:(h + 1) * HEAD_DIM, :])
    o_ref[...] = r_ref[...] + _rmsnorm(y, g_ref[...])


def _mem_out_sample(a, w, g, resid):
    return pl.pallas_call(
        _mem_out_sample_kernel,
        out_shape=jax.ShapeDtypeStruct(resid.shape, F32),
        compiler_params=pltpu.CompilerParams(vmem_limit_bytes=VMEM_LIMIT_BYTES),
        name="mem_out_sample",
    )(a, w, g, resid)


def _ffn_kernel(x_ref, xs_ref, gpre_ref, w1_ref, w2_ref, gpost_ref, o_ref, os_ref,
                xn_ref, xns_ref, accs_ref, *, tm, nf):
    i, f = pl.program_id(0), pl.program_id(1)
    last_f = nf - 1

    @pl.when((i == 0) & (f == 0))
    def _():
        xns_ref[...] = _rmsnorm(xs_ref[...], gpre_ref[...]).astype(BF16)
        accs_ref[...] = jnp.zeros_like(accs_ref)

    def mlp(lhs):
        hidden = jnp.square(jnp.maximum(_dot(lhs, w1_ref[...]), 0.0)).astype(BF16)
        return _dot(hidden, w2_ref[...])

    def step(first, last):
        if first:
            xn = _rmsnorm(x_ref[...], gpre_ref[...]).astype(BF16)
            xn_ref[...] = xn
        else:
            xn = xn_ref[...]

        def finish(acc):
            o_ref[...] = x_ref[...] + _rmsnorm(acc, gpost_ref[...]) if last else acc

        @pl.when(i == 0)
        def _():
            both = mlp(jnp.concatenate([xn, xns_ref[...]], axis=0))
            accs_ref[...] += both[tm:]
            finish(both[0:tm] if first else o_ref[...] + both[0:tm])

        @pl.when(i > 0)
        def _():
            y = mlp(xn)
            finish(y if first else o_ref[...] + y)

    pl.when(f == 0)(lambda: step(True, False))
    pl.when((f > 0) & (f < last_f))(lambda: step(False, False))
    pl.when(f == last_f)(lambda: step(False, True))

    @pl.when((f == last_f) & (i == 0))
    def _():
        os_ref[...] = xs_ref[...] + _rmsnorm(accs_ref[...], gpost_ref[...])


def _ffn(x, xs, g_pre, w1_chunks, w2, g_post, *, tm):
    m, d = x.shape
    ms = xs.shape[0]
    nf, _, tf = w1_chunks.shape
    n_tiles = m // tm
    assert nf >= 2
    sample_spec = _resident((ms, d), lambda i, f: (0, 0))
    return pl.pallas_call(
        functools.partial(_ffn_kernel, tm=tm, nf=nf),
        grid=(n_tiles, nf),
        in_specs=[
            pl.BlockSpec((tm, d), lambda i, f: (i, 0)),
            sample_spec,
            _resident((1, d), lambda i, f: (0, 0)),
            pl.BlockSpec((None, d, tf), lambda i, f: (f, 0, 0)),
            pl.BlockSpec((tf, d), lambda i, f: (f, 0)),
            _resident((1, d), lambda i, f: (0, 0)),
        ],
        out_specs=[pl.BlockSpec((tm, d), lambda i, f: (i, 0)),
                   pl.BlockSpec((ms, d), lambda i, f: (0, 0))],
        out_shape=[jax.ShapeDtypeStruct((m, d), F32), jax.ShapeDtypeStruct((ms, d), F32)],
        scratch_shapes=[pltpu.VMEM((tm, d), BF16), pltpu.VMEM((ms, d), BF16), pltpu.VMEM((ms, d), F32)],
        compiler_params=_params(2),
        name="ffn",
    )(x, xs, g_pre, w1_chunks, w2, g_post)


def kernel(x_prompt, x_sample, state_pool, cache_attn_k, cache_attn_v, cache_mem_k, cache_mem_v, mem_prompt,
           g_mix_pre, g_mix_post, g_mem_pre, g_mem_post, g_ffn_pre, g_ffn_post, g_mem_kv,
           w_in, w_pool, pool_scale, w_out, w_xq, w_mem_kv, w_xo, w_ff1, w_ff2):
    depth = w_in.shape[0]
    assert depth == 1
    batch, seq, d = x_prompt.shape
    dec_batch, dec_seq, _ = x_sample.shape
    assert dec_seq == 1

    l = 0
    scale = pool_scale[l][None, :]
    gain = lambda g: g[l][None, :]

    xp = x_prompt.reshape(batch * seq, d)
    xs = x_sample.reshape(dec_batch, d)
    w_pool_f = w_pool[l]
    tm = ROW_TILE


    mem = mem_prompt.reshape(batch * N_MEM, d)
    (mk, mv, mk_h, mv_h, cos_p, sin_p, cos_s, sin_s), (w_in_b,) = _norm_proj(
        mem, gain(g_mem_kv), w_mem_kv[l], tm=MEM_KV_TILE, ncol=MEM_WIDTH,
        outputs=((0, FLAT), (1, FLAT), (0, HEADS), (1, HEADS)), name="mem_kv", casts=(w_in[l],),
        seq=seq, n_sample=dec_batch, sample_pos=PAST_LEN)

    ((pool_out, q, k, v, k_h, v_h, pool_state, us, qs_h, ks_h, vs_h),
     (w_out_b, w_xq_b, w_xo_b)) = _in_proj_pool(
        xp, xs, gain(g_mix_pre), w_in_b, cos_p, sin_p, cos_s, sin_s, w_pool_f, scale, tm=IN_PROJ_TILE, seq=seq,
        casts=(w_out[l], w_xq[l], w_xo[l]))
    attn, (w_ff1_b, w_ff2_b) = _attn_prompt(
        q, k, v, seq=seq, casts=((w_ff1[l], w_ff1.shape[2] // FFN_CHUNK), w_ff2[l]))
    heads3 = lambda a: a.reshape(dec_batch, -1, HEAD_DIM)
    x2, attn_s, _ = _mix_mem_prompt(
        pool_out, attn, xp, mk, mv, w_out_b, gain(g_mix_post), gain(g_mem_pre), w_xq_b, w_xo_b, gain(g_mem_post),
        (heads3(qs_h), heads3(ks_h), heads3(vs_h), cache_attn_k[l], cache_attn_v[l]), tm=tm, seq=seq)

    new_pool_s, x1s, qms_h = _mix_sample(us, state_pool[l], w_pool_f, scale, attn_s, xs, w_out_b,
                                         gain(g_mix_post), gain(g_mem_pre), w_xq_b, pos=PAST_LEN)
    mem_o_s = _mem_attn_sample(heads3(qms_h), cache_mem_k[l], cache_mem_v[l])
    x2s = _mem_out_sample(mem_o_s, w_xo_b, gain(g_mem_post), x1s)

    yp, ys = _ffn(x2, x2s, gain(g_ffn_pre), w_ff1_b, w_ff2_b, gain(g_ffn_post), tm=tm)

    keep = min(max(w for w, _ in DILATED), seq)
    return (
        yp.reshape(batch, seq, d),
        ys.reshape(dec_batch, 1, d),
        pool_state.reshape(batch, STATE_ROWS, POOL_WIDTH)[:, STATE_ROWS - POOL_STATE:][None],
        new_pool_s[None],
        k_h.reshape(batch, seq, N_HEADS, HEAD_DIM)[:, seq - keep:][None],
        v_h.reshape(batch, seq, N_HEADS, HEAD_DIM)[:, seq - keep:][None],
        ks_h.reshape(dec_batch, 1, N_HEADS, HEAD_DIM)[None],
        vs_h.reshape(dec_batch, 1, N_HEADS, HEAD_DIM)[None],
        mk_h.reshape(batch, N_MEM, MEM_HEADS, HEAD_DIM)[None],
        mv_h.reshape(batch, N_MEM, MEM_HEADS, HEAD_DIM)[None],
    )
```

```python
import functools

import jax
import jax.numpy as jnp
from jax import lax
from jax.experimental import pallas as pl
from jax.experimental.pallas import tpu as pltpu

D_MODEL = 2048
POOL_WIDTH = 1024
POOL_WINDOWS = (2, 4, 8, 16)
POOL_GROUP = POOL_WIDTH // len(POOL_WINDOWS)
POOL_STATE = max(POOL_WINDOWS) - 1
HEAD_DIM = 128
ATTN_WIDTH = 1024
N_HEADS = ATTN_WIDTH // HEAD_DIM
DILATED = ((128, 1), (512, 4), (2048, 16))
PITCHED_DILATION = 16
PITCH = 128 + 8
ROPE_DIM = HEAD_DIM // 4
ROPE_HALF = ROPE_DIM // 2
ROPE_THETA = 500000.0
N_MEM = 256
MEM_HEADS = 4
MEM_WIDTH = MEM_HEADS * HEAD_DIM
EPS = 1e-6
BLOCK = 128
NEG_INF = -1e30
LOG2_E = 1.4426950408889634
PAST_LEN = 8192
HALO = 32
STATE_ROWS = 16

F32 = jnp.float32
BF16 = jnp.bfloat16

VMEM_LIMIT_BYTES = 56 * 1024 * 1024

ROW_TILE = 512
IN_PROJ_TILE = 256
MEM_KV_TILE = 256
FFN_CHUNK = 2048
SAMPLE_GROUP = 8

FLAT, HEADS = "flat", "heads"


def _params(n_grid_axes):
    return pltpu.CompilerParams(
        dimension_semantics=("arbitrary",) * n_grid_axes,
        vmem_limit_bytes=VMEM_LIMIT_BYTES,
    )


def _resident(block_shape, index_map):
    return pl.BlockSpec(block_shape, index_map, pipeline_mode=pl.Buffered(1))


def _rmsnorm(x, g):
    ms = jnp.mean(x * x, axis=-1, keepdims=True)
    return x * lax.rsqrt(ms + EPS) * g


def _dot(a, b):
    return jnp.dot(a, b, preferred_element_type=F32)


def _dot_nt(a, b):
    return lax.dot_general(a, b, (((1,), (1,)), ((), ())), preferred_element_type=F32)


def _call_with_casts(kernel_fn, *, grid, in_specs, out_specs, out_shape, scratch_shapes=(), args, casts=(),
                     step_of=None, name):
    casts = [c if isinstance(c, tuple) else (c, 1) for c in casts]
    n_in, n_out, n_casts = len(in_specs), len(out_specs), len(casts)
    n_steps = 1
    for g in grid:
        n_steps *= g
    if step_of is None:
        step_of = lambda i: i
    cast_in_specs, cast_out_specs, cast_shapes = [], [], []
    for a, n_chunks in casts:
        rows, cols = a.shape
        assert rows % n_steps == 0 and cols % n_chunks == 0
        cast_in_specs.append(pl.BlockSpec((rows // n_steps, cols), lambda *idx: (step_of(*idx), 0)))
        if n_chunks == 1:
            cast_out_specs.append(cast_in_specs[-1])
            cast_shapes.append(jax.ShapeDtypeStruct(a.shape, BF16))
        else:
            cast_out_specs.append(pl.BlockSpec((n_chunks, rows // n_steps, cols // n_chunks),
                                               lambda *idx: (0, step_of(*idx), 0)))
            cast_shapes.append(jax.ShapeDtypeStruct((n_chunks, rows, cols // n_chunks), BF16))

    def body(*refs):
        ins, cast_in = refs[:n_in], refs[n_in:n_in + n_casts]
        outs = refs[n_in + n_casts:n_in + n_casts + n_out]
        cast_out = refs[n_in + n_casts + n_out:n_in + 2 * n_casts + n_out]

        def run_casts():
            for i_ref, o_ref, (_, n_chunks) in zip(cast_in, cast_out, casts):
                if n_chunks == 1:
                    o_ref[...] = i_ref[...].astype(BF16)
                else:
                    width = i_ref.shape[1] // n_chunks
                    for c in range(n_chunks):
                        o_ref[c] = i_ref[:, c * width:(c + 1) * width].astype(BF16)

        kernel_fn(run_casts, *ins, *outs, *refs[n_in + 2 * n_casts + n_out:])

    res = pl.pallas_call(
        body,
        grid=grid,
        in_specs=list(in_specs) + cast_in_specs,
        out_specs=list(out_specs) + cast_out_specs,
        out_shape=list(out_shape) + cast_shapes,
        scratch_shapes=list(scratch_shapes),
        compiler_params=_params(len(grid)),
        name=name,
    )(*args, *[a for a, _ in casts])
    return res[:n_out], res[n_out:]


def _store_heads(o_ref, layout, h, heads, rows, value):
    if layout == FLAT:
        o_ref[:, h * HEAD_DIM:(h + 1) * HEAD_DIM] = value
    else:
        o_ref[pl.ds(h, rows, stride=heads), :] = value


def _out_block(layout, tm, heads):
    return (tm, heads * HEAD_DIM) if layout == FLAT else (tm * heads, HEAD_DIM)


def _rope_rows(pos):
    lane = lax.broadcasted_iota(jnp.int32, (1, HEAD_DIM), 1)
    k = (lane % ROPE_HALF).astype(F32)
    inv = jnp.where(lane < ROPE_DIM, jnp.power(jnp.float32(ROPE_THETA), -k * 2.0 / ROPE_DIM), 0.0)
    ang = pos * inv
    return jnp.cos(ang), jnp.sin(ang) * jnp.where(lane < ROPE_HALF, -1.0, 1.0)


def _norm_proj_kernel(run_casts, x_ref, g_ref, w_ref, *out_refs, tm, ncol, outputs, sample_pos):
    run_casts()
    *out_refs, cos_ref, sin_ref, coss_ref, sins_ref = out_refs
    heads = ncol // HEAD_DIM
    xn = _rmsnorm(x_ref[...], g_ref[...]).astype(BF16)
    for c in sorted({c for c, _ in outputs}):
        acc = _dot(xn, w_ref[:, c * ncol:(c + 1) * ncol].astype(BF16))
        for h in range(heads):
            for (oc, layout), o_ref in zip(outputs, out_refs):
                if oc == c:
                    _store_heads(o_ref, layout, h, heads, tm, acc[:, h * HEAD_DIM:(h + 1) * HEAD_DIM])
    rows = cos_ref.shape[0]
    pos = pl.program_id(0) * rows + lax.broadcasted_iota(jnp.int32, (rows, 1), 0)
    cos_ref[...], sin_ref[...] = _rope_rows(pos.astype(F32))
    coss_ref[...], sins_ref[...] = _rope_rows(jnp.full((coss_ref.shape[0], 1), sample_pos, F32))


def _norm_proj(x, g, w, *, tm, ncol, outputs, name, casts=(), seq, n_sample, sample_pos):
    m, d = x.shape
    heads = ncol // HEAD_DIM
    n_steps = m // tm
    assert w.shape[0] == d and w.shape[1] % ncol == 0 and m % tm == 0 and seq % (8 * n_steps) == 0
    kern = functools.partial(_norm_proj_kernel, tm=tm, ncol=ncol, outputs=outputs, sample_pos=sample_pos)
    blocks = [_out_block(layout, tm, heads) for _, layout in outputs]
    table = pl.BlockSpec((seq // n_steps, HEAD_DIM), lambda i: (i, 0))
    sample_table = pl.BlockSpec((n_sample, HEAD_DIM), lambda i: (0, 0))
    return _call_with_casts(
        kern,
        grid=(n_steps,),
        in_specs=[
            pl.BlockSpec((tm, d), lambda i: (i, 0)),
            _resident((1, d), lambda i: (0, 0)),
            _resident(w.shape, lambda i: (0, 0)),
        ],
        out_specs=[pl.BlockSpec(b, lambda i: (i, 0)) for b in blocks] + [table, table, sample_table, sample_table],
        out_shape=([jax.ShapeDtypeStruct((n_steps * b[0], b[1]), F32) for b in blocks]
                   + [jax.ShapeDtypeStruct((seq, HEAD_DIM), F32)] * 2
                   + [jax.ShapeDtypeStruct((n_sample, HEAD_DIM), F32)] * 2),
        args=(x, g, w),
        casts=casts,
        name=name,
    )


def _window_sums(ext_ref, lvl_refs, g, w, rows):
    cols = slice(g * POOL_GROUP, (g + 1) * POOL_GROUP)
    levels = w.bit_length() - 1
    src, src_cols = ext_ref, cols
    for level in range(levels):
        shift = 1 << level
        last = level == levels - 1
        lo = HALO if last else 8 * (level + 1)
        assert lo - shift >= 8 * level
        total = src[lo:rows, src_cols] + src[lo - shift:rows - shift, src_cols]
        if last:
            return total
        lvl_refs[level % 2][lo:rows, :] = total
        src, src_cols = lvl_refs[level % 2], slice(None)


def _rope(a, cos, sin):
    lane = lax.broadcasted_iota(jnp.int32, a.shape, 1)
    partner = jnp.where(lane < ROPE_HALF, pltpu.roll(a, HEAD_DIM - ROPE_HALF, 1), pltpu.roll(a, ROPE_HALF, 1))
    return a * cos + partner * sin


def _in_proj_pool_kernel(run_casts, x_ref, xnext_ref, xs_ref, g_ref, w_ref, cos_ref, sin_ref, coss_ref, sins_ref,
                         wp_ref, scale_ref,
                         pool_ref, q_ref, k_ref, v_ref, kh_ref, vh_ref, state_ref, us_ref, qsh_ref, ksh_ref, vsh_ref,
                         xn_ref, ext_ref, lvl_a_ref, lvl_b_ref, *, tm, tiles_per_seq):
    step = pl.program_id(0)
    t_in_seq = step % tiles_per_seq
    n_sample = xs_ref.shape[0]

    def tile(xn, with_sample):
        run_casts()
        u = _dot(xn, w_ref[:, 0:POOL_WIDTH])
        ext_ref[HALO:HALO + tm, :] = u[0:tm]
        if with_sample:
            us_ref[...] = u[tm:]
        cos, sin = cos_ref[...], sin_ref[...]
        outs = ((q_ref, None, qsh_ref), (k_ref, kh_ref, ksh_ref), (v_ref, vh_ref, vsh_ref))
        for c, (major_ref, heads_ref, sample_ref) in enumerate(outs, start=1):
            acc = _dot(xn, w_ref[:, c * ATTN_WIDTH:(c + 1) * ATTN_WIDTH])
            for h in range(N_HEADS):
                a = acc[0:tm, h * HEAD_DIM:(h + 1) * HEAD_DIM]
                if major_ref is not v_ref:
                    a = _rope(a, cos, sin)
                major_ref[h] = a
                if heads_ref is not None:
                    _store_heads(heads_ref, HEADS, h, N_HEADS, tm, a)
                if with_sample:
                    a = acc[tm:, h * HEAD_DIM:(h + 1) * HEAD_DIM]
                    if major_ref is not v_ref:
                        a = _rope(a, coss_ref[...], sins_ref[...])
                    _store_heads(sample_ref, HEADS, h, N_HEADS, n_sample, a)

        pos = t_in_seq * tm + lax.broadcasted_iota(jnp.int32, (tm, 1), 0)
        for g, w in enumerate(POOL_WINDOWS):
            cols = slice(g * POOL_GROUP, (g + 1) * POOL_GROUP)
            wsum = _window_sums(ext_ref, (lvl_a_ref, lvl_b_ref), g, w, HALO + tm)
            cnt = jnp.minimum(w, pos + 1).astype(F32)
            pooled = (wsum / cnt - ext_ref[HALO:HALO + tm, cols]).astype(BF16)
            pool_ref[:, cols] = (_dot(pooled, wp_ref[g].astype(BF16)) * scale_ref[:, cols]).astype(pool_ref.dtype)
        state_ref[...] = ext_ref[HALO + tm - STATE_ROWS:HALO + tm, :]
        ext_ref[0:HALO, :] = ext_ref[tm:tm + HALO, :]

        xn_ref[(step + 1) % 2] = _rmsnorm(xnext_ref[...], g_ref[...]).astype(BF16)

    @pl.when(t_in_seq == 0)
    def _():
        ext_ref[0:HALO, :] = jnp.zeros((HALO, POOL_WIDTH), F32)

    @pl.when(step == 0)
    def _():
        rows = jnp.concatenate([x_ref[...], xs_ref[...]], axis=0)
        tile(_rmsnorm(rows, g_ref[...]).astype(BF16), True)

    @pl.when(step > 0)
    def _():
        tile(xn_ref[step % 2], False)


def _in_proj_pool(x, xs, g, w, cos, sin, cos_s, sin_s, w_pool, pool_scale, *, tm, seq, casts=()):
    m, d = x.shape
    ms = xs.shape[0]
    tiles_per_seq = seq // tm
    n_seq = m // seq
    assert POOL_WIDTH == ATTN_WIDTH and w.shape == (d, POOL_WIDTH + 3 * ATTN_WIDTH)
    kern = functools.partial(_in_proj_pool_kernel, tm=tm, tiles_per_seq=tiles_per_seq)
    row_tile = lambda width: pl.BlockSpec((tm, width), lambda i: (i, 0))
    heads_tile = pl.BlockSpec((tm * N_HEADS, HEAD_DIM), lambda i: (i, 0))
    table = pl.BlockSpec((tm, HEAD_DIM), lambda i: (i % tiles_per_seq, 0))
    whole = lambda shape: _resident(shape, lambda i: (0,) * len(shape))
    head_major = jax.ShapeDtypeStruct((N_HEADS, m, HEAD_DIM), F32)
    head_major_tile = pl.BlockSpec((N_HEADS, tm, HEAD_DIM), lambda i: (0, i, 0))
    by_heads = jax.ShapeDtypeStruct((m * N_HEADS, HEAD_DIM), F32)
    sample_heads = jax.ShapeDtypeStruct((ms * N_HEADS, HEAD_DIM), F32)
    sample_heads_spec = pl.BlockSpec((ms * N_HEADS, HEAD_DIM), lambda i: (0, 0))
    return _call_with_casts(
        kern,
        grid=(m // tm,),
        in_specs=[
            row_tile(d),
            pl.BlockSpec((tm, d), lambda i: (jnp.minimum(i + 1, m // tm - 1), 0)),
            whole((ms, d)),
            whole((1, d)),
            whole(w.shape),
            table,
            table,
            whole((ms, HEAD_DIM)),
            whole((ms, HEAD_DIM)),
            whole(w_pool.shape),
            whole((1, POOL_WIDTH)),
        ],
        out_specs=[row_tile(POOL_WIDTH), head_major_tile, head_major_tile, head_major_tile,
                   heads_tile, heads_tile,
                   pl.BlockSpec((STATE_ROWS, POOL_WIDTH), lambda i: (i // tiles_per_seq, 0)),
                   pl.BlockSpec((ms, POOL_WIDTH), lambda i: (0, 0)),
                   sample_heads_spec, sample_heads_spec, sample_heads_spec],
        out_shape=[jax.ShapeDtypeStruct((m, POOL_WIDTH), BF16), head_major, head_major, head_major, by_heads, by_heads,
                   jax.ShapeDtypeStruct((n_seq * STATE_ROWS, POOL_WIDTH), F32),
                   jax.ShapeDtypeStruct((ms, POOL_WIDTH), F32), sample_heads, sample_heads, sample_heads],
        scratch_shapes=[pltpu.VMEM((2, tm, d), BF16),
                        pltpu.VMEM((HALO + tm, POOL_WIDTH), F32),
                        pltpu.VMEM((HALO + tm, POOL_GROUP), F32),
                        pltpu.VMEM((HALO + tm, POOL_GROUP), F32)],
        args=(x, x, xs, g, w, cos, sin, cos_s, sin_s, w_pool, pool_scale),
        casts=casts,
        name="in_proj_pool",
    )


def _mix_sample_kernel(u_ref, prev_ref, wp_ref, scale_ref, attn_ref, x_ref, wo_ref, gpost_ref, gpre_ref, wq_ref,
                       state_ref, x1_ref, qm_ref, *, pos):
    n = u_ref.shape[0]
    for j in range(POOL_STATE - 1):
        state_ref[:, j, :] = prev_ref[:, j + 1, :]
    state_ref[:, POOL_STATE - 1, :] = u_ref[...]
    y = jnp.zeros(x_ref.shape, F32)
    for g, w in enumerate(POOL_WINDOWS):
        cols = slice(g * POOL_GROUP, (g + 1) * POOL_GROUP)
        wsum = u_ref[:, cols]
        for back in range(1, w):
            wsum = wsum + prev_ref[:, POOL_STATE - back, cols]
        cnt = float(min(w, pos + 1))
        pooled = (wsum / cnt - u_ref[:, cols]).astype(BF16)
        pool_out = (_dot(pooled, wp_ref[g].astype(BF16)) * scale_ref[:, cols]).astype(BF16)
        y = y + _dot(pool_out, wo_ref[cols, :])
    for h in range(N_HEADS):
        rows = slice(POOL_WIDTH + h * HEAD_DIM, POOL_WIDTH + (h + 1) * HEAD_DIM)
        y = y + _dot(attn_ref[:, h, :].astype(BF16), wo_ref[rows, :])
    x1 = x_ref[...] + _rmsnorm(y, gpost_ref[...])
    x1_ref[...] = x1
    qm = _dot(_rmsnorm(x1, gpre_ref[...]).astype(BF16), wq_ref[...])
    for h in range(MEM_HEADS):
        _store_heads(qm_ref, HEADS, h, MEM_HEADS, n, qm[:, h * HEAD_DIM:(h + 1) * HEAD_DIM])


def _mix_sample(u, prev, w_pool, pool_scale, attn, x, w_out, g_post, g_pre, w_xq, *, pos):
    n, d = x.shape
    kern = functools.partial(_mix_sample_kernel, pos=pos)
    return pl.pallas_call(
        kern,
        out_shape=[jax.ShapeDtypeStruct(prev.shape, F32), jax.ShapeDtypeStruct((n, d), F32),
                   jax.ShapeDtypeStruct((n * MEM_HEADS, HEAD_DIM), F32)],
        compiler_params=pltpu.CompilerParams(vmem_limit_bytes=VMEM_LIMIT_BYTES),
        name="mix_sample",
    )(u, prev, w_pool, pool_scale, attn, x, w_out, g_post, g_pre, w_xq)


def _attn_prompt_kernel(run_casts, q_ref, k_ref, v_ref, o_ref, *scratch, seq):
    run_casts()
    n_br = len(DILATED)
    ob_refs, mb_refs, sb_refs = scratch[:n_br], scratch[n_br:2 * n_br], scratch[2 * n_br:3 * n_br]
    p_ref, bias_ref, q4_ref, k4_ref, v4_ref = scratch[3 * n_br:]
    exp2_scale = HEAD_DIM ** -0.5 * LOG2_E
    qi = lax.broadcasted_iota(jnp.int32, (BLOCK, 2 * BLOCK), 0)
    kj = lax.broadcasted_iota(jnp.int32, (BLOCK, 2 * BLOCK), 1)
    in_band = ((kj < BLOCK) & (kj >= qi)) | ((kj >= BLOCK) & (kj - BLOCK <= qi))
    bias_ref[...] = jnp.where(in_band, 0.0, NEG_INF)
    ones = jnp.ones((2 * BLOCK, HEAD_DIM), BF16)

    def rows(block, r, d):
        if d == 1:
            start = block * BLOCK
            return pl.ds(start if isinstance(start, int) else pl.multiple_of(start, BLOCK), BLOCK)
        return pl.ds(block * (BLOCK * d) + r, BLOCK, stride=d)

    def out_rows(block, r, d):
        if d == PITCHED_DILATION:
            start = r * PITCH
            return pl.ds(start if isinstance(start, int) else pl.multiple_of(start, 8), BLOCK)
        return rows(block, r, d)

    quarter = seq // 4
    assert [d for _, d in DILATED] == [1, 4, 16]
    for ref, by4_ref in ((q_ref, q4_ref), (k_ref, k4_ref), (v_ref, v4_ref)):
        for b in range(4):
            by4_ref[b * quarter:(b + 1) * quarter, :] = ref[pl.ds(b, quarter, stride=4), :]

    def in_block(refs, block, r, d):
        ref, by4_ref = refs
        if d == 1:
            return ref[rows(block, r, d), :]
        if d == 4:
            return by4_ref[pl.ds(r * quarter + block * BLOCK, BLOCK), :]
        return by4_ref[pl.ds((r % 4) * quarter + r // 4 + block * (4 * BLOCK), BLOCK, stride=4), :]

    def keys(refs, block, r, d, with_prev):
        cur = in_block(refs, block, r, d)
        if not with_prev:
            return cur.astype(BF16)
        return jnp.concatenate([in_block(refs, block - 1, r, d), cur], axis=0).astype(BF16)

    groups, base = [], 0
    for g, (_, d) in enumerate(DILATED):
        n_blocks = seq // d // BLOCK
        for with_prev in (False, True):
            per_r = n_blocks - 1 if with_prev else 1
            count = d * per_r
            if count == 0:
                continue
            cols = slice(0, 2 * BLOCK) if with_prev else slice(BLOCK, 2 * BLOCK)
            width = 2 * BLOCK if with_prev else BLOCK

            def locate(idx, per_r=per_r, with_prev=with_prev):
                return (idx % per_r + 1, idx // per_r) if with_prev else (0, idx)

            def probabilities(idx, g=g, d=d, with_prev=with_prev, cols=cols, width=width, locate=locate, base=base):
                b, r = locate(idx)
                q = in_block((q_ref, q4_ref), b, r, d).astype(BF16)
                s = _dot_nt(q, keys((k_ref, k4_ref), b, r, d, with_prev)) + bias_ref[:, cols]
                m = jnp.max(s, axis=-1, keepdims=True)
                p_ref[base + idx, :, 0:width] = jnp.exp2((s - m) * exp2_scale).astype(BF16)
                mb_refs[g][out_rows(b, r, d), :] = jnp.broadcast_to(m, (BLOCK, HEAD_DIM))

            def values(idx, g=g, d=d, with_prev=with_prev, width=width, locate=locate, base=base):
                b, r = locate(idx)
                v1 = jnp.concatenate([keys((v_ref, v4_ref), b, r, d, with_prev), ones[0:width]], axis=1)
                acc = _dot(p_ref[base + idx, :, 0:width], v1)
                ob_refs[g][out_rows(b, r, d), :] = acc[:, 0:HEAD_DIM]
                sb_refs[g][out_rows(b, r, d), :] = acc[:, HEAD_DIM:]

            groups.append((count, probabilities, values))
            base += count

    for count, probabilities, _ in groups:
        for idx in range(count):
            probabilities(idx)
    for count, _, values in groups:
        for idx in range(count):
            values(idx)

    chunk = 1024

    def combine(c, carry):
        rw = pl.ds(pl.multiple_of(c * chunk, chunk), chunk)

        def read(ref, g):
            if DILATED[g][1] != PITCHED_DILATION:
                return ref[rw, :]
            pieces = [ref[pl.ds((j % 2) * 8 * PITCH + c * (chunk // PITCHED_DILATION) + j // 2, 8, stride=PITCH), :]
                      for j in range(chunk // 8)]
            return jnp.concatenate(pieces, axis=0)

        ms = [read(mb, g) for g, mb in enumerate(mb_refs)]
        m = functools.reduce(jnp.maximum, ms)
        ws = [jnp.exp2((mg - m) * exp2_scale) for mg in ms]
        num = functools.reduce(jnp.add, [w * read(ob, g) for g, (w, ob) in enumerate(zip(ws, ob_refs))])
        den = functools.reduce(jnp.add, [w * read(sb, g) for g, (w, sb) in enumerate(zip(ws, sb_refs))])
        o_ref[rw, :] = (num / den).astype(o_ref.dtype)
        return carry

    lax.fori_loop(0, seq // chunk, combine, 0)


def _attn_prompt(q, k, v, *, seq, casts=()):
    n_heads, m, _ = q.shape
    n_seq = m // seq
    spec = pl.BlockSpec((None, seq, HEAD_DIM), lambda n, h: (h, n, 0))
    kern = functools.partial(_attn_prompt_kernel, seq=seq)
    (attn,), cast_out = _call_with_casts(
        kern,
        grid=(n_seq, n_heads),
        in_specs=[spec, spec, spec],
        out_specs=[spec],
        out_shape=[jax.ShapeDtypeStruct(q.shape, BF16)],
        scratch_shapes=(
            [pltpu.VMEM((PITCHED_DILATION * PITCH if d == PITCHED_DILATION else seq, HEAD_DIM), F32)
             for _ in range(3) for _, d in DILATED]
            + [pltpu.VMEM((len(DILATED) * seq // BLOCK, BLOCK, 2 * BLOCK), BF16),
               pltpu.VMEM((BLOCK, 2 * BLOCK), F32)]
            + [pltpu.VMEM((seq, HEAD_DIM), F32) for _ in range(3)]),
        args=(q, k, v),
        casts=casts,
        step_of=lambda n, h: n * n_heads + h,
        name="attn_prompt",
    )
    return attn, cast_out


def _attn_sample_kernel(q_ref, kn_ref, vn_ref, *refs, group):
    n_br = len(DILATED)
    kc_refs, vc_refs, o_ref = refs[:n_br], refs[n_br:2 * n_br], refs[2 * n_br]
    for j in range(group):
        q = q_ref[j] * (HEAD_DIM ** -0.5 * LOG2_E)
        t_new = jnp.sum(q * kn_ref[j], axis=-1, keepdims=True)
        t_br = [jnp.sum(kc[j] * q[None], axis=-1, keepdims=True) for kc in kc_refs]
        m = t_new
        for t in t_br:
            m = jnp.maximum(m, jnp.max(t, axis=0))
        p_new = jnp.exp2(t_new - m) * float(n_br)
        den = p_new
        num = p_new * vn_ref[j]
        for t, vc in zip(t_br, vc_refs):
            p = jnp.exp2(t - m[None])
            den = den + jnp.sum(p, axis=0)
            num = num + jnp.sum(p * vc[j], axis=0)
        out = num / den
        o_ref[j] = out
    return out


def _attn_sample_operands(q, k_new, v_new, k_cache, v_cache, *, group):
    n, heads, _ = q.shape
    win = k_cache.shape[1]
    row = pl.BlockSpec((group, heads, HEAD_DIM), lambda i: (i, 0, 0))
    cache_specs, k_views, v_views = [], [], []
    for w, d in DILATED:
        assert w == BLOCK * d and win % w == 0
        last = win // w - 1
        cache_specs.append(pl.BlockSpec((group, BLOCK, None, heads, HEAD_DIM),
                                        lambda i, last=last: (i, last, 0, 0, 0)))
        k_views.append(k_cache.reshape(n, win // d, d, heads, HEAD_DIM))
        v_views.append(v_cache.reshape(n, win // d, d, heads, HEAD_DIM))
    return ([row, row, row] + cache_specs + cache_specs, row, jax.ShapeDtypeStruct((n, heads, HEAD_DIM), F32),
            (q, k_new, v_new, *k_views, *v_views))


def _mix_mem_prompt_kernel(run_casts, pool_ref, attn_ref, x_ref, mk_ref, mv_ref, wo_ref,
                           gmix_ref, gpre_ref, wq_ref, wxo_ref, gmem_ref, *refs, sample_group):
    n_sample_in = 3 + 2 * len(DILATED)
    sample_in, (x2_ref, attn_s_ref) = refs[:n_sample_in], refs[n_sample_in:]
    run_casts()
    scale = HEAD_DIM ** -0.5
    mixed = jnp.concatenate([pool_ref[...]] + [attn_ref[h] for h in range(N_HEADS)], axis=-1)
    y = _dot(mixed, wo_ref[...])
    last = _attn_sample_kernel(*sample_in, attn_s_ref, group=sample_group)
    zero = jnp.minimum(jnp.abs(last[0:1, :]), 0.0)
    gmix = gmix_ref[...] + jnp.concatenate([zero] * (gmix_ref.shape[1] // HEAD_DIM), axis=1)
    x1 = x_ref[...] + _rmsnorm(y, gmix)
    qm = _dot(_rmsnorm(x1, gpre_ref[...]).astype(BF16), wq_ref[...]).astype(BF16)
    heads = []
    for h in range(MEM_HEADS):
        cols = slice(h * HEAD_DIM, (h + 1) * HEAD_DIM)
        s = _dot_nt(qm[:, cols], mk_ref[:, cols].astype(BF16)) * scale
        p = jnp.exp(s - jnp.max(s, axis=-1, keepdims=True))
        o = _dot(p.astype(BF16), mv_ref[:, cols].astype(BF16))
        heads.append((o / jnp.sum(p, axis=-1, keepdims=True)).astype(BF16))
    y = _dot(jnp.concatenate(heads, axis=-1), wxo_ref[...])
    x2_ref[...] = x1 + _rmsnorm(y, gmem_ref[...])


def _mix_mem_prompt(pool_out, attn, x, mk, mv, w_out, g_mix_post, g_mem_pre, w_xq, w_xo, g_mem_post,
                    sample_attn, *, tm, seq, casts=()):
    m, d = x.shape
    n_tiles = m // tm
    tiles_per_seq = seq // tm
    half = pool_out.shape[1]
    mem_spec = pl.BlockSpec((N_MEM, MEM_WIDTH), lambda i: (i // tiles_per_seq, 0))
    gain_spec = _resident((1, d), lambda i: (0, 0))
    n_sample = sample_attn[0].shape[0]
    assert n_sample % n_tiles == 0
    sample_group = n_sample // n_tiles
    s_in_specs, s_out_spec, s_out_shape, s_args = _attn_sample_operands(*sample_attn, group=sample_group)
    (x2, attn_s), cast_out = _call_with_casts(
        functools.partial(_mix_mem_prompt_kernel, sample_group=sample_group),
        grid=(n_tiles,),
        in_specs=[
            pl.BlockSpec((tm, half), lambda i: (i, 0)),
            pl.BlockSpec((N_HEADS, tm, HEAD_DIM), lambda i: (0, i, 0)),
            pl.BlockSpec((tm, d), lambda i: (i, 0)),
            mem_spec,
            mem_spec,
            _resident(w_out.shape, lambda i: (0, 0)),
            gain_spec,
            gain_spec,
            _resident(w_xq.shape, lambda i: (0, 0)),
            _resident(w_xo.shape, lambda i: (0, 0)),
            gain_spec,
        ] + s_in_specs,
        out_specs=[pl.BlockSpec((tm, d), lambda i: (i, 0)), s_out_spec],
        out_shape=[jax.ShapeDtypeStruct((m, d), F32), s_out_shape],
        args=(pool_out, attn, x, mk, mv, w_out, g_mix_post, g_mem_pre, w_xq, w_xo, g_mem_post) + tuple(s_args),
        casts=casts,
        name="mix_mem_prompt",
    )
    return x2, attn_s, cast_out


def _mem_attn_sample_kernel(qm_ref, mk_ref, mv_ref, o_ref, *, group):
    heads = qm_ref.shape[1]
    both = lambda a: jnp.concatenate([a, a], axis=0)
    fold = lambda a: a[0:heads] + a[heads:]
    for j in range(group):
        q = both(qm_ref[j] * (HEAD_DIM ** -0.5 * LOG2_E))
        t = jnp.sum(mk_ref[j] * q[None], axis=-1, keepdims=True)
        m = jnp.max(t, axis=0)
        m = both(jnp.maximum(m[0:heads], m[heads:]))
        p = jnp.exp2(t - m[None])
        o_ref[j] = fold(jnp.sum(p * mv_ref[j], axis=0)) / fold(jnp.sum(p, axis=0))


def _mem_attn_sample(qm, mem_k, mem_v):
    n, heads, _ = qm.shape
    group = SAMPLE_GROUP
    row = pl.BlockSpec((group, heads, HEAD_DIM), lambda i: (i, 0, 0))
    mem = pl.BlockSpec((group, N_MEM // 2, 2 * heads, HEAD_DIM), lambda i: (i, 0, 0, 0))
    pairs = lambda a: a.reshape(n, N_MEM // 2, 2 * heads, HEAD_DIM)
    return pl.pallas_call(
        functools.partial(_mem_attn_sample_kernel, group=group),
        grid=(n // group,),
        in_specs=[row, mem, mem],
        out_specs=row,
        out_shape=jax.ShapeDtypeStruct((n, heads, HEAD_DIM), F32),
        compiler_params=_params(1),
        name="mem_attn_sample",
    )(qm, pairs(mem_k), pairs(mem_v))


def _mem_out_sample_kernel(a_ref, w_ref, g_ref, r_ref, o_ref):
    y = jnp.zeros(r_ref.shape, F32)
    for h in range(a_ref.shape[1]):
        y = y + _dot(a_ref[:, h, :].astype(BF16), w_ref[h * HEAD_DIM:(h + 1) * HEAD_DIM, :])
    o_ref[...] = r_ref[...] + _rmsnorm(y, g_ref[...])


def _mem_out_sample(a, w, g, resid):
    return pl.pallas_call(
        _mem_out_sample_kernel,
        out_shape=jax.ShapeDtypeStruct(resid.shape, F32),
        compiler_params=pltpu.CompilerParams(vmem_limit_bytes=VMEM_LIMIT_BYTES),
        name="mem_out_sample",
    )(a, w, g, resid)


def _ffn_kernel(x_ref, xs_ref, gpre_ref, w1_ref, w2_ref, gpost_ref, o_ref, os_ref,
                xn_ref, xns_ref, accs_ref, *, tm, nf):
    i, f = pl.program_id(0), pl.program_id(1)
    last_f = nf - 1

    @pl.when((i == 0) & (f == 0))
    def _():
        xns_ref[...] = _rmsnorm(xs_ref[...], gpre_ref[...]).astype(BF16)
        accs_ref[...] = jnp.zeros_like(accs_ref)

    @pl.when(f == 0)
    def _():
        xn_ref[...] = _rmsnorm(x_ref[...], gpre_ref[...]).astype(BF16)
        o_ref[...] = jnp.zeros_like(o_ref)

    def mlp(lhs):
        hidden = jnp.square(jnp.maximum(_dot(lhs, w1_ref[...]), 0.0)).astype(BF16)
        return _dot(hidden, w2_ref[...])

    @pl.when(i == 0)
    def _():
        both = mlp(jnp.concatenate([xn_ref[...], xns_ref[...]], axis=0))
        o_ref[...] += both[0:tm]
        accs_ref[...] += both[tm:]

    @pl.when(i > 0)
    def _():
        o_ref[...] += mlp(xn_ref[...])

    @pl.when(f == last_f)
    def _():
        o_ref[...] = x_ref[...] + _rmsnorm(o_ref[...], gpost_ref[...])

    @pl.when((f == last_f) & (i == 0))
    def _():
        os_ref[...] = xs_ref[...] + _rmsnorm(accs_ref[...], gpost_ref[...])


def _ffn(x, xs, g_pre, w1_chunks, w2, g_post, *, tm):
    m, d = x.shape
    ms = xs.shape[0]
    nf, _, tf = w1_chunks.shape
    n_tiles = m // tm
    sample_spec = _resident((ms, d), lambda i, f: (0, 0))
    return pl.pallas_call(
        functools.partial(_ffn_kernel, tm=tm, nf=nf),
        grid=(n_tiles, nf),
        in_specs=[
            pl.BlockSpec((tm, d), lambda i, f: (i, 0)),
            sample_spec,
            _resident((1, d), lambda i, f: (0, 0)),
            pl.BlockSpec((None, d, tf), lambda i, f: (f, 0, 0)),
            pl.BlockSpec((tf, d), lambda i, f: (f, 0)),
            _resident((1, d), lambda i, f: (0, 0)),
        ],
        out_specs=[pl.BlockSpec((tm, d), lambda i, f: (i, 0)),
                   pl.BlockSpec((ms, d), lambda i, f: (0, 0))],
        out_shape=[jax.ShapeDtypeStruct((m, d), F32), jax.ShapeDtypeStruct((ms, d), F32)],
        scratch_shapes=[pltpu.VMEM((tm, d), BF16), pltpu.VMEM((ms, d), BF16), pltpu.VMEM((ms, d), F32)],
        compiler_params=_params(2),
        name="ffn",
    )(x, xs, g_pre, w1_chunks, w2, g_post)


def kernel(x_prompt, x_sample, state_pool, cache_attn_k, cache_attn_v, cache_mem_k, cache_mem_v, mem_prompt,
           g_mix_pre, g_mix_post, g_mem_pre, g_mem_post, g_ffn_pre, g_ffn_post, g_mem_kv,
           w_in, w_pool, pool_scale, w_out, w_xq, w_mem_kv, w_xo, w_ff1, w_ff2):
    depth = w_in.shape[0]
    assert depth == 1
    batch, seq, d = x_prompt.shape
    dec_batch, dec_seq, _ = x_sample.shape
    assert dec_seq == 1

    l = 0
    scale = pool_scale[l][None, :]
    gain = lambda g: g[l][None, :]

    xp = x_prompt.reshape(batch * seq, d)
    xs = x_sample.reshape(dec_batch, d)
    w_pool_f = w_pool[l]
    tm = ROW_TILE


    mem = mem_prompt.reshape(batch * N_MEM, d)
    (mk, mv, mk_h, mv_h, cos_p, sin_p, cos_s, sin_s), (w_in_b,) = _norm_proj(
        mem, gain(g_mem_kv), w_mem_kv[l], tm=MEM_KV_TILE, ncol=MEM_WIDTH,
        outputs=((0, FLAT), (1, FLAT), (0, HEADS), (1, HEADS)), name="mem_kv", casts=(w_in[l],),
        seq=seq, n_sample=dec_batch, sample_pos=PAST_LEN)

    ((pool_out, q, k, v, k_h, v_h, pool_state, us, qs_h, ks_h, vs_h),
     (w_out_b, w_xq_b, w_xo_b)) = _in_proj_pool(
        xp, xs, gain(g_mix_pre), w_in_b, cos_p, sin_p, cos_s, sin_s, w_pool_f, scale, tm=IN_PROJ_TILE, seq=seq,
        casts=(w_out[l], w_xq[l], w_xo[l]))
    attn, (w_ff1_b, w_ff2_b) = _attn_prompt(
        q, k, v, seq=seq, casts=((w_ff1[l], w_ff1.shape[2] // FFN_CHUNK), w_ff2[l]))
    heads3 = lambda a: a.reshape(dec_batch, -1, HEAD_DIM)
    x2, attn_s, _ = _mix_mem_prompt(
        pool_out, attn, xp, mk, mv, w_out_b, gain(g_mix_post), gain(g_mem_pre), w_xq_b, w_xo_b, gain(g_mem_post),
        (heads3(qs_h), heads3(ks_h), heads3(vs_h), cache_attn_k[l], cache_attn_v[l]), tm=tm, seq=seq)

    new_pool_s, x1s, qms_h = _mix_sample(us, state_pool[l], w_pool_f, scale, attn_s, xs, w_out_b,
                                         gain(g_mix_post), gain(g_mem_pre), w_xq_b, pos=PAST_LEN)
    mem_o_s = _mem_attn_sample(heads3(qms_h), cache_mem_k[l], cache_mem_v[l])
    x2s = _mem_out_sample(mem_o_s, w_xo_b, gain(g_mem_post), x1s)

    yp, ys = _ffn(x2, x2s, gain(g_ffn_pre), w_ff1_b, w_ff2_b, gain(g_ffn_post), tm=tm)

    keep = min(max(w for w, _ in DILATED), seq)
    return (
        yp.reshape(batch, seq, d),
        ys.reshape(dec_batch, 1, d),
        pool_state.reshape(batch, STATE_ROWS, POOL_WIDTH)[:, STATE_ROWS - POOL_STATE:][None],
        new_pool_s[None],
        k_h.reshape(batch, seq, N_HEADS, HEAD_DIM)[:, seq - keep:][None],
        v_h.reshape(batch, seq, N_HEADS, HEAD_DIM)[:, seq - keep:][None],
        ks_h.reshape(dec_batch, 1, N_HEADS, HEAD_DIM)[None],
        vs_h.reshape(dec_batch, 1, N_HEADS, HEAD_DIM)[None],
        mk_h.reshape(batch, N_MEM, MEM_HEADS, HEAD_DIM)[None],
        mv_h.reshape(batch, N_MEM, MEM_HEADS, HEAD_DIM)[None],
    )
```
